```python
import jax, jax.numpy as jnp
from jax import lax
import numpy as np

D_MODEL = 1024
BATCH = 32
SEQ = 256
DEPTH = 1
DEC_BATCH = 8
DEC_SEQ = 1024
PAST_LEN = 512

GRID_W = 64
MLA_HEADS = 8
QK_NOPE = 64
QK_ROPE = 32
V_DIM = 64
Q_RANK = 768
KV_RANK = 256
CONV_CH = 512
CONV_WIDTH = 31
MIX_WIDTH = MLA_HEADS * V_DIM + CONV_CH
IN_COLS = Q_RANK + KV_RANK + QK_ROPE + 2 * CONV_CH
N_EXPERTS = 16
EXPERT_FF = 1024
EC_FACTOR = 2
Q_BLOCK = 128
ROPE_BASE = 10000.0
EPS = 1e-6
N_MOD = 6

kernel_name = "hybrid_mla_conformer_ec_diffusion_step"


def rmsnorm(x, g):
    x32 = x.astype(jnp.float32)
    y = x32 * lax.rsqrt(jnp.mean(x32 * x32, axis=-1, keepdims=True) + EPS)
    return (y * g.astype(jnp.float32)).astype(x.dtype)


def layernorm(x, g, b):
    x32 = x.astype(jnp.float32)
    mu = jnp.mean(x32, axis=-1, keepdims=True)
    var = jnp.mean(jnp.square(x32 - mu), axis=-1, keepdims=True)
    y = (x32 - mu) * lax.rsqrt(var + EPS)
    return (y * g.astype(jnp.float32) + b.astype(jnp.float32)).astype(x.dtype)


def adaln(c, w_mod, b_mod):
    m = jax.nn.silu(c) @ w_mod + b_mod
    return jnp.split(m[:, None, :], N_MOD, axis=-1)


def modulate(x, shift, scale):
    return x * (1 + scale) + shift


def axial_rope_2d(x, n_tokens):
    n_rows = n_tokens // GRID_W
    row = jnp.repeat(jnp.arange(n_rows), GRID_W)
    col = jnp.tile(jnp.arange(GRID_W), n_rows)
    half = QK_ROPE // 2
    inv_freq = 1.0 / (ROPE_BASE ** (jnp.arange(0, half, 2, dtype=jnp.float32) / half))

    def rot(xp, pos):
        ang = pos.astype(jnp.float32)[:, None] * inv_freq[None, :]
        ang = jnp.concatenate([ang, ang], axis=-1)[None, :, None, :]
        cos = jnp.cos(ang).astype(x.dtype)
        sin = jnp.sin(ang).astype(x.dtype)
        x1, x2 = jnp.split(xp, 2, axis=-1)
        return xp * cos + jnp.concatenate([-x2, x1], axis=-1) * sin

    xr, xc = jnp.split(x, 2, axis=-1)
    return jnp.concatenate([rot(xr, row), rot(xc, col)], axis=-1)


def expand_kv(ckv, krope, w_kv_up):
    B, T, _ = ckv.shape
    kv = (ckv @ w_kv_up).reshape(B, T, MLA_HEADS, QK_NOPE + V_DIM)
    k_nope, v = jnp.split(kv, [QK_NOPE], axis=-1)
    k_r = jnp.broadcast_to(krope[:, :, None, :], (B, T, MLA_HEADS, QK_ROPE))
    return jnp.concatenate([k_nope, k_r], axis=-1), v


def mha_blocks(q, k, v):
    B, S, H, Dh = q.shape
    nb = S // Q_BLOCK
    scale = (QK_NOPE + QK_ROPE) ** -0.5
    qb = q.reshape(B, nb, Q_BLOCK, H, Dh).transpose(1, 0, 2, 3, 4)

    def one_block(qi):
        s = jnp.einsum('bqhd,bkhd->bhqk', qi, k).astype(jnp.float32) * scale
        p = jax.nn.softmax(s, axis=-1).astype(v.dtype)
        return jnp.einsum('bhqk,bkhd->bqhd', p, v)

    o = lax.map(one_block, qb)
    return o.transpose(1, 0, 2, 3, 4).reshape(B, S, H * V_DIM)


def conformer_conv(u, conv_w, conv_b, ln_g, ln_b):
    a, g = jnp.split(u, 2, axis=-1)
    v = a * jax.nn.sigmoid(g)
    pad = (CONV_WIDTH - 1) // 2
    v = lax.conv_general_dilated(
        v, conv_w[:, None, :], window_strides=(1,), padding=[(pad, pad)],
        dimension_numbers=('NWC', 'WIO', 'NWC'), feature_group_count=CONV_CH) + conv_b
    return jax.nn.silu(layernorm(v, ln_g, ln_b))


def mixing_sublayer(x, shift, scale, gate, ctx_cache, attn_norm_g, w_in, q_norm_g, w_q_up,
                    kv_norm_g, w_kv_up, conv_w, conv_b, conv_ln_g, conv_ln_b, w_out):
    B, S, _ = x.shape
    h = modulate(rmsnorm(x, attn_norm_g), shift, scale)
    proj = h @ w_in
    q_lat, ckv, krope, glu_in = jnp.split(
        proj, [Q_RANK, Q_RANK + KV_RANK, Q_RANK + KV_RANK + QK_ROPE], axis=-1)
    q = (rmsnorm(q_lat, q_norm_g) @ w_q_up).reshape(B, S, MLA_HEADS, QK_NOPE + QK_ROPE)
    ckv = rmsnorm(ckv, kv_norm_g)
    if ctx_cache is None:
        k, v = expand_kv(ckv, krope, w_kv_up)
    else:
        q = jnp.concatenate([q[..., :QK_NOPE], axial_rope_2d(q[..., QK_NOPE:], S)], axis=-1)
        krope_rot = axial_rope_2d(krope[:, :, None, :], S)[:, :, 0, :]
        k_lat, v_lat = expand_kv(ckv, krope_rot, w_kv_up)
        k_ctx, v_ctx = expand_kv(ctx_cache[0], ctx_cache[1], w_kv_up)
        k = jnp.concatenate([k_ctx, k_lat], axis=1)
        v = jnp.concatenate([v_ctx, v_lat], axis=1)
    attn = mha_blocks(q, k, v)
    conv = conformer_conv(glu_in, conv_w, conv_b, conv_ln_g, conv_ln_b)
    out = jnp.concatenate([attn, conv], axis=-1) @ w_out
    return x + gate * out, ckv, krope


def expert_choice_moe(h, w_router, w_gate, w_up, w_down):
    B, S, D = h.shape
    xf = h.reshape(B * S, D)
    n_tok = B * S
    cap = EC_FACTOR * n_tok // N_EXPERTS
    aff = jax.nn.softmax((xf @ w_router).astype(jnp.float32), axis=-1)
    gates, idx = lax.top_k(aff.T, cap)
    xs = xf[idx]
    a = jnp.einsum('ecd,edf->ecf', xs, w_gate)
    u = jnp.einsum('ecd,edf->ecf', xs, w_up)
    y = jnp.einsum('ecf,efd->ecd', jax.nn.silu(a) * u, w_down) * gates[..., None].astype(h.dtype)
    out = jnp.zeros((n_tok, D), h.dtype).at[idx.reshape(-1)].add(y.reshape(-1, D))
    return out.reshape(B, S, D)


def moe_sublayer(x, shift, scale, gate, ffn_norm_g, w_router, w_gate, w_up, w_down):
    h = modulate(rmsnorm(x, ffn_norm_g), shift, scale)
    return x + gate * expert_choice_moe(h, w_router, w_gate, w_up, w_down)


def setup_inputs(seed: int = 0) -> dict:
    key = jax.random.key(seed)
    ks = iter(jax.random.split(key, 32))

    def nrm(shape, scale=1.0):
        return jax.random.normal(next(ks), shape, jnp.float32) * scale

    def gain(shape):
        return 1.0 + 0.05 * jax.random.normal(next(ks), shape, jnp.float32)

    D = D_MODEL
    return {
        "x_prompt": nrm((BATCH, SEQ, D)),
        "x_sample": nrm((DEC_BATCH, DEC_SEQ, D)),
        "cache_ckv": nrm((DEC_BATCH, DEPTH, PAST_LEN, KV_RANK)),
        "cache_krope": nrm((DEC_BATCH, DEPTH, PAST_LEN, QK_ROPE)),
        "c": nrm((DEC_BATCH, D)),
        "c_ctx": nrm((D,)),
        "w_mod": nrm((DEPTH, D, N_MOD * D), 0.5 * D ** -0.5),
        "b_mod": nrm((DEPTH, N_MOD * D), 0.01),
        "attn_norm_g": gain((DEPTH, D)),
        "w_in": nrm((DEPTH, D, IN_COLS), D ** -0.5),
        "q_norm_g": gain((DEPTH, Q_RANK)),
        "w_q_up": nrm((DEPTH, Q_RANK, MLA_HEADS * (QK_NOPE + QK_ROPE)), Q_RANK ** -0.5),
        "kv_norm_g": gain((DEPTH, KV_RANK)),
        "w_kv_up": nrm((DEPTH, KV_RANK, MLA_HEADS * (QK_NOPE + V_DIM)), KV_RANK ** -0.5),
        "conv_w": nrm((DEPTH, CONV_WIDTH, CONV_CH), CONV_WIDTH ** -0.5),
        "conv_b": nrm((DEPTH, CONV_CH), 0.01),
        "conv_ln_g": gain((DEPTH, CONV_CH)),
        "conv_ln_b": nrm((DEPTH, CONV_CH), 0.01),
        "w_out": nrm((DEPTH, MIX_WIDTH, D), MIX_WIDTH ** -0.5),
        "ffn_norm_g": gain((DEPTH, D)),
        "w_router": nrm((DEPTH, D, N_EXPERTS), D ** -0.5),
        "w_gate": nrm((DEPTH, N_EXPERTS, D, EXPERT_FF), D ** -0.5),
        "w_up": nrm((DEPTH, N_EXPERTS, D, EXPERT_FF), D ** -0.5),
        "w_down": nrm((DEPTH, N_EXPERTS, EXPERT_FF, D), EXPERT_FF ** -0.5),
        "final_norm_g": gain((D,)),
    }


def reference(x_prompt, x_sample, cache_ckv, cache_krope, c, c_ctx, w_mod, b_mod,
              attn_norm_g, w_in, q_norm_g, w_q_up, kv_norm_g, w_kv_up, conv_w, conv_b,
              conv_ln_g, conv_ln_b, w_out, ffn_norm_g, w_router, w_gate, w_up, w_down,
              final_norm_g):
    xp = x_prompt
    xs = x_sample
    ckv_states = []
    krope_states = []
    for l in range(DEPTH):
        mix_w = (attn_norm_g[l], w_in[l], q_norm_g[l], w_q_up[l], kv_norm_g[l], w_kv_up[l],
                 conv_w[l], conv_b[l], conv_ln_g[l], conv_ln_b[l], w_out[l])
        moe_w = (ffn_norm_g[l], w_router[l], w_gate[l], w_up[l], w_down[l])
        sh1c, sc1c, g1c, sh2c, sc2c, g2c = adaln(c_ctx[None, :], w_mod[l], b_mod[l])
        sh1s, sc1s, g1s, sh2s, sc2s, g2s = adaln(c, w_mod[l], b_mod[l])
        xp, ckv_p, krope_p = mixing_sublayer(xp, sh1c, sc1c, g1c, None, *mix_w)
        xp = moe_sublayer(xp, sh2c, sc2c, g2c, *moe_w)
        ckv_states.append(ckv_p)
        krope_states.append(krope_p)
        xs, _, _ = mixing_sublayer(xs, sh1s, sc1s, g1s,
                                   (cache_ckv[:, l], cache_krope[:, l]), *mix_w)
        xs = moe_sublayer(xs, sh2s, sc2s, g2s, *moe_w)
    y_prompt = rmsnorm(xp, final_norm_g)
    y_sample = rmsnorm(xs, final_norm_g)
    new_ckv = jnp.stack(ckv_states, axis=1)
    new_krope = jnp.stack(krope_states, axis=1)
    return (y_prompt, y_sample, new_ckv, new_krope)
```

```python
import functools

import jax
import jax.numpy as jnp
import numpy as np
from jax import lax
from jax.experimental import pallas as pl
from jax.experimental.pallas import tpu as pltpu

F32 = jnp.float32
BF16 = jnp.bfloat16
I32 = jnp.int32

LANES = 128
BF16_ROWS = 16
VMEM_LIMIT = 56 * 1024 * 1024

D_MODEL = 1024
GRID_W = 64
MLA_HEADS = 8
QK_NOPE = 64
QK_ROPE = 32
V_DIM = 64
Q_RANK = 768
KV_RANK = 256
CONV_CH = 512
CONV_WIDTH = 31
N_EXPERTS = 16
EXPERT_FF = 1024
EC_FACTOR = 2
ROPE_BASE = 10000.0
EPS = 1e-6
N_MOD = 6

HEAD_PAD = LANES
QK_WIDTH = MLA_HEADS * HEAD_PAD
ROW_TILE = 256
SLOT_WIN = 80
AUG = LANES
XS_W = D_MODEL + AUG
HIGHEST = lax.Precision.HIGHEST


def _params(n_axes, vmem=None):
    return pltpu.CompilerParams(
        dimension_semantics=("arbitrary",) * n_axes,
        vmem_limit_bytes=vmem)


def _rms(x, g):
    ms = jnp.mean(x * x, axis=-1, keepdims=True)
    return x * lax.rsqrt(ms + EPS) * g


def _bdot(a, b):
    return jnp.dot(a, b, preferred_element_type=F32)


def _adaln_kernel(c_ref, w_ref, b_ref, o_ref):
    c = c_ref[...]
    s = c * jax.nn.sigmoid(c)
    o_ref[...] = jnp.dot(s, w_ref[...], precision=HIGHEST,
                         preferred_element_type=F32) + b_ref[...]


def _adaln(cc, w_mod, b_mod):
    rows = cc.shape[0]
    n_out = w_mod.shape[1]
    blk = D_MODEL
    return pl.pallas_call(
        _adaln_kernel,
        out_shape=jax.ShapeDtypeStruct((rows, n_out), F32),
        grid=(n_out // blk,),
        in_specs=[pl.BlockSpec((rows, D_MODEL), lambda j: (0, 0)),
                  pl.BlockSpec((D_MODEL, blk), lambda j: (0, j)),
                  pl.BlockSpec((1, blk), lambda j: (0, j))],
        out_specs=pl.BlockSpec((rows, blk), lambda j: (0, j)),
        compiler_params=_params(1),
        name="adaln",
    )(cc, w_mod, b_mod)


def _rot_half(x, lane, base):
    rel = lane - base
    first = ((rel >= 0) & (rel < 8)) | ((rel >= 16) & (rel < 24))
    return jnp.where(first, -pltpu.roll(x, LANES - 8, axis=1), pltpu.roll(x, 8, axis=1))


def _front_kernel(rope, x_ref, sh_ref, sc_ref, g_ref, wqc_ref, wkr_ref, wglu_ref,
                  qg_ref, wq_ref, kvg_ref, wk_ref, wv_ref, pk_ref, *rest):
    if rope:
        cq_ref, sq_ref, q_ref, k_ref, v_ref, glu_ref = rest
    else:
        q_ref, k_ref, v_ref, glu_ref, ckv_ref, kr_ref = rest
    x = x_ref[...]
    h = _rms(x, g_ref[...]) * (1.0 + sc_ref[0]) + sh_ref[0]
    hb = h.astype(BF16)
    qc = _bdot(hb, wqc_ref[...])
    kr = _bdot(hb, wkr_ref[...])
    glu = _bdot(hb, wglu_ref[...])
    glu_ref[...] = glu[:, :CONV_CH] * jax.nn.sigmoid(glu[:, CONV_CH:])

    qn = _rms(qc[:, :Q_RANK], qg_ref[...]).astype(BF16)
    q = _bdot(qn, wq_ref[...])
    ckv = _rms(qc[:, Q_RANK:], kvg_ref[...])
    ckv_b = ckv.astype(BF16)
    kk = _bdot(ckv_b, wk_ref[...])
    v_ref[...] = _bdot(ckv_b, wv_ref[...]).astype(BF16)

    scale = (QK_NOPE + QK_ROPE) ** -0.5
    lane = lax.broadcasted_iota(I32, (x.shape[0], LANES), 1)
    if rope:
        cq = cq_ref[...]
        sq = sq_ref[...]
        for hd in range(MLA_HEADS):
            blk = q[:, hd * HEAD_PAD:(hd + 1) * HEAD_PAD]
            rot = blk * cq + _rot_half(blk, lane, QK_NOPE) * sq
            q_ref[:, hd * HEAD_PAD:(hd + 1) * HEAD_PAD] = (rot * scale).astype(BF16)
        krs = pltpu.roll(kr, QK_NOPE, axis=1)
        kr = pltpu.roll(krs * cq + _rot_half(krs, lane, QK_NOPE) * sq, LANES - QK_NOPE, axis=1)
        kr = jnp.where(lane < QK_ROPE, kr, 0.0)
    else:
        q_ref[...] = (q * scale).astype(BF16)
        ckv_ref[...] = ckv
        kr_ref[...] = kr[:, :QK_ROPE]
    k_ref[...] = (kk + _bdot(kr.astype(BF16), pk_ref[...])).astype(BF16)


def _front(x2d, mod3, mod_row, rows_per_mod, wts, rope_tabs):
    n = x2d.shape[0]
    rope = rope_tabs is not None
    tiles_per_mod = rows_per_mod // ROW_TILE
    full = lambda a: pl.BlockSpec(a.shape, lambda i: (0,) * a.ndim)

    def mod_spec(col):
        if mod_row is None:
            return pl.BlockSpec((1, 1, D_MODEL), lambda i: (i // tiles_per_mod, 0, col))
        return pl.BlockSpec((1, 1, D_MODEL), lambda i: (mod_row, 0, col))

    row = lambda w: pl.BlockSpec((ROW_TILE, w), lambda i: (i, 0))
    in_specs = [row(D_MODEL), mod_spec(0), mod_spec(1)] + [full(w) for w in wts]
    args = [x2d, mod3, mod3] + list(wts)
    out_shape = [jax.ShapeDtypeStruct((n, QK_WIDTH), BF16)] * 3 + \
                [jax.ShapeDtypeStruct((n, CONV_CH), F32)]
    out_specs = [row(QK_WIDTH)] * 3 + [row(CONV_CH)]
    if rope:
        seq_tiles = rope_tabs[0].shape[0] // ROW_TILE
        tab = pl.BlockSpec((ROW_TILE, LANES), lambda i: (i % seq_tiles, 0))
        in_specs += [tab, tab]
        args += list(rope_tabs)
    else:
        out_shape += [jax.ShapeDtypeStruct((n, KV_RANK), F32),
                      jax.ShapeDtypeStruct((n, QK_ROPE), F32)]
        out_specs += [row(KV_RANK), row(QK_ROPE)]
    return pl.pallas_call(
        functools.partial(_front_kernel, rope),
        out_shape=out_shape,
        grid=(n // ROW_TILE,),
        in_specs=in_specs,
        out_specs=out_specs,
        compiler_params=_params(1, VMEM_LIMIT),
        name="front_rope" if rope else "front",
    )(*args)


def _ctx_kernel(ckv_ref, kr_ref, wk_ref, wv_ref, pk_ref, k_ref, v_ref):
    cb = ckv_ref[...].astype(BF16)
    kr = kr_ref[...].astype(BF16)
    k_ref[...] = (_bdot(cb, wk_ref[...]) + _bdot(kr, pk_ref[...])).astype(BF16)
    v_ref[...] = _bdot(cb, wv_ref[...]).astype(BF16)


def _ctx_kv(ckv2d, kr2d, wk, wv, pk):
    n = ckv2d.shape[0]
    full = lambda a: pl.BlockSpec(a.shape, lambda i: (0,) * a.ndim)
    row = lambda w: pl.BlockSpec((ROW_TILE, w), lambda i: (i, 0))
    pk32 = pk[:QK_ROPE]
    return pl.pallas_call(
        _ctx_kernel,
        out_shape=[jax.ShapeDtypeStruct((n, QK_WIDTH), BF16)] * 2,
        grid=(n // ROW_TILE,),
        in_specs=[row(KV_RANK), row(QK_ROPE), full(wk), full(wv), full(pk32)],
        out_specs=[row(QK_WIDTH)] * 2,
        compiler_params=_params(1),
        name="ctx_kv",
    )(ckv2d, kr2d, wk, wv, pk32)


def _attn_kernel(n_kv, q_ref, *rest):
    kv_refs = rest[:2 * n_kv]
    o_ref = rest[2 * n_kv]
    nt = (((1,), (1,)), ((), ()))
    outs = []
    for hd in range(MLA_HEADS):
        sl = slice(hd * HEAD_PAD, (hd + 1) * HEAD_PAD)
        qh = q_ref[:, sl]
        ss = [lax.dot_general(qh, kv_refs[2 * j][:, sl], nt, preferred_element_type=F32)
              for j in range(n_kv)]
        m = ss[0].max(axis=-1, keepdims=True)
        for s in ss[1:]:
            m = jnp.maximum(m, s.max(axis=-1, keepdims=True))
        ps = [jnp.exp(s - m) for s in ss]
        l = ps[0].sum(axis=-1, keepdims=True)
        for p in ps[1:]:
            l = l + p.sum(axis=-1, keepdims=True)
        o = _bdot(ps[0].astype(BF16), kv_refs[1][:, sl])
        for j in range(1, n_kv):
            o = o + _bdot(ps[j].astype(BF16), kv_refs[2 * j + 1][:, sl])
        outs.append(o / l)
    for j in range(MLA_HEADS // 2):
        o_ref[:, j * LANES:(j + 1) * LANES] = (outs[2 * j] + outs[2 * j + 1]).astype(BF16)


def _attention(q, kvs, batch, seq):
    n = q.shape[0]
    q_tiles = seq // ROW_TILE
    in_specs = [pl.BlockSpec((ROW_TILE, QK_WIDTH), lambda b, i: (b * q_tiles + i, 0))]
    args = [q]
    for k, v, t in kvs:
        spec = pl.BlockSpec((t, QK_WIDTH), lambda b, i: (b, 0))
        in_specs += [spec, spec]
        args += [k, v]
    return pl.pallas_call(
        functools.partial(_attn_kernel, len(kvs)),
        out_shape=jax.ShapeDtypeStruct((n, MLA_HEADS * V_DIM), BF16),
        grid=(batch, q_tiles),
        in_specs=in_specs,
        out_specs=pl.BlockSpec((ROW_TILE, MLA_HEADS * V_DIM), lambda b, i: (b * q_tiles + i, 0)),
        compiler_params=_params(2, VMEM_LIMIT),
        name="attention",
    )(*args)


CONV_PAD = (CONV_WIDTH - 1) // 2
CONV_HALO = 16


def _back_kernel(seq, x_ref, attn_ref, glu_ref, cw_ref, cb_ref, lg_ref, lb_ref,
                 woa_ref, woc_ref, g1_ref, sh_ref, sc_ref, fg_ref, wr_ref,
                 x1_ref, h2_ref, afft_ref, vpad_ref):
    i = pl.program_id(1)

    @pl.when(i == 0)
    def _():
        zeros = jnp.zeros((CONV_HALO, CONV_CH), F32)
        vpad_ref[0:CONV_HALO, :] = zeros
        vpad_ref[CONV_HALO + seq:, :] = zeros
        vpad_ref[CONV_HALO:CONV_HALO + seq, :] = glu_ref[0]

    r0 = pl.multiple_of(i * ROW_TILE, ROW_TILE)
    win = vpad_ref[pl.ds(r0, ROW_TILE + 2 * CONV_HALO), :]
    cw = cw_ref[...]
    acc = jnp.broadcast_to(cb_ref[...], (ROW_TILE, CONV_CH))
    for t in range(CONV_WIDTH):
        o = t + CONV_HALO - CONV_PAD
        acc = acc + win[o:o + ROW_TILE, :] * cw[t:t + 1, :]
    mu = jnp.mean(acc, axis=-1, keepdims=True)
    cen = acc - mu
    var = jnp.mean(cen * cen, axis=-1, keepdims=True)
    ln = cen * lax.rsqrt(var + EPS) * lg_ref[...] + lb_ref[...]
    cv = (ln * jax.nn.sigmoid(ln)).astype(BF16)

    mix = _bdot(attn_ref[...], woa_ref[...]) + _bdot(cv, woc_ref[...])
    x1 = x_ref[...] + g1_ref[0] * mix
    x1_ref[...] = x1
    h2 = _rms(x1, fg_ref[...]) * (1.0 + sc_ref[0]) + sh_ref[0]
    h2_ref[:, :D_MODEL] = h2.astype(BF16)

    logits = jnp.dot(h2, wr_ref[...], precision=HIGHEST, preferred_element_type=F32)
    lane = lax.broadcasted_iota(I32, logits.shape, 1)
    real = lane < N_EXPERTS
    m = jnp.max(jnp.where(real, logits, -jnp.inf), axis=-1, keepdims=True)
    e = jnp.exp(logits - m)
    aff = e / jnp.sum(jnp.where(real, e, 0.0), axis=-1, keepdims=True)
    afft_ref[...] = aff.T[:N_EXPERTS, :]
    hi = aff.astype(BF16)
    r1 = aff - hi.astype(F32)
    mid = r1.astype(BF16)
    lo = (r1 - mid.astype(F32)).astype(BF16)
    zero = jnp.zeros_like(hi)
    h2_ref[:, D_MODEL:] = jnp.where(
        real, hi, jnp.where(lane < 2 * N_EXPERTS, mid, jnp.where(lane < 3 * N_EXPERTS, lo, zero)))


def _back(x2d, attn, glu3, mod3, mod_row, wts, batch, seq):
    n = x2d.shape[0]
    tiles = seq // ROW_TILE
    full = lambda a: pl.BlockSpec(a.shape, lambda b, i: (0,) * a.ndim)

    def mod_spec(col):
        if mod_row is None:
            return pl.BlockSpec((1, 1, D_MODEL), lambda b, i: (b, 0, col))
        return pl.BlockSpec((1, 1, D_MODEL), lambda b, i: (mod_row, 0, col))

    row = lambda w: pl.BlockSpec((ROW_TILE, w), lambda b, i: (b * tiles + i, 0))
    cw, cb, lg, lb, woa, woc, fg, wr = wts
    in_specs = [row(D_MODEL), row(MLA_HEADS * V_DIM),
                pl.BlockSpec((1, seq, CONV_CH), lambda b, i: (b, 0, 0)),
                full(cw), full(cb), full(lg), full(lb), full(woa), full(woc),
                mod_spec(2), mod_spec(3), mod_spec(4), full(fg), full(wr)]
    return pl.pallas_call(
        functools.partial(_back_kernel, seq),
        out_shape=[jax.ShapeDtypeStruct((n, D_MODEL), F32),
                   jax.ShapeDtypeStruct((n, XS_W), BF16),
                   jax.ShapeDtypeStruct((N_EXPERTS, n), F32)],
        grid=(batch, tiles),
        in_specs=in_specs,
        out_specs=[row(D_MODEL), row(XS_W),
                   pl.BlockSpec((N_EXPERTS, ROW_TILE), lambda b, i: (0, b * tiles + i))],
        scratch_shapes=[pltpu.VMEM((seq + 2 * CONV_HALO, CONV_CH), F32)],
        compiler_params=_params(2, VMEM_LIMIT),
        name="back",
    )(x2d, attn, glu3, cw, cb, lg, lb, woa, woc, mod3, mod3, mod3, fg, wr)


def _route_kernel(cap, rows_per_expert, aff_ref, pos_ref, base_ref):
    rows = aff_ref.shape[0]
    aff = aff_ref[...]
    ones = jnp.ones((LANES, LANES), BF16)
    r_i = lax.broadcasted_iota(I32, (rows, rows), 0)
    c_i = lax.broadcasted_iota(I32, (rows, rows), 1)
    shift = rows_per_expert.bit_length() - 1
    assert rows_per_expert == 1 << shift
    same = lax.shift_right_logical(r_i, shift) == lax.shift_right_logical(c_i, shift)
    seg_all = jnp.where(same, 1.0, 0.0).astype(BF16)
    seg_before = jnp.where(same & (c_i < r_i), 1.0, 0.0).astype(BF16)
    l_r = lax.broadcasted_iota(I32, (LANES, LANES), 0)
    l_c = lax.broadcasted_iota(I32, (LANES, LANES), 1)
    tri = jnp.where(l_r <= l_c, 1.0, 0.0).astype(BF16)

    def as_b(mask):
        return jnp.where(mask, 1.0, 0.0).astype(BF16)

    def expert_count(mask_b):
        return _bdot(seg_all, _bdot(mask_b, ones).astype(BF16))

    def excl_cumsum(mask_b):
        incl = _bdot(mask_b, tri)
        before = _bdot(seg_before, _bdot(mask_b, ones).astype(BF16))
        return before + incl - mask_b.astype(F32), before

    def step(t, thr):
        cand = thr | lax.shift_left(jnp.int32(1), 30 - t)
        cnt = expert_count(as_b(aff >= pltpu.bitcast(cand, F32)))
        return jnp.where(cnt >= cap, cand, thr)

    thr = pltpu.bitcast(lax.fori_loop(0, 31, step, jnp.zeros(aff.shape, I32)), F32)
    gt = aff > thr
    eq = aff == thr
    need = cap - expert_count(as_b(gt))
    tie_rank, _ = excl_cumsum(as_b(eq))
    sel = gt | (eq & (tie_rank < need))
    pos, before = excl_cumsum(as_b(sel))
    pos_ref[...] = jnp.where(sel, pos, -1.0).astype(I32)
    base_ref[...] = before


def _route(aff_rows, cap, rows_per_expert):
    rows = aff_rows.shape[0]
    return pl.pallas_call(
        functools.partial(_route_kernel, cap, rows_per_expert),
        out_shape=[jax.ShapeDtypeStruct((rows, LANES), I32),
                   jax.ShapeDtypeStruct((rows, LANES), F32)],
        compiler_params=pltpu.CompilerParams(vmem_limit_bytes=VMEM_LIMIT),
        name="route",
    )(aff_rows)


def _window(off_ref, e, i, n_tiles, w, cap):
    first = off_ref[e * n_tiles + i] + w * SLOT_WIN
    return first, pl.multiple_of(jnp.minimum(first, cap - SLOT_WIN), BF16_ROWS)


def _one_hot_rows(pos_tile, wins):
    t = pos_tile.shape[1]
    r = lax.broadcasted_iota(I32, (SLOT_WIN, t), 0)
    blocks = []
    for e, (first, off) in enumerate(wins):
        pe = pos_tile[e:e + 1, :]
        rel = jnp.where(pe >= first, pe - off, -1)
        blocks.append(jnp.where(rel == r, 1.0, 0.0).astype(BF16))
    return jnp.concatenate(blocks, axis=0)


def _dispatch_kernel(cap, n_tiles, off_ref, nwin_ref, h2_ref, pos_ref, xs_hbm, xs_ref, sem):
    i = pl.program_id(0)

    @pl.when(i == 0)
    def _():
        def zero(e, carry):
            xs_ref[e] = jnp.zeros(xs_ref.shape[1:], BF16)
            return carry
        lax.fori_loop(0, N_EXPERTS, zero, 0)

    def window(w, carry):
        wins = [_window(off_ref, e, i, n_tiles, w, cap) for e in range(N_EXPERTS)]
        p = _one_hot_rows(pos_ref[...], wins)
        slab = _bdot(p, h2_ref[...])
        for e, (_, off) in enumerate(wins):
            cur = xs_ref[e, pl.ds(off, SLOT_WIN), :].astype(F32)
            xs_ref[e, pl.ds(off, SLOT_WIN), :] = (
                cur + slab[e * SLOT_WIN:(e + 1) * SLOT_WIN]).astype(BF16)
        return carry

    lax.fori_loop(0, nwin_ref[i], window, 0)

    @pl.when(i == n_tiles - 1)
    def _():
        cp = pltpu.make_async_copy(xs_ref, xs_hbm, sem)
        cp.start()
        cp.wait()


def _dispatch(h2aug, pos, offs, nwin, cap):
    n = h2aug.shape[0]
    n_tiles = n // ROW_TILE
    return pl.pallas_call(
        functools.partial(_dispatch_kernel, cap, n_tiles),
        out_shape=jax.ShapeDtypeStruct((N_EXPERTS, cap, XS_W), BF16),
        grid_spec=pltpu.PrefetchScalarGridSpec(
            num_scalar_prefetch=2,
            grid=(n_tiles,),
            in_specs=[pl.BlockSpec((ROW_TILE, XS_W), lambda i, *_: (i, 0)),
                      pl.BlockSpec((N_EXPERTS, ROW_TILE), lambda i, *_: (0, i))],
            out_specs=pl.BlockSpec(memory_space=pl.ANY),
            scratch_shapes=[pltpu.VMEM((N_EXPERTS, cap, XS_W), BF16),
                            pltpu.SemaphoreType.DMA]),
        compiler_params=_params(1, VMEM_LIMIT),
        name="dispatch",
    )(offs, nwin, h2aug, pos)


EXP_TILE = 256


def _experts_kernel(n_groups, *refs):
    xs_refs = refs[:n_groups]
    wg_ref, wu_ref, wd_ref = refs[n_groups:n_groups + 3]
    ys_refs = refs[n_groups + 3:2 * n_groups + 3]
    wgb, wub, wdb = refs[2 * n_groups + 3:]
    e = pl.program_id(0)

    @pl.when(pl.program_id(1) == 0)
    def _():
        chunk = 256
        for src, dst in ((wg_ref, wgb), (wu_ref, wub), (wd_ref, wdb)):
            for c in range(src.shape[1] // chunk):
                dst[c * chunk:(c + 1) * chunk, :] = src[0, c * chunk:(c + 1) * chunk, :].astype(BF16)

    x = jnp.concatenate([r[0, :, :D_MODEL] for r in xs_refs], axis=0)
    aug = jnp.concatenate([r[0, :, D_MODEL:] for r in xs_refs], axis=0).astype(F32)
    lane = lax.broadcasted_iota(I32, aug.shape, 1)
    mine = (lane == e) | (lane == e + N_EXPERTS) | (lane == e + 2 * N_EXPERTS)
    gate = jnp.sum(jnp.where(mine, aug, 0.0), axis=-1, keepdims=True)
    a = _bdot(x, wgb[...])
    u = _bdot(x, wub[...])
    hm = (a * jax.nn.sigmoid(a) * u).astype(BF16)
    y = _bdot(hm, wdb[...]) * gate
    for g, r in enumerate(ys_refs):
        r[0] = y[g * EXP_TILE:(g + 1) * EXP_TILE]


def _experts(xs_list, w_gate, w_up, w_down):
    n_groups = len(xs_list)
    cap = xs_list[0].shape[1]
    ff = w_gate.shape[2]
    xs_spec = pl.BlockSpec((1, EXP_TILE, XS_W), lambda e, j: (e, j, 0))
    w_spec = lambda a: pl.BlockSpec((1,) + a.shape[1:], lambda e, j: (e, 0, 0))
    ys_spec = pl.BlockSpec((1, EXP_TILE, D_MODEL), lambda e, j: (e, j, 0))
    return pl.pallas_call(
        functools.partial(_experts_kernel, n_groups),
        out_shape=[jax.ShapeDtypeStruct((N_EXPERTS, cap, D_MODEL), F32)] * n_groups,
        grid=(N_EXPERTS, cap // EXP_TILE),
        in_specs=[xs_spec] * n_groups + [w_spec(w_gate), w_spec(w_up), w_spec(w_down)],
        out_specs=[ys_spec] * n_groups,
        scratch_shapes=[pltpu.VMEM((D_MODEL, ff), BF16), pltpu.VMEM((D_MODEL, ff), BF16),
                        pltpu.VMEM((ff, D_MODEL), BF16)],
        compiler_params=_params(2, VMEM_LIMIT),
        name="experts",
    )(*xs_list, w_gate, w_up, w_down)


def _combine_kernel(cap, n_tiles, off_ref, nwin_ref, ys_hbm, pos_ref, x1_ref, g2_ref, fg_ref,
                    y_ref, buf_ref, sems):
    i = pl.program_id(0)
    slot = i % 2

    def copies(tile, w, s):
        out = []
        for e in range(N_EXPERTS):
            _, off = _window(off_ref, e, tile, n_tiles, w, cap)
            out.append(pltpu.make_async_copy(
                ys_hbm.at[e, pl.ds(off, SLOT_WIN), :],
                buf_ref.at[s, pl.ds(e * SLOT_WIN, SLOT_WIN), :],
                sems.at[s]))
        return out

    @pl.when(i == 0)
    def _():
        for cp in copies(0, 0, 0):
            cp.start()

    @pl.when(i + 1 < n_tiles)
    def _():
        for cp in copies(i + 1, 0, 1 - slot):
            cp.start()

    tn = (((0,), (0,)), ((), ()))

    def window(w, acc):
        @pl.when(w > 0)
        def _():
            for cp in copies(i, w, slot):
                cp.start()

        for cp in copies(i, w, slot):
            cp.wait()
        wins = [_window(off_ref, e, i, n_tiles, w, cap) for e in range(N_EXPERTS)]
        p = _one_hot_rows(pos_ref[...], wins)
        slab = buf_ref[slot]
        hi = slab.astype(BF16)
        lo = (slab - hi.astype(F32)).astype(BF16)
        return (acc + lax.dot_general(p, hi, tn, preferred_element_type=F32)
                + lax.dot_general(p, lo, tn, preferred_element_type=F32))

    moe = lax.fori_loop(0, nwin_ref[i], window, jnp.zeros((ROW_TILE, D_MODEL), F32))
    out = x1_ref[...] + g2_ref[0] * moe
    y_ref[...] = _rms(out, fg_ref[...])


def _combine(ys, pos, offs, nwin, x1, mod3, mod_row, rows_per_mod, final_g, cap):
    n = x1.shape[0]
    n_tiles = n // ROW_TILE
    tiles_per_mod = rows_per_mod // ROW_TILE
    if mod_row is None:
        g2 = pl.BlockSpec((1, 1, D_MODEL), lambda i, *_: (i // tiles_per_mod, 0, 5))
    else:
        g2 = pl.BlockSpec((1, 1, D_MODEL), lambda i, *_: (mod_row, 0, 5))
    return pl.pallas_call(
        functools.partial(_combine_kernel, cap, n_tiles),
        out_shape=jax.ShapeDtypeStruct((n, D_MODEL), F32),
        grid_spec=pltpu.PrefetchScalarGridSpec(
            num_scalar_prefetch=2,
            grid=(n_tiles,),
            in_specs=[pl.BlockSpec(memory_space=pl.ANY),
                      pl.BlockSpec((N_EXPERTS, ROW_TILE), lambda i, *_: (0, i)),
                      pl.BlockSpec((ROW_TILE, D_MODEL), lambda i, *_: (i, 0)),
                      g2,
                      pl.BlockSpec((1, D_MODEL), lambda i, *_: (0, 0))],
            out_specs=pl.BlockSpec((ROW_TILE, D_MODEL), lambda i, *_: (i, 0)),
            scratch_shapes=[pltpu.VMEM((2, N_EXPERTS * SLOT_WIN, D_MODEL), F32),
                            pltpu.SemaphoreType.DMA((2,))]),
        compiler_params=_params(1, VMEM_LIMIT),
        name="combine",
    )(offs, nwin, ys, pos, x1, mod3, final_g)


def _head_cols(w, per_head, place):
    k = w.shape[0]
    w3 = w.reshape(k, MLA_HEADS, per_head)
    out = jnp.zeros((k, MLA_HEADS, HEAD_PAD), w.dtype)
    if callable(place):
        for hd in range(MLA_HEADS):
            p = place(hd)
            out = out.at[:, hd, p:p + per_head].set(w3[:, hd])
    else:
        out = out.at[:, :, place:place + per_head].set(w3)
    return out.reshape(k, QK_WIDTH)


def _rope_tables(seq):
    half = QK_ROPE // 2
    pos = jnp.arange(seq)
    inv_freq = 1.0 / (ROPE_BASE ** (jnp.arange(0, half, 2, dtype=F32) / half))

    def cs(p):
        ang = p.astype(F32)[:, None] * inv_freq[None, :]
        ang = jnp.concatenate([ang, ang], axis=-1)
        return jnp.cos(ang), jnp.sin(ang)

    cr, sr = cs(pos // GRID_W)
    cc, sc = cs(pos % GRID_W)
    ones = jnp.ones((seq, QK_NOPE), F32)
    zpad = jnp.zeros((seq, HEAD_PAD - QK_NOPE - QK_ROPE), F32)
    cq = jnp.concatenate([ones, cr, cc, zpad], axis=-1)
    sq = jnp.concatenate([0.0 * ones, sr, sc, zpad], axis=-1)
    return cq, sq


def _prep_weights(w_in, q_norm_g, w_q_up, kv_norm_g, w_kv_up, attn_norm_g):
    wqc = w_in[:, :Q_RANK + KV_RANK].astype(BF16)
    wkr = jnp.pad(w_in[:, Q_RANK + KV_RANK:Q_RANK + KV_RANK + QK_ROPE],
                  ((0, 0), (0, LANES - QK_ROPE))).astype(BF16)
    wglu = w_in[:, Q_RANK + KV_RANK + QK_ROPE:].astype(BF16)
    wq = _head_cols(w_q_up, QK_NOPE + QK_ROPE, 0).astype(BF16)
    kv3 = w_kv_up.reshape(KV_RANK, MLA_HEADS, QK_NOPE + V_DIM)
    wk = _head_cols(kv3[:, :, :QK_NOPE].reshape(KV_RANK, -1), QK_NOPE, 0).astype(BF16)
    wv = _head_cols(kv3[:, :, QK_NOPE:].reshape(KV_RANK, -1), V_DIM,
                    lambda hd: (hd % 2) * V_DIM).astype(BF16)
    pk = np.zeros((LANES, QK_WIDTH), np.float32)
    for hd in range(MLA_HEADS):
        for j in range(QK_ROPE):
            pk[j, hd * HEAD_PAD + QK_NOPE + j] = 1.0
    pk = jnp.asarray(pk, BF16)
    return (attn_norm_g[None, :], wqc, wkr, wglu, q_norm_g[None, :], wq,
            kv_norm_g[None, :], wk, wv, pk)


def _route_plan(base, n_tokens, cap):
    rows_per_tile = ROW_TILE // LANES
    n_tiles = n_tokens // ROW_TILE
    start = base[:, 0].reshape(N_EXPERTS, -1)[:, ::rows_per_tile].astype(I32)
    end = jnp.concatenate([start[:, 1:], jnp.full((N_EXPERTS, 1), cap, I32)], axis=1)
    off = (start // BF16_ROWS) * BF16_ROWS
    span = end - off
    nwin = jnp.maximum(jnp.max((span + SLOT_WIN - 1) // SLOT_WIN, axis=0), 1).astype(I32)
    return off.reshape(N_EXPERTS * n_tiles), nwin


def kernel(x_prompt, x_sample, cache_ckv, cache_krope, c, c_ctx, w_mod, b_mod, attn_norm_g,
           w_in, q_norm_g, w_q_up, kv_norm_g, w_kv_up, conv_w, conv_b, conv_ln_g, conv_ln_b,
           w_out, ffn_norm_g, w_router, w_gate, w_up, w_down, final_norm_g):
    depth = w_mod.shape[0]
    assert depth == 1
    bp, sp, _ = x_prompt.shape
    bs, ss, _ = x_sample.shape
    past = cache_ckv.shape[2]
    ctx_row = bs

    mod_rows = 16
    cc = jnp.zeros((mod_rows, D_MODEL), F32).at[:bs].set(c).at[ctx_row].set(c_ctx)
    mod = _adaln(cc, w_mod[0], b_mod[0][None, :])
    mod3 = mod.reshape(mod_rows, 1, N_MOD * D_MODEL)

    fw = _prep_weights(w_in[0], q_norm_g[0], w_q_up[0], kv_norm_g[0], w_kv_up[0], attn_norm_g[0])
    wk, wv, pk = fw[7], fw[8], fw[9]
    wo = w_out[0].astype(BF16)
    wr = jnp.zeros((D_MODEL, LANES), F32)
    for r in range(3):
        wr = wr.at[:, r * N_EXPERTS:(r + 1) * N_EXPERTS].set(w_router[0])
    bw = (conv_w[0], conv_b[0][None, :], conv_ln_g[0][None, :], conv_ln_b[0][None, :],
          wo[:MLA_HEADS * V_DIM], wo[MLA_HEADS * V_DIM:], ffn_norm_g[0][None, :], wr)

    xp2 = x_prompt.reshape(bp * sp, D_MODEL)
    xs2 = x_sample.reshape(bs * ss, D_MODEL)

    qp, kp, vp, glup, ckvp, krp = _front(xp2, mod3, ctx_row, sp, fw, None)
    attn_p = _attention(qp, [(kp, vp, sp)], bp, sp)
    x1p, h2p, afftp = _back(xp2, attn_p, glup.reshape(bp, sp, CONV_CH), mod3, ctx_row, bw, bp, sp)

    qs, ks, vs, glus = _front(xs2, mod3, None, ss, fw, _rope_tables(ss))
    kc, vc = _ctx_kv(cache_ckv[:, 0].reshape(bs * past, KV_RANK),
                     cache_krope[:, 0].reshape(bs * past, QK_ROPE), wk, wv, pk)
    attn_s = _attention(qs, [(kc, vc, past), (ks, vs, ss)], bs, ss)
    x1s, h2s, affts = _back(xs2, attn_s, glus.reshape(bs, ss, CONV_CH), mod3, None, bw, bs, ss)

    plans = []
    xs_list = []
    for h2, afft in ((h2p, afftp), (h2s, affts)):
        n_tok = h2.shape[0]
        cap = EC_FACTOR * n_tok // N_EXPERTS
        rows_per_expert = n_tok // LANES
        pos_rows, base = _route(afft.reshape(N_EXPERTS * rows_per_expert, LANES), cap, rows_per_expert)
        pos = pos_rows.reshape(N_EXPERTS, n_tok)
        offs, nwin = _route_plan(base, n_tok, cap)
        plans.append((pos, offs, nwin, cap))
        xs_list.append(_dispatch(h2, pos, offs, nwin, cap))
    ys_list = _experts(xs_list, w_gate[0], w_up[0], w_down[0])

    fg = final_norm_g[None, :]
    pos, offs, nwin, cap = plans[0]
    y_prompt = _combine(ys_list[0], pos, offs, nwin, x1p, mod3, ctx_row, sp, fg, cap)
    pos, offs, nwin, cap = plans[1]
    y_sample = _combine(ys_list[1], pos, offs, nwin, x1s, mod3, None, ss, fg, cap)

    return (y_prompt.reshape(bp, sp, D_MODEL), y_sample.reshape(bs, ss, D_MODEL),
            ckvp.reshape(bp, depth, sp, KV_RANK), krp.reshape(bp, depth, sp, QK_ROPE))
```

```python
import functools

import jax
import jax.numpy as jnp
import numpy as np
from jax import lax
from jax.experimental import pallas as pl
from jax.experimental.pallas import tpu as pltpu

F32 = jnp.float32
BF16 = jnp.bfloat16
I32 = jnp.int32

LANES = 128
BF16_ROWS = 16
VMEM_LIMIT = 56 * 1024 * 1024

D_MODEL = 1024
GRID_W = 64
MLA_HEADS = 8
QK_NOPE = 64
QK_ROPE = 32
V_DIM = 64
Q_RANK = 768
KV_RANK = 256
CONV_CH = 512
CONV_WIDTH = 31
N_EXPERTS = 16
EXPERT_FF = 1024
EC_FACTOR = 2
ROPE_BASE = 10000.0
EPS = 1e-6
N_MOD = 6

HEAD_PAD = LANES
QK_WIDTH = MLA_HEADS * HEAD_PAD
ROW_TILE = 256
SLOT_WIN = 80
AUG = LANES
XS_W = D_MODEL + AUG
HIGHEST = lax.Precision.HIGHEST


def _params(n_axes, vmem=None):
    return pltpu.CompilerParams(
        dimension_semantics=("arbitrary",) * n_axes,
        vmem_limit_bytes=vmem)


def _rms(x, g):
    ms = jnp.mean(x * x, axis=-1, keepdims=True)
    return x * lax.rsqrt(ms + EPS) * g


def _bdot(a, b):
    return jnp.dot(a, b, preferred_element_type=F32)


def _adaln_kernel(c_ref, w_ref, b_ref, o_ref):
    c = c_ref[...]
    s = c * jax.nn.sigmoid(c)
    o_ref[...] = jnp.dot(s, w_ref[...], precision=HIGHEST,
                         preferred_element_type=F32) + b_ref[...]


def _adaln(cc, w_mod, b_mod):
    rows = cc.shape[0]
    n_out = w_mod.shape[1]
    blk = D_MODEL
    return pl.pallas_call(
        _adaln_kernel,
        out_shape=jax.ShapeDtypeStruct((rows, n_out), F32),
        grid=(n_out // blk,),
        in_specs=[pl.BlockSpec((rows, D_MODEL), lambda j: (0, 0)),
                  pl.BlockSpec((D_MODEL, blk), lambda j: (0, j)),
                  pl.BlockSpec((1, blk), lambda j: (0, j))],
        out_specs=pl.BlockSpec((rows, blk), lambda j: (0, j)),
        compiler_params=_params(1),
        name="adaln",
    )(cc, w_mod, b_mod)


def _rot_half(x, lane, base):
    rel = lane - base
    first = ((rel >= 0) & (rel < 8)) | ((rel >= 16) & (rel < 24))
    return jnp.where(first, -pltpu.roll(x, LANES - 8, axis=1), pltpu.roll(x, 8, axis=1))


def _front_kernel(rope, x_ref, sh_ref, sc_ref, g_ref, wqc_ref, wkr_ref, wglu_ref,
                  qg_ref, wq_ref, kvg_ref, wk_ref, wv_ref, pk_ref, *rest):
    if rope:
        cq_ref, sq_ref, q_ref, k_ref, v_ref, glu_ref = rest
    else:
        q_ref, k_ref, v_ref, glu_ref, ckv_ref, kr_ref = rest
    x = x_ref[...]
    h = _rms(x, g_ref[...]) * (1.0 + sc_ref[0]) + sh_ref[0]
    hb = h.astype(BF16)
    qc = _bdot(hb, wqc_ref[...])
    kr = _bdot(hb, wkr_ref[...])
    glu = _bdot(hb, wglu_ref[...])
    glu_ref[...] = glu[:, :CONV_CH] * jax.nn.sigmoid(glu[:, CONV_CH:])

    qn = _rms(qc[:, :Q_RANK], qg_ref[...]).astype(BF16)
    q = _bdot(qn, wq_ref[...])
    ckv = _rms(qc[:, Q_RANK:], kvg_ref[...])
    ckv_b = ckv.astype(BF16)
    kk = _bdot(ckv_b, wk_ref[...])
    v_ref[...] = _bdot(ckv_b, wv_ref[...]).astype(BF16)

    scale = (QK_NOPE + QK_ROPE) ** -0.5
    lane = lax.broadcasted_iota(I32, (x.shape[0], LANES), 1)
    if rope:
        cq = cq_ref[...]
        sq = sq_ref[...]
        for hd in range(MLA_HEADS):
            blk = q[:, hd * HEAD_PAD:(hd + 1) * HEAD_PAD]
            rot = blk * cq + _rot_half(blk, lane, QK_NOPE) * sq
            q_ref[:, hd * HEAD_PAD:(hd + 1) * HEAD_PAD] = (rot * scale).astype(BF16)
        krs = pltpu.roll(kr, QK_NOPE, axis=1)
        kr = pltpu.roll(krs * cq + _rot_half(krs, lane, QK_NOPE) * sq, LANES - QK_NOPE, axis=1)
        kr = jnp.where(lane < QK_ROPE, kr, 0.0)
    else:
        q_ref[...] = (q * scale).astype(BF16)
        ckv_ref[...] = ckv
        kr_ref[...] = kr[:, :QK_ROPE]
    k_ref[...] = (kk + _bdot(kr.astype(BF16), pk_ref[...])).astype(BF16)


def _front(x2d, mod3, mod_row, rows_per_mod, wts, rope_tabs):
    n = x2d.shape[0]
    rope = rope_tabs is not None
    tiles_per_mod = rows_per_mod // ROW_TILE
    full = lambda a: pl.BlockSpec(a.shape, lambda i: (0,) * a.ndim)

    def mod_spec(col):
        if mod_row is None:
            return pl.BlockSpec((1, 1, D_MODEL), lambda i: (i // tiles_per_mod, 0, col))
        return pl.BlockSpec((1, 1, D_MODEL), lambda i: (mod_row, 0, col))

    row = lambda w: pl.BlockSpec((ROW_TILE, w), lambda i: (i, 0))
    in_specs = [row(D_MODEL), mod_spec(0), mod_spec(1)] + [full(w) for w in wts]
    args = [x2d, mod3, mod3] + list(wts)
    out_shape = [jax.ShapeDtypeStruct((n, QK_WIDTH), BF16)] * 3 + \
                [jax.ShapeDtypeStruct((n, CONV_CH), F32)]
    out_specs = [row(QK_WIDTH)] * 3 + [row(CONV_CH)]
    if rope:
        seq_tiles = rope_tabs[0].shape[0] // ROW_TILE
        tab = pl.BlockSpec((ROW_TILE, LANES), lambda i: (i % seq_tiles, 0))
        in_specs += [tab, tab]
        args += list(rope_tabs)
    else:
        out_shape += [jax.ShapeDtypeStruct((n, KV_RANK), F32),
                      jax.ShapeDtypeStruct((n, QK_ROPE), F32)]
        out_specs += [row(KV_RANK), row(QK_ROPE)]
    return pl.pallas_call(
        functools.partial(_front_kernel, rope),
        out_shape=out_shape,
        grid=(n // ROW_TILE,),
        in_specs=in_specs,
        out_specs=out_specs,
        compiler_params=_params(1, VMEM_LIMIT),
        name="front_rope" if rope else "front",
    )(*args)


def _ctx_kernel(ckv_ref, kr_ref, wk_ref, wv_ref, pk_ref, k_ref, v_ref):
    cb = ckv_ref[...].astype(BF16)
    kr = kr_ref[...].astype(BF16)
    k_ref[...] = (_bdot(cb, wk_ref[...]) + _bdot(kr, pk_ref[...])).astype(BF16)
    v_ref[...] = _bdot(cb, wv_ref[...]).astype(BF16)


def _ctx_kv(ckv2d, kr2d, wk, wv, pk):
    n = ckv2d.shape[0]
    full = lambda a: pl.BlockSpec(a.shape, lambda i: (0,) * a.ndim)
    row = lambda w: pl.BlockSpec((ROW_TILE, w), lambda i: (i, 0))
    pk32 = pk[:QK_ROPE]
    return pl.pallas_call(
        _ctx_kernel,
        out_shape=[jax.ShapeDtypeStruct((n, QK_WIDTH), BF16)] * 2,
        grid=(n // ROW_TILE,),
        in_specs=[row(KV_RANK), row(QK_ROPE), full(wk), full(wv), full(pk32)],
        out_specs=[row(QK_WIDTH)] * 2,
        compiler_params=_params(1),
        name="ctx_kv",
    )(ckv2d, kr2d, wk, wv, pk32)


def _attn_kernel(n_kv, q_ref, *rest):
    kv_refs = rest[:2 * n_kv]
    o_ref = rest[2 * n_kv]
    nt = (((1,), (1,)), ((), ()))
    outs = []
    for hd in range(MLA_HEADS):
        sl = slice(hd * HEAD_PAD, (hd + 1) * HEAD_PAD)
        qh = q_ref[:, sl]
        ss = [lax.dot_general(qh, kv_refs[2 * j][:, sl], nt, preferred_element_type=F32)
              for j in range(n_kv)]
        m = ss[0].max(axis=-1, keepdims=True)
        for s in ss[1:]:
            m = jnp.maximum(m, s.max(axis=-1, keepdims=True))
        ps = [jnp.exp(s - m) for s in ss]
        l = ps[0].sum(axis=-1, keepdims=True)
        for p in ps[1:]:
            l = l + p.sum(axis=-1, keepdims=True)
        o = _bdot(ps[0].astype(BF16), kv_refs[1][:, sl])
        for j in range(1, n_kv):
            o = o + _bdot(ps[j].astype(BF16), kv_refs[2 * j + 1][:, sl])
        outs.append(o / l)
    for j in range(MLA_HEADS // 2):
        o_ref[:, j * LANES:(j + 1) * LANES] = (outs[2 * j] + outs[2 * j + 1]).astype(BF16)


def _attention(q, kvs, batch, seq):
    n = q.shape[0]
    q_tiles = seq // ROW_TILE
    in_specs = [pl.BlockSpec((ROW_TILE, QK_WIDTH), lambda b, i: (b * q_tiles + i, 0))]
    args = [q]
    for k, v, t in kvs:
        spec = pl.BlockSpec((t, QK_WIDTH), lambda b, i: (b, 0))
        in_specs += [spec, spec]
        args += [k, v]
    return pl.pallas_call(
        functools.partial(_attn_kernel, len(kvs)),
        out_shape=jax.ShapeDtypeStruct((n, MLA_HEADS * V_DIM), BF16),
        grid=(batch, q_tiles),
        in_specs=in_specs,
        out_specs=pl.BlockSpec((ROW_TILE, MLA_HEADS * V_DIM), lambda b, i: (b * q_tiles + i, 0)),
        compiler_params=_params(2, VMEM_LIMIT),
        name="attention",
    )(*args)


CONV_PAD = (CONV_WIDTH - 1) // 2
CONV_HALO = 16


SUBLANES = 8
SHIFT_ROWS = ROW_TILE + 2 * CONV_HALO - SUBLANES


def _back_kernel(seq, x_ref, attn_ref, glu_ref, cw_ref, cb_ref, lg_ref, lb_ref,
                 woa_ref, woc_ref, g1_ref, sh_ref, sc_ref, fg_ref, wrh_ref, wrl_ref,
                 x1_ref, h2_ref, afft_ref, vpad_ref, shift_ref):
    i = pl.program_id(1)

    @pl.when(i == 0)
    def _():
        zeros = jnp.zeros((CONV_HALO, CONV_CH), F32)
        vpad_ref[0:CONV_HALO, :] = zeros
        vpad_ref[CONV_HALO + seq:, :] = zeros
        vpad_ref[CONV_HALO:CONV_HALO + seq, :] = glu_ref[0]

    r0 = pl.multiple_of(i * ROW_TILE, ROW_TILE)
    win = vpad_ref[pl.ds(r0, ROW_TILE + 2 * CONV_HALO), :]
    for ph in range(1, SUBLANES):
        shift_ref[ph - 1] = win[ph:ph + SHIFT_ROWS, :]
    cw = cw_ref[...]
    acc = jnp.broadcast_to(cb_ref[...], (ROW_TILE, CONV_CH))
    for t in range(CONV_WIDTH):
        blk, ph = divmod(t + CONV_HALO - CONV_PAD, SUBLANES)
        if ph == 0:
            tap = vpad_ref[pl.ds(r0 + blk * SUBLANES, ROW_TILE), :]
        else:
            tap = shift_ref[ph - 1, blk * SUBLANES:blk * SUBLANES + ROW_TILE, :]
        acc = acc + tap * cw[t:t + 1, :]
    mu = jnp.mean(acc, axis=-1, keepdims=True)
    cen = acc - mu
    var = jnp.mean(cen * cen, axis=-1, keepdims=True)
    ln = cen * lax.rsqrt(var + EPS) * lg_ref[...] + lb_ref[...]
    cv = (ln * jax.nn.sigmoid(ln)).astype(BF16)

    mix = _bdot(attn_ref[...], woa_ref[...]) + _bdot(cv, woc_ref[...])
    x1 = x_ref[...] + g1_ref[0] * mix
    x1_ref[...] = x1
    h2 = _rms(x1, fg_ref[...]) * (1.0 + sc_ref[0]) + sh_ref[0]
    h2b = h2.astype(BF16)
    h2_ref[:, :D_MODEL] = h2b

    h2l = (h2 - h2b.astype(F32)).astype(BF16)
    wrh = wrh_ref[...]
    logits = _bdot(h2b, wrh) + (_bdot(h2l, wrh) + _bdot(h2b, wrl_ref[...]))
    lane = lax.broadcasted_iota(I32, logits.shape, 1)
    real = lane < N_EXPERTS
    m = jnp.max(jnp.where(real, logits, -jnp.inf), axis=-1, keepdims=True)
    e = jnp.exp(logits - m)
    aff = e / jnp.sum(jnp.where(real, e, 0.0), axis=-1, keepdims=True)
    afft_ref[...] = aff.T[:N_EXPERTS, :]
    hi = aff.astype(BF16)
    r1 = aff - hi.astype(F32)
    mid = r1.astype(BF16)
    lo = (r1 - mid.astype(F32)).astype(BF16)
    zero = jnp.zeros_like(hi)
    h2_ref[:, D_MODEL:] = jnp.where(
        real, hi, jnp.where(lane < 2 * N_EXPERTS, mid, jnp.where(lane < 3 * N_EXPERTS, lo, zero)))


def _back(x2d, attn, glu3, mod3, mod_row, wts, batch, seq):
    n = x2d.shape[0]
    tiles = seq // ROW_TILE
    full = lambda a: pl.BlockSpec(a.shape, lambda b, i: (0,) * a.ndim)

    def mod_spec(col):
        if mod_row is None:
            return pl.BlockSpec((1, 1, D_MODEL), lambda b, i: (b, 0, col))
        return pl.BlockSpec((1, 1, D_MODEL), lambda b, i: (mod_row, 0, col))

    row = lambda w: pl.BlockSpec((ROW_TILE, w), lambda b, i: (b * tiles + i, 0))
    cw, cb, lg, lb, woa, woc, fg, wrh, wrl = wts
    in_specs = [row(D_MODEL), row(MLA_HEADS * V_DIM),
                pl.BlockSpec((1, seq, CONV_CH), lambda b, i: (b, 0, 0)),
                full(cw), full(cb), full(lg), full(lb), full(woa), full(woc),
                mod_spec(2), mod_spec(3), mod_spec(4), full(fg), full(wrh), full(wrl)]
    return pl.pallas_call(
        functools.partial(_back_kernel, seq),
        out_shape=[jax.ShapeDtypeStruct((n, D_MODEL), F32),
                   jax.ShapeDtypeStruct((n, XS_W), BF16),
                   jax.ShapeDtypeStruct((N_EXPERTS, n), F32)],
        grid=(batch, tiles),
        in_specs=in_specs,
        out_specs=[row(D_MODEL), row(XS_W),
                   pl.BlockSpec((N_EXPERTS, ROW_TILE), lambda b, i: (0, b * tiles + i))],
        scratch_shapes=[pltpu.VMEM((seq + 2 * CONV_HALO, CONV_CH), F32),
                        pltpu.VMEM((SUBLANES - 1, SHIFT_ROWS, CONV_CH), F32)],
        compiler_params=_params(2, VMEM_LIMIT),
        name="back",
    )(x2d, attn, glu3, cw, cb, lg, lb, woa, woc, mod3, mod3, mod3, fg, wrh, wrl)


def _route_kernel(cap, rows_per_expert, aff_ref, pos_ref, base_ref):
    rows = aff_ref.shape[0]
    aff = aff_ref[...]
    ones = jnp.ones((LANES, LANES), BF16)
    r_i = lax.broadcasted_iota(I32, (rows, rows), 0)
    c_i = lax.broadcasted_iota(I32, (rows, rows), 1)
    shift = rows_per_expert.bit_length() - 1
    assert rows_per_expert == 1 << shift
    same = lax.shift_right_logical(r_i, shift) == lax.shift_right_logical(c_i, shift)
    seg_all = jnp.where(same, 1.0, 0.0).astype(BF16)
    seg_before = jnp.where(same & (c_i < r_i), 1.0, 0.0).astype(BF16)
    l_r = lax.broadcasted_iota(I32, (LANES, LANES), 0)
    l_c = lax.broadcasted_iota(I32, (LANES, LANES), 1)
    tri = jnp.where(l_r <= l_c, 1.0, 0.0).astype(BF16)

    def as_b(mask):
        return jnp.where(mask, 1.0, 0.0).astype(BF16)

    def expert_count(mask_b):
        return _bdot(seg_all, _bdot(mask_b, ones).astype(BF16))

    def excl_cumsum(mask_b):
        incl = _bdot(mask_b, tri)
        before = _bdot(seg_before, _bdot(mask_b, ones).astype(BF16))
        return before + incl - mask_b.astype(F32), before

    def step(t, thr):
        cand = thr | lax.shift_left(jnp.int32(1), 30 - t)
        cnt = expert_count(as_b(aff >= pltpu.bitcast(cand, F32)))
        return jnp.where(cnt >= cap, cand, thr)

    thr = pltpu.bitcast(lax.fori_loop(0, 31, step, jnp.zeros(aff.shape, I32)), F32)
    gt = aff > thr
    eq = aff == thr
    need = cap - expert_count(as_b(gt))
    tie_rank, _ = excl_cumsum(as_b(eq))
    sel = gt | (eq & (tie_rank < need))
    pos, before = excl_cumsum(as_b(sel))
    pos_ref[...] = jnp.where(sel, pos, -1.0).astype(I32)
    base_ref[...] = before


def _route(aff_rows, cap, rows_per_expert):
    rows = aff_rows.shape[0]
    return pl.pallas_call(
        functools.partial(_route_kernel, cap, rows_per_expert),
        out_shape=[jax.ShapeDtypeStruct((rows, LANES), I32),
                   jax.ShapeDtypeStruct((rows, LANES), F32)],
        compiler_params=pltpu.CompilerParams(vmem_limit_bytes=VMEM_LIMIT),
        name="route",
    )(aff_rows)


def _window(off_ref, e, i, n_tiles, w, cap):
    first = off_ref[e * n_tiles + i] + w * SLOT_WIN
    return first, pl.multiple_of(jnp.minimum(first, cap - SLOT_WIN), BF16_ROWS)


def _one_hot_rows(pos_tile, wins):
    t = pos_tile.shape[1]
    r = lax.broadcasted_iota(I32, (SLOT_WIN, t), 0)
    blocks = []
    for e, (first, off) in enumerate(wins):
        pe = pos_tile[e:e + 1, :]
        rel = jnp.where(pe >= first, pe - off, -1)
        blocks.append(jnp.where(rel == r, 1.0, 0.0).astype(BF16))
    return jnp.concatenate(blocks, axis=0)


def _dispatch_kernel(cap, n_tiles, off_ref, nwin_ref, h2_ref, pos_ref, xs_hbm, xs_ref, sem):
    i = pl.program_id(0)

    @pl.when(i == 0)
    def _():
        def zero(e, carry):
            xs_ref[e] = jnp.zeros(xs_ref.shape[1:], BF16)
            return carry
        lax.fori_loop(0, N_EXPERTS, zero, 0)

    def window(w, carry):
        wins = [_window(off_ref, e, i, n_tiles, w, cap) for e in range(N_EXPERTS)]
        p = _one_hot_rows(pos_ref[...], wins)
        slab = _bdot(p, h2_ref[...])
        for e, (_, off) in enumerate(wins):
            cur = xs_ref[e, pl.ds(off, SLOT_WIN), :].astype(F32)
            xs_ref[e, pl.ds(off, SLOT_WIN), :] = (
                cur + slab[e * SLOT_WIN:(e + 1) * SLOT_WIN]).astype(BF16)
        return carry

    lax.fori_loop(0, nwin_ref[i], window, 0)

    @pl.when(i == n_tiles - 1)
    def _():
        cp = pltpu.make_async_copy(xs_ref, xs_hbm, sem)
        cp.start()
        cp.wait()


def _dispatch(h2aug, pos, offs, nwin, cap):
    n = h2aug.shape[0]
    n_tiles = n // ROW_TILE
    return pl.pallas_call(
        functools.partial(_dispatch_kernel, cap, n_tiles),
        out_shape=jax.ShapeDtypeStruct((N_EXPERTS, cap, XS_W), BF16),
        grid_spec=pltpu.PrefetchScalarGridSpec(
            num_scalar_prefetch=2,
            grid=(n_tiles,),
            in_specs=[pl.BlockSpec((ROW_TILE, XS_W), lambda i, *_: (i, 0)),
                      pl.BlockSpec((N_EXPERTS, ROW_TILE), lambda i, *_: (0, i))],
            out_specs=pl.BlockSpec(memory_space=pl.ANY),
            scratch_shapes=[pltpu.VMEM((N_EXPERTS, cap, XS_W), BF16),
                            pltpu.SemaphoreType.DMA]),
        compiler_params=_params(1, VMEM_LIMIT),
        name="dispatch",
    )(offs, nwin, h2aug, pos)


FF_CHUNK = 256
EXP_ROWS = 512


def _experts_kernel(n_groups, n_chunks, *refs):
    xs_refs = refs[:n_groups]
    wg_ref, wu_ref, wd_ref = refs[n_groups:n_groups + 3]
    ys_refs = refs[n_groups + 3:2 * n_groups + 3]
    acc_ref = refs[2 * n_groups + 3]
    e = pl.program_id(0)
    j = pl.program_id(1)
    cap = xs_refs[0].shape[1]

    @pl.when(j == 0)
    def _():
        for g in range(n_groups):
            acc_ref[g] = jnp.zeros(acc_ref.shape[1:], F32)

    wg = wg_ref[0].astype(BF16)
    wu = wu_ref[0].astype(BF16)
    wd = wd_ref[0].astype(BF16)
    for g, xs_ref in enumerate(xs_refs):
        for r in range(cap // EXP_ROWS):
            rows = slice(r * EXP_ROWS, (r + 1) * EXP_ROWS)
            x = xs_ref[0, rows, :D_MODEL]
            a = _bdot(x, wg)
            u = _bdot(x, wu)
            hm = (a * jax.nn.sigmoid(a) * u).astype(BF16)
            acc_ref[g, rows, :] += _bdot(hm, wd)

    @pl.when(j == n_chunks - 1)
    def _():
        for g, (xs_ref, ys_ref) in enumerate(zip(xs_refs, ys_refs)):
            aug = xs_ref[0, :, D_MODEL:].astype(F32)
            lane = lax.broadcasted_iota(I32, aug.shape, 1)
            mine = (lane == e) | (lane == e + N_EXPERTS) | (lane == e + 2 * N_EXPERTS)
            gate = jnp.sum(jnp.where(mine, aug, 0.0), axis=-1, keepdims=True)
            ys_ref[0] = (acc_ref[g] * gate).astype(BF16)


def _experts(xs_list, w_gate, w_up, w_down):
    n_groups = len(xs_list)
    cap = xs_list[0].shape[1]
    ff = w_gate.shape[2]
    n_chunks = ff // FF_CHUNK
    xs_spec = pl.BlockSpec((1, cap, XS_W), lambda e, j: (e, 0, 0))
    up_spec = pl.BlockSpec((1, D_MODEL, FF_CHUNK), lambda e, j: (e, 0, j))
    down_spec = pl.BlockSpec((1, FF_CHUNK, D_MODEL), lambda e, j: (e, j, 0))
    ys_spec = pl.BlockSpec((1, cap, D_MODEL), lambda e, j: (e, 0, 0))
    return pl.pallas_call(
        functools.partial(_experts_kernel, n_groups, n_chunks),
        out_shape=[jax.ShapeDtypeStruct((N_EXPERTS, cap, D_MODEL), BF16)] * n_groups,
        grid=(N_EXPERTS, n_chunks),
        in_specs=[xs_spec] * n_groups + [up_spec, up_spec, down_spec],
        out_specs=[ys_spec] * n_groups,
        scratch_shapes=[pltpu.VMEM((n_groups, cap, D_MODEL), F32)],
        compiler_params=_params(2, VMEM_LIMIT),
        name="experts",
    )(*xs_list, w_gate, w_up, w_down)


def _combine_kernel(cap, n_tiles, off_ref, nwin_ref, ys_hbm, pos_ref, x1_ref, g2_ref, fg_ref,
                    y_ref, buf_ref, sems):
    i = pl.program_id(0)
    slot = i % 2

    def copies(tile, w, s):
        out = []
        for e in range(N_EXPERTS):
            _, off = _window(off_ref, e, tile, n_tiles, w, cap)
            out.append(pltpu.make_async_copy(
                ys_hbm.at[e, pl.ds(off, SLOT_WIN), :],
                buf_ref.at[s, pl.ds(e * SLOT_WIN, SLOT_WIN), :],
                sems.at[s]))
        return out

    @pl.when(i == 0)
    def _():
        for cp in copies(0, 0, 0):
            cp.start()

    @pl.when(i + 1 < n_tiles)
    def _():
        for cp in copies(i + 1, 0, 1 - slot):
            cp.start()

    tn = (((0,), (0,)), ((), ()))

    def window(w, acc):
        @pl.when(w > 0)
        def _():
            for cp in copies(i, w, slot):
                cp.start()

        for cp in copies(i, w, slot):
            cp.wait()
        wins = [_window(off_ref, e, i, n_tiles, w, cap) for e in range(N_EXPERTS)]
        p = _one_hot_rows(pos_ref[...], wins)
        return acc + lax.dot_general(p, buf_ref[slot], tn, preferred_element_type=F32)

    moe = lax.fori_loop(0, nwin_ref[i], window, jnp.zeros((ROW_TILE, D_MODEL), F32))
    out = x1_ref[...] + g2_ref[0] * moe
    y_ref[...] = _rms(out, fg_ref[...])


def _combine(ys, pos, offs, nwin, x1, mod3, mod_row, rows_per_mod, final_g, cap):
    n = x1.shape[0]
    n_tiles = n // ROW_TILE
    tiles_per_mod = rows_per_mod // ROW_TILE
    if mod_row is None:
        g2 = pl.BlockSpec((1, 1, D_MODEL), lambda i, *_: (i // tiles_per_mod, 0, 5))
    else:
        g2 = pl.BlockSpec((1, 1, D_MODEL), lambda i, *_: (mod_row, 0, 5))
    return pl.pallas_call(
        functools.partial(_combine_kernel, cap, n_tiles),
        out_shape=jax.ShapeDtypeStruct((n, D_MODEL), F32),
        grid_spec=pltpu.PrefetchScalarGridSpec(
            num_scalar_prefetch=2,
            grid=(n_tiles,),
            in_specs=[pl.BlockSpec(memory_space=pl.ANY),
                      pl.BlockSpec((N_EXPERTS, ROW_TILE), lambda i, *_: (0, i)),
                      pl.BlockSpec((ROW_TILE, D_MODEL), lambda i, *_: (i, 0)),
                      g2,
                      pl.BlockSpec((1, D_MODEL), lambda i, *_: (0, 0))],
            out_specs=pl.BlockSpec((ROW_TILE, D_MODEL), lambda i, *_: (i, 0)),
            scratch_shapes=[pltpu.VMEM((2, N_EXPERTS * SLOT_WIN, D_MODEL), BF16),
                            pltpu.SemaphoreType.DMA((2,))]),
        compiler_params=_params(1, VMEM_LIMIT),
        name="combine",
    )(offs, nwin, ys, pos, x1, mod3, final_g)


def _head_cols(w3):
    k, _, d = w3.shape
    return jnp.pad(w3, ((0, 0), (0, 0), (0, HEAD_PAD - d))).reshape(k, QK_WIDTH)


def _value_cols(v3):
    k = v3.shape[0]
    pair = v3.reshape(k, MLA_HEADS // 2, 2, V_DIM)
    zeros = jnp.zeros((k, MLA_HEADS // 2, V_DIM), v3.dtype)
    even = jnp.concatenate([pair[:, :, 0], zeros], axis=-1)
    odd = jnp.concatenate([zeros, pair[:, :, 1]], axis=-1)
    return jnp.stack([even, odd], axis=2).reshape(k, QK_WIDTH)


def _rope_tables(seq):
    half = QK_ROPE // 2
    pos = jnp.arange(seq)
    inv_freq = 1.0 / (ROPE_BASE ** (jnp.arange(0, half, 2, dtype=F32) / half))

    def cs(p):
        ang = p.astype(F32)[:, None] * inv_freq[None, :]
        ang = jnp.concatenate([ang, ang], axis=-1)
        return jnp.cos(ang), jnp.sin(ang)

    cr, sr = cs(pos // GRID_W)
    cc, sc = cs(pos % GRID_W)
    ones = jnp.ones((seq, QK_NOPE), F32)
    zpad = jnp.zeros((seq, HEAD_PAD - QK_NOPE - QK_ROPE), F32)
    cq = jnp.concatenate([ones, cr, cc, zpad], axis=-1)
    sq = jnp.concatenate([0.0 * ones, sr, sc, zpad], axis=-1)
    return cq, sq


def _prep_weights(w_in, q_norm_g, w_q_up, kv_norm_g, w_kv_up, attn_norm_g):
    wqc = w_in[:, :Q_RANK + KV_RANK].astype(BF16)
    wkr = jnp.pad(w_in[:, Q_RANK + KV_RANK:Q_RANK + KV_RANK + QK_ROPE],
                  ((0, 0), (0, LANES - QK_ROPE))).astype(BF16)
    wglu = w_in[:, Q_RANK + KV_RANK + QK_ROPE:].astype(BF16)
    wq = _head_cols(w_q_up.astype(BF16).reshape(Q_RANK, MLA_HEADS, QK_NOPE + QK_ROPE))
    kv3 = w_kv_up.astype(BF16).reshape(KV_RANK, MLA_HEADS, QK_NOPE + V_DIM)
    wk = _head_cols(kv3[:, :, :QK_NOPE])
    wv = _value_cols(kv3[:, :, QK_NOPE:])
    pk = np.zeros((LANES, QK_WIDTH), np.float32)
    for hd in range(MLA_HEADS):
        for j in range(QK_ROPE):
            pk[j, hd * HEAD_PAD + QK_NOPE + j] = 1.0
    pk = jnp.asarray(pk, BF16)
    return (attn_norm_g[None, :], wqc, wkr, wglu, q_norm_g[None, :], wq,
            kv_norm_g[None, :], wk, wv, pk)


def _route_plan(base, n_tokens, cap):
    rows_per_tile = ROW_TILE // LANES
    n_tiles = n_tokens // ROW_TILE
    start = base[:, 0].reshape(N_EXPERTS, -1)[:, ::rows_per_tile].astype(I32)
    end = jnp.concatenate([start[:, 1:], jnp.full((N_EXPERTS, 1), cap, I32)], axis=1)
    off = (start // BF16_ROWS) * BF16_ROWS
    span = end - off
    nwin = jnp.maximum(jnp.max((span + SLOT_WIN - 1) // SLOT_WIN, axis=0), 1).astype(I32)
    return off.reshape(N_EXPERTS * n_tiles), nwin


def kernel(x_prompt, x_sample, cache_ckv, cache_krope, c, c_ctx, w_mod, b_mod, attn_norm_g,
           w_in, q_norm_g, w_q_up, kv_norm_g, w_kv_up, conv_w, conv_b, conv_ln_g, conv_ln_b,
           w_out, ffn_norm_g, w_router, w_gate, w_up, w_down, final_norm_g):
    depth = w_mod.shape[0]
    assert depth == 1
    bp, sp, _ = x_prompt.shape
    bs, ss, _ = x_sample.shape
    past = cache_ckv.shape[2]
    ctx_row = bs

    mod_rows = 16
    cc = jnp.concatenate([c, c_ctx[None, :], jnp.zeros((mod_rows - bs - 1, D_MODEL), F32)], axis=0)
    mod = _adaln(cc, w_mod[0], b_mod[0][None, :])
    mod3 = mod.reshape(mod_rows, 1, N_MOD * D_MODEL)

    fw = _prep_weights(w_in[0], q_norm_g[0], w_q_up[0], kv_norm_g[0], w_kv_up[0], attn_norm_g[0])
    wk, wv, pk = fw[7], fw[8], fw[9]
    wo = w_out[0].astype(BF16)
    wr = jnp.pad(jnp.tile(w_router[0], (1, 3)), ((0, 0), (0, LANES - 3 * N_EXPERTS)))
    wrh = wr.astype(BF16)
    wrl = (wr - wrh.astype(F32)).astype(BF16)
    bw = (conv_w[0], conv_b[0][None, :], conv_ln_g[0][None, :], conv_ln_b[0][None, :],
          wo[:MLA_HEADS * V_DIM], wo[MLA_HEADS * V_DIM:], ffn_norm_g[0][None, :], wrh, wrl)

    xp2 = x_prompt.reshape(bp * sp, D_MODEL)
    xs2 = x_sample.reshape(bs * ss, D_MODEL)

    qp, kp, vp, glup, ckvp, krp = _front(xp2, mod3, ctx_row, sp, fw, None)
    attn_p = _attention(qp, [(kp, vp, sp)], bp, sp)
    x1p, h2p, afftp = _back(xp2, attn_p, glup.reshape(bp, sp, CONV_CH), mod3, ctx_row, bw, bp, sp)

    qs, ks, vs, glus = _front(xs2, mod3, None, ss, fw, _rope_tables(ss))
    kc, vc = _ctx_kv(cache_ckv[:, 0].reshape(bs * past, KV_RANK),
                     cache_krope[:, 0].reshape(bs * past, QK_ROPE), wk, wv, pk)
    attn_s = _attention(qs, [(kc, vc, past), (ks, vs, ss)], bs, ss)
    x1s, h2s, affts = _back(xs2, attn_s, glus.reshape(bs, ss, CONV_CH), mod3, None, bw, bs, ss)

    plans = []
    xs_list = []
    for h2, afft in ((h2p, afftp), (h2s, affts)):
        n_tok = h2.shape[0]
        cap = EC_FACTOR * n_tok // N_EXPERTS
        rows_per_expert = n_tok // LANES
        pos_rows, base = _route(afft.reshape(N_EXPERTS * rows_per_expert, LANES), cap, rows_per_expert)
        pos = pos_rows.reshape(N_EXPERTS, n_tok)
        offs, nwin = _route_plan(base, n_tok, cap)
        plans.append((pos, offs, nwin, cap))
        xs_list.append(_dispatch(h2, pos, offs, nwin, cap))
    ys_list = _experts(xs_list, w_gate[0], w_up[0], w_down[0])

    fg = final_norm_g[None, :]
    pos, offs, nwin, cap = plans[0]
    y_prompt = _combine(ys_list[0], pos, offs, nwin, x1p, mod3, ctx_row, sp, fg, cap)
    pos, offs, nwin, cap = plans[1]
    y_sample = _combine(ys_list[1], pos, offs, nwin, x1s, mod3, None, ss, fg, cap)

    return (y_prompt.reshape(bp, sp, D_MODEL), y_sample.reshape(bs, ss, D_MODEL),
            ckvp.reshape(bp, depth, sp, KV_RANK), krp.reshape(bp, depth, sp, QK_ROPE))
```

```python
import functools

import jax
import jax.numpy as jnp
import numpy as np
from jax import lax
from jax.experimental import pallas as pl
from jax.experimental.pallas import tpu as pltpu

F32 = jnp.float32
BF16 = jnp.bfloat16
I32 = jnp.int32

LANES = 128
BF16_ROWS = 16
VMEM_LIMIT = 56 * 1024 * 1024

D_MODEL = 1024
GRID_W = 64
MLA_HEADS = 8
QK_NOPE = 64
QK_ROPE = 32
V_DIM = 64
Q_RANK = 768
KV_RANK = 256
CONV_CH = 512
CONV_WIDTH = 31
N_EXPERTS = 16
EXPERT_FF = 1024
EC_FACTOR = 2
ROPE_BASE = 10000.0
EPS = 1e-6
N_MOD = 6

HEAD_PAD = LANES
QK_WIDTH = MLA_HEADS * HEAD_PAD
ROW_TILE = 256
FRONT_TILE = 512
SLOT_WIN = 64
AUG = LANES
XS_W = D_MODEL + AUG
HIGHEST = lax.Precision.HIGHEST
LOG2_E = 1.4426950408889634


def _params(n_axes, vmem=None):
    return pltpu.CompilerParams(
        dimension_semantics=("arbitrary",) * n_axes,
        vmem_limit_bytes=vmem)


def _rms(x, g):
    ms = jnp.mean(x * x, axis=-1, keepdims=True)
    return x * lax.rsqrt(ms + EPS) * g


def _bdot(a, b):
    return jnp.dot(a, b, preferred_element_type=F32)


def _adaln_kernel(c_ref, w_ref, b_ref, o_ref):
    c = c_ref[...]
    s = c * jax.nn.sigmoid(c)
    o_ref[...] = jnp.dot(s, w_ref[...], precision=HIGHEST,
                         preferred_element_type=F32) + b_ref[...]


def _adaln(cc, w_mod, b_mod):
    rows = cc.shape[0]
    n_out = w_mod.shape[1]
    blk = D_MODEL
    return pl.pallas_call(
        _adaln_kernel,
        out_shape=jax.ShapeDtypeStruct((rows, n_out), F32),
        grid=(n_out // blk,),
        in_specs=[pl.BlockSpec((rows, D_MODEL), lambda j: (0, 0)),
                  pl.BlockSpec((D_MODEL, blk), lambda j: (0, j)),
                  pl.BlockSpec((1, blk), lambda j: (0, j))],
        out_specs=pl.BlockSpec((rows, blk), lambda j: (0, j)),
        compiler_params=_params(1),
        name="adaln",
    )(cc, w_mod, b_mod)


def _rot_half(x, lane, base):
    rel = lane - base
    first = ((rel >= 0) & (rel < 8)) | ((rel >= 16) & (rel < 24))
    return jnp.where(first, -pltpu.roll(x, LANES - 8, axis=1), pltpu.roll(x, 8, axis=1))


def _front_kernel(rope, x_ref, sh_ref, sc_ref, g_ref, wqc_ref, wkr_ref, wglu_ref,
                  qg_ref, wq_ref, kvg_ref, wk_ref, wv_ref, pk_ref, *rest):
    if rope:
        cq_ref, sq_ref, q_ref, k_ref, v_ref, glu_ref = rest
    else:
        q_ref, k_ref, v_ref, glu_ref, ckv_ref, kr_ref = rest
    x = x_ref[...]
    h = _rms(x, g_ref[...]) * (1.0 + sc_ref[0]) + sh_ref[0]
    hb = h.astype(BF16)
    qc = _bdot(hb, wqc_ref[...])
    kr = _bdot(hb, wkr_ref[...])
    glu = _bdot(hb, wglu_ref[...])
    glu_ref[...] = glu[:, :CONV_CH] * jax.nn.sigmoid(glu[:, CONV_CH:])

    qn = _rms(qc[:, :Q_RANK], qg_ref[...]).astype(BF16)
    q = _bdot(qn, wq_ref[...])
    ckv = _rms(qc[:, Q_RANK:], kvg_ref[...])
    ckv_b = ckv.astype(BF16)
    kk = _bdot(ckv_b, wk_ref[...])
    v_ref[...] = _bdot(ckv_b, wv_ref[...]).astype(BF16)

    scale = (QK_NOPE + QK_ROPE) ** -0.5 * LOG2_E
    lane = lax.broadcasted_iota(I32, (x.shape[0], LANES), 1)
    if rope:
        cq = cq_ref[...]
        sq = sq_ref[...]
        for hd in range(MLA_HEADS):
            blk = q[:, hd * HEAD_PAD:(hd + 1) * HEAD_PAD]
            rot = blk * cq + _rot_half(blk, lane, QK_NOPE) * sq
            q_ref[:, hd * HEAD_PAD:(hd + 1) * HEAD_PAD] = (rot * scale).astype(BF16)
        krs = pltpu.roll(kr, QK_NOPE, axis=1)
        kr = pltpu.roll(krs * cq + _rot_half(krs, lane, QK_NOPE) * sq, LANES - QK_NOPE, axis=1)
        kr = jnp.where(lane < QK_ROPE, kr, 0.0)
    else:
        q_ref[...] = (q * scale).astype(BF16)
        ckv_ref[...] = ckv
        kr_ref[...] = kr[:, :QK_ROPE]
    k_ref[...] = (kk + _bdot(kr.astype(BF16), pk_ref[...])).astype(BF16)


def _front(x2d, mod3, mod_row, rows_per_mod, wts, rope_tabs):
    n = x2d.shape[0]
    rope = rope_tabs is not None
    tile = FRONT_TILE
    tiles_per_mod = rows_per_mod // tile
    full = lambda a: pl.BlockSpec(a.shape, lambda i: (0,) * a.ndim)

    def mod_spec(col):
        if mod_row is None:
            assert rows_per_mod % tile == 0
            return pl.BlockSpec((1, 1, D_MODEL), lambda i: (i // tiles_per_mod, 0, col))
        return pl.BlockSpec((1, 1, D_MODEL), lambda i: (mod_row, 0, col))

    row = lambda w: pl.BlockSpec((tile, w), lambda i: (i, 0))
    in_specs = [row(D_MODEL), mod_spec(0), mod_spec(1)] + [full(w) for w in wts]
    args = [x2d, mod3, mod3] + list(wts)
    out_shape = [jax.ShapeDtypeStruct((n, QK_WIDTH), BF16)] * 3 + \
                [jax.ShapeDtypeStruct((n, CONV_CH), F32)]
    out_specs = [row(QK_WIDTH)] * 3 + [row(CONV_CH)]
    if rope:
        seq_tiles = rope_tabs[0].shape[0] // tile
        tab = pl.BlockSpec((tile, LANES), lambda i: (i % seq_tiles, 0))
        in_specs += [tab, tab]
        args += list(rope_tabs)
    else:
        out_shape += [jax.ShapeDtypeStruct((n, KV_RANK), F32),
                      jax.ShapeDtypeStruct((n, QK_ROPE), F32)]
        out_specs += [row(KV_RANK), row(QK_ROPE)]
    return pl.pallas_call(
        functools.partial(_front_kernel, rope),
        out_shape=out_shape,
        grid=(n // tile,),
        in_specs=in_specs,
        out_specs=out_specs,
        compiler_params=_params(1, VMEM_LIMIT),
        name="front_rope" if rope else "front",
    )(*args)


def _ctx_kernel(ckv_ref, kr_ref, wk_ref, wv_ref, pk_ref, k_ref, v_ref):
    cb = ckv_ref[...].astype(BF16)
    kr = kr_ref[...].astype(BF16)
    k_ref[...] = (_bdot(cb, wk_ref[...]) + _bdot(kr, pk_ref[...])).astype(BF16)
    v_ref[...] = _bdot(cb, wv_ref[...]).astype(BF16)


def _ctx_kv(ckv2d, kr2d, wk, wv, pk):
    n = ckv2d.shape[0]
    full = lambda a: pl.BlockSpec(a.shape, lambda i: (0,) * a.ndim)
    row = lambda w: pl.BlockSpec((ROW_TILE, w), lambda i: (i, 0))
    pk32 = pk[:QK_ROPE]
    return pl.pallas_call(
        _ctx_kernel,
        out_shape=[jax.ShapeDtypeStruct((n, QK_WIDTH), BF16)] * 2,
        grid=(n // ROW_TILE,),
        in_specs=[row(KV_RANK), row(QK_ROPE), full(wk), full(wv), full(pk32)],
        out_specs=[row(QK_WIDTH)] * 2,
        compiler_params=_params(1),
        name="ctx_kv",
    )(ckv2d, kr2d, wk, wv, pk32)


def _attn_kernel(n_kv, q_ref, *rest):
    kv_refs = rest[:2 * n_kv]
    o_ref = rest[2 * n_kv]
    nt = (((1,), (1,)), ((), ()))
    outs = []
    for hd in range(MLA_HEADS):
        sl = slice(hd * HEAD_PAD, (hd + 1) * HEAD_PAD)
        qh = q_ref[:, sl]
        ss = [lax.dot_general(qh, kv_refs[2 * j][:, sl], nt, preferred_element_type=F32)
              for j in range(n_kv)]
        m = ss[0].max(axis=-1, keepdims=True)
        for s in ss[1:]:
            m = jnp.maximum(m, s.max(axis=-1, keepdims=True))
        ps = [jnp.exp2(s - m) for s in ss]
        l = ps[0].sum(axis=-1, keepdims=True)
        for p in ps[1:]:
            l = l + p.sum(axis=-1, keepdims=True)
        o = _bdot(ps[0].astype(BF16), kv_refs[1][:, sl])
        for j in range(1, n_kv):
            o = o + _bdot(ps[j].astype(BF16), kv_refs[2 * j + 1][:, sl])
        outs.append(o / l)
    for j in range(MLA_HEADS // 2):
        o_ref[:, j * LANES:(j + 1) * LANES] = (outs[2 * j] + outs[2 * j + 1]).astype(BF16)


def _attention(q, kvs, batch, seq):
    n = q.shape[0]
    q_tiles = seq // ROW_TILE
    in_specs = [pl.BlockSpec((ROW_TILE, QK_WIDTH), lambda b, i: (b * q_tiles + i, 0))]
    args = [q]
    for k, v, t in kvs:
        spec = pl.BlockSpec((t, QK_WIDTH), lambda b, i: (b, 0))
        in_specs += [spec, spec]
        args += [k, v]
    return pl.pallas_call(
        functools.partial(_attn_kernel, len(kvs)),
        out_shape=jax.ShapeDtypeStruct((n, MLA_HEADS * V_DIM), BF16),
        grid=(batch, q_tiles),
        in_specs=in_specs,
        out_specs=pl.BlockSpec((ROW_TILE, MLA_HEADS * V_DIM), lambda b, i: (b * q_tiles + i, 0)),
        compiler_params=_params(2, VMEM_LIMIT),
        name="attention",
    )(*args)


CONV_PAD = (CONV_WIDTH - 1) // 2
CONV_HALO = 16


SUBLANES = 8
SHIFT_ROWS = ROW_TILE + 2 * CONV_HALO - SUBLANES


def _back_kernel(seq, x_ref, attn_ref, glu_ref, cw_ref, cb_ref, lg_ref, lb_ref,
                 woa_ref, woc_ref, g1_ref, sh_ref, sc_ref, fg_ref, wrh_ref, wrl_ref,
                 x1_ref, h2_ref, afft_ref, vpad_ref, shift_ref):
    i = pl.program_id(1)

    @pl.when(i == 0)
    def _():
        zeros = jnp.zeros((CONV_HALO, CONV_CH), F32)
        vpad_ref[0:CONV_HALO, :] = zeros
        vpad_ref[CONV_HALO + seq:, :] = zeros
        vpad_ref[CONV_HALO:CONV_HALO + seq, :] = glu_ref[0]

    r0 = pl.multiple_of(i * ROW_TILE, ROW_TILE)
    win = vpad_ref[pl.ds(r0, ROW_TILE + 2 * CONV_HALO), :]
    for ph in range(1, SUBLANES):
        shift_ref[ph - 1] = win[ph:ph + SHIFT_ROWS, :]
    cw = cw_ref[...]
    acc = jnp.broadcast_to(cb_ref[...], (ROW_TILE, CONV_CH))
    for t in range(CONV_WIDTH):
        blk, ph = divmod(t + CONV_HALO - CONV_PAD, SUBLANES)
        if ph == 0:
            tap = vpad_ref[pl.ds(r0 + blk * SUBLANES, ROW_TILE), :]
        else:
            tap = shift_ref[ph - 1, blk * SUBLANES:blk * SUBLANES + ROW_TILE, :]
        acc = acc + tap * cw[t:t + 1, :]
    mu = jnp.mean(acc, axis=-1, keepdims=True)
    cen = acc - mu
    var = jnp.mean(cen * cen, axis=-1, keepdims=True)
    ln = cen * lax.rsqrt(var + EPS) * lg_ref[...] + lb_ref[...]
    cv = (ln * jax.nn.sigmoid(ln)).astype(BF16)

    mix = _bdot(attn_ref[...], woa_ref[...]) + _bdot(cv, woc_ref[...])
    x1 = x_ref[...] + g1_ref[0] * mix
    x1_ref[...] = x1
    h2 = _rms(x1, fg_ref[...]) * (1.0 + sc_ref[0]) + sh_ref[0]
    h2b = h2.astype(BF16)
    h2_ref[:, :D_MODEL] = h2b

    h2l = (h2 - h2b.astype(F32)).astype(BF16)
    wrh = wrh_ref[...]
    logits = _bdot(h2b, wrh) + (_bdot(h2l, wrh) + _bdot(h2b, wrl_ref[...]))
    lane = lax.broadcasted_iota(I32, logits.shape, 1)
    real = lane < N_EXPERTS
    m = jnp.max(jnp.where(real, logits, -jnp.inf), axis=-1, keepdims=True)
    e = jnp.exp(logits - m)
    aff = e / jnp.sum(jnp.where(real, e, 0.0), axis=-1, keepdims=True)
    aff_t = aff.T
    for blk in range(ROW_TILE // LANES):
        afft_ref[blk] = aff_t[:N_EXPERTS, blk * LANES:(blk + 1) * LANES]
    hi = aff.astype(BF16)
    r1 = aff - hi.astype(F32)
    mid = r1.astype(BF16)
    lo = (r1 - mid.astype(F32)).astype(BF16)
    zero = jnp.zeros_like(hi)
    h2_ref[:, D_MODEL:] = jnp.where(
        real, hi, jnp.where(lane < 2 * N_EXPERTS, mid, jnp.where(lane < 3 * N_EXPERTS, lo, zero)))


def _back(x2d, attn, glu3, mod3, mod_row, wts, batch, seq):
    n = x2d.shape[0]
    tiles = seq // ROW_TILE
    full = lambda a: pl.BlockSpec(a.shape, lambda b, i: (0,) * a.ndim)

    def mod_spec(col):
        if mod_row is None:
            return pl.BlockSpec((1, 1, D_MODEL), lambda b, i: (b, 0, col))
        return pl.BlockSpec((1, 1, D_MODEL), lambda b, i: (mod_row, 0, col))

    row = lambda w: pl.BlockSpec((ROW_TILE, w), lambda b, i: (b * tiles + i, 0))
    cw, cb, lg, lb, woa, woc, fg, wrh, wrl = wts
    in_specs = [row(D_MODEL), row(MLA_HEADS * V_DIM),
                pl.BlockSpec((1, seq, CONV_CH), lambda b, i: (b, 0, 0)),
                full(cw), full(cb), full(lg), full(lb), full(woa), full(woc),
                mod_spec(2), mod_spec(3), mod_spec(4), full(fg), full(wrh), full(wrl)]
    return pl.pallas_call(
        functools.partial(_back_kernel, seq),
        out_shape=[jax.ShapeDtypeStruct((n, D_MODEL), F32),
                   jax.ShapeDtypeStruct((n, XS_W), BF16),
                   jax.ShapeDtypeStruct((n // LANES, N_EXPERTS, LANES), F32)],
        grid=(batch, tiles),
        in_specs=in_specs,
        out_specs=[row(D_MODEL), row(XS_W),
                   pl.BlockSpec((ROW_TILE // LANES, N_EXPERTS, LANES),
                                lambda b, i: (b * tiles + i, 0, 0))],
        scratch_shapes=[pltpu.VMEM((seq + 2 * CONV_HALO, CONV_CH), F32),
                        pltpu.VMEM((SUBLANES - 1, SHIFT_ROWS, CONV_CH), F32)],
        compiler_params=_params(2, VMEM_LIMIT),
        name="back",
    )(x2d, attn, glu3, cw, cb, lg, lb, woa, woc, mod3, mod3, mod3, fg, wrh, wrl)


def _route_kernel(caps, *refs):
    n_g = len(caps)
    affs = [r[...] for r in refs[:n_g]]
    pos_refs = refs[n_g:2 * n_g]
    base_refs = refs[2 * n_g:]
    ones = jnp.ones((LANES, LANES), BF16)
    l_r = lax.broadcasted_iota(I32, (LANES, LANES), 0)
    l_c = lax.broadcasted_iota(I32, (LANES, LANES), 1)
    tri = jnp.where(l_r <= l_c, 1.0, 0.0).astype(BF16)

    def lane_dot(mask3, w):
        r, e, _ = mask3.shape
        mb = jnp.where(mask3, 1.0, 0.0).astype(BF16).reshape(r * e, LANES)
        return _bdot(mb, w).reshape(r, e, LANES)

    def expert_count(mask3):
        return jnp.sum(lane_dot(mask3, ones), axis=0, keepdims=True)

    def excl_cumsum(mask3):
        rows = mask3.shape[0]
        tot = lane_dot(mask3, ones)
        run = tot
        s = 1
        while s < rows:
            run = run + jnp.concatenate([jnp.zeros((s,) + run.shape[1:], F32), run[:rows - s]], axis=0)
            s *= 2
        before = run - tot
        return before + lane_dot(mask3, tri) - jnp.where(mask3, 1.0, 0.0), before

    def step(t, thrs):
        bit = lax.shift_left(jnp.int32(1), 30 - t)
        out = []
        for aff, thr, cap in zip(affs, thrs, caps):
            cand = thr | bit
            cnt = expert_count(aff >= pltpu.bitcast(cand, F32))
            out.append(jnp.where(cnt >= cap, cand, thr))
        return tuple(out)

    init = tuple(jnp.zeros((1,) + a.shape[1:], I32) for a in affs)
    thrs = lax.fori_loop(0, 31, step, init)
    for aff, thr_bits, cap, pos_ref, base_ref in zip(affs, thrs, caps, pos_refs, base_refs):
        thr = pltpu.bitcast(thr_bits, F32)
        gt = aff > thr
        eq = aff == thr
        need = cap - expert_count(gt)
        tie_rank, _ = excl_cumsum(eq)
        sel = gt | (eq & (tie_rank < need))
        pos, before = excl_cumsum(sel)
        pos_ref[...] = jnp.where(sel, pos, -1.0).astype(I32)
        base_ref[...] = before


def _route(affs, caps):
    shapes = [a.shape for a in affs]
    return pl.pallas_call(
        functools.partial(_route_kernel, tuple(caps)),
        out_shape=[jax.ShapeDtypeStruct(s, I32) for s in shapes] +
                  [jax.ShapeDtypeStruct(s, F32) for s in shapes],
        compiler_params=pltpu.CompilerParams(vmem_limit_bytes=VMEM_LIMIT),
        name="route",
    )(*affs)


def _window(off_ref, e, i, n_tiles, w, cap):
    del n_tiles
    first = off_ref[i * N_EXPERTS + e] + w * SLOT_WIN
    return first, pl.multiple_of(jnp.minimum(first, cap - SLOT_WIN), BF16_ROWS)


def _one_hot_rows(pos_ref, wins):
    r = lax.broadcasted_iota(I32, (SLOT_WIN, LANES), 0)
    blocks = []
    for e, (first, off) in enumerate(wins):
        halves = []
        for h in range(pos_ref.shape[0]):
            pe = pos_ref[h, e:e + 1, :]
            rel = jnp.where(pe >= first, pe - off, -1)
            halves.append(jnp.where(rel == r, 1.0, 0.0).astype(BF16))
        blocks.append(jnp.concatenate(halves, axis=1))
    return jnp.concatenate(blocks, axis=0)


def _dispatch_kernel(cap, n_tiles, off_ref, nwin_ref, h2_ref, pos_ref, xs_hbm, xs_ref, sem):
    i = pl.program_id(0)

    @pl.when(i == 0)
    def _():
        def zero(e, carry):
            xs_ref[e] = jnp.zeros(xs_ref.shape[1:], BF16)
            return carry
        lax.fori_loop(0, N_EXPERTS, zero, 0)

    def window(w, carry):
        wins = [_window(off_ref, e, i, n_tiles, w, cap) for e in range(N_EXPERTS)]
        p = _one_hot_rows(pos_ref, wins)
        slab = _bdot(p, h2_ref[...])
        for e, (_, off) in enumerate(wins):
            cur = xs_ref[e, pl.ds(off, SLOT_WIN), :].astype(F32)
            xs_ref[e, pl.ds(off, SLOT_WIN), :] = (
                cur + slab[e * SLOT_WIN:(e + 1) * SLOT_WIN]).astype(BF16)
        return carry

    lax.fori_loop(0, nwin_ref[i], window, 0)

    @pl.when(i == n_tiles - 1)
    def _():
        cp = pltpu.make_async_copy(xs_ref, xs_hbm, sem)
        cp.start()
        cp.wait()


def _dispatch(h2aug, pos, offs, nwin, cap):
    n = h2aug.shape[0]
    n_tiles = n // ROW_TILE
    return pl.pallas_call(
        functools.partial(_dispatch_kernel, cap, n_tiles),
        out_shape=jax.ShapeDtypeStruct((N_EXPERTS, cap, XS_W), BF16),
        grid_spec=pltpu.PrefetchScalarGridSpec(
            num_scalar_prefetch=2,
            grid=(n_tiles,),
            in_specs=[pl.BlockSpec((ROW_TILE, XS_W), lambda i, *_: (i, 0)),
                      pl.BlockSpec((ROW_TILE // LANES, N_EXPERTS, LANES), lambda i, *_: (i, 0, 0))],
            out_specs=pl.BlockSpec(memory_space=pl.ANY),
            scratch_shapes=[pltpu.VMEM((N_EXPERTS, cap, XS_W), BF16),
                            pltpu.SemaphoreType.DMA]),
        compiler_params=_params(1, VMEM_LIMIT),
        name="dispatch",
    )(offs, nwin, h2aug, pos)


FF_CHUNK = 256
EXP_ROWS = 512


def _experts_kernel(n_groups, n_chunks, *refs):
    xs_refs = refs[:n_groups]
    wg_ref, wu_ref, wd_ref = refs[n_groups:n_groups + 3]
    ys_refs = refs[n_groups + 3:2 * n_groups + 3]
    acc_ref = refs[2 * n_groups + 3]
    e = pl.program_id(0)
    j = pl.program_id(1)
    cap = xs_refs[0].shape[1]

    @pl.when(j == 0)
    def _():
        for g in range(n_groups):
            acc_ref[g] = jnp.zeros(acc_ref.shape[1:], F32)

    wg = wg_ref[0].astype(BF16)
    wu = wu_ref[0].astype(BF16)
    wd = wd_ref[0].astype(BF16)
    for g, xs_ref in enumerate(xs_refs):
        for r in range(cap // EXP_ROWS):
            rows = slice(r * EXP_ROWS, (r + 1) * EXP_ROWS)
            x = xs_ref[0, rows, :D_MODEL]
            a = _bdot(x, wg)
            u = _bdot(x, wu)
            hm = (a * jax.nn.sigmoid(a) * u).astype(BF16)
            acc_ref[g, rows, :] += _bdot(hm, wd)

    @pl.when(j == n_chunks - 1)
    def _():
        for g, (xs_ref, ys_ref) in enumerate(zip(xs_refs, ys_refs)):
            aug = xs_ref[0, :, D_MODEL:].astype(F32)
            lane = lax.broadcasted_iota(I32, aug.shape, 1)
            mine = (lane == e) | (lane == e + N_EXPERTS) | (lane == e + 2 * N_EXPERTS)
            gate = jnp.sum(jnp.where(mine, aug, 0.0), axis=-1, keepdims=True)
            ys_ref[0] = (acc_ref[g] * gate).astype(BF16)


def _experts(xs_list, w_gate, w_up, w_down):
    n_groups = len(xs_list)
    cap = xs_list[0].shape[1]
    ff = w_gate.shape[2]
    n_chunks = ff // FF_CHUNK
    xs_spec = pl.BlockSpec((1, cap, XS_W), lambda e, j: (e, 0, 0))
    up_spec = pl.BlockSpec((1, D_MODEL, FF_CHUNK), lambda e, j: (e, 0, j))
    down_spec = pl.BlockSpec((1, FF_CHUNK, D_MODEL), lambda e, j: (e, j, 0))
    ys_spec = pl.BlockSpec((1, cap, D_MODEL), lambda e, j: (e, 0, 0))
    return pl.pallas_call(
        functools.partial(_experts_kernel, n_groups, n_chunks),
        out_shape=[jax.ShapeDtypeStruct((N_EXPERTS, cap, D_MODEL), BF16)] * n_groups,
        grid=(N_EXPERTS, n_chunks),
        in_specs=[xs_spec] * n_groups + [up_spec, up_spec, down_spec],
        out_specs=[ys_spec] * n_groups,
        scratch_shapes=[pltpu.VMEM((n_groups, cap, D_MODEL), F32)],
        compiler_params=_params(2, VMEM_LIMIT),
        name="experts",
    )(*xs_list, w_gate, w_up, w_down)


def _combine_kernel(cap, n_tiles, off_ref, nwin_ref, ys_hbm, pos_ref, x1_ref, g2_ref, fg_ref,
                    y_ref, buf_ref, sems):
    i = pl.program_id(0)
    slot = i % 2

    def copies(tile, w, s):
        out = []
        for e in range(N_EXPERTS):
            _, off = _window(off_ref, e, tile, n_tiles, w, cap)
            out.append(pltpu.make_async_copy(
                ys_hbm.at[e, pl.ds(off, SLOT_WIN), :],
                buf_ref.at[s, pl.ds(e * SLOT_WIN, SLOT_WIN), :],
                sems.at[s]))
        return out

    @pl.when(i == 0)
    def _():
        for cp in copies(0, 0, 0):
            cp.start()

    @pl.when(i + 1 < n_tiles)
    def _():
        for cp in copies(i + 1, 0, 1 - slot):
            cp.start()

    tn = (((0,), (0,)), ((), ()))

    def window(w, acc):
        @pl.when(w > 0)
        def _():
            for cp in copies(i, w, slot):
                cp.start()

        for cp in copies(i, w, slot):
            cp.wait()
        wins = [_window(off_ref, e, i, n_tiles, w, cap) for e in range(N_EXPERTS)]
        p = _one_hot_rows(pos_ref, wins)
        return acc + lax.dot_general(p, buf_ref[slot], tn, preferred_element_type=F32)

    moe = lax.fori_loop(0, nwin_ref[i], window, jnp.zeros((ROW_TILE, D_MODEL), F32))
    out = x1_ref[...] + g2_ref[0] * moe
    y_ref[...] = _rms(out, fg_ref[...])


def _combine(ys, pos, offs, nwin, x1, mod3, mod_row, rows_per_mod, final_g, cap):
    n = x1.shape[0]
    n_tiles = n // ROW_TILE
    tiles_per_mod = rows_per_mod // ROW_TILE
    if mod_row is None:
        g2 = pl.BlockSpec((1, 1, D_MODEL), lambda i, *_: (i // tiles_per_mod, 0, 5))
    else:
        g2 = pl.BlockSpec((1, 1, D_MODEL), lambda i, *_: (mod_row, 0, 5))
    return pl.pallas_call(
        functools.partial(_combine_kernel, cap, n_tiles),
        out_shape=jax.ShapeDtypeStruct((n, D_MODEL), F32),
        grid_spec=pltpu.PrefetchScalarGridSpec(
            num_scalar_prefetch=2,
            grid=(n_tiles,),
            in_specs=[pl.BlockSpec(memory_space=pl.ANY),
                      pl.BlockSpec((ROW_TILE // LANES, N_EXPERTS, LANES), lambda i, *_: (i, 0, 0)),
                      pl.BlockSpec((ROW_TILE, D_MODEL), lambda i, *_: (i, 0)),
                      g2,
                      pl.BlockSpec((1, D_MODEL), lambda i, *_: (0, 0))],
            out_specs=pl.BlockSpec((ROW_TILE, D_MODEL), lambda i, *_: (i, 0)),
            scratch_shapes=[pltpu.VMEM((2, N_EXPERTS * SLOT_WIN, D_MODEL), BF16),
                            pltpu.SemaphoreType.DMA((2,))]),
        compiler_params=_params(1, VMEM_LIMIT),
        name="combine",
    )(offs, nwin, ys, pos, x1, mod3, final_g)


def _head_cols(w3):
    k, _, d = w3.shape
    return jnp.pad(w3, ((0, 0), (0, 0), (0, HEAD_PAD - d))).reshape(k, QK_WIDTH)


def _value_cols(v3):
    k = v3.shape[0]
    pair = v3.reshape(k, MLA_HEADS // 2, 2, V_DIM)
    zeros = jnp.zeros((k, MLA_HEADS // 2, V_DIM), v3.dtype)
    even = jnp.concatenate([pair[:, :, 0], zeros], axis=-1)
    odd = jnp.concatenate([zeros, pair[:, :, 1]], axis=-1)
    return jnp.stack([even, odd], axis=2).reshape(k, QK_WIDTH)


def _rope_tables(seq):
    half = QK_ROPE // 2
    pos = jnp.arange(seq)
    inv_freq = 1.0 / (ROPE_BASE ** (jnp.arange(0, half, 2, dtype=F32) / half))

    def cs(p):
        ang = p.astype(F32)[:, None] * inv_freq[None, :]
        ang = jnp.concatenate([ang, ang], axis=-1)
        return jnp.cos(ang), jnp.sin(ang)

    cr, sr = cs(pos // GRID_W)
    cc, sc = cs(pos % GRID_W)
    ones = jnp.ones((seq, QK_NOPE), F32)
    zpad = jnp.zeros((seq, HEAD_PAD - QK_NOPE - QK_ROPE), F32)
    cq = jnp.concatenate([ones, cr, cc, zpad], axis=-1)
    sq = jnp.concatenate([0.0 * ones, sr, sc, zpad], axis=-1)
    return cq, sq


def _prep_weights(w_in, q_norm_g, w_q_up, kv_norm_g, w_kv_up, attn_norm_g):
    wqc = w_in[:, :Q_RANK + KV_RANK].astype(BF16)
    wkr = jnp.pad(w_in[:, Q_RANK + KV_RANK:Q_RANK + KV_RANK + QK_ROPE],
                  ((0, 0), (0, LANES - QK_ROPE))).astype(BF16)
    wglu = w_in[:, Q_RANK + KV_RANK + QK_ROPE:].astype(BF16)
    wq = _head_cols(w_q_up.astype(BF16).reshape(Q_RANK, MLA_HEADS, QK_NOPE + QK_ROPE))
    kv3 = w_kv_up.astype(BF16).reshape(KV_RANK, MLA_HEADS, QK_NOPE + V_DIM)
    wk = _head_cols(kv3[:, :, :QK_NOPE])
    wv = _value_cols(kv3[:, :, QK_NOPE:])
    pk = np.zeros((LANES, QK_WIDTH), np.float32)
    for hd in range(MLA_HEADS):
        for j in range(QK_ROPE):
            pk[j, hd * HEAD_PAD + QK_NOPE + j] = 1.0
    pk = jnp.asarray(pk, BF16)
    return (attn_norm_g[None, :], wqc, wkr, wglu, q_norm_g[None, :], wq,
            kv_norm_g[None, :], wk, wv, pk)


def _route_plan(base, cap):
    rows_per_tile = ROW_TILE // LANES
    start = base[::rows_per_tile, :, 0].astype(I32)
    end = jnp.concatenate([start[1:], jnp.full((1, N_EXPERTS), cap, I32)], axis=0)
    off = (start // BF16_ROWS) * BF16_ROWS
    span = end - off
    nwin = jnp.maximum(jnp.max((span + SLOT_WIN - 1) // SLOT_WIN, axis=1), 1).astype(I32)
    return off.reshape(-1), nwin


def kernel(x_prompt, x_sample, cache_ckv, cache_krope, c, c_ctx, w_mod, b_mod, attn_norm_g,
           w_in, q_norm_g, w_q_up, kv_norm_g, w_kv_up, conv_w, conv_b, conv_ln_g, conv_ln_b,
           w_out, ffn_norm_g, w_router, w_gate, w_up, w_down, final_norm_g):
    depth = w_mod.shape[0]
    assert depth == 1
    bp, sp, _ = x_prompt.shape
    bs, ss, _ = x_sample.shape
    past = cache_ckv.shape[2]
    ctx_row = bs

    mod_rows = 16
    cc = jnp.concatenate([c, c_ctx[None, :], jnp.zeros((mod_rows - bs - 1, D_MODEL), F32)], axis=0)
    mod = _adaln(cc, w_mod[0], b_mod[0][None, :])
    mod3 = mod.reshape(mod_rows, 1, N_MOD * D_MODEL)

    fw = _prep_weights(w_in[0], q_norm_g[0], w_q_up[0], kv_norm_g[0], w_kv_up[0], attn_norm_g[0])
    wk, wv, pk = fw[7], fw[8], fw[9]
    wo = w_out[0].astype(BF16)
    wr = jnp.pad(jnp.tile(w_router[0], (1, 3)), ((0, 0), (0, LANES - 3 * N_EXPERTS)))
    wrh = wr.astype(BF16)
    wrl = (wr - wrh.astype(F32)).astype(BF16)
    bw = (conv_w[0], conv_b[0][None, :], conv_ln_g[0][None, :], conv_ln_b[0][None, :],
          wo[:MLA_HEADS * V_DIM], wo[MLA_HEADS * V_DIM:], ffn_norm_g[0][None, :], wrh, wrl)

    xp2 = x_prompt.reshape(bp * sp, D_MODEL)
    xs2 = x_sample.reshape(bs * ss, D_MODEL)

    qp, kp, vp, glup, ckvp, krp = _front(xp2, mod3, ctx_row, sp, fw, None)
    attn_p = _attention(qp, [(kp, vp, sp)], bp, sp)
    x1p, h2p, afftp = _back(xp2, attn_p, glup.reshape(bp, sp, CONV_CH), mod3, ctx_row, bw, bp, sp)

    qs, ks, vs, glus = _front(xs2, mod3, None, ss, fw, _rope_tables(ss))
    kc, vc = _ctx_kv(cache_ckv[:, 0].reshape(bs * past, KV_RANK),
                     cache_krope[:, 0].reshape(bs * past, QK_ROPE), wk, wv, pk)
    attn_s = _attention(qs, [(kc, vc, past), (ks, vs, ss)], bs, ss)
    x1s, h2s, affts = _back(xs2, attn_s, glus.reshape(bs, ss, CONV_CH), mod3, None, bw, bs, ss)

    caps = [EC_FACTOR * h2.shape[0] // N_EXPERTS for h2 in (h2p, h2s)]
    routed = _route([afftp, affts], caps)
    plans = []
    xs_list = []
    for g, (h2, cap) in enumerate(zip((h2p, h2s), caps)):
        pos, base = routed[g], routed[len(caps) + g]
        offs, nwin = _route_plan(base, cap)
        plans.append((pos, offs, nwin, cap))
        xs_list.append(_dispatch(h2, pos, offs, nwin, cap))
    ys_list = _experts(xs_list, w_gate[0], w_up[0], w_down[0])

    fg = final_norm_g[None, :]
    pos, offs, nwin, cap = plans[0]
    y_prompt = _combine(ys_list[0], pos, offs, nwin, x1p, mod3, ctx_row, sp, fg, cap)
    pos, offs, nwin, cap = plans[1]
    y_sample = _combine(ys_list[1], pos, offs, nwin, x1s, mod3, None, ss, fg, cap)

    return (y_prompt.reshape(bp, sp, D_MODEL), y_sample.reshape(bs, ss, D_MODEL),
            ckvp.reshape(bp, depth, sp, KV_RANK), krp.reshape(bp, depth, sp, QK_ROPE))
```

```python
import functools

import jax
import jax.numpy as jnp
import numpy as np
from jax import lax
from jax.experimental import pallas as pl
from jax.experimental.pallas import tpu as pltpu

F32 = jnp.float32
BF16 = jnp.bfloat16
I32 = jnp.int32

LANES = 128
BF16_ROWS = 16
VMEM_LIMIT = 56 * 1024 * 1024

D_MODEL = 1024
GRID_W = 64
MLA_HEADS = 8
QK_NOPE = 64
QK_ROPE = 32
V_DIM = 64
Q_RANK = 768
KV_RANK = 256
CONV_CH = 512
CONV_WIDTH = 31
N_EXPERTS = 16
EXPERT_FF = 1024
EC_FACTOR = 2
ROPE_BASE = 10000.0
EPS = 1e-6
N_MOD = 6

HEAD_PAD = LANES
QK_WIDTH = MLA_HEADS * HEAD_PAD
ROW_TILE = 256
FRONT_TILE = 512
SLOT_WIN = 64
AUG = LANES
XS_W = D_MODEL + AUG
HIGHEST = lax.Precision.HIGHEST
LOG2_E = 1.4426950408889634


def _params(n_axes, vmem=None):
    return pltpu.CompilerParams(
        dimension_semantics=("arbitrary",) * n_axes,
        vmem_limit_bytes=vmem)


def _rms(x, g):
    ms = jnp.mean(x * x, axis=-1, keepdims=True)
    return x * lax.rsqrt(ms + EPS) * g


def _bdot(a, b):
    return jnp.dot(a, b, preferred_element_type=F32)


def _adaln_kernel(c_ref, w_ref, b_ref, o_ref):
    c = c_ref[...]
    s = c * jax.nn.sigmoid(c)
    o_ref[...] = jnp.dot(s, w_ref[...], precision=HIGHEST,
                         preferred_element_type=F32) + b_ref[...]


def _adaln(cc, w_mod, b_mod):
    rows = cc.shape[0]
    n_out = w_mod.shape[1]
    blk = D_MODEL
    return pl.pallas_call(
        _adaln_kernel,
        out_shape=jax.ShapeDtypeStruct((rows, n_out), F32),
        grid=(n_out // blk,),
        in_specs=[pl.BlockSpec((rows, D_MODEL), lambda j: (0, 0)),
                  pl.BlockSpec((D_MODEL, blk), lambda j: (0, j)),
                  pl.BlockSpec((1, blk), lambda j: (0, j))],
        out_specs=pl.BlockSpec((rows, blk), lambda j: (0, j)),
        compiler_params=_params(1),
        name="adaln",
    )(cc, w_mod, b_mod)


def _rot_half(x, lane, base):
    rel = lane - base
    first = ((rel >= 0) & (rel < 8)) | ((rel >= 16) & (rel < 24))
    return jnp.where(first, -pltpu.roll(x, LANES - 8, axis=1), pltpu.roll(x, 8, axis=1))


def _front_kernel(rope, x_ref, sh_ref, sc_ref, g_ref, wqc_ref, wkr_ref, wglu_ref,
                  qg_ref, wq_ref, kvg_ref, wk_ref, wv_ref, pk_ref, *rest):
    if rope:
        cq_ref, sq_ref, q_ref, k_ref, v_ref, glu_ref = rest
    else:
        q_ref, k_ref, v_ref, glu_ref, ckv_ref, kr_ref = rest
    x = x_ref[...]
    h = _rms(x, g_ref[...]) * (1.0 + sc_ref[0]) + sh_ref[0]
    hb = h.astype(BF16)
    qc = _bdot(hb, wqc_ref[...])
    kr = _bdot(hb, wkr_ref[...])
    glu = _bdot(hb, wglu_ref[...])
    glu_ref[...] = glu[:, :CONV_CH] * jax.nn.sigmoid(glu[:, CONV_CH:])

    qn = _rms(qc[:, :Q_RANK], qg_ref[...]).astype(BF16)
    q = _bdot(qn, wq_ref[...])
    ckv = _rms(qc[:, Q_RANK:], kvg_ref[...])
    ckv_b = ckv.astype(BF16)
    kk = _bdot(ckv_b, wk_ref[...])
    v_ref[...] = _bdot(ckv_b, wv_ref[...]).astype(BF16)

    scale = (QK_NOPE + QK_ROPE) ** -0.5 * LOG2_E
    lane = lax.broadcasted_iota(I32, (x.shape[0], LANES), 1)
    if rope:
        cq = cq_ref[...]
        sq = sq_ref[...]
        for hd in range(MLA_HEADS):
            blk = q[:, hd * HEAD_PAD:(hd + 1) * HEAD_PAD]
            rot = blk * cq + _rot_half(blk, lane, QK_NOPE) * sq
            q_ref[:, hd * HEAD_PAD:(hd + 1) * HEAD_PAD] = (rot * scale).astype(BF16)
        krs = pltpu.roll(kr, QK_NOPE, axis=1)
        kr = pltpu.roll(krs * cq + _rot_half(krs, lane, QK_NOPE) * sq, LANES - QK_NOPE, axis=1)
        kr = jnp.where(lane < QK_ROPE, kr, 0.0)
    else:
        q_ref[...] = (q * scale).astype(BF16)
        ckv_ref[...] = ckv
        kr_ref[...] = kr[:, :QK_ROPE]
    k_ref[...] = (kk + _bdot(kr.astype(BF16), pk_ref[...])).astype(BF16)


def _front(x2d, mod3, mod_row, rows_per_mod, wts, rope_tabs):
    n = x2d.shape[0]
    rope = rope_tabs is not None
    tile = FRONT_TILE
    tiles_per_mod = rows_per_mod // tile
    full = lambda a: pl.BlockSpec(a.shape, lambda i: (0,) * a.ndim)

    def mod_spec(col):
        if mod_row is None:
            assert rows_per_mod % tile == 0
            return pl.BlockSpec((1, 1, D_MODEL), lambda i: (i // tiles_per_mod, 0, col))
        return pl.BlockSpec((1, 1, D_MODEL), lambda i: (mod_row, 0, col))

    row = lambda w: pl.BlockSpec((tile, w), lambda i: (i, 0))
    in_specs = [row(D_MODEL), mod_spec(0), mod_spec(1)] + [full(w) for w in wts]
    args = [x2d, mod3, mod3] + list(wts)
    out_shape = [jax.ShapeDtypeStruct((n, QK_WIDTH), BF16)] * 3 + \
                [jax.ShapeDtypeStruct((n, CONV_CH), F32)]
    out_specs = [row(QK_WIDTH)] * 3 + [row(CONV_CH)]
    if rope:
        seq_tiles = rope_tabs[0].shape[0] // tile
        tab = pl.BlockSpec((tile, LANES), lambda i: (i % seq_tiles, 0))
        in_specs += [tab, tab]
        args += list(rope_tabs)
    else:
        out_shape += [jax.ShapeDtypeStruct((n, KV_RANK), F32),
                      jax.ShapeDtypeStruct((n, QK_ROPE), F32)]
        out_specs += [row(KV_RANK), row(QK_ROPE)]
    return pl.pallas_call(
        functools.partial(_front_kernel, rope),
        out_shape=out_shape,
        grid=(n // tile,),
        in_specs=in_specs,
        out_specs=out_specs,
        compiler_params=_params(1, VMEM_LIMIT),
        name="front_rope" if rope else "front",
    )(*args)


def _ctx_kernel(ckv_ref, kr_ref, wk_ref, wv_ref, pk_ref, k_ref, v_ref):
    cb = ckv_ref[...].astype(BF16)
    kr = kr_ref[...].astype(BF16)
    k_ref[...] = (_bdot(cb, wk_ref[...]) + _bdot(kr, pk_ref[...])).astype(BF16)
    v_ref[...] = _bdot(cb, wv_ref[...]).astype(BF16)


def _ctx_kv(ckv2d, kr2d, wk, wv, pk):
    n = ckv2d.shape[0]
    full = lambda a: pl.BlockSpec(a.shape, lambda i: (0,) * a.ndim)
    row = lambda w: pl.BlockSpec((ROW_TILE, w), lambda i: (i, 0))
    pk32 = pk[:QK_ROPE]
    return pl.pallas_call(
        _ctx_kernel,
        out_shape=[jax.ShapeDtypeStruct((n, QK_WIDTH), BF16)] * 2,
        grid=(n // ROW_TILE,),
        in_specs=[row(KV_RANK), row(QK_ROPE), full(wk), full(wv), full(pk32)],
        out_specs=[row(QK_WIDTH)] * 2,
        compiler_params=_params(1),
        name="ctx_kv",
    )(ckv2d, kr2d, wk, wv, pk32)


def _attn_kernel(n_kv, q_ref, *rest):
    kv_refs = rest[:2 * n_kv]
    o_ref = rest[2 * n_kv]
    nt = (((1,), (1,)), ((), ()))
    outs = []
    for hd in range(MLA_HEADS):
        sl = slice(hd * HEAD_PAD, (hd + 1) * HEAD_PAD)
        qh = q_ref[:, sl]
        ss = [lax.dot_general(qh, kv_refs[2 * j][:, sl], nt, preferred_element_type=F32)
              for j in range(n_kv)]
        m = ss[0].max(axis=-1, keepdims=True)
        for s in ss[1:]:
            m = jnp.maximum(m, s.max(axis=-1, keepdims=True))
        ps = [jnp.exp2(s - m) for s in ss]
        l = ps[0].sum(axis=-1, keepdims=True)
        for p in ps[1:]:
            l = l + p.sum(axis=-1, keepdims=True)
        o = _bdot(ps[0].astype(BF16), kv_refs[1][:, sl])
        for j in range(1, n_kv):
            o = o + _bdot(ps[j].astype(BF16), kv_refs[2 * j + 1][:, sl])
        outs.append(o / l)
    for j in range(MLA_HEADS // 2):
        o_ref[:, j * LANES:(j + 1) * LANES] = (outs[2 * j] + outs[2 * j + 1]).astype(BF16)


def _attention(q, kvs, batch, seq):
    n = q.shape[0]
    q_tiles = seq // ROW_TILE
    in_specs = [pl.BlockSpec((ROW_TILE, QK_WIDTH), lambda b, i: (b * q_tiles + i, 0))]
    args = [q]
    for k, v, t in kvs:
        spec = pl.BlockSpec((t, QK_WIDTH), lambda b, i: (b, 0))
        in_specs += [spec, spec]
        args += [k, v]
    return pl.pallas_call(
        functools.partial(_attn_kernel, len(kvs)),
        out_shape=jax.ShapeDtypeStruct((n, MLA_HEADS * V_DIM), BF16),
        grid=(batch, q_tiles),
        in_specs=in_specs,
        out_specs=pl.BlockSpec((ROW_TILE, MLA_HEADS * V_DIM), lambda b, i: (b * q_tiles + i, 0)),
        compiler_params=_params(2, VMEM_LIMIT),
        name="attention",
    )(*args)


CONV_PAD = (CONV_WIDTH - 1) // 2
CONV_HALO = 16


SUBLANES = 8
SHIFT_ROWS = ROW_TILE + 2 * CONV_HALO - SUBLANES


def _back_kernel(seq, x_ref, attn_ref, glu_ref, cw_ref, cb_ref, lg_ref, lb_ref,
                 woa_ref, woc_ref, g1_ref, sh_ref, sc_ref, fg_ref, wrh_ref, wrl_ref,
                 x1_ref, h2_ref, afft_ref, vpad_ref, shift_ref):
    i = pl.program_id(1)

    @pl.when(i == 0)
    def _():
        zeros = jnp.zeros((CONV_HALO, CONV_CH), F32)
        vpad_ref[0:CONV_HALO, :] = zeros
        vpad_ref[CONV_HALO + seq:, :] = zeros
        vpad_ref[CONV_HALO:CONV_HALO + seq, :] = glu_ref[0]

    r0 = pl.multiple_of(i * ROW_TILE, ROW_TILE)
    win = vpad_ref[pl.ds(r0, ROW_TILE + 2 * CONV_HALO), :]
    for ph in range(1, SUBLANES):
        shift_ref[ph - 1] = win[ph:ph + SHIFT_ROWS, :]
    cw = cw_ref[...]
    acc = jnp.broadcast_to(cb_ref[...], (ROW_TILE, CONV_CH))
    for t in range(CONV_WIDTH):
        blk, ph = divmod(t + CONV_HALO - CONV_PAD, SUBLANES)
        if ph == 0:
            tap = vpad_ref[pl.ds(r0 + blk * SUBLANES, ROW_TILE), :]
        else:
            tap = shift_ref[ph - 1, blk * SUBLANES:blk * SUBLANES + ROW_TILE, :]
        acc = acc + tap * cw[t:t + 1, :]
    mu = jnp.mean(acc, axis=-1, keepdims=True)
    cen = acc - mu
    var = jnp.mean(cen * cen, axis=-1, keepdims=True)
    ln = cen * lax.rsqrt(var + EPS) * lg_ref[...] + lb_ref[...]
    cv = (ln * jax.nn.sigmoid(ln)).astype(BF16)

    mix = _bdot(attn_ref[...], woa_ref[...]) + _bdot(cv, woc_ref[...])
    x1 = x_ref[...] + g1_ref[0] * mix
    x1_ref[...] = x1
    h2 = _rms(x1, fg_ref[...]) * (1.0 + sc_ref[0]) + sh_ref[0]
    h2b = h2.astype(BF16)
    h2_ref[:, :D_MODEL] = h2b

    h2l = (h2 - h2b.astype(F32)).astype(BF16)
    wrh = wrh_ref[...]
    logits = _bdot(h2b, wrh) + (_bdot(h2l, wrh) + _bdot(h2b, wrl_ref[...]))
    lane = lax.broadcasted_iota(I32, logits.shape, 1)
    real = lane < N_EXPERTS
    m = jnp.max(jnp.where(real, logits, -jnp.inf), axis=-1, keepdims=True)
    e = jnp.exp(logits - m)
    aff = e / jnp.sum(jnp.where(real, e, 0.0), axis=-1, keepdims=True)
    aff_t = aff.T
    for blk in range(ROW_TILE // LANES):
        afft_ref[blk] = aff_t[:N_EXPERTS, blk * LANES:(blk + 1) * LANES]
    hi = aff.astype(BF16)
    r1 = aff - hi.astype(F32)
    mid = r1.astype(BF16)
    lo = (r1 - mid.astype(F32)).astype(BF16)
    zero = jnp.zeros_like(hi)
    h2_ref[:, D_MODEL:] = jnp.where(
        real, hi, jnp.where(lane < 2 * N_EXPERTS, mid, jnp.where(lane < 3 * N_EXPERTS, lo, zero)))


def _back(x2d, attn, glu3, mod3, mod_row, wts, batch, seq):
    n = x2d.shape[0]
    tiles = seq // ROW_TILE
    full = lambda a: pl.BlockSpec(a.shape, lambda b, i: (0,) * a.ndim)

    def mod_spec(col):
        if mod_row is None:
            return pl.BlockSpec((1, 1, D_MODEL), lambda b, i: (b, 0, col))
        return pl.BlockSpec((1, 1, D_MODEL), lambda b, i: (mod_row, 0, col))

    row = lambda w: pl.BlockSpec((ROW_TILE, w), lambda b, i: (b * tiles + i, 0))
    cw, cb, lg, lb, woa, woc, fg, wrh, wrl = wts
    in_specs = [row(D_MODEL), row(MLA_HEADS * V_DIM),
                pl.BlockSpec((1, seq, CONV_CH), lambda b, i: (b, 0, 0)),
                full(cw), full(cb), full(lg), full(lb), full(woa), full(woc),
                mod_spec(2), mod_spec(3), mod_spec(4), full(fg), full(wrh), full(wrl)]
    return pl.pallas_call(
        functools.partial(_back_kernel, seq),
        out_shape=[jax.ShapeDtypeStruct((n, D_MODEL), F32),
                   jax.ShapeDtypeStruct((n, XS_W), BF16),
                   jax.ShapeDtypeStruct((n // LANES, N_EXPERTS, LANES), F32)],
        grid=(batch, tiles),
        in_specs=in_specs,
        out_specs=[row(D_MODEL), row(XS_W),
                   pl.BlockSpec((ROW_TILE // LANES, N_EXPERTS, LANES),
                                lambda b, i: (b * tiles + i, 0, 0))],
        scratch_shapes=[pltpu.VMEM((seq + 2 * CONV_HALO, CONV_CH), F32),
                        pltpu.VMEM((SUBLANES - 1, SHIFT_ROWS, CONV_CH), F32)],
        compiler_params=_params(2, VMEM_LIMIT),
        name="back",
    )(x2d, attn, glu3, cw, cb, lg, lb, woa, woc, mod3, mod3, mod3, fg, wrh, wrl)


def _route_kernel(caps, *refs):
    n_g = len(caps)
    affs = [r[...] for r in refs[:n_g]]
    pos_refs = refs[n_g:2 * n_g]
    base_refs = refs[2 * n_g:]
    ones = jnp.ones((LANES, LANES), BF16)
    l_r = lax.broadcasted_iota(I32, (LANES, LANES), 0)
    l_c = lax.broadcasted_iota(I32, (LANES, LANES), 1)
    tri = jnp.where(l_r <= l_c, 1.0, 0.0).astype(BF16)

    def lane_dot(mask3, w):
        r, e, _ = mask3.shape
        mb = jnp.where(mask3, 1.0, 0.0).astype(BF16).reshape(r * e, LANES)
        return _bdot(mb, w).reshape(r, e, LANES)

    def expert_count(mask3):
        return jnp.sum(lane_dot(mask3, ones), axis=0, keepdims=True)

    def excl_cumsum(mask3):
        rows = mask3.shape[0]
        tot = lane_dot(mask3, ones)
        run = tot
        s = 1
        while s < rows:
            run = run + jnp.concatenate([jnp.zeros((s,) + run.shape[1:], F32), run[:rows - s]], axis=0)
            s *= 2
        before = run - tot
        return before + lane_dot(mask3, tri) - jnp.where(mask3, 1.0, 0.0), before

    def step(t, thrs):
        bit = lax.shift_left(jnp.int32(1), 30 - t)
        out = []
        for aff, thr, cap in zip(affs, thrs, caps):
            cand = thr | bit
            cnt = expert_count(aff >= pltpu.bitcast(cand, F32))
            out.append(jnp.where(cnt >= cap, cand, thr))
        return tuple(out)

    init = tuple(jnp.zeros((1,) + a.shape[1:], I32) for a in affs)
    thrs = lax.fori_loop(0, 31, step, init)
    for aff, thr_bits, cap, pos_ref, base_ref in zip(affs, thrs, caps, pos_refs, base_refs):
        thr = pltpu.bitcast(thr_bits, F32)
        gt = aff > thr
        eq = aff == thr
        need = cap - expert_count(gt)
        tie_rank, _ = excl_cumsum(eq)
        sel = gt | (eq & (tie_rank < need))
        pos, before = excl_cumsum(sel)
        pos_ref[...] = jnp.where(sel, pos, -1.0).astype(I32)
        base_ref[...] = before


def _route(affs, caps):
    shapes = [a.shape for a in affs]
    return pl.pallas_call(
        functools.partial(_route_kernel, tuple(caps)),
        out_shape=[jax.ShapeDtypeStruct(s, I32) for s in shapes] +
                  [jax.ShapeDtypeStruct(s, F32) for s in shapes],
        compiler_params=pltpu.CompilerParams(vmem_limit_bytes=VMEM_LIMIT),
        name="route",
    )(*affs)


def _window(off_ref, e, i, n_tiles, w, cap):
    del n_tiles
    first = off_ref[i * N_EXPERTS + e] + w * SLOT_WIN
    return first, pl.multiple_of(jnp.minimum(first, cap - SLOT_WIN), BF16_ROWS)


def _one_hot_rows(pos_ref, wins):
    r = lax.broadcasted_iota(I32, (SLOT_WIN, LANES), 0)
    blocks = []
    for e, (first, off) in enumerate(wins):
        halves = []
        for h in range(pos_ref.shape[0]):
            pe = pos_ref[h, e:e + 1, :]
            rel = jnp.where(pe >= first, pe - off, -1)
            halves.append(jnp.where(rel == r, 1.0, 0.0).astype(BF16))
        blocks.append(jnp.concatenate(halves, axis=1))
    return jnp.concatenate(blocks, axis=0)


def _dispatch_kernel(cap, n_tiles, off_ref, nwin_ref, h2_ref, pos_ref, xs_hbm, xs_ref, sem):
    i = pl.program_id(0)

    @pl.when(i == 0)
    def _():
        def zero(e, carry):
            xs_ref[e] = jnp.zeros(xs_ref.shape[1:], BF16)
            return carry
        lax.fori_loop(0, N_EXPERTS, zero, 0)

    def window(w, carry):
        wins = [_window(off_ref, e, i, n_tiles, w, cap) for e in range(N_EXPERTS)]
        p = _one_hot_rows(pos_ref, wins)
        h2 = h2_ref[...].reshape(ROW_TILE, XS_W)
        slab = _bdot(p, h2)
        for e, (_, off) in enumerate(wins):
            cur = xs_ref[e, pl.ds(off, SLOT_WIN), :].astype(F32)
            xs_ref[e, pl.ds(off, SLOT_WIN), :] = (
                cur + slab[e * SLOT_WIN:(e + 1) * SLOT_WIN]).astype(BF16)
        return carry

    lax.fori_loop(0, nwin_ref[i], window, 0)

    @pl.when(i == n_tiles - 1)
    def _():
        cp = pltpu.make_async_copy(xs_ref, xs_hbm, sem)
        cp.start()
        cp.wait()


def _tile_spec(stripes, width):
    return pl.BlockSpec((stripes, ROW_TILE // stripes, width), lambda i, *_: (0, i, 0))


def _dispatch(h2aug, pos, offs, nwin, cap, stripes):
    n = h2aug.shape[0]
    n_tiles = n // ROW_TILE
    return pl.pallas_call(
        functools.partial(_dispatch_kernel, cap, n_tiles),
        out_shape=jax.ShapeDtypeStruct((N_EXPERTS, cap, XS_W), BF16),
        grid_spec=pltpu.PrefetchScalarGridSpec(
            num_scalar_prefetch=2,
            grid=(n_tiles,),
            in_specs=[_tile_spec(stripes, XS_W),
                      pl.BlockSpec((ROW_TILE // LANES, N_EXPERTS, LANES), lambda i, *_: (i, 0, 0))],
            out_specs=pl.BlockSpec(memory_space=pl.ANY),
            scratch_shapes=[pltpu.VMEM((N_EXPERTS, cap, XS_W), BF16),
                            pltpu.SemaphoreType.DMA]),
        compiler_params=_params(1, VMEM_LIMIT),
        name="dispatch",
    )(offs, nwin, h2aug.reshape(stripes, n // stripes, XS_W), pos)


FF_CHUNK = 256
EXP_ROWS = 512


def _experts_kernel(n_groups, n_chunks, *refs):
    xs_refs = refs[:n_groups]
    wg_ref, wu_ref, wd_ref = refs[n_groups:n_groups + 3]
    ys_refs = refs[n_groups + 3:2 * n_groups + 3]
    acc_ref = refs[2 * n_groups + 3]
    e = pl.program_id(0)
    j = pl.program_id(1)
    cap = xs_refs[0].shape[1]

    @pl.when(j == 0)
    def _():
        for g in range(n_groups):
            acc_ref[g] = jnp.zeros(acc_ref.shape[1:], F32)

    wg = wg_ref[0].astype(BF16)
    wu = wu_ref[0].astype(BF16)
    wd = wd_ref[0].astype(BF16)
    for g, xs_ref in enumerate(xs_refs):
        for r in range(cap // EXP_ROWS):
            rows = slice(r * EXP_ROWS, (r + 1) * EXP_ROWS)
            x = xs_ref[0, rows, :D_MODEL]
            a = _bdot(x, wg)
            u = _bdot(x, wu)
            hm = (a * jax.nn.sigmoid(a) * u).astype(BF16)
            acc_ref[g, rows, :] += _bdot(hm, wd)

    @pl.when(j == n_chunks - 1)
    def _():
        for g, (xs_ref, ys_ref) in enumerate(zip(xs_refs, ys_refs)):
            aug = xs_ref[0, :, D_MODEL:].astype(F32)
            lane = lax.broadcasted_iota(I32, aug.shape, 1)
            mine = (lane == e) | (lane == e + N_EXPERTS) | (lane == e + 2 * N_EXPERTS)
            gate = jnp.sum(jnp.where(mine, aug, 0.0), axis=-1, keepdims=True)
            ys_ref[0] = (acc_ref[g] * gate).astype(BF16)


def _experts(xs_list, w_gate, w_up, w_down):
    n_groups = len(xs_list)
    cap = xs_list[0].shape[1]
    ff = w_gate.shape[2]
    n_chunks = ff // FF_CHUNK
    xs_spec = pl.BlockSpec((1, cap, XS_W), lambda e, j: (e, 0, 0))
    up_spec = pl.BlockSpec((1, D_MODEL, FF_CHUNK), lambda e, j: (e, 0, j))
    down_spec = pl.BlockSpec((1, FF_CHUNK, D_MODEL), lambda e, j: (e, j, 0))
    ys_spec = pl.BlockSpec((1, cap, D_MODEL), lambda e, j: (e, 0, 0))
    return pl.pallas_call(
        functools.partial(_experts_kernel, n_groups, n_chunks),
        out_shape=[jax.ShapeDtypeStruct((N_EXPERTS, cap, D_MODEL), BF16)] * n_groups,
        grid=(N_EXPERTS, n_chunks),
        in_specs=[xs_spec] * n_groups + [up_spec, up_spec, down_spec],
        out_specs=[ys_spec] * n_groups,
        scratch_shapes=[pltpu.VMEM((n_groups, cap, D_MODEL), F32)],
        compiler_params=_params(2, VMEM_LIMIT),
        name="experts",
    )(*xs_list, w_gate, w_up, w_down)


def _combine_kernel(cap, n_tiles, off_ref, nwin_ref, ys_hbm, pos_ref, x1_ref, g2_ref, fg_ref,
                    y_ref, buf_ref, sems):
    i = pl.program_id(0)
    slot = i % 2

    def copies(tile, w, s):
        out = []
        for e in range(N_EXPERTS):
            _, off = _window(off_ref, e, tile, n_tiles, w, cap)
            out.append(pltpu.make_async_copy(
                ys_hbm.at[e, pl.ds(off, SLOT_WIN), :],
                buf_ref.at[s, pl.ds(e * SLOT_WIN, SLOT_WIN), :],
                sems.at[s]))
        return out

    @pl.when(i == 0)
    def _():
        for cp in copies(0, 0, 0):
            cp.start()

    @pl.when(i + 1 < n_tiles)
    def _():
        for cp in copies(i + 1, 0, 1 - slot):
            cp.start()

    tn = (((0,), (0,)), ((), ()))

    def window(w, acc):
        @pl.when(w > 0)
        def _():
            for cp in copies(i, w, slot):
                cp.start()

        for cp in copies(i, w, slot):
            cp.wait()
        wins = [_window(off_ref, e, i, n_tiles, w, cap) for e in range(N_EXPERTS)]
        p = _one_hot_rows(pos_ref, wins)
        return acc + lax.dot_general(p, buf_ref[slot], tn, preferred_element_type=F32)

    moe = lax.fori_loop(0, nwin_ref[i], window, jnp.zeros((ROW_TILE, D_MODEL), F32))
    out = x1_ref[...] + g2_ref[...] * moe.reshape(x1_ref.shape)
    y_ref[...] = _rms(out, fg_ref[...])


def _combine(ys, pos, offs, nwin, x1, mod3, mod_row, stripes, final_g, cap):
    n = x1.shape[0]
    n_tiles = n // ROW_TILE
    g2 = pl.BlockSpec((stripes, 1, D_MODEL), lambda i, *_: (0 if mod_row is None else mod_row, 0, 5))
    y = pl.pallas_call(
        functools.partial(_combine_kernel, cap, n_tiles),
        out_shape=jax.ShapeDtypeStruct((stripes, n // stripes, D_MODEL), F32),
        grid_spec=pltpu.PrefetchScalarGridSpec(
            num_scalar_prefetch=2,
            grid=(n_tiles,),
            in_specs=[pl.BlockSpec(memory_space=pl.ANY),
                      pl.BlockSpec((ROW_TILE // LANES, N_EXPERTS, LANES), lambda i, *_: (i, 0, 0)),
                      _tile_spec(stripes, D_MODEL),
                      g2,
                      pl.BlockSpec((1, D_MODEL), lambda i, *_: (0, 0))],
            out_specs=_tile_spec(stripes, D_MODEL),
            scratch_shapes=[pltpu.VMEM((2, N_EXPERTS * SLOT_WIN, D_MODEL), BF16),
                            pltpu.SemaphoreType.DMA((2,))]),
        compiler_params=_params(1, VMEM_LIMIT),
        name="combine",
    )(offs, nwin, ys, pos, x1.reshape(stripes, n // stripes, D_MODEL), mod3, final_g)
    return y.reshape(n, D_MODEL)


def _head_cols(w3):
    k, _, d = w3.shape
    return jnp.pad(w3, ((0, 0), (0, 0), (0, HEAD_PAD - d))).reshape(k, QK_WIDTH)


def _value_cols(v3):
    k = v3.shape[0]
    pair = v3.reshape(k, MLA_HEADS // 2, 2, V_DIM)
    zeros = jnp.zeros((k, MLA_HEADS // 2, V_DIM), v3.dtype)
    even = jnp.concatenate([pair[:, :, 0], zeros], axis=-1)
    odd = jnp.concatenate([zeros, pair[:, :, 1]], axis=-1)
    return jnp.stack([even, odd], axis=2).reshape(k, QK_WIDTH)


def _rope_tables(seq):
    half = QK_ROPE // 2
    pos = jnp.arange(seq)
    inv_freq = 1.0 / (ROPE_BASE ** (jnp.arange(0, half, 2, dtype=F32) / half))

    def cs(p):
        ang = p.astype(F32)[:, None] * inv_freq[None, :]
        ang = jnp.concatenate([ang, ang], axis=-1)
        return jnp.cos(ang), jnp.sin(ang)

    cr, sr = cs(pos // GRID_W)
    cc, sc = cs(pos % GRID_W)
    ones = jnp.ones((seq, QK_NOPE), F32)
    zpad = jnp.zeros((seq, HEAD_PAD - QK_NOPE - QK_ROPE), F32)
    cq = jnp.concatenate([ones, cr, cc, zpad], axis=-1)
    sq = jnp.concatenate([0.0 * ones, sr, sc, zpad], axis=-1)
    return cq, sq


def _prep_weights(w_in, q_norm_g, w_q_up, kv_norm_g, w_kv_up, attn_norm_g):
    wqc = w_in[:, :Q_RANK + KV_RANK].astype(BF16)
    wkr = jnp.pad(w_in[:, Q_RANK + KV_RANK:Q_RANK + KV_RANK + QK_ROPE],
                  ((0, 0), (0, LANES - QK_ROPE))).astype(BF16)
    wglu = w_in[:, Q_RANK + KV_RANK + QK_ROPE:].astype(BF16)
    wq = _head_cols(w_q_up.astype(BF16).reshape(Q_RANK, MLA_HEADS, QK_NOPE + QK_ROPE))
    kv3 = w_kv_up.astype(BF16).reshape(KV_RANK, MLA_HEADS, QK_NOPE + V_DIM)
    wk = _head_cols(kv3[:, :, :QK_NOPE])
    wv = _value_cols(kv3[:, :, QK_NOPE:])
    pk = np.zeros((LANES, QK_WIDTH), np.float32)
    for hd in range(MLA_HEADS):
        for j in range(QK_ROPE):
            pk[j, hd * HEAD_PAD + QK_NOPE + j] = 1.0
    pk = jnp.asarray(pk, BF16)
    return (attn_norm_g[None, :], wqc, wkr, wglu, q_norm_g[None, :], wq,
            kv_norm_g[None, :], wk, wv, pk)


def _stripe_order(aff3, stripes, seq):
    per = ROW_TILE // stripes
    q = LANES // per
    assert per * stripes == ROW_TILE and per * q == LANES and stripes % q == 0 and seq % LANES == 0
    a = aff3.reshape(stripes // q, q, seq // LANES, N_EXPERTS, q, per)
    return a.transpose(2, 4, 0, 3, 1, 5).reshape(aff3.shape)


def _route_plan(base, cap):
    rows_per_tile = ROW_TILE // LANES
    start = base[::rows_per_tile, :, 0].astype(I32)
    end = jnp.concatenate([start[1:], jnp.full((1, N_EXPERTS), cap, I32)], axis=0)
    off = (start // BF16_ROWS) * BF16_ROWS
    span = end - off
    nwin = jnp.maximum(jnp.max((span + SLOT_WIN - 1) // SLOT_WIN, axis=1), 1).astype(I32)
    return off.reshape(-1), nwin


def kernel(x_prompt, x_sample, cache_ckv, cache_krope, c, c_ctx, w_mod, b_mod, attn_norm_g,
           w_in, q_norm_g, w_q_up, kv_norm_g, w_kv_up, conv_w, conv_b, conv_ln_g, conv_ln_b,
           w_out, ffn_norm_g, w_router, w_gate, w_up, w_down, final_norm_g):
    depth = w_mod.shape[0]
    assert depth == 1
    bp, sp, _ = x_prompt.shape
    bs, ss, _ = x_sample.shape
    past = cache_ckv.shape[2]
    ctx_row = bs

    mod_rows = 16
    cc = jnp.concatenate([c, c_ctx[None, :], jnp.zeros((mod_rows - bs - 1, D_MODEL), F32)], axis=0)
    mod = _adaln(cc, w_mod[0], b_mod[0][None, :])
    mod3 = mod.reshape(mod_rows, 1, N_MOD * D_MODEL)

    fw = _prep_weights(w_in[0], q_norm_g[0], w_q_up[0], kv_norm_g[0], w_kv_up[0], attn_norm_g[0])
    wk, wv, pk = fw[7], fw[8], fw[9]
    wo = w_out[0].astype(BF16)
    wr = jnp.pad(jnp.tile(w_router[0], (1, 3)), ((0, 0), (0, LANES - 3 * N_EXPERTS)))
    wrh = wr.astype(BF16)
    wrl = (wr - wrh.astype(F32)).astype(BF16)
    bw = (conv_w[0], conv_b[0][None, :], conv_ln_g[0][None, :], conv_ln_b[0][None, :],
          wo[:MLA_HEADS * V_DIM], wo[MLA_HEADS * V_DIM:], ffn_norm_g[0][None, :], wrh, wrl)

    xp2 = x_prompt.reshape(bp * sp, D_MODEL)
    xs2 = x_sample.reshape(bs * ss, D_MODEL)

    qp, kp, vp, glup, ckvp, krp = _front(xp2, mod3, ctx_row, sp, fw, None)
    attn_p = _attention(qp, [(kp, vp, sp)], bp, sp)
    x1p, h2p, afftp = _back(xp2, attn_p, glup.reshape(bp, sp, CONV_CH), mod3, ctx_row, bw, bp, sp)

    qs, ks, vs, glus = _front(xs2, mod3, None, ss, fw, _rope_tables(ss))
    kc, vc = _ctx_kv(cache_ckv[:, 0].reshape(bs * past, KV_RANK),
                     cache_krope[:, 0].reshape(bs * past, QK_ROPE), wk, wv, pk)
    attn_s = _attention(qs, [(kc, vc, past), (ks, vs, ss)], bs, ss)
    x1s, h2s, affts = _back(xs2, attn_s, glus.reshape(bs, ss, CONV_CH), mod3, None, bw, bs, ss)

    stripes = (1, bs)
    caps = [EC_FACTOR * h2.shape[0] // N_EXPERTS for h2 in (h2p, h2s)]
    routed = _route([afftp, _stripe_order(affts, bs, ss)], caps)
    plans = []
    xs_list = []
    for g, (h2, cap) in enumerate(zip((h2p, h2s), caps)):
        pos, base = routed[g], routed[len(caps) + g]
        offs, nwin = _route_plan(base, cap)
        plans.append((pos, offs, nwin, cap))
        xs_list.append(_dispatch(h2, pos, offs, nwin, cap, stripes[g]))
    ys_list = _experts(xs_list, w_gate[0], w_up[0], w_down[0])

    fg = final_norm_g[None, :]
    pos, offs, nwin, cap = plans[0]
    y_prompt = _combine(ys_list[0], pos, offs, nwin, x1p, mod3, ctx_row, stripes[0], fg, cap)
    pos, offs, nwin, cap = plans[1]
    y_sample = _combine(ys_list[1], pos, offs, nwin, x1s, mod3, None, stripes[1], fg, cap)

    return (y_prompt.reshape(bp, sp, D_MODEL), y_sample.reshape(bs, ss, D_MODEL),
            ckvp.reshape(bp, depth, sp, KV_RANK), krp.reshape(bp, depth, sp, QK_ROPE))
```

```python
import functools

import jax
import jax.numpy as jnp
import numpy as np
from jax import lax
from jax.experimental import pallas as pl
from jax.experimental.pallas import tpu as pltpu

F32 = jnp.float32
BF16 = jnp.bfloat16
I32 = jnp.int32

LANES = 128
BF16_ROWS = 16
VMEM_LIMIT = 56 * 1024 * 1024

D_MODEL = 1024
GRID_W = 64
MLA_HEADS = 8
QK_NOPE = 64
QK_ROPE = 32
V_DIM = 64
Q_RANK = 768
KV_RANK = 256
CONV_CH = 512
CONV_WIDTH = 31
N_EXPERTS = 16
EXPERT_FF = 1024
EC_FACTOR = 2
ROPE_BASE = 10000.0
EPS = 1e-6
N_MOD = 6

HEAD_PAD = LANES
QK_WIDTH = MLA_HEADS * HEAD_PAD
ROW_TILE = 256
FRONT_TILE = 512
SLOT_WIN = 64
AUG = LANES
XS_W = D_MODEL + AUG
LOG2_E = 1.4426950408889634


def _params(n_axes, vmem=None):
    return pltpu.CompilerParams(
        dimension_semantics=("arbitrary",) * n_axes,
        vmem_limit_bytes=vmem)


def _rms(x, g):
    ms = jnp.mean(x * x, axis=-1, keepdims=True)
    return x * lax.rsqrt(ms + EPS) * g


def _bdot(a, b):
    return jnp.dot(a, b, preferred_element_type=F32)


def _adaln_kernel(c_ref, w_ref, b_ref, o_ref):
    c = c_ref[...]
    s = c * jax.nn.sigmoid(c)
    w = w_ref[...]
    sh = s.astype(BF16)
    sl = (s - sh.astype(F32)).astype(BF16)
    wh = w.astype(BF16)
    wl = (w - wh.astype(F32)).astype(BF16)
    o_ref[...] = _bdot(sh, wh) + (_bdot(sl, wh) + _bdot(sh, wl)) + b_ref[...]


def _adaln(cc, w_mod, b_mod):
    rows = cc.shape[0]
    n_out = w_mod.shape[1]
    blk = D_MODEL
    return pl.pallas_call(
        _adaln_kernel,
        out_shape=jax.ShapeDtypeStruct((rows, n_out), F32),
        grid=(n_out // blk,),
        in_specs=[pl.BlockSpec((rows, D_MODEL), lambda j: (0, 0)),
                  pl.BlockSpec((D_MODEL, blk), lambda j: (0, j)),
                  pl.BlockSpec((1, blk), lambda j: (0, j))],
        out_specs=pl.BlockSpec((rows, blk), lambda j: (0, j)),
        compiler_params=_params(1),
        name="adaln",
    )(cc, w_mod, b_mod)


def _rot_half(x, lane, base):
    rel = lane - base
    first = ((rel >= 0) & (rel < 8)) | ((rel >= 16) & (rel < 24))
    return jnp.where(first, -pltpu.roll(x, LANES - 8, axis=1), pltpu.roll(x, 8, axis=1))


def _front_kernel(rope, x_ref, sh_ref, sc_ref, g_ref, wqc_ref, wkr_ref, wglu_ref,
                  qg_ref, wq_ref, kvg_ref, wk_ref, wv_ref, pk_ref, *rest):
    if rope:
        cq_ref, sq_ref, q_ref, k_ref, v_ref, glu_ref = rest
    else:
        q_ref, k_ref, v_ref, glu_ref, ckv_ref, kr_ref = rest
    x = x_ref[...]
    h = _rms(x, g_ref[...]) * (1.0 + sc_ref[0]) + sh_ref[0]
    hb = h.astype(BF16)
    qc = _bdot(hb, wqc_ref[...])
    kr = _bdot(hb, wkr_ref[...])
    glu = _bdot(hb, wglu_ref[...])
    glu_ref[...] = glu[:, :CONV_CH] * jax.nn.sigmoid(glu[:, CONV_CH:])

    qn = _rms(qc[:, :Q_RANK], qg_ref[...]).astype(BF16)
    q = _bdot(qn, wq_ref[...])
    ckv = _rms(qc[:, Q_RANK:], kvg_ref[...])
    ckv_b = ckv.astype(BF16)
    kk = _bdot(ckv_b, wk_ref[...])
    v_ref[...] = _bdot(ckv_b, wv_ref[...]).astype(BF16)

    scale = (QK_NOPE + QK_ROPE) ** -0.5 * LOG2_E
    lane = lax.broadcasted_iota(I32, (x.shape[0], LANES), 1)
    if rope:
        cq = cq_ref[...]
        sq = sq_ref[...]
        for hd in range(MLA_HEADS):
            blk = q[:, hd * HEAD_PAD:(hd + 1) * HEAD_PAD]
            rot = blk * cq + _rot_half(blk, lane, QK_NOPE) * sq
            q_ref[:, hd * HEAD_PAD:(hd + 1) * HEAD_PAD] = (rot * scale).astype(BF16)
        krs = pltpu.roll(kr, QK_NOPE, axis=1)
        kr = pltpu.roll(krs * cq + _rot_half(krs, lane, QK_NOPE) * sq, LANES - QK_NOPE, axis=1)
        kr = jnp.where(lane < QK_ROPE, kr, 0.0)
    else:
        q_ref[...] = (q * scale).astype(BF16)
        ckv_ref[...] = ckv
        kr_ref[...] = kr[:, :QK_ROPE]
    k_ref[...] = (kk + _bdot(kr.astype(BF16), pk_ref[...])).astype(BF16)


def _front(x2d, mod3, mod_row, rows_per_mod, wts, rope_tabs):
    n = x2d.shape[0]
    rope = rope_tabs is not None
    tile = FRONT_TILE
    tiles_per_mod = rows_per_mod // tile
    full = lambda a: pl.BlockSpec(a.shape, lambda i: (0,) * a.ndim)

    def mod_spec(col):
        if mod_row is None:
            assert rows_per_mod % tile == 0
            return pl.BlockSpec((1, 1, D_MODEL), lambda i: (i // tiles_per_mod, 0, col))
        return pl.BlockSpec((1, 1, D_MODEL), lambda i: (mod_row, 0, col))

    row = lambda w: pl.BlockSpec((tile, w), lambda i: (i, 0))
    in_specs = [row(D_MODEL), mod_spec(0), mod_spec(1)] + [full(w) for w in wts]
    args = [x2d, mod3, mod3] + list(wts)
    out_shape = [jax.ShapeDtypeStruct((n, QK_WIDTH), BF16)] * 3 + \
                [jax.ShapeDtypeStruct((n, CONV_CH), F32)]
    out_specs = [row(QK_WIDTH)] * 3 + [row(CONV_CH)]
    if rope:
        seq_tiles = rope_tabs[0].shape[0] // tile
        tab = pl.BlockSpec((tile, LANES), lambda i: (i % seq_tiles, 0))
        in_specs += [tab, tab]
        args += list(rope_tabs)
    else:
        out_shape += [jax.ShapeDtypeStruct((n, KV_RANK), F32),
                      jax.ShapeDtypeStruct((n, QK_ROPE), F32)]
        out_specs += [row(KV_RANK), row(QK_ROPE)]
    return pl.pallas_call(
        functools.partial(_front_kernel, rope),
        out_shape=out_shape,
        grid=(n // tile,),
        in_specs=in_specs,
        out_specs=out_specs,
        compiler_params=_params(1, VMEM_LIMIT),
        name="front_rope" if rope else "front",
    )(*args)


def _ctx_kernel(ckv_ref, kr_ref, wk_ref, wv_ref, pk_ref, k_ref, v_ref):
    cb = ckv_ref[...].astype(BF16)
    kr = kr_ref[...].astype(BF16)
    k_ref[...] = (_bdot(cb, wk_ref[...]) + _bdot(kr, pk_ref[...])).astype(BF16)
    v_ref[...] = _bdot(cb, wv_ref[...]).astype(BF16)


def _ctx_kv(ckv2d, kr2d, wk, wv, pk):
    n = ckv2d.shape[0]
    full = lambda a: pl.BlockSpec(a.shape, lambda i: (0,) * a.ndim)
    row = lambda w: pl.BlockSpec((ROW_TILE, w), lambda i: (i, 0))
    pk32 = pk[:QK_ROPE]
    return pl.pallas_call(
        _ctx_kernel,
        out_shape=[jax.ShapeDtypeStruct((n, QK_WIDTH), BF16)] * 2,
        grid=(n // ROW_TILE,),
        in_specs=[row(KV_RANK), row(QK_ROPE), full(wk), full(wv), full(pk32)],
        out_specs=[row(QK_WIDTH)] * 2,
        compiler_params=_params(1),
        name="ctx_kv",
    )(ckv2d, kr2d, wk, wv, pk32)


def _attn_kernel(n_kv, q_ref, *rest):
    kv_refs = rest[:2 * n_kv]
    o_ref = rest[2 * n_kv]
    nt = (((1,), (1,)), ((), ()))

    def scores(hd):
        sl = slice(hd * HEAD_PAD, (hd + 1) * HEAD_PAD)
        qh = q_ref[:, sl]
        return [lax.dot_general(qh, kv_refs[2 * j][:, sl], nt, preferred_element_type=F32)
                for j in range(n_kv)]

    outs = []
    nxt = scores(0)
    for hd in range(MLA_HEADS):
        sl = slice(hd * HEAD_PAD, (hd + 1) * HEAD_PAD)
        ss = nxt
        if hd + 1 < MLA_HEADS:
            nxt = scores(hd + 1)
        m = ss[0].max(axis=-1, keepdims=True)
        for s in ss[1:]:
            m = jnp.maximum(m, s.max(axis=-1, keepdims=True))
        ps = [jnp.exp2(s - m) for s in ss]
        l = ps[0].sum(axis=-1, keepdims=True)
        for p in ps[1:]:
            l = l + p.sum(axis=-1, keepdims=True)
        o = _bdot(ps[0].astype(BF16), kv_refs[1][:, sl])
        for j in range(1, n_kv):
            o = o + _bdot(ps[j].astype(BF16), kv_refs[2 * j + 1][:, sl])
        outs.append(o / l)
    for j in range(MLA_HEADS // 2):
        o_ref[:, j * LANES:(j + 1) * LANES] = (outs[2 * j] + outs[2 * j + 1]).astype(BF16)


def _attention(q, kvs, batch, seq):
    n = q.shape[0]
    q_tiles = seq // ROW_TILE
    in_specs = [pl.BlockSpec((ROW_TILE, QK_WIDTH), lambda b, i: (b * q_tiles + i, 0))]
    args = [q]
    for k, v, t in kvs:
        spec = pl.BlockSpec((t, QK_WIDTH), lambda b, i: (b, 0))
        in_specs += [spec, spec]
        args += [k, v]
    return pl.pallas_call(
        functools.partial(_attn_kernel, len(kvs)),
        out_shape=jax.ShapeDtypeStruct((n, MLA_HEADS * V_DIM), BF16),
        grid=(batch, q_tiles),
        in_specs=in_specs,
        out_specs=pl.BlockSpec((ROW_TILE, MLA_HEADS * V_DIM), lambda b, i: (b * q_tiles + i, 0)),
        compiler_params=_params(2, VMEM_LIMIT),
        name="attention",
    )(*args)


CONV_PAD = (CONV_WIDTH - 1) // 2
CONV_HALO = 16


SUBLANES = 8
SHIFT_ROWS = ROW_TILE + 2 * CONV_HALO - SUBLANES


def _back_kernel(seq, x_ref, attn_ref, glu_ref, cw_ref, cb_ref, lg_ref, lb_ref,
                 woa_ref, woc_ref, g1_ref, sh_ref, sc_ref, fg_ref, wrh_ref, wrl_ref,
                 x1_ref, h2_ref, afft_ref, vpad_ref, shift_ref):
    i = pl.program_id(1)

    @pl.when(i == 0)
    def _():
        zeros = jnp.zeros((CONV_HALO, CONV_CH), F32)
        vpad_ref[0:CONV_HALO, :] = zeros
        vpad_ref[CONV_HALO + seq:, :] = zeros
        vpad_ref[CONV_HALO:CONV_HALO + seq, :] = glu_ref[0]

    r0 = pl.multiple_of(i * ROW_TILE, ROW_TILE)
    win = vpad_ref[pl.ds(r0, ROW_TILE + 2 * CONV_HALO), :]
    for ph in range(1, SUBLANES):
        shift_ref[ph - 1] = win[ph:ph + SHIFT_ROWS, :]
    cw = cw_ref[...]
    acc = jnp.broadcast_to(cb_ref[...], (ROW_TILE, CONV_CH))
    for t in range(CONV_WIDTH):
        blk, ph = divmod(t + CONV_HALO - CONV_PAD, SUBLANES)
        if ph == 0:
            tap = vpad_ref[pl.ds(r0 + blk * SUBLANES, ROW_TILE), :]
        else:
            tap = shift_ref[ph - 1, blk * SUBLANES:blk * SUBLANES + ROW_TILE, :]
        acc = acc + tap * cw[t:t + 1, :]
    mu = jnp.mean(acc, axis=-1, keepdims=True)
    cen = acc - mu
    var = jnp.mean(cen * cen, axis=-1, keepdims=True)
    ln = cen * lax.rsqrt(var + EPS) * lg_ref[...] + lb_ref[...]
    cv = (ln * jax.nn.sigmoid(ln)).astype(BF16)

    mix = _bdot(attn_ref[...], woa_ref[...]) + _bdot(cv, woc_ref[...])
    x1 = x_ref[...] + g1_ref[0] * mix
    x1_ref[...] = x1
    h2 = _rms(x1, fg_ref[...]) * (1.0 + sc_ref[0]) + sh_ref[0]
    h2b = h2.astype(BF16)
    h2_ref[:, :D_MODEL] = h2b

    h2l = (h2 - h2b.astype(F32)).astype(BF16)
    wrh = wrh_ref[...]
    logits = _bdot(h2b, wrh) + (_bdot(h2l, wrh) + _bdot(h2b, wrl_ref[...]))
    lane = lax.broadcasted_iota(I32, logits.shape, 1)
    real = lane < N_EXPERTS
    m = jnp.max(jnp.where(real, logits, -jnp.inf), axis=-1, keepdims=True)
    e = jnp.exp(logits - m)
    aff = e / jnp.sum(jnp.where(real, e, 0.0), axis=-1, keepdims=True)
    aff_t = aff.T
    for blk in range(ROW_TILE // LANES):
        afft_ref[blk] = aff_t[:N_EXPERTS, blk * LANES:(blk + 1) * LANES]
    hi = aff.astype(BF16)
    r1 = aff - hi.astype(F32)
    mid = r1.astype(BF16)
    lo = (r1 - mid.astype(F32)).astype(BF16)
    zero = jnp.zeros_like(hi)
    h2_ref[:, D_MODEL:] = jnp.where(
        real, hi, jnp.where(lane < 2 * N_EXPERTS, mid, jnp.where(lane < 3 * N_EXPERTS, lo, zero)))


def _back(x2d, attn, glu3, mod3, mod_row, wts, batch, seq):
    n = x2d.shape[0]
    tiles = seq // ROW_TILE
    full = lambda a: pl.BlockSpec(a.shape, lambda b, i: (0,) * a.ndim)

    def mod_spec(col):
        if mod_row is None:
            return pl.BlockSpec((1, 1, D_MODEL), lambda b, i: (b, 0, col))
        return pl.BlockSpec((1, 1, D_MODEL), lambda b, i: (mod_row, 0, col))

    row = lambda w: pl.BlockSpec((ROW_TILE, w), lambda b, i: (b * tiles + i, 0))
    cw, cb, lg, lb, woa, woc, fg, wrh, wrl = wts
    in_specs = [row(D_MODEL), row(MLA_HEADS * V_DIM),
                pl.BlockSpec((1, seq, CONV_CH), lambda b, i: (b, 0, 0)),
                full(cw), full(cb), full(lg), full(lb), full(woa), full(woc),
                mod_spec(2), mod_spec(3), mod_spec(4), full(fg), full(wrh), full(wrl)]
    return pl.pallas_call(
        functools.partial(_back_kernel, seq),
        out_shape=[jax.ShapeDtypeStruct((n, D_MODEL), F32),
                   jax.ShapeDtypeStruct((n, XS_W), BF16),
                   jax.ShapeDtypeStruct((n // LANES, N_EXPERTS, LANES), F32)],
        grid=(batch, tiles),
        in_specs=in_specs,
        out_specs=[row(D_MODEL), row(XS_W),
                   pl.BlockSpec((ROW_TILE // LANES, N_EXPERTS, LANES),
                                lambda b, i: (b * tiles + i, 0, 0))],
        scratch_shapes=[pltpu.VMEM((seq + 2 * CONV_HALO, CONV_CH), F32),
                        pltpu.VMEM((SUBLANES - 1, SHIFT_ROWS, CONV_CH), F32)],
        compiler_params=_params(2, VMEM_LIMIT),
        name="back",
    )(x2d, attn, glu3, cw, cb, lg, lb, woa, woc, mod3, mod3, mod3, fg, wrh, wrl)


def _route_kernel(caps, *refs):
    n_g = len(caps)
    affs = [r[...] for r in refs[:n_g]]
    pos_refs = refs[n_g:2 * n_g]
    base_refs = refs[2 * n_g:]
    ones = jnp.ones((LANES, LANES), BF16)
    l_r = lax.broadcasted_iota(I32, (LANES, LANES), 0)
    l_c = lax.broadcasted_iota(I32, (LANES, LANES), 1)
    tri = jnp.where(l_r <= l_c, 1.0, 0.0).astype(BF16)

    def lane_dot(mask3, w):
        r, e, _ = mask3.shape
        mb = jnp.where(mask3, 1.0, 0.0).astype(BF16).reshape(r * e, LANES)
        return _bdot(mb, w).reshape(r, e, LANES)

    def expert_count(mask3):
        return jnp.sum(lane_dot(mask3, ones), axis=0, keepdims=True)

    def excl_cumsum(mask3):
        rows = mask3.shape[0]
        tot = lane_dot(mask3, ones)
        run = tot
        s = 1
        while s < rows:
            run = run + jnp.concatenate([jnp.zeros((s,) + run.shape[1:], F32), run[:rows - s]], axis=0)
            s *= 2
        before = run - tot
        return before + lane_dot(mask3, tri) - jnp.where(mask3, 1.0, 0.0), before

    def step(t, thrs):
        bit = lax.shift_left(jnp.int32(1), 30 - t)
        out = []
        for aff, thr, cap in zip(affs, thrs, caps):
            cand = thr | bit
            cnt = expert_count(aff >= pltpu.bitcast(cand, F32))
            out.append(jnp.where(cnt >= cap, cand, thr))
        return tuple(out)

    init = tuple(jnp.zeros((1,) + a.shape[1:], I32) for a in affs)
    thrs = lax.fori_loop(0, 31, step, init)
    for aff, thr_bits, cap, pos_ref, base_ref in zip(affs, thrs, caps, pos_refs, base_refs):
        thr = pltpu.bitcast(thr_bits, F32)
        gt = aff > thr
        eq = aff == thr
        need = cap - expert_count(gt)
        tie_rank, _ = excl_cumsum(eq)
        sel = gt | (eq & (tie_rank < need))
        pos, before = excl_cumsum(sel)
        pos_ref[...] = jnp.where(sel, pos, -1.0).astype(I32)
        base_ref[...] = before


def _route(affs, caps):
    shapes = [a.shape for a in affs]
    return pl.pallas_call(
        functools.partial(_route_kernel, tuple(caps)),
        out_shape=[jax.ShapeDtypeStruct(s, I32) for s in shapes] +
                  [jax.ShapeDtypeStruct(s, F32) for s in shapes],
        compiler_params=pltpu.CompilerParams(vmem_limit_bytes=VMEM_LIMIT),
        name="route",
    )(*affs)


def _window(off_ref, e, i, n_tiles, w, cap):
    del n_tiles
    first = off_ref[i * N_EXPERTS + e] + w * SLOT_WIN
    return first, pl.multiple_of(jnp.minimum(first, cap - SLOT_WIN), BF16_ROWS)


def _one_hot_rows(pos_ref, wins):
    r = lax.broadcasted_iota(I32, (SLOT_WIN, LANES), 0)
    blocks = []
    for e, (first, off) in enumerate(wins):
        halves = []
        for h in range(pos_ref.shape[0]):
            pe = pos_ref[h, e:e + 1, :]
            rel = jnp.where(pe >= first, pe - off, -1)
            halves.append(jnp.where(rel == r, 1.0, 0.0).astype(BF16))
        blocks.append(jnp.concatenate(halves, axis=1))
    return jnp.concatenate(blocks, axis=0)


def _dispatch_kernel(cap, n_tiles, off_ref, nwin_ref, h2_ref, pos_ref, xs_hbm, xs_ref, sem):
    i = pl.program_id(0)

    @pl.when(i == 0)
    def _():
        def zero(e, carry):
            xs_ref[e] = jnp.zeros(xs_ref.shape[1:], BF16)
            return carry
        lax.fori_loop(0, N_EXPERTS, zero, 0)

    def window(w, carry):
        wins = [_window(off_ref, e, i, n_tiles, w, cap) for e in range(N_EXPERTS)]
        p = _one_hot_rows(pos_ref, wins)
        h2 = h2_ref[...].reshape(ROW_TILE, XS_W)
        slab = _bdot(p, h2)
        for e, (_, off) in enumerate(wins):
            cur = xs_ref[e, pl.ds(off, SLOT_WIN), :].astype(F32)
            xs_ref[e, pl.ds(off, SLOT_WIN), :] = (
                cur + slab[e * SLOT_WIN:(e + 1) * SLOT_WIN]).astype(BF16)
        return carry

    lax.fori_loop(0, nwin_ref[i], window, 0)

    @pl.when(i == n_tiles - 1)
    def _():
        cp = pltpu.make_async_copy(xs_ref, xs_hbm, sem)
        cp.start()
        cp.wait()


def _tile_spec(stripes, width):
    return pl.BlockSpec((stripes, ROW_TILE // stripes, width), lambda i, *_: (0, i, 0))


def _dispatch(h2aug, pos, offs, nwin, cap, stripes):
    n = h2aug.shape[0]
    n_tiles = n // ROW_TILE
    return pl.pallas_call(
        functools.partial(_dispatch_kernel, cap, n_tiles),
        out_shape=jax.ShapeDtypeStruct((N_EXPERTS, cap, XS_W), BF16),
        grid_spec=pltpu.PrefetchScalarGridSpec(
            num_scalar_prefetch=2,
            grid=(n_tiles,),
            in_specs=[_tile_spec(stripes, XS_W),
                      pl.BlockSpec((ROW_TILE // LANES, N_EXPERTS, LANES), lambda i, *_: (i, 0, 0))],
            out_specs=pl.BlockSpec(memory_space=pl.ANY),
            scratch_shapes=[pltpu.VMEM((N_EXPERTS, cap, XS_W), BF16),
                            pltpu.SemaphoreType.DMA]),
        compiler_params=_params(1, VMEM_LIMIT),
        name="dispatch",
    )(offs, nwin, h2aug.reshape(stripes, n // stripes, XS_W), pos)


FF_CHUNK = 512
EXP_ROWS = 512


def _experts_kernel(n_groups, n_chunks, *refs):
    xs_refs = refs[:n_groups]
    wg_ref, wu_ref, wd_ref = refs[n_groups:n_groups + 3]
    ys_refs = refs[n_groups + 3:2 * n_groups + 3]
    acc_ref = refs[2 * n_groups + 3]
    e = pl.program_id(0)
    j = pl.program_id(1)
    cap = xs_refs[0].shape[1]

    @pl.when(j == 0)
    def _():
        for g in range(n_groups):
            acc_ref[g] = jnp.zeros(acc_ref.shape[1:], F32)

    wg = wg_ref[0].astype(BF16)
    wu = wu_ref[0].astype(BF16)
    wd = wd_ref[0].astype(BF16)
    for g, xs_ref in enumerate(xs_refs):
        for r in range(cap // EXP_ROWS):
            rows = slice(r * EXP_ROWS, (r + 1) * EXP_ROWS)
            x = xs_ref[0, rows, :D_MODEL]
            a = _bdot(x, wg)
            u = _bdot(x, wu)
            hm = (a * jax.nn.sigmoid(a) * u).astype(BF16)
            acc_ref[g, rows, :] += _bdot(hm, wd)

    @pl.when(j == n_chunks - 1)
    def _():
        for g, (xs_ref, ys_ref) in enumerate(zip(xs_refs, ys_refs)):
            aug = xs_ref[0, :, D_MODEL:].astype(F32)
            lane = lax.broadcasted_iota(I32, aug.shape, 1)
            mine = (lane == e) | (lane == e + N_EXPERTS) | (lane == e + 2 * N_EXPERTS)
            gate = jnp.sum(jnp.where(mine, aug, 0.0), axis=-1, keepdims=True)
            ys_ref[0] = (acc_ref[g] * gate).astype(BF16)


def _experts(xs_list, w_gate, w_up, w_down):
    n_groups = len(xs_list)
    cap = xs_list[0].shape[1]
    ff = w_gate.shape[2]
    n_chunks = ff // FF_CHUNK
    xs_spec = pl.BlockSpec((1, cap, XS_W), lambda e, j: (e, 0, 0))
    up_spec = pl.BlockSpec((1, D_MODEL, FF_CHUNK), lambda e, j: (e, 0, j))
    down_spec = pl.BlockSpec((1, FF_CHUNK, D_MODEL), lambda e, j: (e, j, 0))
    ys_spec = pl.BlockSpec((1, cap, D_MODEL), lambda e, j: (e, 0, 0))
    return pl.pallas_call(
        functools.partial(_experts_kernel, n_groups, n_chunks),
        out_shape=[jax.ShapeDtypeStruct((N_EXPERTS, cap, D_MODEL), BF16)] * n_groups,
        grid=(N_EXPERTS, n_chunks),
        in_specs=[xs_spec] * n_groups + [up_spec, up_spec, down_spec],
        out_specs=[ys_spec] * n_groups,
        scratch_shapes=[pltpu.VMEM((n_groups, cap, D_MODEL), F32)],
        compiler_params=_params(2, VMEM_LIMIT),
        name="experts",
    )(*xs_list, w_gate, w_up, w_down)


def _combine_kernel(cap, n_tiles, off_ref, nwin_ref, ys_hbm, pos_ref, x1_ref, g2_ref, fg_ref,
                    y_ref, buf_ref, sems):
    i = pl.program_id(0)
    slot = i % 2

    def copies(tile, w, s):
        out = []
        for e in range(N_EXPERTS):
            _, off = _window(off_ref, e, tile, n_tiles, w, cap)
            out.append(pltpu.make_async_copy(
                ys_hbm.at[e, pl.ds(off, SLOT_WIN), :],
                buf_ref.at[s, pl.ds(e * SLOT_WIN, SLOT_WIN), :],
                sems.at[s]))
        return out

    @pl.when(i == 0)
    def _():
        for cp in copies(0, 0, 0):
            cp.start()

    @pl.when(i + 1 < n_tiles)
    def _():
        for cp in copies(i + 1, 0, 1 - slot):
            cp.start()

    tn = (((0,), (0,)), ((), ()))

    def gathered(w):
        wins = [_window(off_ref, e, i, n_tiles, w, cap) for e in range(N_EXPERTS)]
        p = _one_hot_rows(pos_ref, wins)
        for cp in copies(i, w, slot):
            cp.wait()
        return lax.dot_general(p, buf_ref[slot], tn, preferred_element_type=F32)

    def extra_window(w, acc):
        for cp in copies(i, w, slot):
            cp.start()
        return acc + gathered(w)

    moe = lax.fori_loop(1, nwin_ref[i], extra_window, gathered(0))
    out = x1_ref[...] + g2_ref[...] * moe.reshape(x1_ref.shape)
    y_ref[...] = _rms(out, fg_ref[...])


def _combine(ys, pos, offs, nwin, x1, mod3, mod_row, stripes, final_g, cap):
    n = x1.shape[0]
    n_tiles = n // ROW_TILE
    g2 = pl.BlockSpec((stripes, 1, D_MODEL), lambda i, *_: (0 if mod_row is None else mod_row, 0, 5))
    y = pl.pallas_call(
        functools.partial(_combine_kernel, cap, n_tiles),
        out_shape=jax.ShapeDtypeStruct((stripes, n // stripes, D_MODEL), F32),
        grid_spec=pltpu.PrefetchScalarGridSpec(
            num_scalar_prefetch=2,
            grid=(n_tiles,),
            in_specs=[pl.BlockSpec(memory_space=pl.ANY),
                      pl.BlockSpec((ROW_TILE // LANES, N_EXPERTS, LANES), lambda i, *_: (i, 0, 0)),
                      _tile_spec(stripes, D_MODEL),
                      g2,
                      pl.BlockSpec((1, D_MODEL), lambda i, *_: (0, 0))],
            out_specs=_tile_spec(stripes, D_MODEL),
            scratch_shapes=[pltpu.VMEM((2, N_EXPERTS * SLOT_WIN, D_MODEL), BF16),
                            pltpu.SemaphoreType.DMA((2,))]),
        compiler_params=_params(1, VMEM_LIMIT),
        name="combine",
    )(offs, nwin, ys, pos, x1.reshape(stripes, n // stripes, D_MODEL), mod3, final_g)
    return y.reshape(n, D_MODEL)


def _head_cols(w3):
    k, _, d = w3.shape
    return jnp.pad(w3, ((0, 0), (0, 0), (0, HEAD_PAD - d))).reshape(k, QK_WIDTH)


def _value_cols(v3):
    k = v3.shape[0]
    pair = v3.reshape(k, MLA_HEADS // 2, 2, V_DIM)
    zeros = jnp.zeros((k, MLA_HEADS // 2, V_DIM), v3.dtype)
    even = jnp.concatenate([pair[:, :, 0], zeros], axis=-1)
    odd = jnp.concatenate([zeros, pair[:, :, 1]], axis=-1)
    return jnp.stack([even, odd], axis=2).reshape(k, QK_WIDTH)


def _rope_tables(seq):
    half = QK_ROPE // 2
    pos = jnp.arange(seq)
    inv_freq = 1.0 / (ROPE_BASE ** (jnp.arange(0, half, 2, dtype=F32) / half))

    def cs(p):
        ang = p.astype(F32)[:, None] * inv_freq[None, :]
        ang = jnp.concatenate([ang, ang], axis=-1)
        return jnp.cos(ang), jnp.sin(ang)

    cr, sr = cs(pos // GRID_W)
    cc, sc = cs(pos % GRID_W)
    ones = jnp.ones((seq, QK_NOPE), F32)
    zpad = jnp.zeros((seq, HEAD_PAD - QK_NOPE - QK_ROPE), F32)
    cq = jnp.concatenate([ones, cr, cc, zpad], axis=-1)
    sq = jnp.concatenate([0.0 * ones, sr, sc, zpad], axis=-1)
    return cq, sq


def _prep_weights(w_in, q_norm_g, w_q_up, kv_norm_g, w_kv_up, attn_norm_g):
    wqc = w_in[:, :Q_RANK + KV_RANK].astype(BF16)
    wkr = jnp.pad(w_in[:, Q_RANK + KV_RANK:Q_RANK + KV_RANK + QK_ROPE],
                  ((0, 0), (0, LANES - QK_ROPE))).astype(BF16)
    wglu = w_in[:, Q_RANK + KV_RANK + QK_ROPE:].astype(BF16)
    wq = _head_cols(w_q_up.astype(BF16).reshape(Q_RANK, MLA_HEADS, QK_NOPE + QK_ROPE))
    kv3 = w_kv_up.astype(BF16).reshape(KV_RANK, MLA_HEADS, QK_NOPE + V_DIM)
    wk = _head_cols(kv3[:, :, :QK_NOPE])
    wv = _value_cols(kv3[:, :, QK_NOPE:])
    pk = np.zeros((LANES, QK_WIDTH), np.float32)
    for hd in range(MLA_HEADS):
        for j in range(QK_ROPE):
            pk[j, hd * HEAD_PAD + QK_NOPE + j] = 1.0
    pk = jnp.asarray(pk, BF16)
    return (attn_norm_g[None, :], wqc, wkr, wglu, q_norm_g[None, :], wq,
            kv_norm_g[None, :], wk, wv, pk)


def _stripe_order(aff3, stripes, seq):
    per = ROW_TILE // stripes
    q = LANES // per
    assert per * stripes == ROW_TILE and per * q == LANES and stripes % q == 0 and seq % LANES == 0
    a = aff3.reshape(stripes // q, q, seq // LANES, N_EXPERTS, q, per)
    return a.transpose(2, 4, 0, 3, 1, 5).reshape(aff3.shape)


def _route_plan(base, cap):
    rows_per_tile = ROW_TILE // LANES
    start = base[::rows_per_tile, :, 0].astype(I32)
    end = jnp.concatenate([start[1:], jnp.full((1, N_EXPERTS), cap, I32)], axis=0)
    off = (start // BF16_ROWS) * BF16_ROWS
    span = end - off
    nwin = jnp.maximum(jnp.max((span + SLOT_WIN - 1) // SLOT_WIN, axis=1), 1).astype(I32)
    return off.reshape(-1), nwin


def kernel(x_prompt, x_sample, cache_ckv, cache_krope, c, c_ctx, w_mod, b_mod, attn_norm_g,
           w_in, q_norm_g, w_q_up, kv_norm_g, w_kv_up, conv_w, conv_b, conv_ln_g, conv_ln_b,
           w_out, ffn_norm_g, w_router, w_gate, w_up, w_down, final_norm_g):
    depth = w_mod.shape[0]
    assert depth == 1
    bp, sp, _ = x_prompt.shape
    bs, ss, _ = x_sample.shape
    past = cache_ckv.shape[2]
    ctx_row = bs

    mod_rows = 16
    cc = jnp.concatenate([c, c_ctx[None, :], jnp.zeros((mod_rows - bs - 1, D_MODEL), F32)], axis=0)
    mod = _adaln(cc, w_mod[0], b_mod[0][None, :])
    mod3 = mod.reshape(mod_rows, 1, N_MOD * D_MODEL)

    fw = _prep_weights(w_in[0], q_norm_g[0], w_q_up[0], kv_norm_g[0], w_kv_up[0], attn_norm_g[0])
    wk, wv, pk = fw[7], fw[8], fw[9]
    wo = w_out[0].astype(BF16)
    wr = jnp.pad(jnp.tile(w_router[0], (1, 3)), ((0, 0), (0, LANES - 3 * N_EXPERTS)))
    wrh = wr.astype(BF16)
    wrl = (wr - wrh.astype(F32)).astype(BF16)
    bw = (conv_w[0], conv_b[0][None, :], conv_ln_g[0][None, :], conv_ln_b[0][None, :],
          wo[:MLA_HEADS * V_DIM], wo[MLA_HEADS * V_DIM:], ffn_norm_g[0][None, :], wrh, wrl)

    xp2 = x_prompt.reshape(bp * sp, D_MODEL)
    xs2 = x_sample.reshape(bs * ss, D_MODEL)

    qp, kp, vp, glup, ckvp, krp = _front(xp2, mod3, ctx_row, sp, fw, None)
    attn_p = _attention(qp, [(kp, vp, sp)], bp, sp)
    x1p, h2p, afftp = _back(xp2, attn_p, glup.reshape(bp, sp, CONV_CH), mod3, ctx_row, bw, bp, sp)

    qs, ks, vs, glus = _front(xs2, mod3, None, ss, fw, _rope_tables(ss))
    kc, vc = _ctx_kv(cache_ckv[:, 0].reshape(bs * past, KV_RANK),
                     cache_krope[:, 0].reshape(bs * past, QK_ROPE), wk, wv, pk)
    attn_s = _attention(qs, [(kc, vc, past), (ks, vs, ss)], bs, ss)
    x1s, h2s, affts = _back(xs2, attn_s, glus.reshape(bs, ss, CONV_CH), mod3, None, bw, bs, ss)

    stripes = (1, bs)
    caps = [EC_FACTOR * h2.shape[0] // N_EXPERTS for h2 in (h2p, h2s)]
    routed = _route([afftp, _stripe_order(affts, bs, ss)], caps)
    plans = []
    xs_list = []
    for g, (h2, cap) in enumerate(zip((h2p, h2s), caps)):
        pos, base = routed[g], routed[len(caps) + g]
        offs, nwin = _route_plan(base, cap)
        plans.append((pos, offs, nwin, cap))
        xs_list.append(_dispatch(h2, pos, offs, nwin, cap, stripes[g]))
    ys_list = _experts(xs_list, w_gate[0], w_up[0], w_down[0])

    fg = final_norm_g[None, :]
    pos, offs, nwin, cap = plans[0]
    y_prompt = _combine(ys_list[0], pos, offs, nwin, x1p, mod3, ctx_row, stripes[0], fg, cap)
    pos, offs, nwin, cap = plans[1]
    y_sample = _combine(ys_list[1], pos, offs, nwin, x1s, mod3, None, stripes[1], fg, cap)

    return (y_prompt.reshape(bp, sp, D_MODEL), y_sample.reshape(bs, ss, D_MODEL),
            ckvp.reshape(bp, depth, sp, KV_RANK), krp.reshape(bp, depth, sp, QK_ROPE))
```

```python
import functools

import jax
import jax.numpy as jnp
import numpy as np
from jax import lax
from jax.experimental import pallas as pl
from jax.experimental.pallas import tpu as pltpu

F32 = jnp.float32
BF16 = jnp.bfloat16
I32 = jnp.int32

LANES = 128
BF16_ROWS = 16
VMEM_LIMIT = 56 * 1024 * 1024

D_MODEL = 1024
GRID_W = 64
MLA_HEADS = 8
QK_NOPE = 64
QK_ROPE = 32
V_DIM = 64
Q_RANK = 768
KV_RANK = 256
CONV_CH = 512
CONV_WIDTH = 31
N_EXPERTS = 16
EXPERT_FF = 1024
EC_FACTOR = 2
ROPE_BASE = 10000.0
EPS = 1e-6
N_MOD = 6

HEAD_PAD = LANES
QK_WIDTH = MLA_HEADS * HEAD_PAD
ROW_TILE = 256
FRONT_TILE = 512
SLOT_WIN = 64
AUG = LANES
XS_W = D_MODEL + AUG
LOG2_E = 1.4426950408889634


def _params(n_axes, vmem=None):
    return pltpu.CompilerParams(
        dimension_semantics=("arbitrary",) * n_axes,
        vmem_limit_bytes=vmem)


def _rms(x, g):
    ms = jnp.mean(x * x, axis=-1, keepdims=True)
    return x * lax.rsqrt(ms + EPS) * g


def _bdot(a, b):
    return jnp.dot(a, b, preferred_element_type=F32)


def _adaln_kernel(c_ref, w_ref, b_ref, o_ref):
    c = c_ref[...]
    s = c * jax.nn.sigmoid(c)
    w = w_ref[...]
    sh = s.astype(BF16)
    sl = (s - sh.astype(F32)).astype(BF16)
    wh = w.astype(BF16)
    wl = (w - wh.astype(F32)).astype(BF16)
    o_ref[...] = _bdot(sh, wh) + (_bdot(sl, wh) + _bdot(sh, wl)) + b_ref[...]


def _adaln(cc, w_mod, b_mod):
    rows = cc.shape[0]
    n_out = w_mod.shape[1]
    blk = D_MODEL
    return pl.pallas_call(
        _adaln_kernel,
        out_shape=jax.ShapeDtypeStruct((rows, n_out), F32),
        grid=(n_out // blk,),
        in_specs=[pl.BlockSpec((rows, D_MODEL), lambda j: (0, 0)),
                  pl.BlockSpec((D_MODEL, blk), lambda j: (0, j)),
                  pl.BlockSpec((1, blk), lambda j: (0, j))],
        out_specs=pl.BlockSpec((rows, blk), lambda j: (0, j)),
        compiler_params=_params(1),
        name="adaln",
    )(cc, w_mod, b_mod)


def _rot_half(x, lane, base):
    rel = lane - base
    first = ((rel >= 0) & (rel < 8)) | ((rel >= 16) & (rel < 24))
    return jnp.where(first, -pltpu.roll(x, LANES - 8, axis=1), pltpu.roll(x, 8, axis=1))


def _front_kernel(rope, seq, x_ref, sh_ref, sc_ref, g_ref, wqc_ref, wkr_ref, wglu_ref,
                  qg_ref, wq_ref, kvg_ref, wk_ref, wv_ref, pk_ref, cw_ref, cb_ref, lg_ref, lb_ref,
                  *rest):
    if rope:
        cq_ref, sq_ref, q_ref, k_ref, v_ref, cv_ref, vpad_ref, shift_ref = rest
    else:
        q_ref, k_ref, v_ref, cv_ref, ckv_ref, kr_ref, vpad_ref, shift_ref = rest
    rows = x_ref.shape[0]
    steps_per_seq = max(seq // rows, 1)
    scale = (QK_NOPE + QK_ROPE) ** -0.5 * LOG2_E
    lane = lax.broadcasted_iota(I32, (ROW_TILE, LANES), 1)
    conv_refs = (cw_ref, cb_ref, lg_ref, lb_ref, shift_ref)

    def quarter(qd, row_in_seq, slot, cv_row):
        r = slice(qd * ROW_TILE, (qd + 1) * ROW_TILE)
        h = _rms(x_ref[r, :], g_ref[...]) * (1.0 + sc_ref[0]) + sh_ref[0]
        hb = h.astype(BF16)
        glu = _bdot(hb, wglu_ref[...])
        if row_in_seq == 0:
            vpad_ref[slot, 0:CONV_HALO, :] = jnp.zeros((CONV_HALO, CONV_CH), F32)
        if row_in_seq + ROW_TILE == seq:
            vpad_ref[slot, CONV_HALO + seq:, :] = jnp.zeros((CONV_HALO, CONV_CH), F32)
        vpad_ref[slot, CONV_HALO + row_in_seq:CONV_HALO + row_in_seq + ROW_TILE, :] = (
            glu[:, :CONV_CH] * jax.nn.sigmoid(glu[:, CONV_CH:]))

        ready = ([row_in_seq - ROW_TILE] if row_in_seq > 0 else []) + \
                ([row_in_seq] if row_in_seq + ROW_TILE == seq else [])
        pending = []
        for t0 in ready:
            steps, finish = _conv_stages(CONV_STAGES, vpad_ref, slot, t0, *conv_refs)
            dst = slice(cv_row + t0, cv_row + t0 + ROW_TILE)
            pending += steps + [functools.partial(_store_conv, cv_ref, dst, finish)]

        def tick():
            if pending:
                pending.pop(0)()

        qc = _bdot(hb, wqc_ref[...])
        tick()
        kr = _bdot(hb, wkr_ref[...])
        tick()
        qn = _rms(qc[:, :Q_RANK], qg_ref[...]).astype(BF16)
        q = _bdot(qn, wq_ref[...])
        tick()
        ckv = _rms(qc[:, Q_RANK:], kvg_ref[...])
        ckv_b = ckv.astype(BF16)
        kk = _bdot(ckv_b, wk_ref[...])
        tick()
        v_ref[r, :] = _bdot(ckv_b, wv_ref[...]).astype(BF16)
        tick()
        if rope:
            t = slice(row_in_seq % rows, row_in_seq % rows + ROW_TILE)
            cq = cq_ref[t, :]
            sq = sq_ref[t, :]
            for hd in range(MLA_HEADS):
                blk = q[:, hd * HEAD_PAD:(hd + 1) * HEAD_PAD]
                rot = blk * cq + _rot_half(blk, lane, QK_NOPE) * sq
                q_ref[r, hd * HEAD_PAD:(hd + 1) * HEAD_PAD] = (rot * scale).astype(BF16)
            krs = pltpu.roll(kr, QK_NOPE, axis=1)
            kr = pltpu.roll(krs * cq + _rot_half(krs, lane, QK_NOPE) * sq, LANES - QK_NOPE, axis=1)
            kr = jnp.where(lane < QK_ROPE, kr, 0.0)
        else:
            q_ref[r, :] = (q * scale).astype(BF16)
            ckv_ref[r, :] = ckv
            kr_ref[r, :] = kr[:, :QK_ROPE]
        tick()
        k_ref[r, :] = (kk + _bdot(kr.astype(BF16), pk_ref[...])).astype(BF16)
        while pending:
            tick()

    def step(k):
        for qd in range(rows // ROW_TILE):
            row = k * rows + qd * ROW_TILE
            if seq >= rows:
                quarter(qd, row, 0, 0)
            else:
                quarter(qd, row % seq, row // seq, (row // seq) * seq)

    if steps_per_seq == 1:
        step(0)
    else:
        for k in range(steps_per_seq):
            pl.when(pl.program_id(0) % steps_per_seq == k)(functools.partial(step, k))


def _store_conv(cv_ref, dst, finish):
    cv_ref[dst, :] = finish()


def _front(x2d, mod3, mod_row, seq, wts, conv_wts, rope_tabs):
    n = x2d.shape[0]
    rope = rope_tabs is not None
    tile = FRONT_TILE
    steps_per_seq = max(seq // tile, 1)
    assert seq % ROW_TILE == 0 and (tile % seq == 0 or seq % tile == 0)
    full = lambda a: pl.BlockSpec(a.shape, lambda i: (0,) * a.ndim)

    def mod_spec(col):
        if mod_row is None:
            assert seq % tile == 0
            return pl.BlockSpec((1, 1, D_MODEL), lambda i: (i // steps_per_seq, 0, col))
        return pl.BlockSpec((1, 1, D_MODEL), lambda i: (mod_row, 0, col))

    row = lambda w: pl.BlockSpec((tile, w), lambda i: (i, 0))
    cv_rows = max(seq, tile)
    cv_spec = pl.BlockSpec((cv_rows, CONV_CH), lambda i: (i // steps_per_seq, 0))
    in_specs = [row(D_MODEL), mod_spec(0), mod_spec(1)] + [full(w) for w in wts + conv_wts]
    args = [x2d, mod3, mod3] + list(wts + conv_wts)
    out_shape = [jax.ShapeDtypeStruct((n, QK_WIDTH), BF16)] * 3 + \
                [jax.ShapeDtypeStruct((n, CONV_CH), BF16)]
    out_specs = [row(QK_WIDTH)] * 3 + [cv_spec]
    if rope:
        tab = pl.BlockSpec((tile, LANES), lambda i: (i % steps_per_seq, 0))
        in_specs += [tab, tab]
        args += list(rope_tabs)
    else:
        out_shape += [jax.ShapeDtypeStruct((n, KV_RANK), F32),
                      jax.ShapeDtypeStruct((n, QK_ROPE), F32)]
        out_specs += [row(KV_RANK), row(QK_ROPE)]
    slots = max(tile // seq, 1)
    return pl.pallas_call(
        functools.partial(_front_kernel, rope, seq),
        out_shape=out_shape,
        grid=(n // tile,),
        in_specs=in_specs,
        out_specs=out_specs,
        scratch_shapes=[pltpu.VMEM((slots, seq + 2 * CONV_HALO, CONV_CH), F32),
                        pltpu.VMEM((SUBLANES - 1, SHIFT_ROWS, CONV_CH), F32)],
        compiler_params=_params(1, VMEM_LIMIT),
        name="front_rope" if rope else "front",
    )(*args)


def _ctx_kernel(ckv_ref, kr_ref, wk_ref, wv_ref, pk_ref, k_ref, v_ref):
    cb = ckv_ref[...].astype(BF16)
    kr = kr_ref[...].astype(BF16)
    k_ref[...] = (_bdot(cb, wk_ref[...]) + _bdot(kr, pk_ref[...])).astype(BF16)
    v_ref[...] = _bdot(cb, wv_ref[...]).astype(BF16)


def _ctx_kv(ckv2d, kr2d, wk, wv, pk):
    n = ckv2d.shape[0]
    full = lambda a: pl.BlockSpec(a.shape, lambda i: (0,) * a.ndim)
    row = lambda w: pl.BlockSpec((ROW_TILE, w), lambda i: (i, 0))
    pk32 = pk[:QK_ROPE]
    return pl.pallas_call(
        _ctx_kernel,
        out_shape=[jax.ShapeDtypeStruct((n, QK_WIDTH), BF16)] * 2,
        grid=(n // ROW_TILE,),
        in_specs=[row(KV_RANK), row(QK_ROPE), full(wk), full(wv), full(pk32)],
        out_specs=[row(QK_WIDTH)] * 2,
        compiler_params=_params(1),
        name="ctx_kv",
    )(ckv2d, kr2d, wk, wv, pk32)


CONV_PAD = (CONV_WIDTH - 1) // 2
CONV_HALO = 16
SUBLANES = 8
SHIFT_ROWS = ROW_TILE + 2 * CONV_HALO - SUBLANES


CONV_STAGES = 6


def _conv_stages(n_stages, vpad_ref, slot, t0, cw_ref, cb_ref, lg_ref, lb_ref, shift_ref):
    state = {}

    def taps(lo, hi):
        if lo == 0:
            win = vpad_ref[slot, t0:t0 + ROW_TILE + 2 * CONV_HALO, :]
            for ph in range(1, SUBLANES):
                shift_ref[ph - 1] = win[ph:ph + SHIFT_ROWS, :]
            state["acc"] = jnp.broadcast_to(cb_ref[...], (ROW_TILE, CONV_CH))
        acc = state["acc"]
        for t in range(lo, hi):
            blk, ph = divmod(t + CONV_HALO - CONV_PAD, SUBLANES)
            if ph == 0:
                tap = vpad_ref[slot, t0 + blk * SUBLANES:t0 + blk * SUBLANES + ROW_TILE, :]
            else:
                tap = shift_ref[ph - 1, blk * SUBLANES:blk * SUBLANES + ROW_TILE, :]
            acc = acc + tap * cw_ref[t:t + 1, :]
        state["acc"] = acc

    def finish():
        acc = state["acc"]
        mu = jnp.mean(acc, axis=-1, keepdims=True)
        cen = acc - mu
        var = jnp.mean(cen * cen, axis=-1, keepdims=True)
        ln = cen * lax.rsqrt(var + EPS) * lg_ref[...] + lb_ref[...]
        return (ln * jax.nn.sigmoid(ln)).astype(BF16)

    bounds = [CONV_WIDTH * s // n_stages for s in range(n_stages + 1)]
    stages = [functools.partial(taps, bounds[s], bounds[s + 1]) for s in range(n_stages)]
    return stages, finish


def _attn_kernel(n_kv, q_ref, *rest):
    kv_refs = rest[:2 * n_kv]
    o_ref = rest[2 * n_kv]
    nt = (((1,), (1,)), ((), ()))

    def scores(hd):
        sl = slice(hd * HEAD_PAD, (hd + 1) * HEAD_PAD)
        qh = q_ref[:, sl]
        return [lax.dot_general(qh, kv_refs[2 * j][:, sl], nt, preferred_element_type=F32)
                for j in range(n_kv)]

    outs = []
    nxt = scores(0)
    for hd in range(MLA_HEADS):
        sl = slice(hd * HEAD_PAD, (hd + 1) * HEAD_PAD)
        ss = nxt
        if hd + 1 < MLA_HEADS:
            nxt = scores(hd + 1)
        m = ss[0].max(axis=-1, keepdims=True)
        for s in ss[1:]:
            m = jnp.maximum(m, s.max(axis=-1, keepdims=True))
        ps = [jnp.exp2(s - m) for s in ss]
        l = ps[0].sum(axis=-1, keepdims=True)
        for p in ps[1:]:
            l = l + p.sum(axis=-1, keepdims=True)
        o = _bdot(ps[0].astype(BF16), kv_refs[1][:, sl])
        for j in range(1, n_kv):
            o = o + _bdot(ps[j].astype(BF16), kv_refs[2 * j + 1][:, sl])
        outs.append(o / l)
    for j in range(MLA_HEADS // 2):
        o_ref[:, j * LANES:(j + 1) * LANES] = (outs[2 * j] + outs[2 * j + 1]).astype(BF16)


def _attention(q, kvs, batch, seq):
    n = q.shape[0]
    q_tiles = seq // ROW_TILE
    row = lambda w: pl.BlockSpec((ROW_TILE, w), lambda b, i: (b * q_tiles + i, 0))
    in_specs = [row(QK_WIDTH)]
    args = [q]
    for k, v, t in kvs:
        spec = pl.BlockSpec((t, QK_WIDTH), lambda b, i: (b, 0))
        in_specs += [spec, spec]
        args += [k, v]
    return pl.pallas_call(
        functools.partial(_attn_kernel, len(kvs)),
        out_shape=jax.ShapeDtypeStruct((n, MLA_HEADS * V_DIM), BF16),
        grid=(batch, q_tiles),
        in_specs=in_specs,
        out_specs=row(MLA_HEADS * V_DIM),
        compiler_params=_params(2, VMEM_LIMIT),
        name="attention",
    )(*args)


def _back_kernel(x_ref, attn_ref, cv_ref, woa_ref, woc_ref, g1_ref, sh_ref, sc_ref, fg_ref,
                 wrh_ref, wrl_ref, x1_ref, h2_ref, afft_ref):
    mix = _bdot(attn_ref[...], woa_ref[...]) + _bdot(cv_ref[...], woc_ref[...])
    x1 = x_ref[...] + g1_ref[0] * mix
    x1_ref[...] = x1
    h2 = _rms(x1, fg_ref[...]) * (1.0 + sc_ref[0]) + sh_ref[0]
    h2b = h2.astype(BF16)
    h2_ref[:, :D_MODEL] = h2b

    h2l = (h2 - h2b.astype(F32)).astype(BF16)
    wrh = wrh_ref[...]
    logits = _bdot(h2b, wrh) + (_bdot(h2l, wrh) + _bdot(h2b, wrl_ref[...]))
    lane = lax.broadcasted_iota(I32, logits.shape, 1)
    real = lane < N_EXPERTS
    m = jnp.max(jnp.where(real, logits, -jnp.inf), axis=-1, keepdims=True)
    e = jnp.exp(logits - m)
    aff = e / jnp.sum(jnp.where(real, e, 0.0), axis=-1, keepdims=True)
    aff_t = aff.T
    for blk in range(ROW_TILE // LANES):
        afft_ref[blk] = aff_t[:N_EXPERTS, blk * LANES:(blk + 1) * LANES]
    hi = aff.astype(BF16)
    r1 = aff - hi.astype(F32)
    mid = r1.astype(BF16)
    lo = (r1 - mid.astype(F32)).astype(BF16)
    zero = jnp.zeros_like(hi)
    h2_ref[:, D_MODEL:] = jnp.where(
        real, hi, jnp.where(lane < 2 * N_EXPERTS, mid, jnp.where(lane < 3 * N_EXPERTS, lo, zero)))


def _back(x2d, attn, cv, mod3, mod_row, wts, batch, seq):
    n = x2d.shape[0]
    tiles = seq // ROW_TILE
    full = lambda a: pl.BlockSpec(a.shape, lambda b, i: (0,) * a.ndim)

    def mod_spec(col):
        if mod_row is None:
            return pl.BlockSpec((1, 1, D_MODEL), lambda b, i: (b, 0, col))
        return pl.BlockSpec((1, 1, D_MODEL), lambda b, i: (mod_row, 0, col))

    row = lambda w: pl.BlockSpec((ROW_TILE, w), lambda b, i: (b * tiles + i, 0))
    woa, woc, fg, wrh, wrl = wts
    in_specs = [row(D_MODEL), row(MLA_HEADS * V_DIM), row(CONV_CH), full(woa), full(woc),
                mod_spec(2), mod_spec(3), mod_spec(4), full(fg), full(wrh), full(wrl)]
    return pl.pallas_call(
        _back_kernel,
        out_shape=[jax.ShapeDtypeStruct((n, D_MODEL), F32),
                   jax.ShapeDtypeStruct((n, XS_W), BF16),
                   jax.ShapeDtypeStruct((n // LANES, N_EXPERTS, LANES), F32)],
        grid=(batch, tiles),
        in_specs=in_specs,
        out_specs=[row(D_MODEL), row(XS_W),
                   pl.BlockSpec((ROW_TILE // LANES, N_EXPERTS, LANES),
                                lambda b, i: (b * tiles + i, 0, 0))],
        compiler_params=_params(2, VMEM_LIMIT),
        name="back",
    )(x2d, attn, cv, woa, woc, mod3, mod3, mod3, fg, wrh, wrl)


def _route_kernel(caps, *refs):
    n_g = len(caps)
    affs = [r[...] for r in refs[:n_g]]
    pos_refs = refs[n_g:2 * n_g]
    base_refs = refs[2 * n_g:]
    ones = jnp.ones((LANES, LANES), BF16)
    l_r = lax.broadcasted_iota(I32, (LANES, LANES), 0)
    l_c = lax.broadcasted_iota(I32, (LANES, LANES), 1)
    tri = jnp.where(l_r <= l_c, 1.0, 0.0).astype(BF16)

    def lane_dot(mask3, w):
        r, e, _ = mask3.shape
        mb = jnp.where(mask3, 1.0, 0.0).astype(BF16).reshape(r * e, LANES)
        return _bdot(mb, w).reshape(r, e, LANES)

    def expert_count(mask3):
        return jnp.sum(lane_dot(mask3, ones), axis=0, keepdims=True)

    def excl_cumsum(mask3):
        rows = mask3.shape[0]
        tot = lane_dot(mask3, ones)
        run = tot
        s = 1
        while s < rows:
            run = run + jnp.concatenate([jnp.zeros((s,) + run.shape[1:], F32), run[:rows - s]], axis=0)
            s *= 2
        before = run - tot
        return before + lane_dot(mask3, tri) - jnp.where(mask3, 1.0, 0.0), before

    def step(t, thrs):
        bit = lax.shift_left(jnp.int32(1), 30 - t)
        out = []
        for aff, thr, cap in zip(affs, thrs, caps):
            cand = thr | bit
            cnt = expert_count(aff >= pltpu.bitcast(cand, F32))
            out.append(jnp.where(cnt >= cap, cand, thr))
        return tuple(out)

    init = tuple(jnp.zeros((1,) + a.shape[1:], I32) for a in affs)
    thrs = lax.fori_loop(0, 31, step, init)
    for aff, thr_bits, cap, pos_ref, base_ref in zip(affs, thrs, caps, pos_refs, base_refs):
        thr = pltpu.bitcast(thr_bits, F32)
        gt = aff > thr
        eq = aff == thr
        need = cap - expert_count(gt)
        tie_rank, _ = excl_cumsum(eq)
        sel = gt | (eq & (tie_rank < need))
        pos, before = excl_cumsum(sel)
        pos_ref[...] = jnp.where(sel, pos, -1.0).astype(I32)
        base_ref[...] = before


def _route(affs, caps):
    shapes = [a.shape for a in affs]
    return pl.pallas_call(
        functools.partial(_route_kernel, tuple(caps)),
        out_shape=[jax.ShapeDtypeStruct(s, I32) for s in shapes] +
                  [jax.ShapeDtypeStruct(s, F32) for s in shapes],
        compiler_params=pltpu.CompilerParams(vmem_limit_bytes=VMEM_LIMIT),
        name="route",
    )(*affs)


def _window(off_ref, e, i, n_tiles, w, cap):
    del n_tiles
    first = off_ref[i * N_EXPERTS + e] + w * SLOT_WIN
    return first, pl.multiple_of(jnp.minimum(first, cap - SLOT_WIN), BF16_ROWS)


def _one_hot_rows(pos_ref, wins):
    r = lax.broadcasted_iota(I32, (SLOT_WIN, LANES), 0)
    blocks = []
    for e, (first, off) in enumerate(wins):
        halves = []
        for h in range(pos_ref.shape[0]):
            pe = pos_ref[h, e:e + 1, :]
            rel = jnp.where(pe >= first, pe - off, -1)
            halves.append(jnp.where(rel == r, 1.0, 0.0).astype(BF16))
        blocks.append(jnp.concatenate(halves, axis=1))
    return jnp.concatenate(blocks, axis=0)


def _dispatch_kernel(cap, n_tiles, off_ref, nwin_ref, h2_ref, pos_ref, xs_hbm, xs_ref, sem):
    i = pl.program_id(0)

    @pl.when(i == 0)
    def _():
        def zero(e, carry):
            xs_ref[e] = jnp.zeros(xs_ref.shape[1:], BF16)
            return carry
        lax.fori_loop(0, N_EXPERTS, zero, 0)

    def window(w, carry):
        wins = [_window(off_ref, e, i, n_tiles, w, cap) for e in range(N_EXPERTS)]
        p = _one_hot_rows(pos_ref, wins)
        h2 = h2_ref[...].reshape(ROW_TILE, XS_W)
        slab = _bdot(p, h2)
        for e, (_, off) in enumerate(wins):
            cur = xs_ref[e, pl.ds(off, SLOT_WIN), :].astype(F32)
            xs_ref[e, pl.ds(off, SLOT_WIN), :] = (
                cur + slab[e * SLOT_WIN:(e + 1) * SLOT_WIN]).astype(BF16)
        return carry

    lax.fori_loop(0, nwin_ref[i], window, 0)

    @pl.when(i == n_tiles - 1)
    def _():
        cp = pltpu.make_async_copy(xs_ref, xs_hbm, sem)
        cp.start()
        cp.wait()


def _tile_spec(stripes, width):
    return pl.BlockSpec((stripes, ROW_TILE // stripes, width), lambda i, *_: (0, i, 0))


def _dispatch(h2aug, pos, offs, nwin, cap, stripes):
    n = h2aug.shape[0]
    n_tiles = n // ROW_TILE
    return pl.pallas_call(
        functools.partial(_dispatch_kernel, cap, n_tiles),
        out_shape=jax.ShapeDtypeStruct((N_EXPERTS, cap, XS_W), BF16),
        grid_spec=pltpu.PrefetchScalarGridSpec(
            num_scalar_prefetch=2,
            grid=(n_tiles,),
            in_specs=[_tile_spec(stripes, XS_W),
                      pl.BlockSpec((ROW_TILE // LANES, N_EXPERTS, LANES), lambda i, *_: (i, 0, 0))],
            out_specs=pl.BlockSpec(memory_space=pl.ANY),
            scratch_shapes=[pltpu.VMEM((N_EXPERTS, cap, XS_W), BF16),
                            pltpu.SemaphoreType.DMA]),
        compiler_params=_params(1, VMEM_LIMIT),
        name="dispatch",
    )(offs, nwin, h2aug.reshape(stripes, n // stripes, XS_W), pos)


FF_CHUNK = 512
EXP_ROWS = 512


def _experts_kernel(n_groups, n_chunks, *refs):
    xs_refs = refs[:n_groups]
    wg_ref, wu_ref, wd_ref = refs[n_groups:n_groups + 3]
    ys_refs = refs[n_groups + 3:2 * n_groups + 3]
    acc_ref = refs[2 * n_groups + 3]
    e = pl.program_id(0)
    j = pl.program_id(1)
    cap = xs_refs[0].shape[1]

    @pl.when(j == 0)
    def _():
        for g in range(n_groups):
            acc_ref[g] = jnp.zeros(acc_ref.shape[1:], F32)

    wg = wg_ref[0].astype(BF16)
    wu = wu_ref[0].astype(BF16)
    wd = wd_ref[0].astype(BF16)
    blocks = [(g, slice(r * EXP_ROWS, (r + 1) * EXP_ROWS))
              for g in range(n_groups) for r in range(cap // EXP_ROWS)]

    def up(blk):
        g, rows = blk
        x = xs_refs[g][0, rows, :D_MODEL]
        return _bdot(x, wg), _bdot(x, wu)

    nxt = up(blocks[0])
    for k, (g, rows) in enumerate(blocks):
        a, u = nxt
        if k + 1 < len(blocks):
            nxt = up(blocks[k + 1])
        hm = (a * jax.nn.sigmoid(a) * u).astype(BF16)
        acc_ref[g, rows, :] += _bdot(hm, wd)

    @pl.when(j == n_chunks - 1)
    def _():
        for g, (xs_ref, ys_ref) in enumerate(zip(xs_refs, ys_refs)):
            aug = xs_ref[0, :, D_MODEL:].astype(F32)
            lane = lax.broadcasted_iota(I32, aug.shape, 1)
            mine = (lane == e) | (lane == e + N_EXPERTS) | (lane == e + 2 * N_EXPERTS)
            gate = jnp.sum(jnp.where(mine, aug, 0.0), axis=-1, keepdims=True)
            ys_ref[0] = (acc_ref[g] * gate).astype(BF16)


def _experts(xs_list, w_gate, w_up, w_down):
    n_groups = len(xs_list)
    cap = xs_list[0].shape[1]
    ff = w_gate.shape[2]
    n_chunks = ff // FF_CHUNK
    xs_spec = pl.BlockSpec((1, cap, XS_W), lambda e, j: (e, 0, 0))
    up_spec = pl.BlockSpec((1, D_MODEL, FF_CHUNK), lambda e, j: (e, 0, j))
    down_spec = pl.BlockSpec((1, FF_CHUNK, D_MODEL), lambda e, j: (e, j, 0))
    ys_spec = pl.BlockSpec((1, cap, D_MODEL), lambda e, j: (e, 0, 0))
    return pl.pallas_call(
        functools.partial(_experts_kernel, n_groups, n_chunks),
        out_shape=[jax.ShapeDtypeStruct((N_EXPERTS, cap, D_MODEL), BF16)] * n_groups,
        grid=(N_EXPERTS, n_chunks),
        in_specs=[xs_spec] * n_groups + [up_spec, up_spec, down_spec],
        out_specs=[ys_spec] * n_groups,
        scratch_shapes=[pltpu.VMEM((n_groups, cap, D_MODEL), F32)],
        compiler_params=_params(2, VMEM_LIMIT),
        name="experts",
    )(*xs_list, w_gate, w_up, w_down)


def _combine_kernel(cap, n_tiles, off_ref, nwin_ref, ys_hbm, pos_ref, x1_ref, g2_ref, fg_ref,
                    y_ref, buf_ref, sems):
    i = pl.program_id(0)
    slot = i % 2

    def copies(tile, w, s):
        out = []
        for e in range(N_EXPERTS):
            _, off = _window(off_ref, e, tile, n_tiles, w, cap)
            out.append(pltpu.make_async_copy(
                ys_hbm.at[e, pl.ds(off, SLOT_WIN), :],
                buf_ref.at[s, pl.ds(e * SLOT_WIN, SLOT_WIN), :],
                sems.at[s]))
        return out

    @pl.when(i == 0)
    def _():
        for cp in copies(0, 0, 0):
            cp.start()

    @pl.when(i + 1 < n_tiles)
    def _():
        for cp in copies(i + 1, 0, 1 - slot):
            cp.start()

    tn = (((0,), (0,)), ((), ()))

    def gathered(w):
        wins = [_window(off_ref, e, i, n_tiles, w, cap) for e in range(N_EXPERTS)]
        p = _one_hot_rows(pos_ref, wins)
        for cp in copies(i, w, slot):
            cp.wait()
        return lax.dot_general(p, buf_ref[slot], tn, preferred_element_type=F32)

    def extra_window(w, acc):
        for cp in copies(i, w, slot):
            cp.start()
        return acc + gathered(w)

    moe = lax.fori_loop(1, nwin_ref[i], extra_window, gathered(0))
    out = x1_ref[...] + g2_ref[...] * moe.reshape(x1_ref.shape)
    y_ref[...] = _rms(out, fg_ref[...])


def _combine(ys, pos, offs, nwin, x1, mod3, mod_row, stripes, final_g, cap):
    n = x1.shape[0]
    n_tiles = n // ROW_TILE
    g2 = pl.BlockSpec((stripes, 1, D_MODEL), lambda i, *_: (0 if mod_row is None else mod_row, 0, 5))
    y = pl.pallas_call(
        functools.partial(_combine_kernel, cap, n_tiles),
        out_shape=jax.ShapeDtypeStruct((stripes, n // stripes, D_MODEL), F32),
        grid_spec=pltpu.PrefetchScalarGridSpec(
            num_scalar_prefetch=2,
            grid=(n_tiles,),
            in_specs=[pl.BlockSpec(memory_space=pl.ANY),
                      pl.BlockSpec((ROW_TILE // LANES, N_EXPERTS, LANES), lambda i, *_: (i, 0, 0)),
                      _tile_spec(stripes, D_MODEL),
                      g2,
                      pl.BlockSpec((1, D_MODEL), lambda i, *_: (0, 0))],
            out_specs=_tile_spec(stripes, D_MODEL),
            scratch_shapes=[pltpu.VMEM((2, N_EXPERTS * SLOT_WIN, D_MODEL), BF16),
                            pltpu.SemaphoreType.DMA((2,))]),
        compiler_params=_params(1, VMEM_LIMIT),
        name="combine",
    )(offs, nwin, ys, pos, x1.reshape(stripes, n // stripes, D_MODEL), mod3, final_g)
    return y.reshape(n, D_MODEL)


def _head_cols(w3):
    k, _, d = w3.shape
    return jnp.pad(w3, ((0, 0), (0, 0), (0, HEAD_PAD - d))).reshape(k, QK_WIDTH)


def _value_cols(v3):
    k = v3.shape[0]
    pair = v3.reshape(k, MLA_HEADS // 2, 2, V_DIM)
    zeros = jnp.zeros((k, MLA_HEADS // 2, V_DIM), v3.dtype)
    even = jnp.concatenate([pair[:, :, 0], zeros], axis=-1)
    odd = jnp.concatenate([zeros, pair[:, :, 1]], axis=-1)
    return jnp.stack([even, odd], axis=2).reshape(k, QK_WIDTH)


def _rope_tables(seq):
    half = QK_ROPE // 2
    pos = jnp.arange(seq)
    inv_freq = 1.0 / (ROPE_BASE ** (jnp.arange(0, half, 2, dtype=F32) / half))

    def cs(p):
        ang = p.astype(F32)[:, None] * inv_freq[None, :]
        ang = jnp.concatenate([ang, ang], axis=-1)
        return jnp.cos(ang), jnp.sin(ang)

    cr, sr = cs(pos // GRID_W)
    cc, sc = cs(pos % GRID_W)
    ones = jnp.ones((seq, QK_NOPE), F32)
    zpad = jnp.zeros((seq, HEAD_PAD - QK_NOPE - QK_ROPE), F32)
    cq = jnp.concatenate([ones, cr, cc, zpad], axis=-1)
    sq = jnp.concatenate([0.0 * ones, sr, sc, zpad], axis=-1)
    return cq, sq


def _prep_weights(w_in, q_norm_g, w_q_up, kv_norm_g, w_kv_up, attn_norm_g):
    wqc = w_in[:, :Q_RANK + KV_RANK].astype(BF16)
    wkr = jnp.pad(w_in[:, Q_RANK + KV_RANK:Q_RANK + KV_RANK + QK_ROPE],
                  ((0, 0), (0, LANES - QK_ROPE))).astype(BF16)
    wglu = w_in[:, Q_RANK + KV_RANK + QK_ROPE:].astype(BF16)
    wq = _head_cols(w_q_up.astype(BF16).reshape(Q_RANK, MLA_HEADS, QK_NOPE + QK_ROPE))
    kv3 = w_kv_up.astype(BF16).reshape(KV_RANK, MLA_HEADS, QK_NOPE + V_DIM)
    wk = _head_cols(kv3[:, :, :QK_NOPE])
    wv = _value_cols(kv3[:, :, QK_NOPE:])
    pk = np.zeros((LANES, QK_WIDTH), np.float32)
    for hd in range(MLA_HEADS):
        for j in range(QK_ROPE):
            pk[j, hd * HEAD_PAD + QK_NOPE + j] = 1.0
    pk = jnp.asarray(pk, BF16)
    return (attn_norm_g[None, :], wqc, wkr, wglu, q_norm_g[None, :], wq,
            kv_norm_g[None, :], wk, wv, pk)


def _stripe_order(aff3, stripes, seq):
    per = ROW_TILE // stripes
    q = LANES // per
    assert per * stripes == ROW_TILE and per * q == LANES and stripes % q == 0 and seq % LANES == 0
    a = aff3.reshape(stripes // q, q, seq // LANES, N_EXPERTS, q, per)
    return a.transpose(2, 4, 0, 3, 1, 5).reshape(aff3.shape)


def _route_plan(base, cap):
    rows_per_tile = ROW_TILE // LANES
    start = base[::rows_per_tile, :, 0].astype(I32)
    end = jnp.concatenate([start[1:], jnp.full((1, N_EXPERTS), cap, I32)], axis=0)
    off = (start // BF16_ROWS) * BF16_ROWS
    span = end - off
    nwin = jnp.maximum(jnp.max((span + SLOT_WIN - 1) // SLOT_WIN, axis=1), 1).astype(I32)
    return off.reshape(-1), nwin


def kernel(x_prompt, x_sample, cache_ckv, cache_krope, c, c_ctx, w_mod, b_mod, attn_norm_g,
           w_in, q_norm_g, w_q_up, kv_norm_g, w_kv_up, conv_w, conv_b, conv_ln_g, conv_ln_b,
           w_out, ffn_norm_g, w_router, w_gate, w_up, w_down, final_norm_g):
    depth = w_mod.shape[0]
    assert depth == 1
    bp, sp, _ = x_prompt.shape
    bs, ss, _ = x_sample.shape
    past = cache_ckv.shape[2]
    ctx_row = bs

    mod_rows = 16
    cc = jnp.concatenate([c, c_ctx[None, :], jnp.zeros((mod_rows - bs - 1, D_MODEL), F32)], axis=0)
    mod = _adaln(cc, w_mod[0], b_mod[0][None, :])
    mod3 = mod.reshape(mod_rows, 1, N_MOD * D_MODEL)

    fw = _prep_weights(w_in[0], q_norm_g[0], w_q_up[0], kv_norm_g[0], w_kv_up[0], attn_norm_g[0])
    wk, wv, pk = fw[7], fw[8], fw[9]
    wo = w_out[0].astype(BF16)
    wr = jnp.pad(jnp.tile(w_router[0], (1, 3)), ((0, 0), (0, LANES - 3 * N_EXPERTS)))
    wrh = wr.astype(BF16)
    wrl = (wr - wrh.astype(F32)).astype(BF16)
    cvw = (conv_w[0], conv_b[0][None, :], conv_ln_g[0][None, :], conv_ln_b[0][None, :])
    bw = (wo[:MLA_HEADS * V_DIM], wo[MLA_HEADS * V_DIM:], ffn_norm_g[0][None, :], wrh, wrl)

    xp2 = x_prompt.reshape(bp * sp, D_MODEL)
    xs2 = x_sample.reshape(bs * ss, D_MODEL)

    qp, kp, vp, cv_p, ckvp, krp = _front(xp2, mod3, ctx_row, sp, fw, cvw, None)
    attn_p = _attention(qp, [(kp, vp, sp)], bp, sp)
    x1p, h2p, afftp = _back(xp2, attn_p, cv_p, mod3, ctx_row, bw, bp, sp)

    qs, ks, vs, cv_s = _front(xs2, mod3, None, ss, fw, cvw, _rope_tables(ss))
    kc, vc = _ctx_kv(cache_ckv[:, 0].reshape(bs * past, KV_RANK),
                     cache_krope[:, 0].reshape(bs * past, QK_ROPE), wk, wv, pk)
    attn_s = _attention(qs, [(kc, vc, past), (ks, vs, ss)], bs, ss)
    x1s, h2s, affts = _back(xs2, attn_s, cv_s, mod3, None, bw, bs, ss)

    stripes = (1, bs)
    caps = [EC_FACTOR * h2.shape[0] // N_EXPERTS for h2 in (h2p, h2s)]
    routed = _route([afftp, _stripe_order(affts, bs, ss)], caps)
    plans = []
    xs_list = []
    for g, (h2, cap) in enumerate(zip((h2p, h2s), caps)):
        pos, base = routed[g], routed[len(caps) + g]
        offs, nwin = _route_plan(base, cap)
        plans.append((pos, offs, nwin, cap))
        xs_list.append(_dispatch(h2, pos, offs, nwin, cap, stripes[g]))
    ys_list = _experts(xs_list, w_gate[0], w_up[0], w_down[0])

    fg = final_norm_g[None, :]
    pos, offs, nwin, cap = plans[0]
    y_prompt = _combine(ys_list[0], pos, offs, nwin, x1p, mod3, ctx_row, stripes[0], fg, cap)
    pos, offs, nwin, cap = plans[1]
    y_sample = _combine(ys_list[1], pos, offs, nwin, x1s, mod3, None, stripes[1], fg, cap)

    return (y_prompt.reshape(bp, sp, D_MODEL), y_sample.reshape(bs, ss, D_MODEL),
            ckvp.reshape(bp, depth, sp, KV_RANK), krp.reshape(bp, depth, sp, QK_ROPE))
```

```python
import functools

import jax
import jax.numpy as jnp
import numpy as np
from jax import lax
from jax.experimental import pallas as pl
from jax.experimental.pallas import tpu as pltpu

F32 = jnp.float32
BF16 = jnp.bfloat16
I32 = jnp.int32

LANES = 128
BF16_ROWS = 16
VMEM_LIMIT = 56 * 1024 * 1024

D_MODEL = 1024
GRID_W = 64
MLA_HEADS = 8
QK_NOPE = 64
QK_ROPE = 32
V_DIM = 64
Q_RANK = 768
KV_RANK = 256
CONV_CH = 512
CONV_WIDTH = 31
N_EXPERTS = 16
EXPERT_FF = 1024
EC_FACTOR = 2
ROPE_BASE = 10000.0
EPS = 1e-6
N_MOD = 6

HEAD_PAD = LANES
QK_WIDTH = MLA_HEADS * HEAD_PAD
ROW_TILE = 256
FRONT_TILE = 512
ATTN_TILE = 512
BACK_TILE = 256
SLOT_WIN = 64
AUG = LANES
XS_W = D_MODEL + AUG
LOG2_E = 1.4426950408889634


def _params(n_axes, vmem=None):
    return pltpu.CompilerParams(
        dimension_semantics=("arbitrary",) * n_axes,
        vmem_limit_bytes=vmem)


def _rms(x, g):
    ms = jnp.mean(x * x, axis=-1, keepdims=True)
    return x * lax.rsqrt(ms + EPS) * g


def _bdot(a, b):
    return jnp.dot(a, b, preferred_element_type=F32)


def _adaln_kernel(c_ref, w_ref, b_ref, o_ref):
    c = c_ref[...]
    s = c * jax.nn.sigmoid(c)
    w = w_ref[...]
    sh = s.astype(BF16)
    sl = (s - sh.astype(F32)).astype(BF16)
    wh = w.astype(BF16)
    wl = (w - wh.astype(F32)).astype(BF16)
    o_ref[...] = _bdot(sh, wh) + (_bdot(sl, wh) + _bdot(sh, wl)) + b_ref[...]


def _adaln(cc, w_mod, b_mod):
    rows = cc.shape[0]
    n_out = w_mod.shape[1]
    blk = D_MODEL
    return pl.pallas_call(
        _adaln_kernel,
        out_shape=jax.ShapeDtypeStruct((rows, n_out), F32),
        grid=(n_out // blk,),
        in_specs=[pl.BlockSpec((rows, D_MODEL), lambda j: (0, 0)),
                  pl.BlockSpec((D_MODEL, blk), lambda j: (0, j)),
                  pl.BlockSpec((1, blk), lambda j: (0, j))],
        out_specs=pl.BlockSpec((rows, blk), lambda j: (0, j)),
        compiler_params=_params(1),
        name="adaln",
    )(cc, w_mod, b_mod)


def _rot_half(x, lane, base):
    rel = lane - base
    first = ((rel >= 0) & (rel < 8)) | ((rel >= 16) & (rel < 24))
    return jnp.where(first, -pltpu.roll(x, LANES - 8, axis=1), pltpu.roll(x, 8, axis=1))


def _front_kernel(rope, seq, x_ref, sh_ref, sc_ref, g_ref, wqc_ref, wkr_ref, wglu_ref,
                  qg_ref, wq_ref, kvg_ref, wk_ref, wv_ref, pk_ref, cw_ref, cb_ref, lg_ref, lb_ref,
                  *rest):
    if rope:
        cq_ref, sq_ref, q_ref, k_ref, v_ref, cv_ref, vpad_ref, shift_ref = rest
    else:
        q_ref, k_ref, v_ref, cv_ref, ckv_ref, kr_ref, vpad_ref, shift_ref = rest
    rows = x_ref.shape[0]
    steps_per_seq = max(seq // rows, 1)
    scale = (QK_NOPE + QK_ROPE) ** -0.5 * LOG2_E
    lane = lax.broadcasted_iota(I32, (ROW_TILE, LANES), 1)
    conv_refs = (cw_ref, cb_ref, lg_ref, lb_ref, shift_ref)

    def quarter(qd, row_in_seq, slot, cv_row):
        r = slice(qd * ROW_TILE, (qd + 1) * ROW_TILE)
        h = _rms(x_ref[r, :], g_ref[...]) * (1.0 + sc_ref[0]) + sh_ref[0]
        hb = h.astype(BF16)
        glu = _bdot(hb, wglu_ref[...])
        if row_in_seq == 0:
            vpad_ref[slot, 0:CONV_HALO, :] = jnp.zeros((CONV_HALO, CONV_CH), F32)
        if row_in_seq + ROW_TILE == seq:
            vpad_ref[slot, CONV_HALO + seq:, :] = jnp.zeros((CONV_HALO, CONV_CH), F32)
        vpad_ref[slot, CONV_HALO + row_in_seq:CONV_HALO + row_in_seq + ROW_TILE, :] = (
            glu[:, :CONV_CH] * jax.nn.sigmoid(glu[:, CONV_CH:]))

        ready = ([row_in_seq - ROW_TILE] if row_in_seq > 0 else []) + \
                ([row_in_seq] if row_in_seq + ROW_TILE == seq else [])
        pending = []
        for t0 in ready:
            steps, finish = _conv_stages(CONV_STAGES, vpad_ref, slot, t0, *conv_refs)
            dst = slice(cv_row + t0, cv_row + t0 + ROW_TILE)
            pending += steps + [functools.partial(_store_conv, cv_ref, dst, finish)]

        def tick():
            if pending:
                pending.pop(0)()

        qc = _bdot(hb, wqc_ref[...])
        tick()
        kr = _bdot(hb, wkr_ref[...])
        tick()
        qn = _rms(qc[:, :Q_RANK], qg_ref[...]).astype(BF16)
        q = _bdot(qn, wq_ref[...])
        tick()
        ckv = _rms(qc[:, Q_RANK:], kvg_ref[...])
        ckv_b = ckv.astype(BF16)
        kk = _bdot(ckv_b, wk_ref[...])
        tick()
        v_ref[r, :] = _bdot(ckv_b, wv_ref[...]).astype(BF16)
        tick()
        if rope:
            t = slice(row_in_seq % rows, row_in_seq % rows + ROW_TILE)
            cq = cq_ref[t, :]
            sq = sq_ref[t, :]
            for hd in range(MLA_HEADS):
                blk = q[:, hd * HEAD_PAD:(hd + 1) * HEAD_PAD]
                rot = blk * cq + _rot_half(blk, lane, QK_NOPE) * sq
                q_ref[r, hd * HEAD_PAD:(hd + 1) * HEAD_PAD] = (rot * scale).astype(BF16)
            krs = pltpu.roll(kr, QK_NOPE, axis=1)
            kr = pltpu.roll(krs * cq + _rot_half(krs, lane, QK_NOPE) * sq, LANES - QK_NOPE, axis=1)
            kr = jnp.where(lane < QK_ROPE, kr, 0.0)
        else:
            q_ref[r, :] = (q * scale).astype(BF16)
            ckv_ref[r, :] = ckv
            kr_ref[r, :] = kr[:, :QK_ROPE]
        tick()
        k_ref[r, :] = (kk + _bdot(kr.astype(BF16), pk_ref[...])).astype(BF16)
        while pending:
            tick()

    def step(k):
        for qd in range(rows // ROW_TILE):
            row = k * rows + qd * ROW_TILE
            if seq >= rows:
                quarter(qd, row, 0, 0)
            else:
                quarter(qd, row % seq, row // seq, (row // seq) * seq)

    if steps_per_seq == 1:
        step(0)
    else:
        for k in range(steps_per_seq):
            pl.when(pl.program_id(0) % steps_per_seq == k)(functools.partial(step, k))


def _store_conv(cv_ref, dst, finish):
    cv_ref[dst, :] = finish()


def _front(x2d, mod3, mod_row, seq, wts, conv_wts, rope_tabs):
    n = x2d.shape[0]
    rope = rope_tabs is not None
    tile = FRONT_TILE
    steps_per_seq = max(seq // tile, 1)
    assert seq % ROW_TILE == 0 and (tile % seq == 0 or seq % tile == 0)
    full = lambda a: pl.BlockSpec(a.shape, lambda i: (0,) * a.ndim)

    def mod_spec(col):
        if mod_row is None:
            assert seq % tile == 0
            return pl.BlockSpec((1, 1, D_MODEL), lambda i: (i // steps_per_seq, 0, col))
        return pl.BlockSpec((1, 1, D_MODEL), lambda i: (mod_row, 0, col))

    row = lambda w: pl.BlockSpec((tile, w), lambda i: (i, 0))
    cv_rows = max(seq, tile)
    cv_spec = pl.BlockSpec((cv_rows, CONV_CH), lambda i: (i // steps_per_seq, 0))
    in_specs = [row(D_MODEL), mod_spec(0), mod_spec(1)] + [full(w) for w in wts + conv_wts]
    args = [x2d, mod3, mod3] + list(wts + conv_wts)
    out_shape = [jax.ShapeDtypeStruct((n, QK_WIDTH), BF16)] * 3 + \
                [jax.ShapeDtypeStruct((n, CONV_CH), BF16)]
    out_specs = [row(QK_WIDTH)] * 3 + [cv_spec]
    if rope:
        tab = pl.BlockSpec((tile, LANES), lambda i: (i % steps_per_seq, 0))
        in_specs += [tab, tab]
        args += list(rope_tabs)
    else:
        out_shape += [jax.ShapeDtypeStruct((n, KV_RANK), F32),
                      jax.ShapeDtypeStruct((n, QK_ROPE), F32)]
        out_specs += [row(KV_RANK), row(QK_ROPE)]
    slots = max(tile // seq, 1)
    return pl.pallas_call(
        functools.partial(_front_kernel, rope, seq),
        out_shape=out_shape,
        grid=(n // tile,),
        in_specs=in_specs,
        out_specs=out_specs,
        scratch_shapes=[pltpu.VMEM((slots, seq + 2 * CONV_HALO, CONV_CH), F32),
                        pltpu.VMEM((SUBLANES - 1, SHIFT_ROWS, CONV_CH), F32)],
        compiler_params=_params(1, VMEM_LIMIT),
        name="front_rope" if rope else "front",
    )(*args)


def _ctx_kernel(ckv_ref, kr_ref, wk_ref, wv_ref, pk_ref, k_ref, v_ref):
    cb = ckv_ref[...].astype(BF16)
    kr = kr_ref[...].astype(BF16)
    k_ref[...] = (_bdot(cb, wk_ref[...]) + _bdot(kr, pk_ref[...])).astype(BF16)
    v_ref[...] = _bdot(cb, wv_ref[...]).astype(BF16)


def _ctx_kv(ckv2d, kr2d, wk, wv, pk):
    n = ckv2d.shape[0]
    full = lambda a: pl.BlockSpec(a.shape, lambda i: (0,) * a.ndim)
    row = lambda w: pl.BlockSpec((ROW_TILE, w), lambda i: (i, 0))
    pk32 = pk[:QK_ROPE]
    return pl.pallas_call(
        _ctx_kernel,
        out_shape=[jax.ShapeDtypeStruct((n, QK_WIDTH), BF16)] * 2,
        grid=(n // ROW_TILE,),
        in_specs=[row(KV_RANK), row(QK_ROPE), full(wk), full(wv), full(pk32)],
        out_specs=[row(QK_WIDTH)] * 2,
        compiler_params=_params(1),
        name="ctx_kv",
    )(ckv2d, kr2d, wk, wv, pk32)


CONV_PAD = (CONV_WIDTH - 1) // 2
CONV_HALO = 16
SUBLANES = 8
SHIFT_ROWS = ROW_TILE + 2 * CONV_HALO - SUBLANES


CONV_STAGES = 6


def _conv_stages(n_stages, vpad_ref, slot, t0, cw_ref, cb_ref, lg_ref, lb_ref, shift_ref):
    state = {}

    def taps(lo, hi):
        if lo == 0:
            win = vpad_ref[slot, t0:t0 + ROW_TILE + 2 * CONV_HALO, :]
            for ph in range(1, SUBLANES):
                shift_ref[ph - 1] = win[ph:ph + SHIFT_ROWS, :]
            state["acc"] = jnp.broadcast_to(cb_ref[...], (ROW_TILE, CONV_CH))
        acc = state["acc"]
        for t in range(lo, hi):
            blk, ph = divmod(t + CONV_HALO - CONV_PAD, SUBLANES)
            if ph == 0:
                tap = vpad_ref[slot, t0 + blk * SUBLANES:t0 + blk * SUBLANES + ROW_TILE, :]
            else:
                tap = shift_ref[ph - 1, blk * SUBLANES:blk * SUBLANES + ROW_TILE, :]
            acc = acc + tap * cw_ref[t:t + 1, :]
        state["acc"] = acc

    def finish():
        acc = state["acc"]
        mu = jnp.mean(acc, axis=-1, keepdims=True)
        cen = acc - mu
        var = jnp.mean(cen * cen, axis=-1, keepdims=True)
        ln = cen * lax.rsqrt(var + EPS) * lg_ref[...] + lb_ref[...]
        return (ln * jax.nn.sigmoid(ln)).astype(BF16)

    bounds = [CONV_WIDTH * s // n_stages for s in range(n_stages + 1)]
    stages = [functools.partial(taps, bounds[s], bounds[s + 1]) for s in range(n_stages)]
    return stages, finish


def _attn_kernel(n_kv, n_seq, q_ref, *rest):
    kv_refs = rest[:2 * n_kv]
    o_ref = rest[2 * n_kv]
    nt = (((1,), (1,)), ((), ()))
    tq = q_ref.shape[0] // n_seq
    work = [(b, hd) for b in range(n_seq) for hd in range(MLA_HEADS)]

    def kv_rows(ref, b):
        t = ref.shape[0] // n_seq
        return slice(b * t, (b + 1) * t)

    def scores(item):
        b, hd = item
        sl = slice(hd * HEAD_PAD, (hd + 1) * HEAD_PAD)
        qh = q_ref[b * tq:(b + 1) * tq, sl]
        return [lax.dot_general(qh, kv_refs[2 * j][kv_rows(kv_refs[2 * j], b), sl], nt,
                                preferred_element_type=F32) for j in range(n_kv)]

    outs = {}
    nxt = scores(work[0])
    for w, (b, hd) in enumerate(work):
        sl = slice(hd * HEAD_PAD, (hd + 1) * HEAD_PAD)
        ss = nxt
        if w + 1 < len(work):
            nxt = scores(work[w + 1])
        m = ss[0].max(axis=-1, keepdims=True)
        for s in ss[1:]:
            m = jnp.maximum(m, s.max(axis=-1, keepdims=True))
        ps = [jnp.exp2(s - m) for s in ss]
        l = ps[0].sum(axis=-1, keepdims=True)
        for p in ps[1:]:
            l = l + p.sum(axis=-1, keepdims=True)
        o = None
        for j in range(n_kv):
            v_ref = kv_refs[2 * j + 1]
            pv = _bdot(ps[j].astype(BF16), v_ref[kv_rows(v_ref, b), sl])
            o = pv if o is None else o + pv
        outs[hd] = o / l
        if hd % 2 == 1:
            o_ref[b * tq:(b + 1) * tq, (hd // 2) * LANES:(hd // 2 + 1) * LANES] = (
                outs[hd - 1] + outs[hd]).astype(BF16)


def _attention(q, kvs, batch, seq):
    n = q.shape[0]
    n_seq = max(ATTN_TILE // seq, 1)
    tile = min(seq, ATTN_TILE) * n_seq
    q_tiles = max(seq // tile, 1)
    assert batch % n_seq == 0
    row = lambda w: pl.BlockSpec((tile, w), lambda b, i: (b * q_tiles + i, 0))
    in_specs = [row(QK_WIDTH)]
    args = [q]
    for k, v, t in kvs:
        spec = pl.BlockSpec((n_seq * t, QK_WIDTH), lambda b, i: (b, 0))
        in_specs += [spec, spec]
        args += [k, v]
    return pl.pallas_call(
        functools.partial(_attn_kernel, len(kvs), n_seq),
        out_shape=jax.ShapeDtypeStruct((n, MLA_HEADS * V_DIM), BF16),
        grid=(batch // n_seq, q_tiles),
        in_specs=in_specs,
        out_specs=row(MLA_HEADS * V_DIM),
        compiler_params=_params(2, VMEM_LIMIT),
        name="attention",
    )(*args)


def _back_kernel(x_ref, attn_ref, cv_ref, woa_ref, woc_ref, g1_ref, sh_ref, sc_ref, fg_ref,
                 wrh_ref, wrl_ref, x1_ref, h2_ref, afft_ref):
    mix = _bdot(attn_ref[...], woa_ref[...]) + _bdot(cv_ref[...], woc_ref[...])
    x1 = x_ref[...] + g1_ref[0] * mix
    x1_ref[...] = x1
    h2 = _rms(x1, fg_ref[...]) * (1.0 + sc_ref[0]) + sh_ref[0]
    h2b = h2.astype(BF16)
    h2_ref[:, :D_MODEL] = h2b

    h2l = (h2 - h2b.astype(F32)).astype(BF16)
    wrh = wrh_ref[...]
    logits = _bdot(h2b, wrh) + (_bdot(h2l, wrh) + _bdot(h2b, wrl_ref[...]))
    lane = lax.broadcasted_iota(I32, logits.shape, 1)
    real = lane < N_EXPERTS
    m = jnp.max(jnp.where(real, logits, -jnp.inf), axis=-1, keepdims=True)
    e = jnp.exp(logits - m)
    aff = e / jnp.sum(jnp.where(real, e, 0.0), axis=-1, keepdims=True)
    aff_t = aff.T
    for blk in range(afft_ref.shape[0]):
        afft_ref[blk] = aff_t[:N_EXPERTS, blk * LANES:(blk + 1) * LANES]
    hi = aff.astype(BF16)
    r1 = aff - hi.astype(F32)
    mid = r1.astype(BF16)
    lo = (r1 - mid.astype(F32)).astype(BF16)
    zero = jnp.zeros_like(hi)
    h2_ref[:, D_MODEL:] = jnp.where(
        real, hi, jnp.where(lane < 2 * N_EXPERTS, mid, jnp.where(lane < 3 * N_EXPERTS, lo, zero)))


def _back(x2d, attn, cv, mod3, mod_row, wts, seq):
    n = x2d.shape[0]
    tile = BACK_TILE
    full = lambda a: pl.BlockSpec(a.shape, lambda i: (0,) * a.ndim)

    def mod_spec(col):
        if mod_row is None:
            assert seq % tile == 0
            return pl.BlockSpec((1, 1, D_MODEL), lambda i: (i // (seq // tile), 0, col))
        return pl.BlockSpec((1, 1, D_MODEL), lambda i: (mod_row, 0, col))

    row = lambda w: pl.BlockSpec((tile, w), lambda i: (i, 0))
    woa, woc, fg, wrh, wrl = wts
    in_specs = [row(D_MODEL), row(MLA_HEADS * V_DIM), row(CONV_CH), full(woa), full(woc),
                mod_spec(2), mod_spec(3), mod_spec(4), full(fg), full(wrh), full(wrl)]
    return pl.pallas_call(
        _back_kernel,
        out_shape=[jax.ShapeDtypeStruct((n, D_MODEL), F32),
                   jax.ShapeDtypeStruct((n, XS_W), BF16),
                   jax.ShapeDtypeStruct((n // LANES, N_EXPERTS, LANES), F32)],
        grid=(n // tile,),
        in_specs=in_specs,
        out_specs=[row(D_MODEL), row(XS_W),
                   pl.BlockSpec((tile // LANES, N_EXPERTS, LANES), lambda i: (i, 0, 0))],
        compiler_params=_params(1, VMEM_LIMIT),
        name="back",
    )(x2d, attn, cv, woa, woc, mod3, mod3, mod3, fg, wrh, wrl)


def _route_kernel(caps, *refs):
    n_g = len(caps)
    affs = [r[...] for r in refs[:n_g]]
    pos_refs = refs[n_g:2 * n_g]
    base_refs = refs[2 * n_g:]
    ones = jnp.ones((LANES, LANES), BF16)
    l_r = lax.broadcasted_iota(I32, (LANES, LANES), 0)
    l_c = lax.broadcasted_iota(I32, (LANES, LANES), 1)
    tri = jnp.where(l_r <= l_c, 1.0, 0.0).astype(BF16)

    def lane_dot(mask3, w):
        r, e, _ = mask3.shape
        mb = jnp.where(mask3, 1.0, 0.0).astype(BF16).reshape(r * e, LANES)
        return _bdot(mb, w).reshape(r, e, LANES)

    def expert_count(mask3):
        return jnp.sum(lane_dot(mask3, ones), axis=0, keepdims=True)

    def excl_cumsum(mask3):
        rows = mask3.shape[0]
        tot = lane_dot(mask3, ones)
        run = tot
        s = 1
        while s < rows:
            run = run + jnp.concatenate([jnp.zeros((s,) + run.shape[1:], F32), run[:rows - s]], axis=0)
            s *= 2
        before = run - tot
        return before + lane_dot(mask3, tri) - jnp.where(mask3, 1.0, 0.0), before

    def step(t, thrs):
        bit = lax.shift_left(jnp.int32(1), 30 - t)
        out = []
        for aff, thr, cap in zip(affs, thrs, caps):
            cand = thr | bit
            cnt = expert_count(aff >= pltpu.bitcast(cand, F32))
            out.append(jnp.where(cnt >= cap, cand, thr))
        return tuple(out)

    init = tuple(jnp.zeros((1,) + a.shape[1:], I32) for a in affs)
    thrs = lax.fori_loop(0, 31, step, init)
    for aff, thr_bits, cap, pos_ref, base_ref in zip(affs, thrs, caps, pos_refs, base_refs):
        thr = pltpu.bitcast(thr_bits, F32)
        gt = aff > thr
        eq = aff == thr
        need = cap - expert_count(gt)
        tie_rank, _ = excl_cumsum(eq)
        sel = gt | (eq & (tie_rank < need))
        pos, before = excl_cumsum(sel)
        pos_ref[...] = jnp.where(sel, pos, -1.0).astype(I32)
        base_ref[...] = before


def _route(affs, caps):
    shapes = [a.shape for a in affs]
    return pl.pallas_call(
        functools.partial(_route_kernel, tuple(caps)),
        out_shape=[jax.ShapeDtypeStruct(s, I32) for s in shapes] +
                  [jax.ShapeDtypeStruct(s, F32) for s in shapes],
        compiler_params=pltpu.CompilerParams(vmem_limit_bytes=VMEM_LIMIT),
        name="route",
    )(*affs)


def _window(off_ref, e, i, n_tiles, w, cap):
    del n_tiles
    first = off_ref[i * N_EXPERTS + e] + w * SLOT_WIN
    return first, pl.multiple_of(jnp.minimum(first, cap - SLOT_WIN), BF16_ROWS)


def _one_hot_rows(pos_ref, wins):
    r = lax.broadcasted_iota(I32, (SLOT_WIN, LANES), 0)
    blocks = []
    for e, (first, off) in enumerate(wins):
        halves = []
        for h in range(pos_ref.shape[0]):
            pe = pos_ref[h, e:e + 1, :]
            rel = jnp.where(pe >= first, pe - off, -1)
            halves.append(jnp.where(rel == r, 1.0, 0.0).astype(BF16))
        blocks.append(jnp.concatenate(halves, axis=1))
    return jnp.concatenate(blocks, axis=0)


def _dispatch_kernel(cap, n_tiles, off_ref, nwin_ref, h2_ref, pos_ref, xs_hbm, xs_ref, sem):
    i = pl.program_id(0)

    @pl.when(i == 0)
    def _():
        def zero(e, carry):
            xs_ref[e] = jnp.zeros(xs_ref.shape[1:], BF16)
            return carry
        lax.fori_loop(0, N_EXPERTS, zero, 0)

    def window(w, carry):
        wins = [_window(off_ref, e, i, n_tiles, w, cap) for e in range(N_EXPERTS)]
        p = _one_hot_rows(pos_ref, wins)
        h2 = h2_ref[...].reshape(ROW_TILE, XS_W)
        slab = _bdot(p, h2)
        for e, (_, off) in enumerate(wins):
            cur = xs_ref[e, pl.ds(off, SLOT_WIN), :].astype(F32)
            xs_ref[e, pl.ds(off, SLOT_WIN), :] = (
                cur + slab[e * SLOT_WIN:(e + 1) * SLOT_WIN]).astype(BF16)
        return carry

    lax.fori_loop(0, nwin_ref[i], window, 0)

    @pl.when(i == n_tiles - 1)
    def _():
        cp = pltpu.make_async_copy(xs_ref, xs_hbm, sem)
        cp.start()
        cp.wait()


def _tile_spec(stripes, width):
    return pl.BlockSpec((stripes, ROW_TILE // stripes, width), lambda i, *_: (0, i, 0))


def _dispatch(h2aug, pos, offs, nwin, cap, stripes):
    n = h2aug.shape[0]
    n_tiles = n // ROW_TILE
    return pl.pallas_call(
        functools.partial(_dispatch_kernel, cap, n_tiles),
        out_shape=jax.ShapeDtypeStruct((N_EXPERTS, cap, XS_W), BF16),
        grid_spec=pltpu.PrefetchScalarGridSpec(
            num_scalar_prefetch=2,
            grid=(n_tiles,),
            in_specs=[_tile_spec(stripes, XS_W),
                      pl.BlockSpec((ROW_TILE // LANES, N_EXPERTS, LANES), lambda i, *_: (i, 0, 0))],
            out_specs=pl.BlockSpec(memory_space=pl.ANY),
            scratch_shapes=[pltpu.VMEM((N_EXPERTS, cap, XS_W), BF16),
                            pltpu.SemaphoreType.DMA]),
        compiler_params=_params(1, VMEM_LIMIT),
        name="dispatch",
    )(offs, nwin, h2aug.reshape(stripes, n // stripes, XS_W), pos)


FF_CHUNK = 512
EXP_ROWS = 512


def _experts_kernel(n_groups, n_chunks, *refs):
    xs_refs = refs[:n_groups]
    wg_ref, wu_ref, wd_ref = refs[n_groups:n_groups + 3]
    ys_refs = refs[n_groups + 3:2 * n_groups + 3]
    acc_ref = refs[2 * n_groups + 3]
    e = pl.program_id(0)
    j = pl.program_id(1)
    cap = xs_refs[0].shape[1]

    @pl.when(j == 0)
    def _():
        for g in range(n_groups):
            acc_ref[g] = jnp.zeros(acc_ref.shape[1:], F32)

    wg = wg_ref[0].astype(BF16)
    wu = wu_ref[0].astype(BF16)
    wd = wd_ref[0].astype(BF16)
    blocks = [(g, slice(r * EXP_ROWS, (r + 1) * EXP_ROWS))
              for g in range(n_groups) for r in range(cap // EXP_ROWS)]

    def up(blk):
        g, rows = blk
        x = xs_refs[g][0, rows, :D_MODEL]
        return _bdot(x, wg), _bdot(x, wu)

    nxt = up(blocks[0])
    for k, (g, rows) in enumerate(blocks):
        a, u = nxt
        if k + 1 < len(blocks):
            nxt = up(blocks[k + 1])
        hm = (a * jax.nn.sigmoid(a) * u).astype(BF16)
        acc_ref[g, rows, :] += _bdot(hm, wd)

    @pl.when(j == n_chunks - 1)
    def _():
        for g, (xs_ref, ys_ref) in enumerate(zip(xs_refs, ys_refs)):
            aug = xs_ref[0, :, D_MODEL:].astype(F32)
            lane = lax.broadcasted_iota(I32, aug.shape, 1)
            mine = (lane == e) | (lane == e + N_EXPERTS) | (lane == e + 2 * N_EXPERTS)
            gate = jnp.sum(jnp.where(mine, aug, 0.0), axis=-1, keepdims=True)
            ys_ref[0] = (acc_ref[g] * gate).astype(BF16)


def _experts(xs_list, w_gate, w_up, w_down):
    n_groups = len(xs_list)
    cap = xs_list[0].shape[1]
    ff = w_gate.shape[2]
    n_chunks = ff // FF_CHUNK
    xs_spec = pl.BlockSpec((1, cap, XS_W), lambda e, j: (e, 0, 0))
    up_spec = pl.BlockSpec((1, D_MODEL, FF_CHUNK), lambda e, j: (e, 0, j))
    down_spec = pl.BlockSpec((1, FF_CHUNK, D_MODEL), lambda e, j: (e, j, 0))
    ys_spec = pl.BlockSpec((1, cap, D_MODEL), lambda e, j: (e, 0, 0))
    return pl.pallas_call(
        functools.partial(_experts_kernel, n_groups, n_chunks),
        out_shape=[jax.ShapeDtypeStruct((N_EXPERTS, cap, D_MODEL), BF16)] * n_groups,
        grid=(N_EXPERTS, n_chunks),
        in_specs=[xs_spec] * n_groups + [up_spec, up_spec, down_spec],
        out_specs=[ys_spec] * n_groups,
        scratch_shapes=[pltpu.VMEM((n_groups, cap, D_MODEL), F32)],
        compiler_params=_params(2, VMEM_LIMIT),
        name="experts",
    )(*xs_list, w_gate, w_up, w_down)


def _combine_kernel(cap, n_tiles, off_ref, nwin_ref, ys_hbm, pos_ref, x1_ref, g2_ref, fg_ref,
                    y_ref, buf_ref, sems):
    i = pl.program_id(0)
    slot = i % 2

    def copies(tile, w, s):
        out = []
        for e in range(N_EXPERTS):
            _, off = _window(off_ref, e, tile, n_tiles, w, cap)
            out.append(pltpu.make_async_copy(
                ys_hbm.at[e, pl.ds(off, SLOT_WIN), :],
                buf_ref.at[s, pl.ds(e * SLOT_WIN, SLOT_WIN), :],
                sems.at[s]))
        return out

    @pl.when(i == 0)
    def _():
        for cp in copies(0, 0, 0):
            cp.start()

    @pl.when(i + 1 < n_tiles)
    def _():
        for cp in copies(i + 1, 0, 1 - slot):
            cp.start()

    tn = (((0,), (0,)), ((), ()))

    def gathered(w):
        wins = [_window(off_ref, e, i, n_tiles, w, cap) for e in range(N_EXPERTS)]
        p = _one_hot_rows(pos_ref, wins)
        for cp in copies(i, w, slot):
            cp.wait()
        return lax.dot_general(p, buf_ref[slot], tn, preferred_element_type=F32)

    def extra_window(w, acc):
        for cp in copies(i, w, slot):
            cp.start()
        return acc + gathered(w)

    moe = lax.fori_loop(1, nwin_ref[i], extra_window, gathered(0))
    out = x1_ref[...] + g2_ref[...] * moe.reshape(x1_ref.shape)
    y_ref[...] = _rms(out, fg_ref[...])


def _combine(ys, pos, offs, nwin, x1, mod3, mod_row, stripes, final_g, cap):
    n = x1.shape[0]
    n_tiles = n // ROW_TILE
    g2 = pl.BlockSpec((stripes, 1, D_MODEL), lambda i, *_: (0 if mod_row is None else mod_row, 0, 5))
    y = pl.pallas_call(
        functools.partial(_combine_kernel, cap, n_tiles),
        out_shape=jax.ShapeDtypeStruct((stripes, n // stripes, D_MODEL), F32),
        grid_spec=pltpu.PrefetchScalarGridSpec(
            num_scalar_prefetch=2,
            grid=(n_tiles,),
            in_specs=[pl.BlockSpec(memory_space=pl.ANY),
                      pl.BlockSpec((ROW_TILE // LANES, N_EXPERTS, LANES), lambda i, *_: (i, 0, 0)),
                      _tile_spec(stripes, D_MODEL),
                      g2,
                      pl.BlockSpec((1, D_MODEL), lambda i, *_: (0, 0))],
            out_specs=_tile_spec(stripes, D_MODEL),
            scratch_shapes=[pltpu.VMEM((2, N_EXPERTS * SLOT_WIN, D_MODEL), BF16),
                            pltpu.SemaphoreType.DMA((2,))]),
        compiler_params=_params(1, VMEM_LIMIT),
        name="combine",
    )(offs, nwin, ys, pos, x1.reshape(stripes, n // stripes, D_MODEL), mod3, final_g)
    return y.reshape(n, D_MODEL)


def _head_cols(w3):
    k, _, d = w3.shape
    return jnp.pad(w3, ((0, 0), (0, 0), (0, HEAD_PAD - d))).reshape(k, QK_WIDTH)


def _value_cols(v3):
    k = v3.shape[0]
    pair = v3.reshape(k, MLA_HEADS // 2, 2, V_DIM)
    zeros = jnp.zeros((k, MLA_HEADS // 2, V_DIM), v3.dtype)
    even = jnp.concatenate([pair[:, :, 0], zeros], axis=-1)
    odd = jnp.concatenate([zeros, pair[:, :, 1]], axis=-1)
    return jnp.stack([even, odd], axis=2).reshape(k, QK_WIDTH)


def _rope_tables(seq):
    half = QK_ROPE // 2
    pos = jnp.arange(seq)
    inv_freq = 1.0 / (ROPE_BASE ** (jnp.arange(0, half, 2, dtype=F32) / half))

    def cs(p):
        ang = p.astype(F32)[:, None] * inv_freq[None, :]
        ang = jnp.concatenate([ang, ang], axis=-1)
        return jnp.cos(ang), jnp.sin(ang)

    cr, sr = cs(pos // GRID_W)
    cc, sc = cs(pos % GRID_W)
    ones = jnp.ones((seq, QK_NOPE), F32)
    zpad = jnp.zeros((seq, HEAD_PAD - QK_NOPE - QK_ROPE), F32)
    cq = jnp.concatenate([ones, cr, cc, zpad], axis=-1)
    sq = jnp.concatenate([0.0 * ones, sr, sc, zpad], axis=-1)
    return cq, sq


def _prep_weights(w_in, q_norm_g, w_q_up, kv_norm_g, w_kv_up, attn_norm_g):
    wqc = w_in[:, :Q_RANK + KV_RANK].astype(BF16)
    wkr = jnp.pad(w_in[:, Q_RANK + KV_RANK:Q_RANK + KV_RANK + QK_ROPE],
                  ((0, 0), (0, LANES - QK_ROPE))).astype(BF16)
    wglu = w_in[:, Q_RANK + KV_RANK + QK_ROPE:].astype(BF16)
    wq = _head_cols(w_q_up.astype(BF16).reshape(Q_RANK, MLA_HEADS, QK_NOPE + QK_ROPE))
    kv3 = w_kv_up.astype(BF16).reshape(KV_RANK, MLA_HEADS, QK_NOPE + V_DIM)
    wk = _head_cols(kv3[:, :, :QK_NOPE])
    wv = _value_cols(kv3[:, :, QK_NOPE:])
    pk = np.zeros((LANES, QK_WIDTH), np.float32)
    for hd in range(MLA_HEADS):
        for j in range(QK_ROPE):
            pk[j, hd * HEAD_PAD + QK_NOPE + j] = 1.0
    pk = jnp.asarray(pk, BF16)
    return (attn_norm_g[None, :], wqc, wkr, wglu, q_norm_g[None, :], wq,
            kv_norm_g[None, :], wk, wv, pk)


def _stripe_order(aff3, stripes, seq):
    per = ROW_TILE // stripes
    q = LANES // per
    assert per * stripes == ROW_TILE and per * q == LANES and stripes % q == 0 and seq % LANES == 0
    a = aff3.reshape(stripes // q, q, seq // LANES, N_EXPERTS, q, per)
    return a.transpose(2, 4, 0, 3, 1, 5).reshape(aff3.shape)


def _route_plan(base, cap):
    rows_per_tile = ROW_TILE // LANES
    start = base[::rows_per_tile, :, 0].astype(I32)
    end = jnp.concatenate([start[1:], jnp.full((1, N_EXPERTS), cap, I32)], axis=0)
    off = (start // BF16_ROWS) * BF16_ROWS
    span = end - off
    nwin = jnp.maximum(jnp.max((span + SLOT_WIN - 1) // SLOT_WIN, axis=1), 1).astype(I32)
    return off.reshape(-1), nwin


def kernel(x_prompt, x_sample, cache_ckv, cache_krope, c, c_ctx, w_mod, b_mod, attn_norm_g,
           w_in, q_norm_g, w_q_up, kv_norm_g, w_kv_up, conv_w, conv_b, conv_ln_g, conv_ln_b,
           w_out, ffn_norm_g, w_router, w_gate, w_up, w_down, final_norm_g):
    depth = w_mod.shape[0]
    assert depth == 1
    bp, sp, _ = x_prompt.shape
    bs, ss, _ = x_sample.shape
    past = cache_ckv.shape[2]
    ctx_row = bs

    mod_rows = 16
    cc = jnp.concatenate([c, c_ctx[None, :], jnp.zeros((mod_rows - bs - 1, D_MODEL), F32)], axis=0)
    mod = _adaln(cc, w_mod[0], b_mod[0][None, :])
    mod3 = mod.reshape(mod_rows, 1, N_MOD * D_MODEL)

    fw = _prep_weights(w_in[0], q_norm_g[0], w_q_up[0], kv_norm_g[0], w_kv_up[0], attn_norm_g[0])
    wk, wv, pk = fw[7], fw[8], fw[9]
    wo = w_out[0].astype(BF16)
    wr = jnp.pad(jnp.tile(w_router[0], (1, 3)), ((0, 0), (0, LANES - 3 * N_EXPERTS)))
    wrh = wr.astype(BF16)
    wrl = (wr - wrh.astype(F32)).astype(BF16)
    cvw = (conv_w[0], conv_b[0][None, :], conv_ln_g[0][None, :], conv_ln_b[0][None, :])
    bw = (wo[:MLA_HEADS * V_DIM], wo[MLA_HEADS * V_DIM:], ffn_norm_g[0][None, :], wrh, wrl)

    xp2 = x_prompt.reshape(bp * sp, D_MODEL)
    xs2 = x_sample.reshape(bs * ss, D_MODEL)

    qp, kp, vp, cv_p, ckvp, krp = _front(xp2, mod3, ctx_row, sp, fw, cvw, None)
    attn_p = _attention(qp, [(kp, vp, sp)], bp, sp)
    x1p, h2p, afftp = _back(xp2, attn_p, cv_p, mod3, ctx_row, bw, sp)

    qs, ks, vs, cv_s = _front(xs2, mod3, None, ss, fw, cvw, _rope_tables(ss))
    kc, vc = _ctx_kv(cache_ckv[:, 0].reshape(bs * past, KV_RANK),
                     cache_krope[:, 0].reshape(bs * past, QK_ROPE), wk, wv, pk)
    attn_s = _attention(qs, [(kc, vc, past), (ks, vs, ss)], bs, ss)
    x1s, h2s, affts = _back(xs2, attn_s, cv_s, mod3, None, bw, ss)

    stripes = (1, bs)
    caps = [EC_FACTOR * h2.shape[0] // N_EXPERTS for h2 in (h2p, h2s)]
    routed = _route([afftp, _stripe_order(affts, bs, ss)], caps)
    plans = []
    xs_list = []
    for g, (h2, cap) in enumerate(zip((h2p, h2s), caps)):
        pos, base = routed[g], routed[len(caps) + g]
        offs, nwin = _route_plan(base, cap)
        plans.append((pos, offs, nwin, cap))
        xs_list.append(_dispatch(h2, pos, offs, nwin, cap, stripes[g]))
    ys_list = _experts(xs_list, w_gate[0], w_up[0], w_down[0])

    fg = final_norm_g[None, :]
    pos, offs, nwin, cap = plans[0]
    y_prompt = _combine(ys_list[0], pos, offs, nwin, x1p, mod3, ctx_row, stripes[0], fg, cap)
    pos, offs, nwin, cap = plans[1]
    y_sample = _combine(ys_list[1], pos, offs, nwin, x1s, mod3, None, stripes[1], fg, cap)

    return (y_prompt.reshape(bp, sp, D_MODEL), y_sample.reshape(bs, ss, D_MODEL),
            ckvp.reshape(bp, depth, sp, KV_RANK), krp.reshape(bp, depth, sp, QK_ROPE))
```

```python
import functools

import jax
import jax.numpy as jnp
import numpy as np
from jax import lax
from jax.experimental import pallas as pl
from jax.experimental.pallas import tpu as pltpu

F32 = jnp.float32
BF16 = jnp.bfloat16
I32 = jnp.int32

LANES = 128
BF16_ROWS = 16
VMEM_LIMIT = 56 * 1024 * 1024

D_MODEL = 1024
GRID_W = 64
MLA_HEADS = 8
QK_NOPE = 64
QK_ROPE = 32
V_DIM = 64
Q_RANK = 768
KV_RANK = 256
CONV_CH = 512
CONV_WIDTH = 31
N_EXPERTS = 16
EXPERT_FF = 1024
EC_FACTOR = 2
ROPE_BASE = 10000.0
EPS = 1e-6
N_MOD = 6

HEAD_PAD = LANES
QK_WIDTH = MLA_HEADS * HEAD_PAD
ROW_TILE = 256
FRONT_TILE = 512
ATTN_TILE = 512
BACK_TILE = 1024
BACK_SUB = 256
SLOT_WIN = 64
AUG = LANES
XS_W = D_MODEL + AUG
LOG2_E = 1.4426950408889634


def _params(n_axes, vmem=None):
    return pltpu.CompilerParams(
        dimension_semantics=("arbitrary",) * n_axes,
        vmem_limit_bytes=vmem)


def _rms(x, g):
    ms = jnp.mean(x * x, axis=-1, keepdims=True)
    return x * lax.rsqrt(ms + EPS) * g


def _bdot(a, b):
    return jnp.dot(a, b, preferred_element_type=F32)


def _adaln_kernel(c_ref, w_ref, b_ref, o_ref):
    c = c_ref[...]
    s = c * jax.nn.sigmoid(c)
    w = w_ref[...]
    sh = s.astype(BF16)
    sl = (s - sh.astype(F32)).astype(BF16)
    wh = w.astype(BF16)
    wl = (w - wh.astype(F32)).astype(BF16)
    o_ref[...] = _bdot(sh, wh) + (_bdot(sl, wh) + _bdot(sh, wl)) + b_ref[...]


def _adaln(cc, w_mod, b_mod):
    rows = cc.shape[0]
    n_out = w_mod.shape[1]
    blk = D_MODEL
    return pl.pallas_call(
        _adaln_kernel,
        out_shape=jax.ShapeDtypeStruct((rows, n_out), F32),
        grid=(n_out // blk,),
        in_specs=[pl.BlockSpec((rows, D_MODEL), lambda j: (0, 0)),
                  pl.BlockSpec((D_MODEL, blk), lambda j: (0, j)),
                  pl.BlockSpec((1, blk), lambda j: (0, j))],
        out_specs=pl.BlockSpec((rows, blk), lambda j: (0, j)),
        compiler_params=_params(1),
        name="adaln",
    )(cc, w_mod, b_mod)


def _rot_half(x, lane, base):
    rel = lane - base
    first = ((rel >= 0) & (rel < 8)) | ((rel >= 16) & (rel < 24))
    return jnp.where(first, -pltpu.roll(x, LANES - 8, axis=1), pltpu.roll(x, 8, axis=1))


def _front_kernel(rope, seq, x_ref, sh_ref, sc_ref, g_ref, wqc_ref, wkr_ref, wglu_ref,
                  qg_ref, wq_ref, kvg_ref, wk_ref, wv_ref, pk_ref, cw_ref, cb_ref, lg_ref, lb_ref,
                  *rest):
    if rope:
        cq_ref, sq_ref, q_ref, k_ref, v_ref, cv_ref, vpad_ref, shift_ref = rest
    else:
        q_ref, k_ref, v_ref, cv_ref, ckv_ref, kr_ref, vpad_ref, shift_ref = rest
    rows = x_ref.shape[0]
    steps_per_seq = max(seq // rows, 1)
    scale = (QK_NOPE + QK_ROPE) ** -0.5 * LOG2_E
    lane = lax.broadcasted_iota(I32, (ROW_TILE, LANES), 1)
    conv_refs = (cw_ref, cb_ref, lg_ref, lb_ref, shift_ref)

    pending = []

    def tick(n=1):
        for _ in range(n):
            if pending:
                pending.pop(0)()

    def gated_values(qd, row_in_seq, slot, cv_row):
        r = slice(qd * ROW_TILE, (qd + 1) * ROW_TILE)
        h = _rms(x_ref[r, :], g_ref[...]) * (1.0 + sc_ref[0]) + sh_ref[0]
        hb = h.astype(BF16)
        glu = _bdot(hb, wglu_ref[...])
        if row_in_seq == 0:
            vpad_ref[slot, 0:CONV_HALO, :] = jnp.zeros((CONV_HALO, CONV_CH), F32)
        if row_in_seq + ROW_TILE == seq:
            vpad_ref[slot, CONV_HALO + seq:, :] = jnp.zeros((CONV_HALO, CONV_CH), F32)
        vpad_ref[slot, CONV_HALO + row_in_seq:CONV_HALO + row_in_seq + ROW_TILE, :] = (
            glu[:, :CONV_CH] * jax.nn.sigmoid(glu[:, CONV_CH:]))

        ready = ([row_in_seq - ROW_TILE] if row_in_seq > 0 else []) + \
                ([row_in_seq] if row_in_seq + ROW_TILE == seq else [])
        for t0 in ready:
            steps, finish = _conv_stages(CONV_STAGES, vpad_ref, slot, t0, *conv_refs)
            dst = slice(cv_row + t0, cv_row + t0 + ROW_TILE)
            pending.extend(steps + [functools.partial(_store_conv, cv_ref, dst, finish)])
        return hb

    def projections(qd, row_in_seq, hb, ticks):
        r = slice(qd * ROW_TILE, (qd + 1) * ROW_TILE)
        qc = _bdot(hb, wqc_ref[...])
        tick(ticks)
        kr = _bdot(hb, wkr_ref[...])
        tick(ticks)
        qn = _rms(qc[:, :Q_RANK], qg_ref[...]).astype(BF16)
        q = _bdot(qn, wq_ref[...])
        tick(ticks)
        ckv = _rms(qc[:, Q_RANK:], kvg_ref[...])
        ckv_b = ckv.astype(BF16)
        kk = _bdot(ckv_b, wk_ref[...])
        tick(ticks)
        v_ref[r, :] = _bdot(ckv_b, wv_ref[...]).astype(BF16)
        tick(ticks)
        if rope:
            t = slice(row_in_seq % rows, row_in_seq % rows + ROW_TILE)
            cq = cq_ref[t, :]
            sq = sq_ref[t, :]
            for hd in range(MLA_HEADS):
                blk = q[:, hd * HEAD_PAD:(hd + 1) * HEAD_PAD]
                rot = blk * cq + _rot_half(blk, lane, QK_NOPE) * sq
                q_ref[r, hd * HEAD_PAD:(hd + 1) * HEAD_PAD] = (rot * scale).astype(BF16)
            krs = pltpu.roll(kr, QK_NOPE, axis=1)
            kr = pltpu.roll(krs * cq + _rot_half(krs, lane, QK_NOPE) * sq, LANES - QK_NOPE, axis=1)
            kr = jnp.where(lane < QK_ROPE, kr, 0.0)
        else:
            q_ref[r, :] = (q * scale).astype(BF16)
            ckv_ref[r, :] = ckv
            kr_ref[r, :] = kr[:, :QK_ROPE]
        tick(ticks)
        k_ref[r, :] = (kk + _bdot(kr.astype(BF16), pk_ref[...])).astype(BF16)

    def step(k):
        n_q = rows // ROW_TILE
        hbs = []
        for qd in range(n_q):
            row = k * rows + qd * ROW_TILE
            if seq >= rows:
                hbs.append((row, gated_values(qd, row, 0, 0)))
            else:
                hbs.append((row % seq, gated_values(qd, row % seq, row // seq, (row // seq) * seq)))
        ticks = -(-len(pending) // (6 * n_q))
        for qd, (row_in_seq, hb) in enumerate(hbs):
            projections(qd, row_in_seq, hb, ticks)
        tick(len(pending))

    if steps_per_seq == 1:
        step(0)
    else:
        for k in range(steps_per_seq):
            pl.when(pl.program_id(0) % steps_per_seq == k)(functools.partial(step, k))


def _store_conv(cv_ref, dst, finish):
    cv_ref[dst, :] = finish()


def _front(x2d, mod3, mod_row, seq, wts, conv_wts, rope_tabs):
    n = x2d.shape[0]
    rope = rope_tabs is not None
    tile = FRONT_TILE
    steps_per_seq = max(seq // tile, 1)
    assert seq % ROW_TILE == 0 and (tile % seq == 0 or seq % tile == 0)
    full = lambda a: pl.BlockSpec(a.shape, lambda i: (0,) * a.ndim)

    def mod_spec(col):
        if mod_row is None:
            assert seq % tile == 0
            return pl.BlockSpec((1, 1, D_MODEL), lambda i: (i // steps_per_seq, 0, col))
        return pl.BlockSpec((1, 1, D_MODEL), lambda i: (mod_row, 0, col))

    row = lambda w: pl.BlockSpec((tile, w), lambda i: (i, 0))
    cv_rows = max(seq, tile)
    cv_spec = pl.BlockSpec((cv_rows, CONV_CH), lambda i: (i // steps_per_seq, 0))
    in_specs = [row(D_MODEL), mod_spec(0), mod_spec(1)] + [full(w) for w in wts + conv_wts]
    args = [x2d, mod3, mod3] + list(wts + conv_wts)
    out_shape = [jax.ShapeDtypeStruct((n, QK_WIDTH), BF16)] * 3 + \
                [jax.ShapeDtypeStruct((n, CONV_CH), BF16)]
    out_specs = [row(QK_WIDTH)] * 3 + [cv_spec]
    if rope:
        tab = pl.BlockSpec((tile, LANES), lambda i: (i % steps_per_seq, 0))
        in_specs += [tab, tab]
        args += list(rope_tabs)
    else:
        out_shape += [jax.ShapeDtypeStruct((n, KV_RANK), F32),
                      jax.ShapeDtypeStruct((n, QK_ROPE), F32)]
        out_specs += [row(KV_RANK), row(QK_ROPE)]
    slots = max(tile // seq, 1)
    return pl.pallas_call(
        functools.partial(_front_kernel, rope, seq),
        out_shape=out_shape,
        grid=(n // tile,),
        in_specs=in_specs,
        out_specs=out_specs,
        scratch_shapes=[pltpu.VMEM((slots, seq + 2 * CONV_HALO, CONV_CH), F32),
                        pltpu.VMEM((SUBLANES - 1, SHIFT_ROWS, CONV_CH), F32)],
        compiler_params=_params(1, VMEM_LIMIT),
        name="front_rope" if rope else "front",
    )(*args)


def _ctx_kernel(ckv_ref, kr_ref, wk_ref, wv_ref, pk_ref, k_ref, v_ref):
    cb = ckv_ref[...].astype(BF16)
    kr = kr_ref[...].astype(BF16)
    k_ref[...] = (_bdot(cb, wk_ref[...]) + _bdot(kr, pk_ref[...])).astype(BF16)
    v_ref[...] = _bdot(cb, wv_ref[...]).astype(BF16)


def _ctx_kv(ckv2d, kr2d, wk, wv, pk):
    n = ckv2d.shape[0]
    full = lambda a: pl.BlockSpec(a.shape, lambda i: (0,) * a.ndim)
    row = lambda w: pl.BlockSpec((ROW_TILE, w), lambda i: (i, 0))
    pk32 = pk[:QK_ROPE]
    return pl.pallas_call(
        _ctx_kernel,
        out_shape=[jax.ShapeDtypeStruct((n, QK_WIDTH), BF16)] * 2,
        grid=(n // ROW_TILE,),
        in_specs=[row(KV_RANK), row(QK_ROPE), full(wk), full(wv), full(pk32)],
        out_specs=[row(QK_WIDTH)] * 2,
        compiler_params=_params(1),
        name="ctx_kv",
    )(ckv2d, kr2d, wk, wv, pk32)


CONV_PAD = (CONV_WIDTH - 1) // 2
CONV_HALO = 16
SUBLANES = 8
SHIFT_ROWS = ROW_TILE + 2 * CONV_HALO - SUBLANES


CONV_STAGES = 6


def _conv_stages(n_stages, vpad_ref, slot, t0, cw_ref, cb_ref, lg_ref, lb_ref, shift_ref):
    state = {}

    def taps(lo, hi):
        if lo == 0:
            win = vpad_ref[slot, t0:t0 + ROW_TILE + 2 * CONV_HALO, :]
            for ph in range(1, SUBLANES):
                shift_ref[ph - 1] = win[ph:ph + SHIFT_ROWS, :]
            state["acc"] = jnp.broadcast_to(cb_ref[...], (ROW_TILE, CONV_CH))
        acc = state["acc"]
        for t in range(lo, hi):
            blk, ph = divmod(t + CONV_HALO - CONV_PAD, SUBLANES)
            if ph == 0:
                tap = vpad_ref[slot, t0 + blk * SUBLANES:t0 + blk * SUBLANES + ROW_TILE, :]
            else:
                tap = shift_ref[ph - 1, blk * SUBLANES:blk * SUBLANES + ROW_TILE, :]
            acc = acc + tap * cw_ref[t:t + 1, :]
        state["acc"] = acc

    def finish():
        acc = state["acc"]
        mu = jnp.mean(acc, axis=-1, keepdims=True)
        cen = acc - mu
        var = jnp.mean(cen * cen, axis=-1, keepdims=True)
        ln = cen * lax.rsqrt(var + EPS) * lg_ref[...] + lb_ref[...]
        return (ln * jax.nn.sigmoid(ln)).astype(BF16)

    bounds = [CONV_WIDTH * s // n_stages for s in range(n_stages + 1)]
    stages = [functools.partial(taps, bounds[s], bounds[s + 1]) for s in range(n_stages)]
    return stages, finish


def _attn_kernel(n_kv, n_seq, q_ref, *rest):
    kv_refs = rest[:2 * n_kv]
    o_ref = rest[2 * n_kv]
    nt = (((1,), (1,)), ((), ()))
    tq = q_ref.shape[0] // n_seq
    work = [(b, hd) for b in range(n_seq) for hd in range(MLA_HEADS)]

    def kv_rows(ref, b):
        t = ref.shape[0] // n_seq
        return slice(b * t, (b + 1) * t)

    def scores(item):
        b, hd = item
        sl = slice(hd * HEAD_PAD, (hd + 1) * HEAD_PAD)
        qh = q_ref[b * tq:(b + 1) * tq, sl]
        return [lax.dot_general(qh, kv_refs[2 * j][kv_rows(kv_refs[2 * j], b), sl], nt,
                                preferred_element_type=F32) for j in range(n_kv)]

    outs = {}
    nxt = scores(work[0])
    for w, (b, hd) in enumerate(work):
        sl = slice(hd * HEAD_PAD, (hd + 1) * HEAD_PAD)
        ss = nxt
        if w + 1 < len(work):
            nxt = scores(work[w + 1])
        m = ss[0].max(axis=-1, keepdims=True)
        for s in ss[1:]:
            m = jnp.maximum(m, s.max(axis=-1, keepdims=True))
        ps = [jnp.exp2(s - m) for s in ss]
        l = ps[0].sum(axis=-1, keepdims=True)
        for p in ps[1:]:
            l = l + p.sum(axis=-1, keepdims=True)
        o = None
        for j in range(n_kv):
            v_ref = kv_refs[2 * j + 1]
            pv = _bdot(ps[j].astype(BF16), v_ref[kv_rows(v_ref, b), sl])
            o = pv if o is None else o + pv
        outs[hd] = o / l
        if hd % 2 == 1:
            o_ref[b * tq:(b + 1) * tq, (hd // 2) * LANES:(hd // 2 + 1) * LANES] = (
                outs[hd - 1] + outs[hd]).astype(BF16)


def _attention(q, kvs, batch, seq):
    n = q.shape[0]
    n_seq = max(ATTN_TILE // seq, 1)
    tile = min(seq, ATTN_TILE) * n_seq
    q_tiles = max(seq // tile, 1)
    assert batch % n_seq == 0
    row = lambda w: pl.BlockSpec((tile, w), lambda b, i: (b * q_tiles + i, 0))
    in_specs = [row(QK_WIDTH)]
    args = [q]
    for k, v, t in kvs:
        spec = pl.BlockSpec((n_seq * t, QK_WIDTH), lambda b, i: (b, 0))
        in_specs += [spec, spec]
        args += [k, v]
    return pl.pallas_call(
        functools.partial(_attn_kernel, len(kvs), n_seq),
        out_shape=jax.ShapeDtypeStruct((n, MLA_HEADS * V_DIM), BF16),
        grid=(batch // n_seq, q_tiles),
        in_specs=in_specs,
        out_specs=row(MLA_HEADS * V_DIM),
        compiler_params=_params(2, VMEM_LIMIT),
        name="attention",
    )(*args)


def _back_kernel(x_ref, attn_ref, cv_ref, woa_ref, woc_ref, g1_ref, sh_ref, sc_ref, fg_ref,
                 wrh_ref, wrl_ref, x1_ref, h2_ref, afft_ref):
    n_sub = x_ref.shape[0] // BACK_SUB
    rows = [slice(s * BACK_SUB, (s + 1) * BACK_SUB) for s in range(n_sub)]

    def project(r):
        return _bdot(attn_ref[r, :], woa_ref[...]) + _bdot(cv_ref[r, :], woc_ref[...])

    def normalise(r, mix):
        x1 = x_ref[r, :] + g1_ref[0] * mix
        x1_ref[r, :] = x1
        h2 = _rms(x1, fg_ref[...]) * (1.0 + sc_ref[0]) + sh_ref[0]
        h2b = h2.astype(BF16)
        h2_ref[r, :D_MODEL] = h2b
        h2l = (h2 - h2b.astype(F32)).astype(BF16)
        wrh = wrh_ref[...]
        return _bdot(h2b, wrh) + (_bdot(h2l, wrh) + _bdot(h2b, wrl_ref[...]))

    def route(s, logits):
        r = rows[s]
        lane = lax.broadcasted_iota(I32, logits.shape, 1)
        real = lane < N_EXPERTS
        m = jnp.max(jnp.where(real, logits, -jnp.inf), axis=-1, keepdims=True)
        e = jnp.exp(logits - m)
        aff = e / jnp.sum(jnp.where(real, e, 0.0), axis=-1, keepdims=True)
        aff_t = aff.T
        for blk in range(BACK_SUB // LANES):
            afft_ref[s * (BACK_SUB // LANES) + blk] = aff_t[:N_EXPERTS, blk * LANES:(blk + 1) * LANES]
        hi = aff.astype(BF16)
        r1 = aff - hi.astype(F32)
        mid = r1.astype(BF16)
        lo = (r1 - mid.astype(F32)).astype(BF16)
        zero = jnp.zeros_like(hi)
        h2_ref[r, D_MODEL:] = jnp.where(
            real, hi, jnp.where(lane < 2 * N_EXPERTS, mid, jnp.where(lane < 3 * N_EXPERTS, lo, zero)))

    mixes = {0: project(rows[0])}
    logits = {}
    for s in range(n_sub):
        if s + 1 < n_sub:
            mixes[s + 1] = project(rows[s + 1])
        logits[s] = normalise(rows[s], mixes.pop(s))
        if s > 0:
            route(s - 1, logits.pop(s - 1))
    route(n_sub - 1, logits.pop(n_sub - 1))


def _back(x2d, attn, cv, mod3, mod_row, wts, seq):
    n = x2d.shape[0]
    tile = BACK_TILE
    full = lambda a: pl.BlockSpec(a.shape, lambda i: (0,) * a.ndim)

    def mod_spec(col):
        if mod_row is None:
            assert seq % tile == 0
            return pl.BlockSpec((1, 1, D_MODEL), lambda i: (i // (seq // tile), 0, col))
        return pl.BlockSpec((1, 1, D_MODEL), lambda i: (mod_row, 0, col))

    row = lambda w: pl.BlockSpec((tile, w), lambda i: (i, 0))
    woa, woc, fg, wrh, wrl = wts
    in_specs = [row(D_MODEL), row(MLA_HEADS * V_DIM), row(CONV_CH), full(woa), full(woc),
                mod_spec(2), mod_spec(3), mod_spec(4), full(fg), full(wrh), full(wrl)]
    return pl.pallas_call(
        _back_kernel,
        out_shape=[jax.ShapeDtypeStruct((n, D_MODEL), F32),
                   jax.ShapeDtypeStruct((n, XS_W), BF16),
                   jax.ShapeDtypeStruct((n // LANES, N_EXPERTS, LANES), F32)],
        grid=(n // tile,),
        in_specs=in_specs,
        out_specs=[row(D_MODEL), row(XS_W),
                   pl.BlockSpec((tile // LANES, N_EXPERTS, LANES), lambda i: (i, 0, 0))],
        compiler_params=_params(1, VMEM_LIMIT),
        name="back",
    )(x2d, attn, cv, woa, woc, mod3, mod3, mod3, fg, wrh, wrl)


def _route_kernel(caps, *refs):
    n_g = len(caps)
    affs = [r[...] for r in refs[:n_g]]
    pos_refs = refs[n_g:2 * n_g]
    base_refs = refs[2 * n_g:]
    ones = jnp.ones((LANES, LANES), BF16)
    l_r = lax.broadcasted_iota(I32, (LANES, LANES), 0)
    l_c = lax.broadcasted_iota(I32, (LANES, LANES), 1)
    tri = jnp.where(l_r <= l_c, 1.0, 0.0).astype(BF16)

    def lane_dot(mask3, w):
        r, e, _ = mask3.shape
        mb = jnp.where(mask3, 1.0, 0.0).astype(BF16).reshape(r * e, LANES)
        return _bdot(mb, w).reshape(r, e, LANES)

    def expert_count(mask3):
        return jnp.sum(lane_dot(mask3, ones), axis=0, keepdims=True)

    def excl_cumsum(mask3):
        rows = mask3.shape[0]
        tot = lane_dot(mask3, ones)
        run = tot
        s = 1
        while s < rows:
            run = run + jnp.concatenate([jnp.zeros((s,) + run.shape[1:], F32), run[:rows - s]], axis=0)
            s *= 2
        before = run - tot
        return before + lane_dot(mask3, tri) - jnp.where(mask3, 1.0, 0.0), before

    def step(t, thrs):
        bit = lax.shift_left(jnp.int32(1), 30 - t)
        out = []
        for aff, thr, cap in zip(affs, thrs, caps):
            cand = thr | bit
            cnt = expert_count(aff >= pltpu.bitcast(cand, F32))
            out.append(jnp.where(cnt >= cap, cand, thr))
        return tuple(out)

    init = tuple(jnp.zeros((1,) + a.shape[1:], I32) for a in affs)
    thrs = lax.fori_loop(0, 31, step, init)
    for aff, thr_bits, cap, pos_ref, base_ref in zip(affs, thrs, caps, pos_refs, base_refs):
        thr = pltpu.bitcast(thr_bits, F32)
        gt = aff > thr
        eq = aff == thr
        need = cap - expert_count(gt)
        tie_rank, _ = excl_cumsum(eq)
        sel = gt | (eq & (tie_rank < need))
        pos, before = excl_cumsum(sel)
        pos_ref[...] = jnp.where(sel, pos, -1.0).astype(I32)
        base_ref[...] = before


def _route(affs, caps):
    shapes = [a.shape for a in affs]
    return pl.pallas_call(
        functools.partial(_route_kernel, tuple(caps)),
        out_shape=[jax.ShapeDtypeStruct(s, I32) for s in shapes] +
                  [jax.ShapeDtypeStruct(s, F32) for s in shapes],
        compiler_params=pltpu.CompilerParams(vmem_limit_bytes=VMEM_LIMIT),
        name="route",
    )(*affs)


def _window(off_ref, e, i, n_tiles, w, cap):
    del n_tiles
    first = off_ref[i * N_EXPERTS + e] + w * SLOT_WIN
    return first, pl.multiple_of(jnp.minimum(first, cap - SLOT_WIN), BF16_ROWS)


def _one_hot_rows(pos_ref, wins, row0=0):
    r = lax.broadcasted_iota(I32, (SLOT_WIN, LANES), 0)
    blocks = []
    for e, (first, off) in enumerate(wins):
        halves = []
        for h in range(ROW_TILE // LANES):
            pe = pos_ref[row0 + h, e:e + 1, :]
            rel = jnp.where(pe >= first, pe - off, -1)
            halves.append(jnp.where(rel == r, 1.0, 0.0).astype(BF16))
        blocks.append(jnp.concatenate(halves, axis=1))
    return jnp.concatenate(blocks, axis=0)


DISPATCH_TILES = 2


def _dispatch_kernel(cap, n_steps, off_ref, nwin_ref, h2_ref, pos_ref, xs_hbm, xs_ref, sem):
    i = pl.program_id(0)
    per = h2_ref.shape[1] // DISPATCH_TILES

    @pl.when(i == 0)
    def _():
        def zero(e, carry):
            xs_ref[e] = jnp.zeros(xs_ref.shape[1:], BF16)
            return carry
        lax.fori_loop(0, N_EXPERTS, zero, 0)

    def gather(t, w):
        tile = i * DISPATCH_TILES + t
        wins = [_window(off_ref, e, tile, None, w, cap) for e in range(N_EXPERTS)]
        p = _one_hot_rows(pos_ref, wins, t * (ROW_TILE // LANES))
        h2 = h2_ref[:, t * per:(t + 1) * per, :].reshape(ROW_TILE, XS_W)
        return wins, _bdot(p, h2)

    def merge(wins, slab):
        for e, (_, off) in enumerate(wins):
            cur = xs_ref[e, pl.ds(off, SLOT_WIN), :].astype(F32)
            xs_ref[e, pl.ds(off, SLOT_WIN), :] = (
                cur + slab[e * SLOT_WIN:(e + 1) * SLOT_WIN]).astype(BF16)

    for wins, slab in [gather(t, 0) for t in range(DISPATCH_TILES)]:
        merge(wins, slab)
    for t in range(DISPATCH_TILES):
        def extra(w, carry, t=t):
            merge(*gather(t, w))
            return carry
        lax.fori_loop(1, nwin_ref[i * DISPATCH_TILES + t], extra, 0)

    @pl.when(i == n_steps - 1)
    def _():
        cp = pltpu.make_async_copy(xs_ref, xs_hbm, sem)
        cp.start()
        cp.wait()


def _tile_spec(stripes, width, tiles=1):
    return pl.BlockSpec((stripes, tiles * ROW_TILE // stripes, width), lambda i, *_: (0, i, 0))


def _dispatch(h2aug, pos, offs, nwin, cap, stripes):
    n = h2aug.shape[0]
    n_steps = n // (ROW_TILE * DISPATCH_TILES)
    return pl.pallas_call(
        functools.partial(_dispatch_kernel, cap, n_steps),
        out_shape=jax.ShapeDtypeStruct((N_EXPERTS, cap, XS_W), BF16),
        grid_spec=pltpu.PrefetchScalarGridSpec(
            num_scalar_prefetch=2,
            grid=(n_steps,),
            in_specs=[_tile_spec(stripes, XS_W, DISPATCH_TILES),
                      pl.BlockSpec((DISPATCH_TILES * ROW_TILE // LANES, N_EXPERTS, LANES),
                                   lambda i, *_: (i, 0, 0))],
            out_specs=pl.BlockSpec(memory_space=pl.ANY),
            scratch_shapes=[pltpu.VMEM((N_EXPERTS, cap, XS_W), BF16),
                            pltpu.SemaphoreType.DMA]),
        compiler_params=_params(1, VMEM_LIMIT),
        name="dispatch",
    )(offs, nwin, h2aug.reshape(stripes, n // stripes, XS_W), pos)


FF_CHUNK = 512
EXP_ROWS = 512


def _experts_kernel(n_groups, n_chunks, *refs):
    xs_refs = refs[:n_groups]
    wg_ref, wu_ref, wd_ref = refs[n_groups:n_groups + 3]
    ys_refs = refs[n_groups + 3:2 * n_groups + 3]
    acc_ref = refs[2 * n_groups + 3]
    e = pl.program_id(0)
    j = pl.program_id(1)
    cap = xs_refs[0].shape[1]

    @pl.when(j == 0)
    def _():
        for g in range(n_groups):
            acc_ref[g] = jnp.zeros(acc_ref.shape[1:], F32)

    wg = wg_ref[0].astype(BF16)
    wu = wu_ref[0].astype(BF16)
    wd = wd_ref[0].astype(BF16)
    blocks = [(g, slice(r * EXP_ROWS, (r + 1) * EXP_ROWS))
              for g in range(n_groups) for r in range(cap // EXP_ROWS)]

    def up(blk):
        g, rows = blk
        x = xs_refs[g][0, rows, :D_MODEL]
        return _bdot(x, wg), _bdot(x, wu)

    nxt = up(blocks[0])
    for k, (g, rows) in enumerate(blocks):
        a, u = nxt
        if k + 1 < len(blocks):
            nxt = up(blocks[k + 1])
        hm = (a * jax.nn.sigmoid(a) * u).astype(BF16)
        acc_ref[g, rows, :] += _bdot(hm, wd)

    @pl.when(j == n_chunks - 1)
    def _():
        for g, (xs_ref, ys_ref) in enumerate(zip(xs_refs, ys_refs)):
            aug = xs_ref[0, :, D_MODEL:].astype(F32)
            lane = lax.broadcasted_iota(I32, aug.shape, 1)
            mine = (lane == e) | (lane == e + N_EXPERTS) | (lane == e + 2 * N_EXPERTS)
            gate = jnp.sum(jnp.where(mine, aug, 0.0), axis=-1, keepdims=True)
            ys_ref[0] = (acc_ref[g] * gate).astype(BF16)


def _experts(xs_list, w_gate, w_up, w_down):
    n_groups = len(xs_list)
    cap = xs_list[0].shape[1]
    ff = w_gate.shape[2]
    n_chunks = ff // FF_CHUNK
    xs_spec = pl.BlockSpec((1, cap, XS_W), lambda e, j: (e, 0, 0))
    up_spec = pl.BlockSpec((1, D_MODEL, FF_CHUNK), lambda e, j: (e, 0, j))
    down_spec = pl.BlockSpec((1, FF_CHUNK, D_MODEL), lambda e, j: (e, j, 0))
    ys_spec = pl.BlockSpec((1, cap, D_MODEL), lambda e, j: (e, 0, 0))
    return pl.pallas_call(
        functools.partial(_experts_kernel, n_groups, n_chunks),
        out_shape=[jax.ShapeDtypeStruct((N_EXPERTS, cap, D_MODEL), BF16)] * n_groups,
        grid=(N_EXPERTS, n_chunks),
        in_specs=[xs_spec] * n_groups + [up_spec, up_spec, down_spec],
        out_specs=[ys_spec] * n_groups,
        scratch_shapes=[pltpu.VMEM((n_groups, cap, D_MODEL), F32)],
        compiler_params=_params(2, VMEM_LIMIT),
        name="experts",
    )(*xs_list, w_gate, w_up, w_down)


def _combine_kernel(cap, n_tiles, off_ref, nwin_ref, ys_hbm, pos_ref, x1_ref, g2_ref, fg_ref,
                    y_ref, buf_ref, sems):
    i = pl.program_id(0)
    slot = i % 2

    def copies(tile, w, s):
        out = []
        for e in range(N_EXPERTS):
            _, off = _window(off_ref, e, tile, n_tiles, w, cap)
            out.append(pltpu.make_async_copy(
                ys_hbm.at[e, pl.ds(off, SLOT_WIN), :],
                buf_ref.at[s, pl.ds(e * SLOT_WIN, SLOT_WIN), :],
                sems.at[s]))
        return out

    @pl.when(i == 0)
    def _():
        for cp in copies(0, 0, 0):
            cp.start()

    @pl.when(i + 1 < n_tiles)
    def _():
        for cp in copies(i + 1, 0, 1 - slot):
            cp.start()

    tn = (((0,), (0,)), ((), ()))

    def gathered(w):
        wins = [_window(off_ref, e, i, n_tiles, w, cap) for e in range(N_EXPERTS)]
        p = _one_hot_rows(pos_ref, wins)
        for cp in copies(i, w, slot):
            cp.wait()
        return lax.dot_general(p, buf_ref[slot], tn, preferred_element_type=F32)

    def extra_window(w, acc):
        for cp in copies(i, w, slot):
            cp.start()
        return acc + gathered(w)

    moe = lax.fori_loop(1, nwin_ref[i], extra_window, gathered(0))
    out = x1_ref[...] + g2_ref[...] * moe.reshape(x1_ref.shape)
    y_ref[...] = _rms(out, fg_ref[...])


def _combine(ys, pos, offs, nwin, x1, mod3, mod_row, stripes, final_g, cap):
    n = x1.shape[0]
    n_tiles = n // ROW_TILE
    g2 = pl.BlockSpec((stripes, 1, D_MODEL), lambda i, *_: (0 if mod_row is None else mod_row, 0, 5))
    y = pl.pallas_call(
        functools.partial(_combine_kernel, cap, n_tiles),
        out_shape=jax.ShapeDtypeStruct((stripes, n // stripes, D_MODEL), F32),
        grid_spec=pltpu.PrefetchScalarGridSpec(
            num_scalar_prefetch=2,
            grid=(n_tiles,),
            in_specs=[pl.BlockSpec(memory_space=pl.ANY),
                      pl.BlockSpec((ROW_TILE // LANES, N_EXPERTS, LANES), lambda i, *_: (i, 0, 0)),
                      _tile_spec(stripes, D_MODEL),
                      g2,
                      pl.BlockSpec((1, D_MODEL), lambda i, *_: (0, 0))],
            out_specs=_tile_spec(stripes, D_MODEL),
            scratch_shapes=[pltpu.VMEM((2, N_EXPERTS * SLOT_WIN, D_MODEL), BF16),
                            pltpu.SemaphoreType.DMA((2,))]),
        compiler_params=_params(1, VMEM_LIMIT),
        name="combine",
    )(offs, nwin, ys, pos, x1.reshape(stripes, n // stripes, D_MODEL), mod3, final_g)
    return y.reshape(n, D_MODEL)


def _head_cols(w3):
    k, _, d = w3.shape
    return jnp.pad(w3, ((0, 0), (0, 0), (0, HEAD_PAD - d))).reshape(k, QK_WIDTH)


def _value_cols(v3):
    k = v3.shape[0]
    pair = v3.reshape(k, MLA_HEADS // 2, 2, V_DIM)
    zeros = jnp.zeros((k, MLA_HEADS // 2, V_DIM), v3.dtype)
    even = jnp.concatenate([pair[:, :, 0], zeros], axis=-1)
    odd = jnp.concatenate([zeros, pair[:, :, 1]], axis=-1)
    return jnp.stack([even, odd], axis=2).reshape(k, QK_WIDTH)


def _rope_tables(seq):
    half = QK_ROPE // 2
    pos = jnp.arange(seq)
    inv_freq = 1.0 / (ROPE_BASE ** (jnp.arange(0, half, 2, dtype=F32) / half))

    def cs(p):
        ang = p.astype(F32)[:, None] * inv_freq[None, :]
        ang = jnp.concatenate([ang, ang], axis=-1)
        return jnp.cos(ang), jnp.sin(ang)

    cr, sr = cs(pos // GRID_W)
    cc, sc = cs(pos % GRID_W)
    ones = jnp.ones((seq, QK_NOPE), F32)
    zpad = jnp.zeros((seq, HEAD_PAD - QK_NOPE - QK_ROPE), F32)
    cq = jnp.concatenate([ones, cr, cc, zpad], axis=-1)
    sq = jnp.concatenate([0.0 * ones, sr, sc, zpad], axis=-1)
    return cq, sq


def _prep_weights(w_in, q_norm_g, w_q_up, kv_norm_g, w_kv_up, attn_norm_g):
    wqc = w_in[:, :Q_RANK + KV_RANK].astype(BF16)
    wkr = jnp.pad(w_in[:, Q_RANK + KV_RANK:Q_RANK + KV_RANK + QK_ROPE],
                  ((0, 0), (0, LANES - QK_ROPE))).astype(BF16)
    wglu = w_in[:, Q_RANK + KV_RANK + QK_ROPE:].astype(BF16)
    wq = _head_cols(w_q_up.astype(BF16).reshape(Q_RANK, MLA_HEADS, QK_NOPE + QK_ROPE))
    kv3 = w_kv_up.astype(BF16).reshape(KV_RANK, MLA_HEADS, QK_NOPE + V_DIM)
    wk = _head_cols(kv3[:, :, :QK_NOPE])
    wv = _value_cols(kv3[:, :, QK_NOPE:])
    pk = np.zeros((LANES, QK_WIDTH), np.float32)
    for hd in range(MLA_HEADS):
        for j in range(QK_ROPE):
            pk[j, hd * HEAD_PAD + QK_NOPE + j] = 1.0
    pk = jnp.asarray(pk, BF16)
    return (attn_norm_g[None, :], wqc, wkr, wglu, q_norm_g[None, :], wq,
            kv_norm_g[None, :], wk, wv, pk)


def _stripe_order(aff3, stripes, seq):
    per = ROW_TILE // stripes
    q = LANES // per
    assert per * stripes == ROW_TILE and per * q == LANES and stripes % q == 0 and seq % LANES == 0
    a = aff3.reshape(stripes // q, q, seq // LANES, N_EXPERTS, q, per)
    return a.transpose(2, 4, 0, 3, 1, 5).reshape(aff3.shape)


def _route_plan(base, cap):
    rows_per_tile = ROW_TILE // LANES
    start = base[::rows_per_tile, :, 0].astype(I32)
    end = jnp.concatenate([start[1:], jnp.full((1, N_EXPERTS), cap, I32)], axis=0)
    off = (start // BF16_ROWS) * BF16_ROWS
    span = end - off
    nwin = jnp.maximum(jnp.max((span + SLOT_WIN - 1) // SLOT_WIN, axis=1), 1).astype(I32)
    return off.reshape(-1), nwin


def kernel(x_prompt, x_sample, cache_ckv, cache_krope, c, c_ctx, w_mod, b_mod, attn_norm_g,
           w_in, q_norm_g, w_q_up, kv_norm_g, w_kv_up, conv_w, conv_b, conv_ln_g, conv_ln_b,
           w_out, ffn_norm_g, w_router, w_gate, w_up, w_down, final_norm_g):
    depth = w_mod.shape[0]
    assert depth == 1
    bp, sp, _ = x_prompt.shape
    bs, ss, _ = x_sample.shape
    past = cache_ckv.shape[2]
    ctx_row = bs

    mod_rows = 16
    cc = jnp.concatenate([c, c_ctx[None, :], jnp.zeros((mod_rows - bs - 1, D_MODEL), F32)], axis=0)
    mod = _adaln(cc, w_mod[0], b_mod[0][None, :])
    mod3 = mod.reshape(mod_rows, 1, N_MOD * D_MODEL)

    fw = _prep_weights(w_in[0], q_norm_g[0], w_q_up[0], kv_norm_g[0], w_kv_up[0], attn_norm_g[0])
    wk, wv, pk = fw[7], fw[8], fw[9]
    wo = w_out[0].astype(BF16)
    wr = jnp.pad(jnp.tile(w_router[0], (1, 3)), ((0, 0), (0, LANES - 3 * N_EXPERTS)))
    wrh = wr.astype(BF16)
    wrl = (wr - wrh.astype(F32)).astype(BF16)
    cvw = (conv_w[0], conv_b[0][None, :], conv_ln_g[0][None, :], conv_ln_b[0][None, :])
    bw = (wo[:MLA_HEADS * V_DIM], wo[MLA_HEADS * V_DIM:], ffn_norm_g[0][None, :], wrh, wrl)

    xp2 = x_prompt.reshape(bp * sp, D_MODEL)
    xs2 = x_sample.reshape(bs * ss, D_MODEL)

    qp, kp, vp, cv_p, ckvp, krp = _front(xp2, mod3, ctx_row, sp, fw, cvw, None)
    attn_p = _attention(qp, [(kp, vp, sp)], bp, sp)
    x1p, h2p, afftp = _back(xp2, attn_p, cv_p, mod3, ctx_row, bw, sp)

    qs, ks, vs, cv_s = _front(xs2, mod3, None, ss, fw, cvw, _rope_tables(ss))
    kc, vc = _ctx_kv(cache_ckv[:, 0].reshape(bs * past, KV_RANK),
                     cache_krope[:, 0].reshape(bs * past, QK_ROPE), wk, wv, pk)
    attn_s = _attention(qs, [(kc, vc, past), (ks, vs, ss)], bs, ss)
    x1s, h2s, affts = _back(xs2, attn_s, cv_s, mod3, None, bw, ss)

    stripes = (1, bs)
    caps = [EC_FACTOR * h2.shape[0] // N_EXPERTS for h2 in (h2p, h2s)]
    routed = _route([afftp, _stripe_order(affts, bs, ss)], caps)
    plans = []
    xs_list = []
    for g, (h2, cap) in enumerate(zip((h2p, h2s), caps)):
        pos, base = routed[g], routed[len(caps) + g]
        offs, nwin = _route_plan(base, cap)
        plans.append((pos, offs, nwin, cap))
        xs_list.append(_dispatch(h2, pos, offs, nwin, cap, stripes[g]))
    ys_list = _experts(xs_list, w_gate[0], w_up[0], w_down[0])

    fg = final_norm_g[None, :]
    pos, offs, nwin, cap = plans[0]
    y_prompt = _combine(ys_list[0], pos, offs, nwin, x1p, mod3, ctx_row, stripes[0], fg, cap)
    pos, offs, nwin, cap = plans[1]
    y_sample = _combine(ys_list[1], pos, offs, nwin, x1s, mod3, None, stripes[1], fg, cap)

    return (y_prompt.reshape(bp, sp, D_MODEL), y_sample.reshape(bs, ss, D_MODEL),
            ckvp.reshape(bp, depth, sp, KV_RANK), krp.reshape(bp, depth, sp, QK_ROPE))
```

```python
import functools

import jax
import jax.numpy as jnp
import numpy as np
from jax import lax
from jax.experimental import pallas as pl
from jax.experimental.pallas import tpu as pltpu

F32 = jnp.float32
BF16 = jnp.bfloat16
I32 = jnp.int32

LANES = 128
BF16_ROWS = 16
VMEM_LIMIT = 56 * 1024 * 1024

D_MODEL = 1024
GRID_W = 64
MLA_HEADS = 8
QK_NOPE = 64
QK_ROPE = 32
V_DIM = 64
Q_RANK = 768
KV_RANK = 256
CONV_CH = 512
CONV_WIDTH = 31
N_EXPERTS = 16
EXPERT_FF = 1024
EC_FACTOR = 2
ROPE_BASE = 10000.0
EPS = 1e-6
N_MOD = 6

HEAD_PAD = LANES
QK_WIDTH = MLA_HEADS * HEAD_PAD
ROW_TILE = 256
FRONT_TILE = 512
ATTN_TILE = 512
BACK_TILE = 1024
BACK_SUB = 256
SLOT_WIN = 64
AUG = LANES
XS_W = D_MODEL + AUG
LOG2_E = 1.4426950408889634


def _params(n_axes, vmem=None):
    return pltpu.CompilerParams(
        dimension_semantics=("arbitrary",) * n_axes,
        vmem_limit_bytes=vmem)


def _rms(x, g):
    ms = jnp.mean(x * x, axis=-1, keepdims=True)
    return x * lax.rsqrt(ms + EPS) * g


def _bdot(a, b):
    return jnp.dot(a, b, preferred_element_type=F32)


def _adaln_kernel(c_ref, w_ref, b_ref, o_ref):
    c = c_ref[...]
    s = c * jax.nn.sigmoid(c)
    w = w_ref[...]
    sh = s.astype(BF16)
    sl = (s - sh.astype(F32)).astype(BF16)
    wh = w.astype(BF16)
    wl = (w - wh.astype(F32)).astype(BF16)
    o_ref[...] = _bdot(sh, wh) + (_bdot(sl, wh) + _bdot(sh, wl)) + b_ref[...]


def _adaln(cc, w_mod, b_mod):
    rows = cc.shape[0]
    n_out = w_mod.shape[1]
    blk = D_MODEL
    return pl.pallas_call(
        _adaln_kernel,
        out_shape=jax.ShapeDtypeStruct((rows, n_out), F32),
        grid=(n_out // blk,),
        in_specs=[pl.BlockSpec((rows, D_MODEL), lambda j: (0, 0)),
                  pl.BlockSpec((D_MODEL, blk), lambda j: (0, j)),
                  pl.BlockSpec((1, blk), lambda j: (0, j))],
        out_specs=pl.BlockSpec((rows, blk), lambda j: (0, j)),
        compiler_params=_params(1),
        name="adaln",
    )(cc, w_mod, b_mod)


def _rot_half(x, lane, base):
    rel = lane - base
    first = ((rel >= 0) & (rel < 8)) | ((rel >= 16) & (rel < 24))
    return jnp.where(first, -pltpu.roll(x, LANES - 8, axis=1), pltpu.roll(x, 8, axis=1))


def _front_kernel(rope, seq, x_ref, sh_ref, sc_ref, g_ref, wqc_ref, wkr_ref, wglu_ref,
                  qg_ref, wq_ref, kvg_ref, wk_ref, wv_ref, pk_ref, cw_ref, cb_ref, lg_ref, lb_ref,
                  *rest):
    if rope:
        cq_ref, sq_ref, q_ref, k_ref, v_ref, cv_ref, vpad_ref, shift_ref = rest
    else:
        q_ref, k_ref, v_ref, cv_ref, ckv_ref, kr_ref, vpad_ref, shift_ref = rest
    rows = x_ref.shape[0]
    steps_per_seq = max(seq // rows, 1)
    scale = (QK_NOPE + QK_ROPE) ** -0.5 * LOG2_E
    lane = lax.broadcasted_iota(I32, (ROW_TILE, LANES), 1)
    conv_refs = (cw_ref, cb_ref, lg_ref, lb_ref, shift_ref)

    pending = []

    def tick(n=1):
        for _ in range(n):
            if pending:
                pending.pop(0)()

    def gated_values(qd, row_in_seq, slot, cv_row):
        r = slice(qd * ROW_TILE, (qd + 1) * ROW_TILE)
        h = _rms(x_ref[r, :], g_ref[...]) * (1.0 + sc_ref[0]) + sh_ref[0]
        hb = h.astype(BF16)
        glu = _bdot(hb, wglu_ref[...])
        if row_in_seq == 0:
            vpad_ref[slot, 0:CONV_HALO, :] = jnp.zeros((CONV_HALO, CONV_CH), F32)
        if row_in_seq + ROW_TILE == seq:
            vpad_ref[slot, CONV_HALO + seq:, :] = jnp.zeros((CONV_HALO, CONV_CH), F32)
        vpad_ref[slot, CONV_HALO + row_in_seq:CONV_HALO + row_in_seq + ROW_TILE, :] = (
            glu[:, :CONV_CH] * jax.nn.sigmoid(glu[:, CONV_CH:]))

        ready = ([row_in_seq - ROW_TILE] if row_in_seq > 0 else []) + \
                ([row_in_seq] if row_in_seq + ROW_TILE == seq else [])
        for t0 in ready:
            steps, finish = _conv_stages(CONV_STAGES, vpad_ref, slot, t0, *conv_refs)
            dst = slice(cv_row + t0, cv_row + t0 + ROW_TILE)
            pending.extend(steps + [functools.partial(_store_conv, cv_ref, dst, finish)])
        return hb

    def projections(qd, row_in_seq, hb, ticks):
        r = slice(qd * ROW_TILE, (qd + 1) * ROW_TILE)
        qc = _bdot(hb, wqc_ref[...])
        tick(ticks)
        kr = _bdot(hb, wkr_ref[...])
        tick(ticks)
        qn = _rms(qc[:, :Q_RANK], qg_ref[...]).astype(BF16)
        q = _bdot(qn, wq_ref[...])
        tick(ticks)
        ckv = _rms(qc[:, Q_RANK:], kvg_ref[...])
        ckv_b = ckv.astype(BF16)
        kk = _bdot(ckv_b, wk_ref[...])
        tick(ticks)
        v_ref[r, :] = _bdot(ckv_b, wv_ref[...]).astype(BF16)
        tick(ticks)
        if rope:
            t = slice(row_in_seq % rows, row_in_seq % rows + ROW_TILE)
            cq = cq_ref[t, :]
            sq = sq_ref[t, :]
            for hd in range(MLA_HEADS):
                blk = q[:, hd * HEAD_PAD:(hd + 1) * HEAD_PAD]
                rot = blk * cq + _rot_half(blk, lane, QK_NOPE) * sq
                q_ref[r, hd * HEAD_PAD:(hd + 1) * HEAD_PAD] = (rot * scale).astype(BF16)
            krs = pltpu.roll(kr, QK_NOPE, axis=1)
            kr = pltpu.roll(krs * cq + _rot_half(krs, lane, QK_NOPE) * sq, LANES - QK_NOPE, axis=1)
            kr = jnp.where(lane < QK_ROPE, kr, 0.0)
        else:
            q_ref[r, :] = (q * scale).astype(BF16)
            ckv_ref[r, :] = ckv
            kr_ref[r, :] = kr[:, :QK_ROPE]
        tick(ticks)
        k_ref[r, :] = (kk + _bdot(kr.astype(BF16), pk_ref[...])).astype(BF16)

    def step(k):
        n_q = rows // ROW_TILE
        hbs = []
        for qd in range(n_q):
            row = k * rows + qd * ROW_TILE
            if seq >= rows:
                hbs.append((row, gated_values(qd, row, 0, 0)))
            else:
                hbs.append((row % seq, gated_values(qd, row % seq, row // seq, (row // seq) * seq)))
        ticks = -(-len(pending) // (6 * n_q))
        for qd, (row_in_seq, hb) in enumerate(hbs):
            projections(qd, row_in_seq, hb, ticks)
        tick(len(pending))

    if steps_per_seq == 1:
        step(0)
    else:
        for k in range(steps_per_seq):
            pl.when(pl.program_id(0) % steps_per_seq == k)(functools.partial(step, k))


def _store_conv(cv_ref, dst, finish):
    cv_ref[dst, :] = finish()


def _front(x2d, mod3, mod_row, seq, wts, conv_wts, rope_tabs):
    n = x2d.shape[0]
    rope = rope_tabs is not None
    tile = FRONT_TILE
    steps_per_seq = max(seq // tile, 1)
    assert seq % ROW_TILE == 0 and (tile % seq == 0 or seq % tile == 0)
    full = lambda a: pl.BlockSpec(a.shape, lambda i: (0,) * a.ndim)

    def mod_spec(col):
        if mod_row is None:
            assert seq % tile == 0
            return pl.BlockSpec((1, 1, D_MODEL), lambda i: (i // steps_per_seq, 0, col))
        return pl.BlockSpec((1, 1, D_MODEL), lambda i: (mod_row, 0, col))

    row = lambda w: pl.BlockSpec((tile, w), lambda i: (i, 0))
    cv_rows = max(seq, tile)
    cv_spec = pl.BlockSpec((cv_rows, CONV_CH), lambda i: (i // steps_per_seq, 0))
    in_specs = [row(D_MODEL), mod_spec(0), mod_spec(1)] + [full(w) for w in wts + conv_wts]
    args = [x2d, mod3, mod3] + list(wts + conv_wts)
    out_shape = [jax.ShapeDtypeStruct((n, QK_WIDTH), BF16)] * 3 + \
                [jax.ShapeDtypeStruct((n, CONV_CH), BF16)]
    out_specs = [row(QK_WIDTH)] * 3 + [cv_spec]
    if rope:
        tab = pl.BlockSpec((tile, LANES), lambda i: (i % steps_per_seq, 0))
        in_specs += [tab, tab]
        args += list(rope_tabs)
    else:
        out_shape += [jax.ShapeDtypeStruct((n, KV_RANK), F32),
                      jax.ShapeDtypeStruct((n, QK_ROPE), F32)]
        out_specs += [row(KV_RANK), row(QK_ROPE)]
    slots = max(tile // seq, 1)
    return pl.pallas_call(
        functools.partial(_front_kernel, rope, seq),
        out_shape=out_shape,
        grid=(n // tile,),
        in_specs=in_specs,
        out_specs=out_specs,
        scratch_shapes=[pltpu.VMEM((slots, seq + 2 * CONV_HALO, CONV_CH), F32),
                        pltpu.VMEM((SUBLANES - 1, SHIFT_ROWS, CONV_CH), F32)],
        compiler_params=_params(1, VMEM_LIMIT),
        name="front_rope" if rope else "front",
    )(*args)


def _ctx_kernel(ckv_ref, kr_ref, wk_ref, wv_ref, pk_ref, k_ref, v_ref):
    cb = ckv_ref[...].astype(BF16)
    kr = kr_ref[...].astype(BF16)
    k_ref[...] = (_bdot(cb, wk_ref[...]) + _bdot(kr, pk_ref[...])).astype(BF16)
    v_ref[...] = _bdot(cb, wv_ref[...]).astype(BF16)


def _ctx_kv(ckv2d, kr2d, wk, wv, pk):
    n = ckv2d.shape[0]
    full = lambda a: pl.BlockSpec(a.shape, lambda i: (0,) * a.ndim)
    row = lambda w: pl.BlockSpec((ROW_TILE, w), lambda i: (i, 0))
    pk32 = pk[:QK_ROPE]
    return pl.pallas_call(
        _ctx_kernel,
        out_shape=[jax.ShapeDtypeStruct((n, QK_WIDTH), BF16)] * 2,
        grid=(n // ROW_TILE,),
        in_specs=[row(KV_RANK), row(QK_ROPE), full(wk), full(wv), full(pk32)],
        out_specs=[row(QK_WIDTH)] * 2,
        compiler_params=_params(1),
        name="ctx_kv",
    )(ckv2d, kr2d, wk, wv, pk32)


CONV_PAD = (CONV_WIDTH - 1) // 2
CONV_HALO = 16
SUBLANES = 8
SHIFT_ROWS = ROW_TILE + 2 * CONV_HALO - SUBLANES


CONV_STAGES = 6


def _conv_stages(n_stages, vpad_ref, slot, t0, cw_ref, cb_ref, lg_ref, lb_ref, shift_ref):
    state = {}

    def taps(lo, hi):
        if lo == 0:
            win = vpad_ref[slot, t0:t0 + ROW_TILE + 2 * CONV_HALO, :]
            for ph in range(1, SUBLANES):
                shift_ref[ph - 1] = win[ph:ph + SHIFT_ROWS, :]
            state["acc"] = jnp.broadcast_to(cb_ref[...], (ROW_TILE, CONV_CH))
        acc = state["acc"]
        for t in range(lo, hi):
            blk, ph = divmod(t + CONV_HALO - CONV_PAD, SUBLANES)
            if ph == 0:
                tap = vpad_ref[slot, t0 + blk * SUBLANES:t0 + blk * SUBLANES + ROW_TILE, :]
            else:
                tap = shift_ref[ph - 1, blk * SUBLANES:blk * SUBLANES + ROW_TILE, :]
            acc = acc + tap * cw_ref[t:t + 1, :]
        state["acc"] = acc

    def finish():
        acc = state["acc"]
        mu = jnp.mean(acc, axis=-1, keepdims=True)
        cen = acc - mu
        var = jnp.mean(cen * cen, axis=-1, keepdims=True)
        ln = cen * lax.rsqrt(var + EPS) * lg_ref[...] + lb_ref[...]
        return (ln * jax.nn.sigmoid(ln)).astype(BF16)

    bounds = [CONV_WIDTH * s // n_stages for s in range(n_stages + 1)]
    stages = [functools.partial(taps, bounds[s], bounds[s + 1]) for s in range(n_stages)]
    return stages, finish


def _attn_kernel(n_kv, n_seq, q_ref, *rest):
    kv_refs = rest[:2 * n_kv]
    o_ref = rest[2 * n_kv]
    nt = (((1,), (1,)), ((), ()))
    tq = q_ref.shape[0] // n_seq
    work = [(b, hd) for b in range(n_seq) for hd in range(MLA_HEADS)]

    def kv_rows(ref, b):
        t = ref.shape[0] // n_seq
        return slice(b * t, (b + 1) * t)

    def scores(item):
        b, hd = item
        sl = slice(hd * HEAD_PAD, (hd + 1) * HEAD_PAD)
        qh = q_ref[b * tq:(b + 1) * tq, sl]
        return [lax.dot_general(qh, kv_refs[2 * j][kv_rows(kv_refs[2 * j], b), sl], nt,
                                preferred_element_type=F32) for j in range(n_kv)]

    outs = {}
    nxt = scores(work[0])
    for w, (b, hd) in enumerate(work):
        sl = slice(hd * HEAD_PAD, (hd + 1) * HEAD_PAD)
        ss = nxt
        if w + 1 < len(work):
            nxt = scores(work[w + 1])
        m = ss[0].max(axis=-1, keepdims=True)
        for s in ss[1:]:
            m = jnp.maximum(m, s.max(axis=-1, keepdims=True))
        ps = [jnp.exp2(s - m) for s in ss]
        l = ps[0].sum(axis=-1, keepdims=True)
        for p in ps[1:]:
            l = l + p.sum(axis=-1, keepdims=True)
        o = None
        for j in range(n_kv):
            v_ref = kv_refs[2 * j + 1]
            pv = _bdot(ps[j].astype(BF16), v_ref[kv_rows(v_ref, b), sl])
            o = pv if o is None else o + pv
        outs[hd] = o / l
        if hd % 2 == 1:
            o_ref[b * tq:(b + 1) * tq, (hd // 2) * LANES:(hd // 2 + 1) * LANES] = (
                outs[hd - 1] + outs[hd]).astype(BF16)


def _attention(q, kvs, batch, seq):
    n = q.shape[0]
    n_seq = max(ATTN_TILE // seq, 1)
    tile = min(seq, ATTN_TILE) * n_seq
    q_tiles = max(seq // tile, 1)
    assert batch % n_seq == 0
    row = lambda w: pl.BlockSpec((tile, w), lambda b, i: (b * q_tiles + i, 0))
    in_specs = [row(QK_WIDTH)]
    args = [q]
    for k, v, t in kvs:
        spec = pl.BlockSpec((n_seq * t, QK_WIDTH), lambda b, i: (b, 0))
        in_specs += [spec, spec]
        args += [k, v]
    return pl.pallas_call(
        functools.partial(_attn_kernel, len(kvs), n_seq),
        out_shape=jax.ShapeDtypeStruct((n, MLA_HEADS * V_DIM), BF16),
        grid=(batch // n_seq, q_tiles),
        in_specs=in_specs,
        out_specs=row(MLA_HEADS * V_DIM),
        compiler_params=_params(2, VMEM_LIMIT),
        name="attention",
    )(*args)


def _back_kernel(x_ref, attn_ref, cv_ref, woa_ref, woc_ref, g1_ref, sh_ref, sc_ref, fg_ref,
                 wrh_ref, wrl_ref, x1_ref, h2_ref, afft_ref):
    n_sub = x_ref.shape[0] // BACK_SUB
    rows = [slice(s * BACK_SUB, (s + 1) * BACK_SUB) for s in range(n_sub)]

    def project(r):
        return _bdot(attn_ref[r, :], woa_ref[...]) + _bdot(cv_ref[r, :], woc_ref[...])

    def normalise(r, mix):
        x1 = x_ref[r, :] + g1_ref[0] * mix
        x1_ref[r, :] = x1
        h2 = _rms(x1, fg_ref[...]) * (1.0 + sc_ref[0]) + sh_ref[0]
        h2b = h2.astype(BF16)
        h2_ref[r, :D_MODEL] = h2b
        h2l = (h2 - h2b.astype(F32)).astype(BF16)
        wrh = wrh_ref[...]
        return _bdot(h2b, wrh) + (_bdot(h2l, wrh) + _bdot(h2b, wrl_ref[...]))

    def route(s, logits):
        r = rows[s]
        lane = lax.broadcasted_iota(I32, logits.shape, 1)
        real = lane < N_EXPERTS
        m = jnp.max(jnp.where(real, logits, -jnp.inf), axis=-1, keepdims=True)
        e = jnp.exp(logits - m)
        aff = e / jnp.sum(jnp.where(real, e, 0.0), axis=-1, keepdims=True)
        aff_t = aff.T
        for blk in range(BACK_SUB // LANES):
            afft_ref[s * (BACK_SUB // LANES) + blk] = aff_t[:N_EXPERTS, blk * LANES:(blk + 1) * LANES]
        hi = aff.astype(BF16)
        r1 = aff - hi.astype(F32)
        mid = r1.astype(BF16)
        lo = (r1 - mid.astype(F32)).astype(BF16)
        zero = jnp.zeros_like(hi)
        h2_ref[r, D_MODEL:] = jnp.where(
            real, hi, jnp.where(lane < 2 * N_EXPERTS, mid, jnp.where(lane < 3 * N_EXPERTS, lo, zero)))

    mixes = {0: project(rows[0])}
    logits = {}
    for s in range(n_sub):
        if s + 1 < n_sub:
            mixes[s + 1] = project(rows[s + 1])
        logits[s] = normalise(rows[s], mixes.pop(s))
        if s > 0:
            route(s - 1, logits.pop(s - 1))
    route(n_sub - 1, logits.pop(n_sub - 1))


def _back(x2d, attn, cv, mod3, mod_row, wts, seq):
    n = x2d.shape[0]
    tile = BACK_TILE
    full = lambda a: pl.BlockSpec(a.shape, lambda i: (0,) * a.ndim)

    def mod_spec(col):
        if mod_row is None:
            assert seq % tile == 0
            return pl.BlockSpec((1, 1, D_MODEL), lambda i: (i // (seq // tile), 0, col))
        return pl.BlockSpec((1, 1, D_MODEL), lambda i: (mod_row, 0, col))

    row = lambda w: pl.BlockSpec((tile, w), lambda i: (i, 0))
    woa, woc, fg, wrh, wrl = wts
    in_specs = [row(D_MODEL), row(MLA_HEADS * V_DIM), row(CONV_CH), full(woa), full(woc),
                mod_spec(2), mod_spec(3), mod_spec(4), full(fg), full(wrh), full(wrl)]
    return pl.pallas_call(
        _back_kernel,
        out_shape=[jax.ShapeDtypeStruct((n, D_MODEL), F32),
                   jax.ShapeDtypeStruct((n, XS_W), BF16),
                   jax.ShapeDtypeStruct((n // LANES, N_EXPERTS, LANES), F32)],
        grid=(n // tile,),
        in_specs=in_specs,
        out_specs=[row(D_MODEL), row(XS_W),
                   pl.BlockSpec((tile // LANES, N_EXPERTS, LANES), lambda i: (i, 0, 0))],
        compiler_params=_params(1, VMEM_LIMIT),
        name="back",
    )(x2d, attn, cv, woa, woc, mod3, mod3, mod3, fg, wrh, wrl)


def _route_kernel(caps, *refs):
    n_g = len(caps)
    affs = [r[...] for r in refs[:n_g]]
    pos_refs = refs[n_g:2 * n_g]
    base_refs = refs[2 * n_g:]
    ones = jnp.ones((LANES, LANES), BF16)
    l_r = lax.broadcasted_iota(I32, (LANES, LANES), 0)
    l_c = lax.broadcasted_iota(I32, (LANES, LANES), 1)
    tri = jnp.where(l_r <= l_c, 1.0, 0.0).astype(BF16)

    def lane_dot(mask3, w):
        r, e, _ = mask3.shape
        mb = jnp.where(mask3, 1.0, 0.0).astype(BF16).reshape(r * e, LANES)
        return _bdot(mb, w).reshape(r, e, LANES)

    def expert_count(mask3):
        return jnp.sum(lane_dot(mask3, ones), axis=0, keepdims=True)

    def excl_cumsum(mask3):
        rows = mask3.shape[0]
        tot = lane_dot(mask3, ones)
        run = tot
        s = 1
        while s < rows:
            run = run + jnp.concatenate([jnp.zeros((s,) + run.shape[1:], F32), run[:rows - s]], axis=0)
            s *= 2
        before = run - tot
        return before + lane_dot(mask3, tri) - jnp.where(mask3, 1.0, 0.0), before

    def step(t, thrs):
        bit = lax.shift_left(jnp.int32(1), 30 - t)
        out = []
        for aff, thr, cap in zip(affs, thrs, caps):
            cand = thr | bit
            cnt = expert_count(aff >= pltpu.bitcast(cand, F32))
            out.append(jnp.where(cnt >= cap, cand, thr))
        return tuple(out)

    init = tuple(jnp.zeros((1,) + a.shape[1:], I32) for a in affs)
    thrs = lax.fori_loop(0, 31, step, init)
    for aff, thr_bits, cap, pos_ref, base_ref in zip(affs, thrs, caps, pos_refs, base_refs):
        thr = pltpu.bitcast(thr_bits, F32)
        gt = aff > thr
        eq = aff == thr
        need = cap - expert_count(gt)
        tie_rank, _ = excl_cumsum(eq)
        sel = gt | (eq & (tie_rank < need))
        pos, before = excl_cumsum(sel)
        pos_ref[...] = jnp.where(sel, pos, -1.0).astype(I32)
        base_ref[...] = before


def _route(affs, caps):
    shapes = [a.shape for a in affs]
    return pl.pallas_call(
        functools.partial(_route_kernel, tuple(caps)),
        out_shape=[jax.ShapeDtypeStruct(s, I32) for s in shapes] +
                  [jax.ShapeDtypeStruct(s, F32) for s in shapes],
        compiler_params=pltpu.CompilerParams(vmem_limit_bytes=VMEM_LIMIT),
        name="route",
    )(*affs)


def _window(off_ref, e, i, n_tiles, w, cap):
    del n_tiles
    first = off_ref[i * N_EXPERTS + e] + w * SLOT_WIN
    return first, pl.multiple_of(jnp.minimum(first, cap - SLOT_WIN), BF16_ROWS)


def _one_hot_rows(pos_ref, wins, row0=0):
    r = lax.broadcasted_iota(I32, (SLOT_WIN, LANES), 0)
    blocks = []
    for e, (first, off) in enumerate(wins):
        halves = []
        for h in range(ROW_TILE // LANES):
            pe = pos_ref[row0 + h, e:e + 1, :]
            rel = jnp.where(pe >= first, pe - off, -1)
            halves.append(jnp.where(rel == r, 1.0, 0.0).astype(BF16))
        blocks.append(jnp.concatenate(halves, axis=1))
    return jnp.concatenate(blocks, axis=0)


DISPATCH_TILES = 2
FLUSH_PARTS = 4


def _dispatch_kernel(cap, n_steps, off_ref, nwin_ref, flush_ref, h2_ref, pos_ref, xs_hbm,
                     xs_ref, sems):
    i = pl.program_id(0)
    per = h2_ref.shape[1] // DISPATCH_TILES

    @pl.when(i == 0)
    def _():
        def zero(e, carry):
            xs_ref[e] = jnp.zeros(xs_ref.shape[1:], BF16)
            return carry
        lax.fori_loop(0, N_EXPERTS, zero, 0)

    def gather(t, w):
        tile = i * DISPATCH_TILES + t
        wins = [_window(off_ref, e, tile, None, w, cap) for e in range(N_EXPERTS)]
        p = _one_hot_rows(pos_ref, wins, t * (ROW_TILE // LANES))
        h2 = h2_ref[:, t * per:(t + 1) * per, :].reshape(ROW_TILE, XS_W)
        return wins, _bdot(p, h2)

    def merge(wins, slab):
        for e, (_, off) in enumerate(wins):
            cur = xs_ref[e, pl.ds(off, SLOT_WIN), :].astype(F32)
            xs_ref[e, pl.ds(off, SLOT_WIN), :] = (
                cur + slab[e * SLOT_WIN:(e + 1) * SLOT_WIN]).astype(BF16)

    for wins, slab in [gather(t, 0) for t in range(DISPATCH_TILES)]:
        merge(wins, slab)
    for t in range(DISPATCH_TILES):
        def extra(w, carry, t=t):
            merge(*gather(t, w))
            return carry
        lax.fori_loop(1, nwin_ref[i * DISPATCH_TILES + t], extra, 0)

    part = cap // FLUSH_PARTS

    def flush(q):
        rows = pl.ds(q * part, part)
        return pltpu.make_async_copy(xs_ref.at[:, rows, :], xs_hbm.at[:, rows, :], sems.at[q])

    for q in range(FLUSH_PARTS):
        @pl.when(i == flush_ref[q])
        def _(q=q):
            flush(q).start()

    @pl.when(i == n_steps - 1)
    def _():
        for q in range(FLUSH_PARTS):
            flush(q).wait()


def _tile_spec(stripes, width, tiles=1):
    return pl.BlockSpec((stripes, tiles * ROW_TILE // stripes, width), lambda i, *_: (0, i, 0))


def _dispatch(h2aug, pos, offs, nwin, cap, stripes):
    n = h2aug.shape[0]
    n_steps = n // (ROW_TILE * DISPATCH_TILES)
    low = jnp.min(offs.reshape(-1, N_EXPERTS), axis=1)[DISPATCH_TILES::DISPATCH_TILES]
    low = jnp.concatenate([low, jnp.full((1,), cap, I32)])
    bounds = (jnp.arange(FLUSH_PARTS, dtype=I32) + 1) * (cap // FLUSH_PARTS)
    flush = jnp.argmax(low[None, :] >= bounds[:, None], axis=1).astype(I32)
    return pl.pallas_call(
        functools.partial(_dispatch_kernel, cap, n_steps),
        out_shape=jax.ShapeDtypeStruct((N_EXPERTS, cap, XS_W), BF16),
        grid_spec=pltpu.PrefetchScalarGridSpec(
            num_scalar_prefetch=3,
            grid=(n_steps,),
            in_specs=[_tile_spec(stripes, XS_W, DISPATCH_TILES),
                      pl.BlockSpec((DISPATCH_TILES * ROW_TILE // LANES, N_EXPERTS, LANES),
                                   lambda i, *_: (i, 0, 0))],
            out_specs=pl.BlockSpec(memory_space=pl.ANY),
            scratch_shapes=[pltpu.VMEM((N_EXPERTS, cap, XS_W), BF16),
                            pltpu.SemaphoreType.DMA((FLUSH_PARTS,))]),
        compiler_params=_params(1, VMEM_LIMIT),
        name="dispatch",
    )(offs, nwin, flush, h2aug.reshape(stripes, n // stripes, XS_W), pos)


FF_CHUNK = 512
EXP_ROWS = 512


def _experts_kernel(n_groups, n_chunks, *refs):
    xs_refs = refs[:n_groups]
    wg_ref, wu_ref, wd_ref = refs[n_groups:n_groups + 3]
    ys_refs = refs[n_groups + 3:2 * n_groups + 3]
    acc_ref = refs[2 * n_groups + 3]
    e = pl.program_id(0)
    j = pl.program_id(1)
    cap = xs_refs[0].shape[1]

    @pl.when(j == 0)
    def _():
        for g in range(n_groups):
            acc_ref[g] = jnp.zeros(acc_ref.shape[1:], F32)

    wg = wg_ref[0].astype(BF16)
    wu = wu_ref[0].astype(BF16)
    wd = wd_ref[0].astype(BF16)
    blocks = [(g, slice(r * EXP_ROWS, (r + 1) * EXP_ROWS))
              for g in range(n_groups) for r in range(cap // EXP_ROWS)]

    def up(blk):
        g, rows = blk
        x = xs_refs[g][0, rows, :D_MODEL]
        return _bdot(x, wg), _bdot(x, wu)

    nxt = up(blocks[0])
    for k, (g, rows) in enumerate(blocks):
        a, u = nxt
        if k + 1 < len(blocks):
            nxt = up(blocks[k + 1])
        hm = (a * jax.nn.sigmoid(a) * u).astype(BF16)
        acc_ref[g, rows, :] += _bdot(hm, wd)

    @pl.when(j == n_chunks - 1)
    def _():
        for g, (xs_ref, ys_ref) in enumerate(zip(xs_refs, ys_refs)):
            aug = xs_ref[0, :, D_MODEL:].astype(F32)
            lane = lax.broadcasted_iota(I32, aug.shape, 1)
            mine = (lane == e) | (lane == e + N_EXPERTS) | (lane == e + 2 * N_EXPERTS)
            gate = jnp.sum(jnp.where(mine, aug, 0.0), axis=-1, keepdims=True)
            ys_ref[0] = (acc_ref[g] * gate).astype(BF16)


def _experts(xs_list, w_gate, w_up, w_down):
    n_groups = len(xs_list)
    cap = xs_list[0].shape[1]
    ff = w_gate.shape[2]
    n_chunks = ff // FF_CHUNK
    xs_spec = pl.BlockSpec((1, cap, XS_W), lambda e, j: (e, 0, 0))
    up_spec = pl.BlockSpec((1, D_MODEL, FF_CHUNK), lambda e, j: (e, 0, j))
    down_spec = pl.BlockSpec((1, FF_CHUNK, D_MODEL), lambda e, j: (e, j, 0))
    ys_spec = pl.BlockSpec((1, cap, D_MODEL), lambda e, j: (e, 0, 0))
    return pl.pallas_call(
        functools.partial(_experts_kernel, n_groups, n_chunks),
        out_shape=[jax.ShapeDtypeStruct((N_EXPERTS, cap, D_MODEL), BF16)] * n_groups,
        grid=(N_EXPERTS, n_chunks),
        in_specs=[xs_spec] * n_groups + [up_spec, up_spec, down_spec],
        out_specs=[ys_spec] * n_groups,
        scratch_shapes=[pltpu.VMEM((n_groups, cap, D_MODEL), F32)],
        compiler_params=_params(2, VMEM_LIMIT),
        name="experts",
    )(*xs_list, w_gate, w_up, w_down)


def _combine_kernel(cap, n_tiles, off_ref, nwin_ref, ys_hbm, pos_ref, x1_ref, g2_ref, fg_ref,
                    y_ref, buf_ref, sems):
    i = pl.program_id(0)
    slot = i % 2

    def copies(tile, w, s):
        out = []
        for e in range(N_EXPERTS):
            _, off = _window(off_ref, e, tile, n_tiles, w, cap)
            out.append(pltpu.make_async_copy(
                ys_hbm.at[e, pl.ds(off, SLOT_WIN), :],
                buf_ref.at[s, pl.ds(e * SLOT_WIN, SLOT_WIN), :],
                sems.at[s]))
        return out

    @pl.when(i == 0)
    def _():
        for cp in copies(0, 0, 0):
            cp.start()

    @pl.when(i + 1 < n_tiles)
    def _():
        for cp in copies(i + 1, 0, 1 - slot):
            cp.start()

    tn = (((0,), (0,)), ((), ()))

    def gathered(w):
        wins = [_window(off_ref, e, i, n_tiles, w, cap) for e in range(N_EXPERTS)]
        p = _one_hot_rows(pos_ref, wins)
        for cp in copies(i, w, slot):
            cp.wait()
        return lax.dot_general(p, buf_ref[slot], tn, preferred_element_type=F32)

    def extra_window(w, acc):
        for cp in copies(i, w, slot):
            cp.start()
        return acc + gathered(w)

    moe = lax.fori_loop(1, nwin_ref[i], extra_window, gathered(0))
    out = x1_ref[...] + g2_ref[...] * moe.reshape(x1_ref.shape)
    y_ref[...] = _rms(out, fg_ref[...])


def _combine(ys, pos, offs, nwin, x1, mod3, mod_row, stripes, final_g, cap):
    n = x1.shape[0]
    n_tiles = n // ROW_TILE
    g2 = pl.BlockSpec((stripes, 1, D_MODEL), lambda i, *_: (0 if mod_row is None else mod_row, 0, 5))
    y = pl.pallas_call(
        functools.partial(_combine_kernel, cap, n_tiles),
        out_shape=jax.ShapeDtypeStruct((stripes, n // stripes, D_MODEL), F32),
        grid_spec=pltpu.PrefetchScalarGridSpec(
            num_scalar_prefetch=2,
            grid=(n_tiles,),
            in_specs=[pl.BlockSpec(memory_space=pl.ANY),
                      pl.BlockSpec((ROW_TILE // LANES, N_EXPERTS, LANES), lambda i, *_: (i, 0, 0)),
                      _tile_spec(stripes, D_MODEL),
                      g2,
                      pl.BlockSpec((1, D_MODEL), lambda i, *_: (0, 0))],
            out_specs=_tile_spec(stripes, D_MODEL),
            scratch_shapes=[pltpu.VMEM((2, N_EXPERTS * SLOT_WIN, D_MODEL), BF16),
                            pltpu.SemaphoreType.DMA((2,))]),
        compiler_params=_params(1, VMEM_LIMIT),
        name="combine",
    )(offs, nwin, ys, pos, x1.reshape(stripes, n // stripes, D_MODEL), mod3, final_g)
    return y.reshape(n, D_MODEL)


def _head_cols(w3):
    k, _, d = w3.shape
    return jnp.pad(w3, ((0, 0), (0, 0), (0, HEAD_PAD - d))).reshape(k, QK_WIDTH)


def _value_cols(v3):
    k = v3.shape[0]
    pair = v3.reshape(k, MLA_HEADS // 2, 2, V_DIM)
    zeros = jnp.zeros((k, MLA_HEADS // 2, V_DIM), v3.dtype)
    even = jnp.concatenate([pair[:, :, 0], zeros], axis=-1)
    odd = jnp.concatenate([zeros, pair[:, :, 1]], axis=-1)
    return jnp.stack([even, odd], axis=2).reshape(k, QK_WIDTH)


def _rope_tables(seq):
    half = QK_ROPE // 2
    pos = jnp.arange(seq)
    inv_freq = 1.0 / (ROPE_BASE ** (jnp.arange(0, half, 2, dtype=F32) / half))

    def cs(p):
        ang = p.astype(F32)[:, None] * inv_freq[None, :]
        ang = jnp.concatenate([ang, ang], axis=-1)
        return jnp.cos(ang), jnp.sin(ang)

    cr, sr = cs(pos // GRID_W)
    cc, sc = cs(pos % GRID_W)
    ones = jnp.ones((seq, QK_NOPE), F32)
    zpad = jnp.zeros((seq, HEAD_PAD - QK_NOPE - QK_ROPE), F32)
    cq = jnp.concatenate([ones, cr, cc, zpad], axis=-1)
    sq = jnp.concatenate([0.0 * ones, sr, sc, zpad], axis=-1)
    return cq, sq


def _prep_weights(w_in, q_norm_g, w_q_up, kv_norm_g, w_kv_up, attn_norm_g):
    wqc = w_in[:, :Q_RANK + KV_RANK].astype(BF16)
    wkr = jnp.pad(w_in[:, Q_RANK + KV_RANK:Q_RANK + KV_RANK + QK_ROPE],
                  ((0, 0), (0, LANES - QK_ROPE))).astype(BF16)
    wglu = w_in[:, Q_RANK + KV_RANK + QK_ROPE:].astype(BF16)
    wq = _head_cols(w_q_up.astype(BF16).reshape(Q_RANK, MLA_HEADS, QK_NOPE + QK_ROPE))
    kv3 = w_kv_up.astype(BF16).reshape(KV_RANK, MLA_HEADS, QK_NOPE + V_DIM)
    wk = _head_cols(kv3[:, :, :QK_NOPE])
    wv = _value_cols(kv3[:, :, QK_NOPE:])
    pk = np.zeros((LANES, QK_WIDTH), np.float32)
    for hd in range(MLA_HEADS):
        for j in range(QK_ROPE):
            pk[j, hd * HEAD_PAD + QK_NOPE + j] = 1.0
    pk = jnp.asarray(pk, BF16)
    return (attn_norm_g[None, :], wqc, wkr, wglu, q_norm_g[None, :], wq,
            kv_norm_g[None, :], wk, wv, pk)


def _stripe_order(aff3, stripes, seq):
    per = ROW_TILE // stripes
    q = LANES // per
    assert per * stripes == ROW_TILE and per * q == LANES and stripes % q == 0 and seq % LANES == 0
    a = aff3.reshape(stripes // q, q, seq // LANES, N_EXPERTS, q, per)
    return a.transpose(2, 4, 0, 3, 1, 5).reshape(aff3.shape)


def _route_plan(base, cap):
    rows_per_tile = ROW_TILE // LANES
    start = base[::rows_per_tile, :, 0].astype(I32)
    end = jnp.concatenate([start[1:], jnp.full((1, N_EXPERTS), cap, I32)], axis=0)
    off = (start // BF16_ROWS) * BF16_ROWS
    span = end - off
    nwin = jnp.maximum(jnp.max((span + SLOT_WIN - 1) // SLOT_WIN, axis=1), 1).astype(I32)
    return off.reshape(-1), nwin


def kernel(x_prompt, x_sample, cache_ckv, cache_krope, c, c_ctx, w_mod, b_mod, attn_norm_g,
           w_in, q_norm_g, w_q_up, kv_norm_g, w_kv_up, conv_w, conv_b, conv_ln_g, conv_ln_b,
           w_out, ffn_norm_g, w_router, w_gate, w_up, w_down, final_norm_g):
    depth = w_mod.shape[0]
    assert depth == 1
    bp, sp, _ = x_prompt.shape
    bs, ss, _ = x_sample.shape
    past = cache_ckv.shape[2]
    ctx_row = bs

    mod_rows = 16
    cc = jnp.concatenate([c, c_ctx[None, :], jnp.zeros((mod_rows - bs - 1, D_MODEL), F32)], axis=0)
    mod = _adaln(cc, w_mod[0], b_mod[0][None, :])
    mod3 = mod.reshape(mod_rows, 1, N_MOD * D_MODEL)

    fw = _prep_weights(w_in[0], q_norm_g[0], w_q_up[0], kv_norm_g[0], w_kv_up[0], attn_norm_g[0])
    wk, wv, pk = fw[7], fw[8], fw[9]
    wo = w_out[0].astype(BF16)
    wr = jnp.pad(jnp.tile(w_router[0], (1, 3)), ((0, 0), (0, LANES - 3 * N_EXPERTS)))
    wrh = wr.astype(BF16)
    wrl = (wr - wrh.astype(F32)).astype(BF16)
    cvw = (conv_w[0], conv_b[0][None, :], conv_ln_g[0][None, :], conv_ln_b[0][None, :])
    bw = (wo[:MLA_HEADS * V_DIM], wo[MLA_HEADS * V_DIM:], ffn_norm_g[0][None, :], wrh, wrl)

    xp2 = x_prompt.reshape(bp * sp, D_MODEL)
    xs2 = x_sample.reshape(bs * ss, D_MODEL)

    qp, kp, vp, cv_p, ckvp, krp = _front(xp2, mod3, ctx_row, sp, fw, cvw, None)
    attn_p = _attention(qp, [(kp, vp, sp)], bp, sp)
    x1p, h2p, afftp = _back(xp2, attn_p, cv_p, mod3, ctx_row, bw, sp)

    qs, ks, vs, cv_s = _front(xs2, mod3, None, ss, fw, cvw, _rope_tables(ss))
    kc, vc = _ctx_kv(cache_ckv[:, 0].reshape(bs * past, KV_RANK),
                     cache_krope[:, 0].reshape(bs * past, QK_ROPE), wk, wv, pk)
    attn_s = _attention(qs, [(kc, vc, past), (ks, vs, ss)], bs, ss)
    x1s, h2s, affts = _back(xs2, attn_s, cv_s, mod3, None, bw, ss)

    stripes = (1, bs)
    caps = [EC_FACTOR * h2.shape[0] // N_EXPERTS for h2 in (h2p, h2s)]
    routed = _route([afftp, _stripe_order(affts, bs, ss)], caps)
    plans = []
    xs_list = []
    for g, (h2, cap) in enumerate(zip((h2p, h2s), caps)):
        pos, base = routed[g], routed[len(caps) + g]
        offs, nwin = _route_plan(base, cap)
        plans.append((pos, offs, nwin, cap))
        xs_list.append(_dispatch(h2, pos, offs, nwin, cap, stripes[g]))
    ys_list = _experts(xs_list, w_gate[0], w_up[0], w_down[0])

    fg = final_norm_g[None, :]
    pos, offs, nwin, cap = plans[0]
    y_prompt = _combine(ys_list[0], pos, offs, nwin, x1p, mod3, ctx_row, stripes[0], fg, cap)
    pos, offs, nwin, cap = plans[1]
    y_sample = _combine(ys_list[1], pos, offs, nwin, x1s, mod3, None, stripes[1], fg, cap)

    return (y_prompt.reshape(bp, sp, D_MODEL), y_sample.reshape(bs, ss, D_MODEL),
            ckvp.reshape(bp, depth, sp, KV_RANK), krp.reshape(bp, depth, sp, QK_ROPE))
```

```python
import functools

import jax
import jax.numpy as jnp
import numpy as np
from jax import lax
from jax.experimental import pallas as pl
from jax.experimental.pallas import tpu as pltpu

F32 = jnp.float32
BF16 = jnp.bfloat16
I32 = jnp.int32

LANES = 128
BF16_ROWS = 16
VMEM_LIMIT = 56 * 1024 * 1024

D_MODEL = 1024
GRID_W = 64
MLA_HEADS = 8
QK_NOPE = 64
QK_ROPE = 32
V_DIM = 64
Q_RANK = 768
KV_RANK = 256
CONV_CH = 512
CONV_WIDTH = 31
N_EXPERTS = 16
EXPERT_FF = 1024
EC_FACTOR = 2
ROPE_BASE = 10000.0
EPS = 1e-6
N_MOD = 6

HEAD_PAD = LANES
QK_WIDTH = MLA_HEADS * HEAD_PAD
ROW_TILE = 256
FRONT_TILE = 512
ATTN_TILE = 512
BACK_TILE = 1024
BACK_SUB = 256
SLOT_WIN = 64
AUG = LANES
XS_W = D_MODEL + AUG
LOG2_E = 1.4426950408889634


def _params(n_axes, vmem=None):
    return pltpu.CompilerParams(
        dimension_semantics=("arbitrary",) * n_axes,
        vmem_limit_bytes=vmem)


def _rms(x, g):
    ms = jnp.mean(x * x, axis=-1, keepdims=True)
    return x * lax.rsqrt(ms + EPS) * g


def _bdot(a, b):
    return jnp.dot(a, b, preferred_element_type=F32)


def _adaln_kernel(c_ref, w_ref, b_ref, o_ref):
    c = c_ref[...]
    s = c * jax.nn.sigmoid(c)
    w = w_ref[...]
    sh = s.astype(BF16)
    sl = (s - sh.astype(F32)).astype(BF16)
    wh = w.astype(BF16)
    wl = (w - wh.astype(F32)).astype(BF16)
    o_ref[...] = _bdot(sh, wh) + (_bdot(sl, wh) + _bdot(sh, wl)) + b_ref[...]


def _adaln(cc, w_mod, b_mod):
    rows = cc.shape[0]
    n_out = w_mod.shape[1]
    blk = D_MODEL
    return pl.pallas_call(
        _adaln_kernel,
        out_shape=jax.ShapeDtypeStruct((rows, n_out), F32),
        grid=(n_out // blk,),
        in_specs=[pl.BlockSpec((rows, D_MODEL), lambda j: (0, 0)),
                  pl.BlockSpec((D_MODEL, blk), lambda j: (0, j)),
                  pl.BlockSpec((1, blk), lambda j: (0, j))],
        out_specs=pl.BlockSpec((rows, blk), lambda j: (0, j)),
        compiler_params=_params(1),
        name="adaln",
    )(cc, w_mod, b_mod)


def _rot_half(x, lane, base):
    rel = lane - base
    first = ((rel >= 0) & (rel < 8)) | ((rel >= 16) & (rel < 24))
    return jnp.where(first, -pltpu.roll(x, LANES - 8, axis=1), pltpu.roll(x, 8, axis=1))


def _front_kernel(rope, seq, x_ref, sh_ref, sc_ref, g_ref, wqc_ref, wkr_ref, wglu_ref,
                  qg_ref, wq_ref, kvg_ref, wk_ref, wv_ref, pk_ref, cw_ref, cb_ref, lg_ref, lb_ref,
                  *rest):
    if rope:
        cq_ref, sq_ref, q_ref, k_ref, v_ref, cv_ref, vpad_ref, shift_ref = rest
    else:
        q_ref, k_ref, v_ref, cv_ref, ckv_ref, kr_ref, vpad_ref, shift_ref = rest
    rows = x_ref.shape[0]
    steps_per_seq = max(seq // rows, 1)
    scale = (QK_NOPE + QK_ROPE) ** -0.5 * LOG2_E
    lane = lax.broadcasted_iota(I32, (ROW_TILE, LANES), 1)
    conv_refs = (cw_ref, cb_ref, lg_ref, lb_ref, shift_ref)

    pending = []

    def tick(n=1):
        for _ in range(n):
            if pending:
                pending.pop(0)()

    def gated_values(qd, row_in_seq, slot, cv_row):
        r = slice(qd * ROW_TILE, (qd + 1) * ROW_TILE)
        h = _rms(x_ref[r, :], g_ref[...]) * (1.0 + sc_ref[0]) + sh_ref[0]
        hb = h.astype(BF16)
        glu = _bdot(hb, wglu_ref[...])
        if row_in_seq == 0:
            vpad_ref[slot, 0:CONV_HALO, :] = jnp.zeros((CONV_HALO, CONV_CH), F32)
        if row_in_seq + ROW_TILE == seq:
            vpad_ref[slot, CONV_HALO + seq:, :] = jnp.zeros((CONV_HALO, CONV_CH), F32)
        vpad_ref[slot, CONV_HALO + row_in_seq:CONV_HALO + row_in_seq + ROW_TILE, :] = (
            glu[:, :CONV_CH] * jax.nn.sigmoid(glu[:, CONV_CH:]))

        ready = ([row_in_seq - ROW_TILE] if row_in_seq > 0 else []) + \
                ([row_in_seq] if row_in_seq + ROW_TILE == seq else [])
        for t0 in ready:
            steps, finish = _conv_stages(CONV_STAGES, vpad_ref, slot, t0, *conv_refs)
            dst = slice(cv_row + t0, cv_row + t0 + ROW_TILE)
            pending.extend(steps + [functools.partial(_store_conv, cv_ref, dst, finish)])
        return hb

    def projections(qd, row_in_seq, hb, ticks):
        r = slice(qd * ROW_TILE, (qd + 1) * ROW_TILE)
        qc = _bdot(hb, wqc_ref[...])
        tick(ticks)
        kr = _bdot(hb, wkr_ref[...])
        tick(ticks)
        qn = _rms(qc[:, :Q_RANK], qg_ref[...]).astype(BF16)
        q = _bdot(qn, wq_ref[...])
        tick(ticks)
        ckv = _rms(qc[:, Q_RANK:], kvg_ref[...])
        ckv_b = ckv.astype(BF16)
        kk = _bdot(ckv_b, wk_ref[...])
        tick(ticks)
        v_ref[r, :] = _bdot(ckv_b, wv_ref[...]).astype(BF16)
        tick(ticks)
        if rope:
            t = slice(row_in_seq % rows, row_in_seq % rows + ROW_TILE)
            cq = cq_ref[t, :]
            sq = sq_ref[t, :]
            for hd in range(MLA_HEADS):
                blk = q[:, hd * HEAD_PAD:(hd + 1) * HEAD_PAD]
                rot = blk * cq + _rot_half(blk, lane, QK_NOPE) * sq
                q_ref[r, hd * HEAD_PAD:(hd + 1) * HEAD_PAD] = (rot * scale).astype(BF16)
            krs = pltpu.roll(kr, QK_NOPE, axis=1)
            kr = pltpu.roll(krs * cq + _rot_half(krs, lane, QK_NOPE) * sq, LANES - QK_NOPE, axis=1)
            kr = jnp.where(lane < QK_ROPE, kr, 0.0)
        else:
            q_ref[r, :] = (q * scale).astype(BF16)
            ckv_ref[r, :] = ckv
            kr_ref[r, :] = kr[:, :QK_ROPE]
        tick(ticks)
        k_ref[r, :] = (kk + _bdot(kr.astype(BF16), pk_ref[...])).astype(BF16)

    def step(k):
        n_q = rows // ROW_TILE
        hbs = []
        for qd in range(n_q):
            row = k * rows + qd * ROW_TILE
            if seq >= rows:
                hbs.append((row, gated_values(qd, row, 0, 0)))
            else:
                hbs.append((row % seq, gated_values(qd, row % seq, row // seq, (row // seq) * seq)))
        ticks = -(-len(pending) // (6 * n_q))
        for qd, (row_in_seq, hb) in enumerate(hbs):
            projections(qd, row_in_seq, hb, ticks)
        tick(len(pending))

    if steps_per_seq == 1:
        step(0)
    else:
        for k in range(steps_per_seq):
            pl.when(pl.program_id(0) % steps_per_seq == k)(functools.partial(step, k))


def _store_conv(cv_ref, dst, finish):
    cv_ref[dst, :] = finish()


def _front(x2d, mod3, mod_row, seq, wts, conv_wts, rope_tabs):
    n = x2d.shape[0]
    rope = rope_tabs is not None
    tile = FRONT_TILE
    steps_per_seq = max(seq // tile, 1)
    assert seq % ROW_TILE == 0 and (tile % seq == 0 or seq % tile == 0)
    full = lambda a: pl.BlockSpec(a.shape, lambda i: (0,) * a.ndim)

    def mod_spec(col):
        if mod_row is None:
            assert seq % tile == 0
            return pl.BlockSpec((1, 1, D_MODEL), lambda i: (i // steps_per_seq, 0, col))
        return pl.BlockSpec((1, 1, D_MODEL), lambda i: (mod_row, 0, col))

    row = lambda w: pl.BlockSpec((tile, w), lambda i: (i, 0))
    cv_rows = max(seq, tile)
    cv_spec = pl.BlockSpec((cv_rows, CONV_CH), lambda i: (i // steps_per_seq, 0))
    in_specs = [row(D_MODEL), mod_spec(0), mod_spec(1)] + [full(w) for w in wts + conv_wts]
    args = [x2d, mod3, mod3] + list(wts + conv_wts)
    out_shape = [jax.ShapeDtypeStruct((n, QK_WIDTH), BF16)] * 3 + \
                [jax.ShapeDtypeStruct((n, CONV_CH), BF16)]
    out_specs = [row(QK_WIDTH)] * 3 + [cv_spec]
    if rope:
        tab = pl.BlockSpec((tile, LANES), lambda i: (i % steps_per_seq, 0))
        in_specs += [tab, tab]
        args += list(rope_tabs)
    else:
        out_shape += [jax.ShapeDtypeStruct((n, KV_RANK), F32),
                      jax.ShapeDtypeStruct((n, QK_ROPE), F32)]
        out_specs += [row(KV_RANK), row(QK_ROPE)]
    slots = max(tile // seq, 1)
    return pl.pallas_call(
        functools.partial(_front_kernel, rope, seq),
        out_shape=out_shape,
        grid=(n // tile,),
        in_specs=in_specs,
        out_specs=out_specs,
        scratch_shapes=[pltpu.VMEM((slots, seq + 2 * CONV_HALO, CONV_CH), F32),
                        pltpu.VMEM((SUBLANES - 1, SHIFT_ROWS, CONV_CH), F32)],
        compiler_params=_params(1, VMEM_LIMIT),
        name="front_rope" if rope else "front",
    )(*args)


def _ctx_kernel(ckv_ref, kr_ref, wk_ref, wv_ref, pk_ref, k_ref, v_ref):
    cb = ckv_ref[...].astype(BF16)
    kr = kr_ref[...].astype(BF16)
    k_ref[...] = (_bdot(cb, wk_ref[...]) + _bdot(kr, pk_ref[...])).astype(BF16)
    v_ref[...] = _bdot(cb, wv_ref[...]).astype(BF16)


def _ctx_kv(ckv2d, kr2d, wk, wv, pk):
    n = ckv2d.shape[0]
    full = lambda a: pl.BlockSpec(a.shape, lambda i: (0,) * a.ndim)
    row = lambda w: pl.BlockSpec((ROW_TILE, w), lambda i: (i, 0))
    pk32 = pk[:QK_ROPE]
    return pl.pallas_call(
        _ctx_kernel,
        out_shape=[jax.ShapeDtypeStruct((n, QK_WIDTH), BF16)] * 2,
        grid=(n // ROW_TILE,),
        in_specs=[row(KV_RANK), row(QK_ROPE), full(wk), full(wv), full(pk32)],
        out_specs=[row(QK_WIDTH)] * 2,
        compiler_params=_params(1),
        name="ctx_kv",
    )(ckv2d, kr2d, wk, wv, pk32)


CONV_PAD = (CONV_WIDTH - 1) // 2
CONV_HALO = 16
SUBLANES = 8
SHIFT_ROWS = ROW_TILE + 2 * CONV_HALO - SUBLANES


CONV_STAGES = 6


def _conv_stages(n_stages, vpad_ref, slot, t0, cw_ref, cb_ref, lg_ref, lb_ref, shift_ref):
    state = {}

    def taps(lo, hi):
        if lo == 0:
            win = vpad_ref[slot, t0:t0 + ROW_TILE + 2 * CONV_HALO, :]
            for ph in range(1, SUBLANES):
                shift_ref[ph - 1] = win[ph:ph + SHIFT_ROWS, :]
            state["acc"] = jnp.broadcast_to(cb_ref[...], (ROW_TILE, CONV_CH))
        acc = state["acc"]
        for t in range(lo, hi):
            blk, ph = divmod(t + CONV_HALO - CONV_PAD, SUBLANES)
            if ph == 0:
                tap = vpad_ref[slot, t0 + blk * SUBLANES:t0 + blk * SUBLANES + ROW_TILE, :]
            else:
                tap = shift_ref[ph - 1, blk * SUBLANES:blk * SUBLANES + ROW_TILE, :]
            acc = acc + tap * cw_ref[t:t + 1, :]
        state["acc"] = acc

    def finish():
        acc = state["acc"]
        mu = jnp.mean(acc, axis=-1, keepdims=True)
        cen = acc - mu
        var = jnp.mean(cen * cen, axis=-1, keepdims=True)
        ln = cen * lax.rsqrt(var + EPS) * lg_ref[...] + lb_ref[...]
        return (ln * jax.nn.sigmoid(ln)).astype(BF16)

    bounds = [CONV_WIDTH * s // n_stages for s in range(n_stages + 1)]
    stages = [functools.partial(taps, bounds[s], bounds[s + 1]) for s in range(n_stages)]
    return stages, finish


def _attn_kernel(n_kv, n_seq, q_ref, *rest):
    kv_refs = rest[:2 * n_kv]
    o_ref = rest[2 * n_kv]
    nt = (((1,), (1,)), ((), ()))
    tq = q_ref.shape[0] // n_seq
    work = [(b, hd) for b in range(n_seq) for hd in range(MLA_HEADS)]

    def kv_rows(ref, b):
        t = ref.shape[0] // n_seq
        return slice(b * t, (b + 1) * t)

    def scores(item):
        b, hd = item
        sl = slice(hd * HEAD_PAD, (hd + 1) * HEAD_PAD)
        qh = q_ref[b * tq:(b + 1) * tq, sl]
        return [lax.dot_general(qh, kv_refs[2 * j][kv_rows(kv_refs[2 * j], b), sl], nt,
                                preferred_element_type=F32) for j in range(n_kv)]

    outs = {}
    nxt = scores(work[0])
    for w, (b, hd) in enumerate(work):
        sl = slice(hd * HEAD_PAD, (hd + 1) * HEAD_PAD)
        ss = nxt
        if w + 1 < len(work):
            nxt = scores(work[w + 1])
        m = ss[0].max(axis=-1, keepdims=True)
        for s in ss[1:]:
            m = jnp.maximum(m, s.max(axis=-1, keepdims=True))
        ps = [jnp.exp2(s - m) for s in ss]
        l = ps[0].sum(axis=-1, keepdims=True)
        for p in ps[1:]:
            l = l + p.sum(axis=-1, keepdims=True)
        o = None
        for j in range(n_kv):
            v_ref = kv_refs[2 * j + 1]
            pv = _bdot(ps[j].astype(BF16), v_ref[kv_rows(v_ref, b), sl])
            o = pv if o is None else o + pv
        outs[hd] = o / l
        if hd % 2 == 1:
            o_ref[b * tq:(b + 1) * tq, (hd // 2) * LANES:(hd // 2 + 1) * LANES] = (
                outs[hd - 1] + outs[hd]).astype(BF16)


def _attention(q, kvs, batch, seq):
    n = q.shape[0]
    n_seq = max(ATTN_TILE // seq, 1)
    tile = min(seq, ATTN_TILE) * n_seq
    q_tiles = max(seq // tile, 1)
    assert batch % n_seq == 0
    row = lambda w: pl.BlockSpec((tile, w), lambda b, i: (b * q_tiles + i, 0))
    in_specs = [row(QK_WIDTH)]
    args = [q]
    for k, v, t in kvs:
        spec = pl.BlockSpec((n_seq * t, QK_WIDTH), lambda b, i: (b, 0))
        in_specs += [spec, spec]
        args += [k, v]
    return pl.pallas_call(
        functools.partial(_attn_kernel, len(kvs), n_seq),
        out_shape=jax.ShapeDtypeStruct((n, MLA_HEADS * V_DIM), BF16),
        grid=(batch // n_seq, q_tiles),
        in_specs=in_specs,
        out_specs=row(MLA_HEADS * V_DIM),
        compiler_params=_params(2, VMEM_LIMIT),
        name="attention",
    )(*args)


def _back_kernel(x_ref, attn_ref, cv_ref, woa_ref, woc_ref, g1_ref, sh_ref, sc_ref, fg_ref,
                 wrh_ref, wrl_ref, x1_ref, h2_ref, afft_ref):
    n_sub = x_ref.shape[0] // BACK_SUB
    rows = [slice(s * BACK_SUB, (s + 1) * BACK_SUB) for s in range(n_sub)]

    def project(r):
        return _bdot(attn_ref[r, :], woa_ref[...]) + _bdot(cv_ref[r, :], woc_ref[...])

    def normalise(r, mix):
        x1 = x_ref[r, :] + g1_ref[0] * mix
        x1_ref[r, :] = x1
        h2 = _rms(x1, fg_ref[...]) * (1.0 + sc_ref[0]) + sh_ref[0]
        h2b = h2.astype(BF16)
        h2_ref[r, :D_MODEL] = h2b
        h2l = (h2 - h2b.astype(F32)).astype(BF16)
        wrh = wrh_ref[...]
        return _bdot(h2b, wrh) + (_bdot(h2l, wrh) + _bdot(h2b, wrl_ref[...]))

    def route(s, logits):
        r = rows[s]
        lane = lax.broadcasted_iota(I32, logits.shape, 1)
        real = lane < N_EXPERTS
        m = jnp.max(jnp.where(real, logits, -jnp.inf), axis=-1, keepdims=True)
        e = jnp.exp(logits - m)
        aff = e / jnp.sum(jnp.where(real, e, 0.0), axis=-1, keepdims=True)
        aff_t = aff.T
        for blk in range(BACK_SUB // LANES):
            afft_ref[s * (BACK_SUB // LANES) + blk] = aff_t[:N_EXPERTS, blk * LANES:(blk + 1) * LANES]
        hi = aff.astype(BF16)
        r1 = aff - hi.astype(F32)
        mid = r1.astype(BF16)
        lo = (r1 - mid.astype(F32)).astype(BF16)
        zero = jnp.zeros_like(hi)
        h2_ref[r, D_MODEL:] = jnp.where(
            real, hi, jnp.where(lane < 2 * N_EXPERTS, mid, jnp.where(lane < 3 * N_EXPERTS, lo, zero)))

    mixes = {0: project(rows[0])}
    logits = {}
    for s in range(n_sub):
        if s + 1 < n_sub:
            mixes[s + 1] = project(rows[s + 1])
        logits[s] = normalise(rows[s], mixes.pop(s))
        if s > 0:
            route(s - 1, logits.pop(s - 1))
    route(n_sub - 1, logits.pop(n_sub - 1))


def _back(x2d, attn, cv, mod3, mod_row, wts, seq):
    n = x2d.shape[0]
    tile = BACK_TILE
    full = lambda a: pl.BlockSpec(a.shape, lambda i: (0,) * a.ndim)

    def mod_spec(col):
        if mod_row is None:
            assert seq % tile == 0
            return pl.BlockSpec((1, 1, D_MODEL), lambda i: (i // (seq // tile), 0, col))
        return pl.BlockSpec((1, 1, D_MODEL), lambda i: (mod_row, 0, col))

    row = lambda w: pl.BlockSpec((tile, w), lambda i: (i, 0))
    woa, woc, fg, wrh, wrl = wts
    in_specs = [row(D_MODEL), row(MLA_HEADS * V_DIM), row(CONV_CH), full(woa), full(woc),
                mod_spec(2), mod_spec(3), mod_spec(4), full(fg), full(wrh), full(wrl)]
    return pl.pallas_call(
        _back_kernel,
        out_shape=[jax.ShapeDtypeStruct((n, D_MODEL), F32),
                   jax.ShapeDtypeStruct((n, XS_W), BF16),
                   jax.ShapeDtypeStruct((n // LANES, N_EXPERTS, LANES), F32)],
        grid=(n // tile,),
        in_specs=in_specs,
        out_specs=[row(D_MODEL), row(XS_W),
                   pl.BlockSpec((tile // LANES, N_EXPERTS, LANES), lambda i: (i, 0, 0))],
        compiler_params=_params(1, VMEM_LIMIT),
        name="back",
    )(x2d, attn, cv, woa, woc, mod3, mod3, mod3, fg, wrh, wrl)


def _route_kernel(caps, *refs):
    n_g = len(caps)
    affs = [r[...] for r in refs[:n_g]]
    pos_refs = refs[n_g:2 * n_g]
    base_refs = refs[2 * n_g:]
    ones = jnp.ones((LANES, LANES), BF16)
    l_r = lax.broadcasted_iota(I32, (LANES, LANES), 0)
    l_c = lax.broadcasted_iota(I32, (LANES, LANES), 1)
    tri = jnp.where(l_r <= l_c, 1.0, 0.0).astype(BF16)

    def lane_dot(mask3, w):
        r, e, _ = mask3.shape
        mb = jnp.where(mask3, 1.0, 0.0).astype(BF16).reshape(r * e, LANES)
        return _bdot(mb, w).reshape(r, e, LANES)

    def expert_count(mask3):
        return jnp.sum(lane_dot(mask3, ones), axis=0, keepdims=True)

    def excl_cumsum(mask3):
        rows = mask3.shape[0]
        tot = lane_dot(mask3, ones)
        run = tot
        s = 1
        while s < rows:
            run = run + jnp.concatenate([jnp.zeros((s,) + run.shape[1:], F32), run[:rows - s]], axis=0)
            s *= 2
        before = run - tot
        return before + lane_dot(mask3, tri) - jnp.where(mask3, 1.0, 0.0), before

    def step(t, thrs):
        bit = lax.shift_left(jnp.int32(1), 30 - t)
        out = []
        for aff, thr, cap in zip(affs, thrs, caps):
            cand = thr | bit
            cnt = expert_count(aff >= pltpu.bitcast(cand, F32))
            out.append(jnp.where(cnt >= cap, cand, thr))
        return tuple(out)

    init = tuple(jnp.zeros((1,) + a.shape[1:], I32) for a in affs)
    thrs = lax.fori_loop(0, 31, step, init)
    for aff, thr_bits, cap, pos_ref, base_ref in zip(affs, thrs, caps, pos_refs, base_refs):
        thr = pltpu.bitcast(thr_bits, F32)
        gt = aff > thr
        eq = aff == thr
        need = cap - expert_count(gt)
        tie_rank, _ = excl_cumsum(eq)
        sel = gt | (eq & (tie_rank < need))
        pos, before = excl_cumsum(sel)
        pos_ref[...] = jnp.where(sel, pos, -1.0).astype(I32)
        base_ref[...] = before


def _route(affs, caps):
    shapes = [a.shape for a in affs]
    return pl.pallas_call(
        functools.partial(_route_kernel, tuple(caps)),
        out_shape=[jax.ShapeDtypeStruct(s, I32) for s in shapes] +
                  [jax.ShapeDtypeStruct(s, F32) for s in shapes],
        compiler_params=pltpu.CompilerParams(vmem_limit_bytes=VMEM_LIMIT),
        name="route",
    )(*affs)


def _window(off_ref, e, i, n_tiles, w, cap):
    del n_tiles
    first = off_ref[i * N_EXPERTS + e] + w * SLOT_WIN
    return first, pl.multiple_of(jnp.minimum(first, cap - SLOT_WIN), BF16_ROWS)


def _one_hot_rows(pos_ref, wins, row0=0):
    r = lax.broadcasted_iota(I32, (SLOT_WIN, LANES), 0)
    blocks = []
    for e, (first, off) in enumerate(wins):
        halves = []
        for h in range(ROW_TILE // LANES):
            pe = pos_ref[row0 + h, e:e + 1, :]
            rel = jnp.where(pe >= first, pe - off, -1)
            halves.append(jnp.where(rel == r, 1.0, 0.0).astype(BF16))
        blocks.append(jnp.concatenate(halves, axis=1))
    return jnp.concatenate(blocks, axis=0)


DISPATCH_TILES = 2
FLUSH_PARTS = 4


def _dispatch_kernel(cap, n_steps, off_ref, nwin_ref, flush_ref, h2_ref, pos_ref, xs_hbm,
                     xs_ref, sems):
    i = pl.program_id(0)
    per = h2_ref.shape[1] // DISPATCH_TILES

    @pl.when(i == 0)
    def _():
        def zero(e, carry):
            xs_ref[e] = jnp.zeros(xs_ref.shape[1:], BF16)
            return carry
        lax.fori_loop(0, N_EXPERTS, zero, 0)

    def gather(t, w):
        tile = i * DISPATCH_TILES + t
        wins = [_window(off_ref, e, tile, None, w, cap) for e in range(N_EXPERTS)]
        p = _one_hot_rows(pos_ref, wins, t * (ROW_TILE // LANES))
        h2 = h2_ref[:, t * per:(t + 1) * per, :].reshape(ROW_TILE, XS_W)
        return wins, _bdot(p, h2)

    def merge(wins, slab):
        for e, (_, off) in enumerate(wins):
            cur = xs_ref[e, pl.ds(off, SLOT_WIN), :].astype(F32)
            xs_ref[e, pl.ds(off, SLOT_WIN), :] = (
                cur + slab[e * SLOT_WIN:(e + 1) * SLOT_WIN]).astype(BF16)

    for wins, slab in [gather(t, 0) for t in range(DISPATCH_TILES)]:
        merge(wins, slab)
    for t in range(DISPATCH_TILES):
        def extra(w, carry, t=t):
            merge(*gather(t, w))
            return carry
        lax.fori_loop(1, nwin_ref[i * DISPATCH_TILES + t], extra, 0)

    part = cap // FLUSH_PARTS

    def flush(q):
        rows = pl.ds(q * part, part)
        return pltpu.make_async_copy(xs_ref.at[:, rows, :], xs_hbm.at[:, rows, :], sems.at[q])

    for q in range(FLUSH_PARTS):
        @pl.when(i == flush_ref[q])
        def _(q=q):
            flush(q).start()

    @pl.when(i == n_steps - 1)
    def _():
        for q in range(FLUSH_PARTS):
            flush(q).wait()


def _tile_spec(stripes, width, tiles=1):
    return pl.BlockSpec((stripes, tiles * ROW_TILE // stripes, width), lambda i, *_: (0, i, 0))


def _dispatch(h2aug, pos, offs, nwin, cap, stripes):
    n = h2aug.shape[0]
    n_steps = n // (ROW_TILE * DISPATCH_TILES)
    low = jnp.min(offs.reshape(-1, N_EXPERTS), axis=1)[DISPATCH_TILES::DISPATCH_TILES]
    low = jnp.concatenate([low, jnp.full((1,), cap, I32)])
    bounds = (jnp.arange(FLUSH_PARTS, dtype=I32) + 1) * (cap // FLUSH_PARTS)
    flush = jnp.argmax(low[None, :] >= bounds[:, None], axis=1).astype(I32)
    return pl.pallas_call(
        functools.partial(_dispatch_kernel, cap, n_steps),
        out_shape=jax.ShapeDtypeStruct((N_EXPERTS, cap, XS_W), BF16),
        grid_spec=pltpu.PrefetchScalarGridSpec(
            num_scalar_prefetch=3,
            grid=(n_steps,),
            in_specs=[_tile_spec(stripes, XS_W, DISPATCH_TILES),
                      pl.BlockSpec((DISPATCH_TILES * ROW_TILE // LANES, N_EXPERTS, LANES),
                                   lambda i, *_: (i, 0, 0))],
            out_specs=pl.BlockSpec(memory_space=pl.ANY),
            scratch_shapes=[pltpu.VMEM((N_EXPERTS, cap, XS_W), BF16),
                            pltpu.SemaphoreType.DMA((FLUSH_PARTS,))]),
        compiler_params=_params(1, VMEM_LIMIT),
        name="dispatch",
    )(offs, nwin, flush, h2aug.reshape(stripes, n // stripes, XS_W), pos)


FF_CHUNK = 512
EXP_ROWS = 512


def _experts_kernel(n_groups, n_chunks, *refs):
    xs_refs = refs[:n_groups]
    wg_ref, wu_ref, wd_ref = refs[n_groups:n_groups + 3]
    ys_refs = refs[n_groups + 3:2 * n_groups + 3]
    acc_ref = refs[2 * n_groups + 3]
    e = pl.program_id(0)
    j = pl.program_id(1)
    cap = xs_refs[0].shape[1]

    @pl.when(j == 0)
    def _():
        for g in range(n_groups):
            acc_ref[g] = jnp.zeros(acc_ref.shape[1:], F32)

    wg = wg_ref[0].astype(BF16)
    wu = wu_ref[0].astype(BF16)
    wd = wd_ref[0].astype(BF16)
    blocks = [(g, slice(r * EXP_ROWS, (r + 1) * EXP_ROWS))
              for g in range(n_groups) for r in range(cap // EXP_ROWS)]

    def up(blk):
        g, rows = blk
        x = xs_refs[g][0, rows, :D_MODEL]
        return _bdot(x, wg), _bdot(x, wu)

    nxt = up(blocks[0])
    for k, (g, rows) in enumerate(blocks):
        a, u = nxt
        if k + 1 < len(blocks):
            nxt = up(blocks[k + 1])
        hm = (a * jax.nn.sigmoid(a) * u).astype(BF16)
        acc_ref[g, rows, :] += _bdot(hm, wd)

    @pl.when(j == n_chunks - 1)
    def _():
        for g, (xs_ref, ys_ref) in enumerate(zip(xs_refs, ys_refs)):
            aug = xs_ref[0, :, D_MODEL:].astype(F32)
            lane = lax.broadcasted_iota(I32, aug.shape, 1)
            mine = (lane == e) | (lane == e + N_EXPERTS) | (lane == e + 2 * N_EXPERTS)
            gate = jnp.sum(jnp.where(mine, aug, 0.0), axis=-1, keepdims=True)
            ys_ref[0] = (acc_ref[g] * gate).astype(BF16)


def _experts(xs_list, w_gate, w_up, w_down):
    n_groups = len(xs_list)
    cap = xs_list[0].shape[1]
    ff = w_gate.shape[2]
    n_chunks = ff // FF_CHUNK
    xs_spec = pl.BlockSpec((1, cap, XS_W), lambda e, j: (e, 0, 0))
    up_spec = pl.BlockSpec((1, D_MODEL, FF_CHUNK), lambda e, j: (e, 0, j))
    down_spec = pl.BlockSpec((1, FF_CHUNK, D_MODEL), lambda e, j: (e, j, 0))
    ys_spec = pl.BlockSpec((1, cap, D_MODEL), lambda e, j: (e, 0, 0))
    return pl.pallas_call(
        functools.partial(_experts_kernel, n_groups, n_chunks),
        out_shape=[jax.ShapeDtypeStruct((N_EXPERTS, cap, D_MODEL), BF16)] * n_groups,
        grid=(N_EXPERTS, n_chunks),
        in_specs=[xs_spec] * n_groups + [up_spec, up_spec, down_spec],
        out_specs=[ys_spec] * n_groups,
        scratch_shapes=[pltpu.VMEM((n_groups, cap, D_MODEL), F32)],
        compiler_params=_params(2, VMEM_LIMIT),
        name="experts",
    )(*xs_list, w_gate, w_up, w_down)


COMBINE_TILES = 2


def _combine_kernel(cap, n_steps, off_ref, nwin_ref, ys_hbm, pos_ref, x1_ref, g2_ref, fg_ref,
                    y_ref, buf_ref, sems):
    i = pl.program_id(0)
    slot = i % 2
    per = x1_ref.shape[1] // COMBINE_TILES

    def copies(step, t, w, s):
        out = []
        for e in range(N_EXPERTS):
            _, off = _window(off_ref, e, step * COMBINE_TILES + t, None, w, cap)
            out.append(pltpu.make_async_copy(
                ys_hbm.at[e, pl.ds(off, SLOT_WIN), :],
                buf_ref.at[s, t, pl.ds(e * SLOT_WIN, SLOT_WIN), :],
                sems.at[s, t]))
        return out

    @pl.when(i == 0)
    def _():
        for t in range(COMBINE_TILES):
            for cp in copies(0, t, 0, 0):
                cp.start()

    @pl.when(i + 1 < n_steps)
    def _():
        for t in range(COMBINE_TILES):
            for cp in copies(i + 1, t, 0, 1 - slot):
                cp.start()

    tn = (((0,), (0,)), ((), ()))

    def one_hot(t, w):
        wins = [_window(off_ref, e, i * COMBINE_TILES + t, None, w, cap) for e in range(N_EXPERTS)]
        return _one_hot_rows(pos_ref, wins, t * (ROW_TILE // LANES))

    def scattered(t, w, p):
        for cp in copies(i, t, w, slot):
            cp.wait()
        return lax.dot_general(p, buf_ref[slot, t], tn, preferred_element_type=F32)

    ps = [one_hot(t, 0) for t in range(COMBINE_TILES)]
    moes = [scattered(t, 0, ps[t]) for t in range(COMBINE_TILES)]
    for t in range(COMBINE_TILES):
        def extra_window(w, acc, t=t):
            for cp in copies(i, t, w, slot):
                cp.start()
            return acc + scattered(t, w, one_hot(t, w))
        moe = lax.fori_loop(1, nwin_ref[i * COMBINE_TILES + t], extra_window, moes[t])
        rows = slice(t * per, (t + 1) * per)
        x1 = x1_ref[:, rows, :]
        out = x1 + g2_ref[...] * moe.reshape(x1.shape)
        y_ref[:, rows, :] = _rms(out, fg_ref[...])


def _combine(ys, pos, offs, nwin, x1, mod3, mod_row, stripes, final_g, cap):
    n = x1.shape[0]
    n_steps = n // (ROW_TILE * COMBINE_TILES)
    g2 = pl.BlockSpec((stripes, 1, D_MODEL), lambda i, *_: (0 if mod_row is None else mod_row, 0, 5))
    y = pl.pallas_call(
        functools.partial(_combine_kernel, cap, n_steps),
        out_shape=jax.ShapeDtypeStruct((stripes, n // stripes, D_MODEL), F32),
        grid_spec=pltpu.PrefetchScalarGridSpec(
            num_scalar_prefetch=2,
            grid=(n_steps,),
            in_specs=[pl.BlockSpec(memory_space=pl.ANY),
                      pl.BlockSpec((COMBINE_TILES * ROW_TILE // LANES, N_EXPERTS, LANES),
                                   lambda i, *_: (i, 0, 0)),
                      _tile_spec(stripes, D_MODEL, COMBINE_TILES),
                      g2,
                      pl.BlockSpec((1, D_MODEL), lambda i, *_: (0, 0))],
            out_specs=_tile_spec(stripes, D_MODEL, COMBINE_TILES),
            scratch_shapes=[pltpu.VMEM((2, COMBINE_TILES, N_EXPERTS * SLOT_WIN, D_MODEL), BF16),
                            pltpu.SemaphoreType.DMA((2, COMBINE_TILES))]),
        compiler_params=_params(1, VMEM_LIMIT),
        name="combine",
    )(offs, nwin, ys, pos, x1.reshape(stripes, n // stripes, D_MODEL), mod3, final_g)
    return y.reshape(n, D_MODEL)


def _head_cols(w3):
    k, _, d = w3.shape
    return jnp.pad(w3, ((0, 0), (0, 0), (0, HEAD_PAD - d))).reshape(k, QK_WIDTH)


def _value_cols(v3):
    k = v3.shape[0]
    pair = v3.reshape(k, MLA_HEADS // 2, 2, V_DIM)
    zeros = jnp.zeros((k, MLA_HEADS // 2, V_DIM), v3.dtype)
    even = jnp.concatenate([pair[:, :, 0], zeros], axis=-1)
    odd = jnp.concatenate([zeros, pair[:, :, 1]], axis=-1)
    return jnp.stack([even, odd], axis=2).reshape(k, QK_WIDTH)


def _rope_tables(seq):
    half = QK_ROPE // 2
    pos = jnp.arange(seq)
    inv_freq = 1.0 / (ROPE_BASE ** (jnp.arange(0, half, 2, dtype=F32) / half))

    def cs(p):
        ang = p.astype(F32)[:, None] * inv_freq[None, :]
        ang = jnp.concatenate([ang, ang], axis=-1)
        return jnp.cos(ang), jnp.sin(ang)

    cr, sr = cs(pos // GRID_W)
    cc, sc = cs(pos % GRID_W)
    ones = jnp.ones((seq, QK_NOPE), F32)
    zpad = jnp.zeros((seq, HEAD_PAD - QK_NOPE - QK_ROPE), F32)
    cq = jnp.concatenate([ones, cr, cc, zpad], axis=-1)
    sq = jnp.concatenate([0.0 * ones, sr, sc, zpad], axis=-1)
    return cq, sq


def _prep_weights(w_in, q_norm_g, w_q_up, kv_norm_g, w_kv_up, attn_norm_g):
    wqc = w_in[:, :Q_RANK + KV_RANK].astype(BF16)
    wkr = jnp.pad(w_in[:, Q_RANK + KV_RANK:Q_RANK + KV_RANK + QK_ROPE],
                  ((0, 0), (0, LANES - QK_ROPE))).astype(BF16)
    wglu = w_in[:, Q_RANK + KV_RANK + QK_ROPE:].astype(BF16)
    wq = _head_cols(w_q_up.astype(BF16).reshape(Q_RANK, MLA_HEADS, QK_NOPE + QK_ROPE))
    kv3 = w_kv_up.astype(BF16).reshape(KV_RANK, MLA_HEADS, QK_NOPE + V_DIM)
    wk = _head_cols(kv3[:, :, :QK_NOPE])
    wv = _value_cols(kv3[:, :, QK_NOPE:])
    pk = np.zeros((LANES, QK_WIDTH), np.float32)
    for hd in range(MLA_HEADS):
        for j in range(QK_ROPE):
            pk[j, hd * HEAD_PAD + QK_NOPE + j] = 1.0
    pk = jnp.asarray(pk, BF16)
    return (attn_norm_g[None, :], wqc, wkr, wglu, q_norm_g[None, :], wq,
            kv_norm_g[None, :], wk, wv, pk)


def _stripe_order(aff3, stripes, seq):
    per = ROW_TILE // stripes
    q = LANES // per
    assert per * stripes == ROW_TILE and per * q == LANES and stripes % q == 0 and seq % LANES == 0
    a = aff3.reshape(stripes // q, q, seq // LANES, N_EXPERTS, q, per)
    return a.transpose(2, 4, 0, 3, 1, 5).reshape(aff3.shape)


def _route_plan(base, cap):
    rows_per_tile = ROW_TILE // LANES
    start = base[::rows_per_tile, :, 0].astype(I32)
    end = jnp.concatenate([start[1:], jnp.full((1, N_EXPERTS), cap, I32)], axis=0)
    off = (start // BF16_ROWS) * BF16_ROWS
    span = end - off
    nwin = jnp.maximum(jnp.max((span + SLOT_WIN - 1) // SLOT_WIN, axis=1), 1).astype(I32)
    return off.reshape(-1), nwin


def kernel(x_prompt, x_sample, cache_ckv, cache_krope, c, c_ctx, w_mod, b_mod, attn_norm_g,
           w_in, q_norm_g, w_q_up, kv_norm_g, w_kv_up, conv_w, conv_b, conv_ln_g, conv_ln_b,
           w_out, ffn_norm_g, w_router, w_gate, w_up, w_down, final_norm_g):
    depth = w_mod.shape[0]
    assert depth == 1
    bp, sp, _ = x_prompt.shape
    bs, ss, _ = x_sample.shape
    past = cache_ckv.shape[2]
    ctx_row = bs

    mod_rows = 16
    cc = jnp.concatenate([c, c_ctx[None, :], jnp.zeros((mod_rows - bs - 1, D_MODEL), F32)], axis=0)
    mod = _adaln(cc, w_mod[0], b_mod[0][None, :])
    mod3 = mod.reshape(mod_rows, 1, N_MOD * D_MODEL)

    fw = _prep_weights(w_in[0], q_norm_g[0], w_q_up[0], kv_norm_g[0], w_kv_up[0], attn_norm_g[0])
    wk, wv, pk = fw[7], fw[8], fw[9]
    wo = w_out[0].astype(BF16)
    wr = jnp.pad(jnp.tile(w_router[0], (1, 3)), ((0, 0), (0, LANES - 3 * N_EXPERTS)))
    wrh = wr.astype(BF16)
    wrl = (wr - wrh.astype(F32)).astype(BF16)
    cvw = (conv_w[0], conv_b[0][None, :], conv_ln_g[0][None, :], conv_ln_b[0][None, :])
    bw = (wo[:MLA_HEADS * V_DIM], wo[MLA_HEADS * V_DIM:], ffn_norm_g[0][None, :], wrh, wrl)

    xp2 = x_prompt.reshape(bp * sp, D_MODEL)
    xs2 = x_sample.reshape(bs * ss, D_MODEL)

    qp, kp, vp, cv_p, ckvp, krp = _front(xp2, mod3, ctx_row, sp, fw, cvw, None)
    attn_p = _attention(qp, [(kp, vp, sp)], bp, sp)
    x1p, h2p, afftp = _back(xp2, attn_p, cv_p, mod3, ctx_row, bw, sp)

    qs, ks, vs, cv_s = _front(xs2, mod3, None, ss, fw, cvw, _rope_tables(ss))
    kc, vc = _ctx_kv(cache_ckv[:, 0].reshape(bs * past, KV_RANK),
                     cache_krope[:, 0].reshape(bs * past, QK_ROPE), wk, wv, pk)
    attn_s = _attention(qs, [(kc, vc, past), (ks, vs, ss)], bs, ss)
    x1s, h2s, affts = _back(xs2, attn_s, cv_s, mod3, None, bw, ss)

    stripes = (1, bs)
    caps = [EC_FACTOR * h2.shape[0] // N_EXPERTS for h2 in (h2p, h2s)]
    routed = _route([afftp, _stripe_order(affts, bs, ss)], caps)
    plans = []
    xs_list = []
    for g, (h2, cap) in enumerate(zip((h2p, h2s), caps)):
        pos, base = routed[g], routed[len(caps) + g]
        offs, nwin = _route_plan(base, cap)
        plans.append((pos, offs, nwin, cap))
        xs_list.append(_dispatch(h2, pos, offs, nwin, cap, stripes[g]))
    ys_list = _experts(xs_list, w_gate[0], w_up[0], w_down[0])

    fg = final_norm_g[None, :]
    pos, offs, nwin, cap = plans[0]
    y_prompt = _combine(ys_list[0], pos, offs, nwin, x1p, mod3, ctx_row, stripes[0], fg, cap)
    pos, offs, nwin, cap = plans[1]
    y_sample = _combine(ys_list[1], pos, offs, nwin, x1s, mod3, None, stripes[1], fg, cap)

    return (y_prompt.reshape(bp, sp, D_MODEL), y_sample.reshape(bs, ss, D_MODEL),
            ckvp.reshape(bp, depth, sp, KV_RANK), krp.reshape(bp, depth, sp, QK_ROPE))
```

```python
import functools

import jax
import jax.numpy as jnp
import numpy as np
from jax import lax
from jax.experimental import pallas as pl
from jax.experimental.pallas import tpu as pltpu

F32 = jnp.float32
BF16 = jnp.bfloat16
I32 = jnp.int32

LANES = 128
BF16_ROWS = 16
VMEM_LIMIT = 56 * 1024 * 1024

D_MODEL = 1024
GRID_W = 64
MLA_HEADS = 8
QK_NOPE = 64
QK_ROPE = 32
V_DIM = 64
Q_RANK = 768
KV_RANK = 256
CONV_CH = 512
CONV_WIDTH = 31
N_EXPERTS = 16
EXPERT_FF = 1024
EC_FACTOR = 2
ROPE_BASE = 10000.0
EPS = 1e-6
N_MOD = 6

HEAD_PAD = LANES
QK_WIDTH = MLA_HEADS * HEAD_PAD
ROW_TILE = 256
FRONT_TILE = 512
ATTN_TILE = 512
BACK_TILE = 1024
BACK_SUB = 256
SLOT_WIN = 64
AUG = LANES
XS_W = D_MODEL + AUG
LOG2_E = 1.4426950408889634


def _params(n_axes, vmem=None):
    return pltpu.CompilerParams(
        dimension_semantics=("arbitrary",) * n_axes,
        vmem_limit_bytes=vmem)


def _rms(x, g):
    ms = jnp.mean(x * x, axis=-1, keepdims=True)
    return x * lax.rsqrt(ms + EPS) * g


def _bdot(a, b):
    return jnp.dot(a, b, preferred_element_type=F32)


def _adaln_kernel(c_ref, w_ref, b_ref, o_ref):
    c = c_ref[...]
    s = c * jax.nn.sigmoid(c)
    w = w_ref[...]
    sh = s.astype(BF16)
    sl = (s - sh.astype(F32)).astype(BF16)
    wh = w.astype(BF16)
    wl = (w - wh.astype(F32)).astype(BF16)
    o_ref[...] = _bdot(sh, wh) + (_bdot(sl, wh) + _bdot(sh, wl)) + b_ref[...]


def _adaln(cc, w_mod, b_mod):
    rows = cc.shape[0]
    n_out = w_mod.shape[1]
    blk = D_MODEL
    return pl.pallas_call(
        _adaln_kernel,
        out_shape=jax.ShapeDtypeStruct((rows, n_out), F32),
        grid=(n_out // blk,),
        in_specs=[pl.BlockSpec((rows, D_MODEL), lambda j: (0, 0)),
                  pl.BlockSpec((D_MODEL, blk), lambda j: (0, j)),
                  pl.BlockSpec((1, blk), lambda j: (0, j))],
        out_specs=pl.BlockSpec((rows, blk), lambda j: (0, j)),
        compiler_params=_params(1),
        name="adaln",
    )(cc, w_mod, b_mod)


def _rot_half(x, lane, base):
    rel = lane - base
    first = ((rel >= 0) & (rel < 8)) | ((rel >= 16) & (rel < 24))
    return jnp.where(first, -pltpu.roll(x, LANES - 8, axis=1), pltpu.roll(x, 8, axis=1))


def _front_kernel(rope, seq, x_ref, sh_ref, sc_ref, g_ref, wqc_ref, wkr_ref, wglu_ref,
                  qg_ref, wq_ref, kvg_ref, wk_ref, wv_ref, pk_ref, cw_ref, cb_ref, lg_ref, lb_ref,
                  *rest):
    if rope:
        cq_ref, sq_ref, q_ref, k_ref, v_ref, cv_ref, vpad_ref, shift_ref = rest
    else:
        q_ref, k_ref, v_ref, cv_ref, ckv_ref, kr_ref, vpad_ref, shift_ref = rest
    rows = x_ref.shape[0]
    steps_per_seq = max(seq // rows, 1)
    scale = (QK_NOPE + QK_ROPE) ** -0.5 * LOG2_E
    lane = lax.broadcasted_iota(I32, (ROW_TILE, LANES), 1)
    conv_refs = (cw_ref, cb_ref, lg_ref, lb_ref, shift_ref)

    pending = []

    def tick(n=1):
        for _ in range(n):
            if pending:
                pending.pop(0)()

    def gated_values(qd, row_in_seq, slot, cv_row):
        r = slice(qd * ROW_TILE, (qd + 1) * ROW_TILE)
        h = _rms(x_ref[r, :], g_ref[...]) * (1.0 + sc_ref[0]) + sh_ref[0]
        hb = h.astype(BF16)
        glu = _bdot(hb, wglu_ref[...])
        if row_in_seq == 0:
            vpad_ref[slot, 0:CONV_HALO, :] = jnp.zeros((CONV_HALO, CONV_CH), F32)
        if row_in_seq + ROW_TILE == seq:
            vpad_ref[slot, CONV_HALO + seq:, :] = jnp.zeros((CONV_HALO, CONV_CH), F32)
        vpad_ref[slot, CONV_HALO + row_in_seq:CONV_HALO + row_in_seq + ROW_TILE, :] = (
            glu[:, :CONV_CH] * jax.nn.sigmoid(glu[:, CONV_CH:]))

        ready = ([row_in_seq - ROW_TILE] if row_in_seq > 0 else []) + \
                ([row_in_seq] if row_in_seq + ROW_TILE == seq else [])
        for t0 in ready:
            steps, finish = _conv_stages(CONV_STAGES, vpad_ref, slot, t0, *conv_refs)
            dst = slice(cv_row + t0, cv_row + t0 + ROW_TILE)
            pending.extend(steps + [functools.partial(_store_conv, cv_ref, dst, finish)])
        return hb

    def projections(qd, row_in_seq, hb, ticks):
        r = slice(qd * ROW_TILE, (qd + 1) * ROW_TILE)
        qc = _bdot(hb, wqc_ref[...])
        tick(ticks)
        kr = _bdot(hb, wkr_ref[...])
        tick(ticks)
        qn = _rms(qc[:, :Q_RANK], qg_ref[...]).astype(BF16)
        q = _bdot(qn, wq_ref[...])
        tick(ticks)
        ckv = _rms(qc[:, Q_RANK:], kvg_ref[...])
        ckv_b = ckv.astype(BF16)
        kk = _bdot(ckv_b, wk_ref[...])
        tick(ticks)
        v_ref[r, :] = _bdot(ckv_b, wv_ref[...]).astype(BF16)
        tick(ticks)
        if rope:
            t = slice(row_in_seq % rows, row_in_seq % rows + ROW_TILE)
            cq = cq_ref[t, :]
            sq = sq_ref[t, :]
            for hd in range(MLA_HEADS):
                blk = q[:, hd * HEAD_PAD:(hd + 1) * HEAD_PAD]
                rot = blk * cq + _rot_half(blk, lane, QK_NOPE) * sq
                q_ref[r, hd * HEAD_PAD:(hd + 1) * HEAD_PAD] = (rot * scale).astype(BF16)
            krs = pltpu.roll(kr, QK_NOPE, axis=1)
            kr = pltpu.roll(krs * cq + _rot_half(krs, lane, QK_NOPE) * sq, LANES - QK_NOPE, axis=1)
            kr = jnp.where(lane < QK_ROPE, kr, 0.0)
        else:
            q_ref[r, :] = (q * scale).astype(BF16)
            ckv_ref[r, :] = ckv
            kr_ref[r, :] = kr[:, :QK_ROPE]
        tick(ticks)
        k_ref[r, :] = (kk + _bdot(kr.astype(BF16), pk_ref[...])).astype(BF16)

    def step(k):
        n_q = rows // ROW_TILE
        hbs = []
        for qd in range(n_q):
            row = k * rows + qd * ROW_TILE
            if seq >= rows:
                hbs.append((row, gated_values(qd, row, 0, 0)))
            else:
                hbs.append((row % seq, gated_values(qd, row % seq, row // seq, (row // seq) * seq)))
        ticks = -(-len(pending) // (6 * n_q))
        for qd, (row_in_seq, hb) in enumerate(hbs):
            projections(qd, row_in_seq, hb, ticks)
        tick(len(pending))

    if steps_per_seq == 1:
        step(0)
    else:
        for k in range(steps_per_seq):
            pl.when(pl.program_id(0) % steps_per_seq == k)(functools.partial(step, k))


def _store_conv(cv_ref, dst, finish):
    cv_ref[dst, :] = finish()


def _front(x2d, mod3, mod_row, seq, wts, conv_wts, rope_tabs):
    n = x2d.shape[0]
    rope = rope_tabs is not None
    tile = FRONT_TILE
    steps_per_seq = max(seq // tile, 1)
    assert seq % ROW_TILE == 0 and (tile % seq == 0 or seq % tile == 0)
    full = lambda a: pl.BlockSpec(a.shape, lambda i: (0,) * a.ndim)

    def mod_spec(col):
        if mod_row is None:
            assert seq % tile == 0
            return pl.BlockSpec((1, 1, D_MODEL), lambda i: (i // steps_per_seq, 0, col))
        return pl.BlockSpec((1, 1, D_MODEL), lambda i: (mod_row, 0, col))

    row = lambda w: pl.BlockSpec((tile, w), lambda i: (i, 0))
    cv_rows = max(seq, tile)
    cv_spec = pl.BlockSpec((cv_rows, CONV_CH), lambda i: (i // steps_per_seq, 0))
    in_specs = [row(D_MODEL), mod_spec(0), mod_spec(1)] + [full(w) for w in wts + conv_wts]
    args = [x2d, mod3, mod3] + list(wts + conv_wts)
    out_shape = [jax.ShapeDtypeStruct((n, QK_WIDTH), BF16)] * 3 + \
                [jax.ShapeDtypeStruct((n, CONV_CH), BF16)]
    out_specs = [row(QK_WIDTH)] * 3 + [cv_spec]
    if rope:
        tab = pl.BlockSpec((tile, LANES), lambda i: (i % steps_per_seq, 0))
        in_specs += [tab, tab]
        args += list(rope_tabs)
    else:
        out_shape += [jax.ShapeDtypeStruct((n, KV_RANK), F32),
                      jax.ShapeDtypeStruct((n, QK_ROPE), F32)]
        out_specs += [row(KV_RANK), row(QK_ROPE)]
    slots = max(tile // seq, 1)
    return pl.pallas_call(
        functools.partial(_front_kernel, rope, seq),
        out_shape=out_shape,
        grid=(n // tile,),
        in_specs=in_specs,
        out_specs=out_specs,
        scratch_shapes=[pltpu.VMEM((slots, seq + 2 * CONV_HALO, CONV_CH), F32),
                        pltpu.VMEM((SUBLANES - 1, SHIFT_ROWS, CONV_CH), F32)],
        compiler_params=_params(1, VMEM_LIMIT),
        name="front_rope" if rope else "front",
    )(*args)


CONV_PAD = (CONV_WIDTH - 1) // 2
CONV_HALO = 16
SUBLANES = 8
SHIFT_ROWS = ROW_TILE + 2 * CONV_HALO - SUBLANES


CONV_STAGES = 6


def _conv_stages(n_stages, vpad_ref, slot, t0, cw_ref, cb_ref, lg_ref, lb_ref, shift_ref):
    state = {}

    def taps(lo, hi):
        if lo == 0:
            win = vpad_ref[slot, t0:t0 + ROW_TILE + 2 * CONV_HALO, :]
            for ph in range(1, SUBLANES):
                shift_ref[ph - 1] = win[ph:ph + SHIFT_ROWS, :]
            state["acc"] = jnp.broadcast_to(cb_ref[...], (ROW_TILE, CONV_CH))
        acc = state["acc"]
        for t in range(lo, hi):
            blk, ph = divmod(t + CONV_HALO - CONV_PAD, SUBLANES)
            if ph == 0:
                tap = vpad_ref[slot, t0 + blk * SUBLANES:t0 + blk * SUBLANES + ROW_TILE, :]
            else:
                tap = shift_ref[ph - 1, blk * SUBLANES:blk * SUBLANES + ROW_TILE, :]
            acc = acc + tap * cw_ref[t:t + 1, :]
        state["acc"] = acc

    def finish():
        acc = state["acc"]
        mu = jnp.mean(acc, axis=-1, keepdims=True)
        cen = acc - mu
        var = jnp.mean(cen * cen, axis=-1, keepdims=True)
        ln = cen * lax.rsqrt(var + EPS) * lg_ref[...] + lb_ref[...]
        return (ln * jax.nn.sigmoid(ln)).astype(BF16)

    bounds = [CONV_WIDTH * s // n_stages for s in range(n_stages + 1)]
    stages = [functools.partial(taps, bounds[s], bounds[s + 1]) for s in range(n_stages)]
    return stages, finish


def _attn_kernel(n_kv, n_seq, with_ctx, q_ref, *rest):
    kv_refs = list(rest[:2 * n_kv])
    if with_ctx:
        ckv_ref, kr_ref, wk_ref, wv_ref, pk_ref, o_ref, kc_ref, vc_ref = rest[2 * n_kv:]

        @pl.when(pl.program_id(1) == 0)
        def _():
            cb = ckv_ref[...].astype(BF16)
            kc_ref[...] = (_bdot(cb, wk_ref[...]) +
                           _bdot(kr_ref[...].astype(BF16), pk_ref[...])).astype(BF16)
            vc_ref[...] = _bdot(cb, wv_ref[...]).astype(BF16)

        kv_refs += [kc_ref, vc_ref]
        n_kv += 1
    else:
        o_ref = rest[2 * n_kv]
    nt = (((1,), (1,)), ((), ()))
    tq = q_ref.shape[0] // n_seq
    work = [(b, hd) for b in range(n_seq) for hd in range(MLA_HEADS)]

    def kv_rows(ref, b):
        t = ref.shape[0] // n_seq
        return slice(b * t, (b + 1) * t)

    def scores(item):
        b, hd = item
        sl = slice(hd * HEAD_PAD, (hd + 1) * HEAD_PAD)
        qh = q_ref[b * tq:(b + 1) * tq, sl]
        return [lax.dot_general(qh, kv_refs[2 * j][kv_rows(kv_refs[2 * j], b), sl], nt,
                                preferred_element_type=F32) for j in range(n_kv)]

    outs = {}
    nxt = scores(work[0])
    for w, (b, hd) in enumerate(work):
        sl = slice(hd * HEAD_PAD, (hd + 1) * HEAD_PAD)
        ss = nxt
        if w + 1 < len(work):
            nxt = scores(work[w + 1])
        m = ss[0].max(axis=-1, keepdims=True)
        for s in ss[1:]:
            m = jnp.maximum(m, s.max(axis=-1, keepdims=True))
        ps = [jnp.exp2(s - m) for s in ss]
        l = ps[0].sum(axis=-1, keepdims=True)
        for p in ps[1:]:
            l = l + p.sum(axis=-1, keepdims=True)
        o = None
        for j in range(n_kv):
            v_ref = kv_refs[2 * j + 1]
            pv = _bdot(ps[j].astype(BF16), v_ref[kv_rows(v_ref, b), sl])
            o = pv if o is None else o + pv
        outs[hd] = o / l
        if hd % 2 == 1:
            o_ref[b * tq:(b + 1) * tq, (hd // 2) * LANES:(hd // 2 + 1) * LANES] = (
                outs[hd - 1] + outs[hd]).astype(BF16)


def _attention(q, kvs, batch, seq, ctx=None):
    n = q.shape[0]
    n_seq = max(ATTN_TILE // seq, 1)
    tile = min(seq, ATTN_TILE) * n_seq
    q_tiles = max(seq // tile, 1)
    assert batch % n_seq == 0
    full = lambda a: pl.BlockSpec(a.shape, lambda b, i: (0,) * a.ndim)
    row = lambda w: pl.BlockSpec((tile, w), lambda b, i: (b * q_tiles + i, 0))
    in_specs = [row(QK_WIDTH)]
    args = [q]
    for k, v, t in kvs:
        spec = pl.BlockSpec((n_seq * t, QK_WIDTH), lambda b, i: (b, 0))
        in_specs += [spec, spec]
        args += [k, v]
    scratch = []
    if ctx is not None:
        assert n_seq == 1
        ckv, kr, past, wk, wv, pk = ctx
        pk = pk[:QK_ROPE]
        in_specs += [pl.BlockSpec((past, KV_RANK), lambda b, i: (b, 0)),
                     pl.BlockSpec((past, QK_ROPE), lambda b, i: (b, 0)), full(wk), full(wv), full(pk)]
        args += [ckv, kr, wk, wv, pk]
        scratch = [pltpu.VMEM((past, QK_WIDTH), BF16), pltpu.VMEM((past, QK_WIDTH), BF16)]
    return pl.pallas_call(
        functools.partial(_attn_kernel, len(kvs), n_seq, ctx is not None),
        out_shape=jax.ShapeDtypeStruct((n, MLA_HEADS * V_DIM), BF16),
        grid=(batch // n_seq, q_tiles),
        in_specs=in_specs,
        out_specs=row(MLA_HEADS * V_DIM),
        scratch_shapes=scratch,
        compiler_params=_params(2, VMEM_LIMIT),
        name="attention",
    )(*args)


def _back_kernel(x_ref, attn_ref, cv_ref, woa_ref, woc_ref, g1_ref, sh_ref, sc_ref, fg_ref,
                 wrh_ref, wrl_ref, x1_ref, h2_ref, afft_ref):
    n_sub = x_ref.shape[0] // BACK_SUB
    rows = [slice(s * BACK_SUB, (s + 1) * BACK_SUB) for s in range(n_sub)]

    def project(r):
        return _bdot(attn_ref[r, :], woa_ref[...]) + _bdot(cv_ref[r, :], woc_ref[...])

    def normalise(r, mix):
        x1 = x_ref[r, :] + g1_ref[0] * mix
        x1_ref[r, :] = x1
        h2 = _rms(x1, fg_ref[...]) * (1.0 + sc_ref[0]) + sh_ref[0]
        h2b = h2.astype(BF16)
        h2_ref[r, :D_MODEL] = h2b
        h2l = (h2 - h2b.astype(F32)).astype(BF16)
        wrh = wrh_ref[...]
        return _bdot(h2b, wrh) + (_bdot(h2l, wrh) + _bdot(h2b, wrl_ref[...]))

    def route(s, logits):
        r = rows[s]
        lane = lax.broadcasted_iota(I32, logits.shape, 1)
        real = lane < N_EXPERTS
        m = jnp.max(jnp.where(real, logits, -jnp.inf), axis=-1, keepdims=True)
        e = jnp.exp(logits - m)
        aff = e / jnp.sum(jnp.where(real, e, 0.0), axis=-1, keepdims=True)
        aff_t = aff.T
        for blk in range(BACK_SUB // LANES):
            afft_ref[s * (BACK_SUB // LANES) + blk] = aff_t[:N_EXPERTS, blk * LANES:(blk + 1) * LANES]
        hi = aff.astype(BF16)
        r1 = aff - hi.astype(F32)
        mid = r1.astype(BF16)
        lo = (r1 - mid.astype(F32)).astype(BF16)
        zero = jnp.zeros_like(hi)
        h2_ref[r, D_MODEL:] = jnp.where(
            real, hi, jnp.where(lane < 2 * N_EXPERTS, mid, jnp.where(lane < 3 * N_EXPERTS, lo, zero)))

    mixes = {0: project(rows[0])}
    logits = {}
    for s in range(n_sub):
        if s + 1 < n_sub:
            mixes[s + 1] = project(rows[s + 1])
        logits[s] = normalise(rows[s], mixes.pop(s))
        if s > 0:
            route(s - 1, logits.pop(s - 1))
    route(n_sub - 1, logits.pop(n_sub - 1))


def _back(x2d, attn, cv, mod3, mod_row, wts, seq):
    n = x2d.shape[0]
    tile = BACK_TILE
    full = lambda a: pl.BlockSpec(a.shape, lambda i: (0,) * a.ndim)

    def mod_spec(col):
        if mod_row is None:
            assert seq % tile == 0
            return pl.BlockSpec((1, 1, D_MODEL), lambda i: (i // (seq // tile), 0, col))
        return pl.BlockSpec((1, 1, D_MODEL), lambda i: (mod_row, 0, col))

    row = lambda w: pl.BlockSpec((tile, w), lambda i: (i, 0))
    woa, woc, fg, wrh, wrl = wts
    in_specs = [row(D_MODEL), row(MLA_HEADS * V_DIM), row(CONV_CH), full(woa), full(woc),
                mod_spec(2), mod_spec(3), mod_spec(4), full(fg), full(wrh), full(wrl)]
    return pl.pallas_call(
        _back_kernel,
        out_shape=[jax.ShapeDtypeStruct((n, D_MODEL), F32),
                   jax.ShapeDtypeStruct((n, XS_W), BF16),
                   jax.ShapeDtypeStruct((n // LANES, N_EXPERTS, LANES), F32)],
        grid=(n // tile,),
        in_specs=in_specs,
        out_specs=[row(D_MODEL), row(XS_W),
                   pl.BlockSpec((tile // LANES, N_EXPERTS, LANES), lambda i: (i, 0, 0))],
        compiler_params=_params(1, VMEM_LIMIT),
        name="back",
    )(x2d, attn, cv, woa, woc, mod3, mod3, mod3, fg, wrh, wrl)


def _route_kernel(caps, *refs):
    n_g = len(caps)
    affs = [r[...] for r in refs[:n_g]]
    pos_refs = refs[n_g:2 * n_g]
    base_refs = refs[2 * n_g:]
    ones = jnp.ones((LANES, LANES), BF16)
    l_r = lax.broadcasted_iota(I32, (LANES, LANES), 0)
    l_c = lax.broadcasted_iota(I32, (LANES, LANES), 1)
    tri = jnp.where(l_r <= l_c, 1.0, 0.0).astype(BF16)

    def lane_dot(mask3, w):
        r, e, _ = mask3.shape
        mb = jnp.where(mask3, 1.0, 0.0).astype(BF16).reshape(r * e, LANES)
        return _bdot(mb, w).reshape(r, e, LANES)

    def expert_count(mask3):
        return jnp.sum(lane_dot(mask3, ones), axis=0, keepdims=True)

    def excl_cumsum(mask3):
        rows = mask3.shape[0]
        tot = lane_dot(mask3, ones)
        run = tot
        s = 1
        while s < rows:
            run = run + jnp.concatenate([jnp.zeros((s,) + run.shape[1:], F32), run[:rows - s]], axis=0)
            s *= 2
        before = run - tot
        return before + lane_dot(mask3, tri) - jnp.where(mask3, 1.0, 0.0), before

    def step(t, thrs):
        bit = lax.shift_left(jnp.int32(1), 30 - t)
        out = []
        for aff, thr, cap in zip(affs, thrs, caps):
            cand = thr | bit
            cnt = expert_count(aff >= pltpu.bitcast(cand, F32))
            out.append(jnp.where(cnt >= cap, cand, thr))
        return tuple(out)

    init = tuple(jnp.zeros((1,) + a.shape[1:], I32) for a in affs)
    thrs = lax.fori_loop(0, 31, step, init)
    for aff, thr_bits, cap, pos_ref, base_ref in zip(affs, thrs, caps, pos_refs, base_refs):
        thr = pltpu.bitcast(thr_bits, F32)
        gt = aff > thr
        eq = aff == thr
        need = cap - expert_count(gt)
        tie_rank, _ = excl_cumsum(eq)
        sel = gt | (eq & (tie_rank < need))
        pos, before = excl_cumsum(sel)
        pos_ref[...] = jnp.where(sel, pos, -1.0).astype(I32)
        base_ref[...] = before


def _route(affs, caps):
    shapes = [a.shape for a in affs]
    return pl.pallas_call(
        functools.partial(_route_kernel, tuple(caps)),
        out_shape=[jax.ShapeDtypeStruct(s, I32) for s in shapes] +
                  [jax.ShapeDtypeStruct(s, F32) for s in shapes],
        compiler_params=pltpu.CompilerParams(vmem_limit_bytes=VMEM_LIMIT),
        name="route",
    )(*affs)


def _window(off_ref, e, i, n_tiles, w, cap):
    del n_tiles
    first = off_ref[i * N_EXPERTS + e] + w * SLOT_WIN
    return first, pl.multiple_of(jnp.minimum(first, cap - SLOT_WIN), BF16_ROWS)


def _one_hot_rows(pos_ref, wins, row0=0):
    r = lax.broadcasted_iota(I32, (SLOT_WIN, LANES), 0)
    blocks = []
    for e, (first, off) in enumerate(wins):
        halves = []
        for h in range(ROW_TILE // LANES):
            pe = pos_ref[row0 + h, e:e + 1, :]
            rel = jnp.where(pe >= first, pe - off, -1)
            halves.append(jnp.where(rel == r, 1.0, 0.0).astype(BF16))
        blocks.append(jnp.concatenate(halves, axis=1))
    return jnp.concatenate(blocks, axis=0)


DISPATCH_TILES = 2
FLUSH_PARTS = 4


def _dispatch_kernel(cap, n_steps, off_ref, nwin_ref, flush_ref, h2_ref, pos_ref, xs_hbm,
                     xs_ref, sems):
    i = pl.program_id(0)
    per = h2_ref.shape[1] // DISPATCH_TILES

    @pl.when(i == 0)
    def _():
        def zero(e, carry):
            xs_ref[e] = jnp.zeros(xs_ref.shape[1:], BF16)
            return carry
        lax.fori_loop(0, N_EXPERTS, zero, 0)

    def gather(t, w):
        tile = i * DISPATCH_TILES + t
        wins = [_window(off_ref, e, tile, None, w, cap) for e in range(N_EXPERTS)]
        p = _one_hot_rows(pos_ref, wins, t * (ROW_TILE // LANES))
        h2 = h2_ref[:, t * per:(t + 1) * per, :].reshape(ROW_TILE, XS_W)
        return wins, _bdot(p, h2)

    def merge(wins, slab):
        for e, (_, off) in enumerate(wins):
            cur = xs_ref[e, pl.ds(off, SLOT_WIN), :].astype(F32)
            xs_ref[e, pl.ds(off, SLOT_WIN), :] = (
                cur + slab[e * SLOT_WIN:(e + 1) * SLOT_WIN]).astype(BF16)

    for wins, slab in [gather(t, 0) for t in range(DISPATCH_TILES)]:
        merge(wins, slab)
    for t in range(DISPATCH_TILES):
        def extra(w, carry, t=t):
            merge(*gather(t, w))
            return carry
        lax.fori_loop(1, nwin_ref[i * DISPATCH_TILES + t], extra, 0)

    part = cap // FLUSH_PARTS

    def flush(q):
        rows = pl.ds(q * part, part)
        return pltpu.make_async_copy(xs_ref.at[:, rows, :], xs_hbm.at[:, rows, :], sems.at[q])

    for q in range(FLUSH_PARTS):
        @pl.when(i == flush_ref[q])
        def _(q=q):
            flush(q).start()

    @pl.when(i == n_steps - 1)
    def _():
        for q in range(FLUSH_PARTS):
            flush(q).wait()


def _tile_spec(stripes, width, tiles=1):
    return pl.BlockSpec((stripes, tiles * ROW_TILE // stripes, width), lambda i, *_: (0, i, 0))


def _dispatch(h2aug, pos, offs, nwin, cap, stripes):
    n = h2aug.shape[0]
    n_steps = n // (ROW_TILE * DISPATCH_TILES)
    low = jnp.min(offs.reshape(-1, N_EXPERTS), axis=1)[DISPATCH_TILES::DISPATCH_TILES]
    low = jnp.concatenate([low, jnp.full((1,), cap, I32)])
    bounds = (jnp.arange(FLUSH_PARTS, dtype=I32) + 1) * (cap // FLUSH_PARTS)
    flush = jnp.argmax(low[None, :] >= bounds[:, None], axis=1).astype(I32)
    return pl.pallas_call(
        functools.partial(_dispatch_kernel, cap, n_steps),
        out_shape=jax.ShapeDtypeStruct((N_EXPERTS, cap, XS_W), BF16),
        grid_spec=pltpu.PrefetchScalarGridSpec(
            num_scalar_prefetch=3,
            grid=(n_steps,),
            in_specs=[_tile_spec(stripes, XS_W, DISPATCH_TILES),
                      pl.BlockSpec((DISPATCH_TILES * ROW_TILE // LANES, N_EXPERTS, LANES),
                                   lambda i, *_: (i, 0, 0))],
            out_specs=pl.BlockSpec(memory_space=pl.ANY),
            scratch_shapes=[pltpu.VMEM((N_EXPERTS, cap, XS_W), BF16),
                            pltpu.SemaphoreType.DMA((FLUSH_PARTS,))]),
        compiler_params=_params(1, VMEM_LIMIT),
        name="dispatch",
    )(offs, nwin, flush, h2aug.reshape(stripes, n // stripes, XS_W), pos)


FF_CHUNK = 512
EXP_ROWS = 512


def _experts_kernel(n_groups, n_chunks, *refs):
    xs_refs = refs[:n_groups]
    wg_ref, wu_ref, wd_ref = refs[n_groups:n_groups + 3]
    ys_refs = refs[n_groups + 3:2 * n_groups + 3]
    acc_ref = refs[2 * n_groups + 3]
    e = pl.program_id(0)
    j = pl.program_id(1)
    cap = xs_refs[0].shape[1]

    @pl.when(j == 0)
    def _():
        for g in range(n_groups):
            acc_ref[g] = jnp.zeros(acc_ref.shape[1:], F32)

    wg = wg_ref[0].astype(BF16)
    wu = wu_ref[0].astype(BF16)
    wd = wd_ref[0].astype(BF16)
    blocks = [(g, slice(r * EXP_ROWS, (r + 1) * EXP_ROWS))
              for g in range(n_groups) for r in range(cap // EXP_ROWS)]

    def up(blk):
        g, rows = blk
        x = xs_refs[g][0, rows, :D_MODEL]
        return _bdot(x, wg), _bdot(x, wu)

    nxt = up(blocks[0])
    for k, (g, rows) in enumerate(blocks):
        a, u = nxt
        if k + 1 < len(blocks):
            nxt = up(blocks[k + 1])
        hm = (a * jax.nn.sigmoid(a) * u).astype(BF16)
        acc_ref[g, rows, :] += _bdot(hm, wd)

    @pl.when(j == n_chunks - 1)
    def _():
        for g, (xs_ref, ys_ref) in enumerate(zip(xs_refs, ys_refs)):
            aug = xs_ref[0, :, D_MODEL:].astype(F32)
            lane = lax.broadcasted_iota(I32, aug.shape, 1)
            mine = (lane == e) | (lane == e + N_EXPERTS) | (lane == e + 2 * N_EXPERTS)
            gate = jnp.sum(jnp.where(mine, aug, 0.0), axis=-1, keepdims=True)
            ys_ref[0] = (acc_ref[g] * gate).astype(BF16)


def _experts(xs_list, w_gate, w_up, w_down):
    n_groups = len(xs_list)
    cap = xs_list[0].shape[1]
    ff = w_gate.shape[2]
    n_chunks = ff // FF_CHUNK
    xs_spec = pl.BlockSpec((1, cap, XS_W), lambda e, j: (e, 0, 0))
    up_spec = pl.BlockSpec((1, D_MODEL, FF_CHUNK), lambda e, j: (e, 0, j))
    down_spec = pl.BlockSpec((1, FF_CHUNK, D_MODEL), lambda e, j: (e, j, 0))
    ys_spec = pl.BlockSpec((1, cap, D_MODEL), lambda e, j: (e, 0, 0))
    return pl.pallas_call(
        functools.partial(_experts_kernel, n_groups, n_chunks),
        out_shape=[jax.ShapeDtypeStruct((N_EXPERTS, cap, D_MODEL), BF16)] * n_groups,
        grid=(N_EXPERTS, n_chunks),
        in_specs=[xs_spec] * n_groups + [up_spec, up_spec, down_spec],
        out_specs=[ys_spec] * n_groups,
        scratch_shapes=[pltpu.VMEM((n_groups, cap, D_MODEL), F32)],
        compiler_params=_params(2, VMEM_LIMIT),
        name="experts",
    )(*xs_list, w_gate, w_up, w_down)


COMBINE_TILES = 4


def _combine_kernel(cap, n_steps, off_ref, nwin_ref, ys_hbm, pos_ref, x1_ref, g2_ref, fg_ref,
                    y_ref, buf_ref, sems):
    i = pl.program_id(0)
    slot = i % 2
    per = x1_ref.shape[1] // COMBINE_TILES

    def copies(step, t, w, s):
        out = []
        for e in range(N_EXPERTS):
            _, off = _window(off_ref, e, step * COMBINE_TILES + t, None, w, cap)
            out.append(pltpu.make_async_copy(
                ys_hbm.at[e, pl.ds(off, SLOT_WIN), :],
                buf_ref.at[s, t, pl.ds(e * SLOT_WIN, SLOT_WIN), :],
                sems.at[s, t]))
        return out

    @pl.when(i == 0)
    def _():
        for t in range(COMBINE_TILES):
            for cp in copies(0, t, 0, 0):
                cp.start()

    @pl.when(i + 1 < n_steps)
    def _():
        for t in range(COMBINE_TILES):
            for cp in copies(i + 1, t, 0, 1 - slot):
                cp.start()

    tn = (((0,), (0,)), ((), ()))

    def one_hot(t, w):
        wins = [_window(off_ref, e, i * COMBINE_TILES + t, None, w, cap) for e in range(N_EXPERTS)]
        return _one_hot_rows(pos_ref, wins, t * (ROW_TILE // LANES))

    def scattered(t, w, p):
        for cp in copies(i, t, w, slot):
            cp.wait()
        return lax.dot_general(p, buf_ref[slot, t], tn, preferred_element_type=F32)

    ps = [one_hot(t, 0) for t in range(COMBINE_TILES)]
    moes = [scattered(t, 0, ps[t]) for t in range(COMBINE_TILES)]
    for t in range(COMBINE_TILES):
        def extra_window(w, acc, t=t):
            for cp in copies(i, t, w, slot):
                cp.start()
            return acc + scattered(t, w, one_hot(t, w))
        moe = lax.fori_loop(1, nwin_ref[i * COMBINE_TILES + t], extra_window, moes[t])
        rows = slice(t * per, (t + 1) * per)
        x1 = x1_ref[:, rows, :]
        out = x1 + g2_ref[...] * moe.reshape(x1.shape)
        y_ref[:, rows, :] = _rms(out, fg_ref[...])


def _combine(ys, pos, offs, nwin, x1, mod3, mod_row, stripes, final_g, cap):
    n = x1.shape[0]
    n_steps = n // (ROW_TILE * COMBINE_TILES)
    g2 = pl.BlockSpec((stripes, 1, D_MODEL), lambda i, *_: (0 if mod_row is None else mod_row, 0, 5))
    y = pl.pallas_call(
        functools.partial(_combine_kernel, cap, n_steps),
        out_shape=jax.ShapeDtypeStruct((stripes, n // stripes, D_MODEL), F32),
        grid_spec=pltpu.PrefetchScalarGridSpec(
            num_scalar_prefetch=2,
            grid=(n_steps,),
            in_specs=[pl.BlockSpec(memory_space=pl.ANY),
                      pl.BlockSpec((COMBINE_TILES * ROW_TILE // LANES, N_EXPERTS, LANES),
                                   lambda i, *_: (i, 0, 0)),
                      _tile_spec(stripes, D_MODEL, COMBINE_TILES),
                      g2,
                      pl.BlockSpec((1, D_MODEL), lambda i, *_: (0, 0))],
            out_specs=_tile_spec(stripes, D_MODEL, COMBINE_TILES),
            scratch_shapes=[pltpu.VMEM((2, COMBINE_TILES, N_EXPERTS * SLOT_WIN, D_MODEL), BF16),
                            pltpu.SemaphoreType.DMA((2, COMBINE_TILES))]),
        compiler_params=_params(1, VMEM_LIMIT),
        name="combine",
    )(offs, nwin, ys, pos, x1.reshape(stripes, n // stripes, D_MODEL), mod3, final_g)
    return y.reshape(n, D_MODEL)


def _head_cols(w3):
    k, _, d = w3.shape
    return jnp.pad(w3, ((0, 0), (0, 0), (0, HEAD_PAD - d))).reshape(k, QK_WIDTH)


def _value_cols(v3):
    k = v3.shape[0]
    pair = v3.reshape(k, MLA_HEADS // 2, 2, V_DIM)
    zeros = jnp.zeros((k, MLA_HEADS // 2, V_DIM), v3.dtype)
    even = jnp.concatenate([pair[:, :, 0], zeros], axis=-1)
    odd = jnp.concatenate([zeros, pair[:, :, 1]], axis=-1)
    return jnp.stack([even, odd], axis=2).reshape(k, QK_WIDTH)


def _rope_tables(seq):
    half = QK_ROPE // 2
    pos = jnp.arange(seq)
    inv_freq = 1.0 / (ROPE_BASE ** (jnp.arange(0, half, 2, dtype=F32) / half))

    def cs(p):
        ang = p.astype(F32)[:, None] * inv_freq[None, :]
        ang = jnp.concatenate([ang, ang], axis=-1)
        return jnp.cos(ang), jnp.sin(ang)

    cr, sr = cs(pos // GRID_W)
    cc, sc = cs(pos % GRID_W)
    ones = jnp.ones((seq, QK_NOPE), F32)
    zpad = jnp.zeros((seq, HEAD_PAD - QK_NOPE - QK_ROPE), F32)
    cq = jnp.concatenate([ones, cr, cc, zpad], axis=-1)
    sq = jnp.concatenate([0.0 * ones, sr, sc, zpad], axis=-1)
    return cq, sq


def _prep_weights(w_in, q_norm_g, w_q_up, kv_norm_g, w_kv_up, attn_norm_g):
    wqc = w_in[:, :Q_RANK + KV_RANK].astype(BF16)
    wkr = jnp.pad(w_in[:, Q_RANK + KV_RANK:Q_RANK + KV_RANK + QK_ROPE],
                  ((0, 0), (0, LANES - QK_ROPE))).astype(BF16)
    wglu = w_in[:, Q_RANK + KV_RANK + QK_ROPE:].astype(BF16)
    wq = _head_cols(w_q_up.astype(BF16).reshape(Q_RANK, MLA_HEADS, QK_NOPE + QK_ROPE))
    kv3 = w_kv_up.astype(BF16).reshape(KV_RANK, MLA_HEADS, QK_NOPE + V_DIM)
    wk = _head_cols(kv3[:, :, :QK_NOPE])
    wv = _value_cols(kv3[:, :, QK_NOPE:])
    pk = np.zeros((LANES, QK_WIDTH), np.float32)
    for hd in range(MLA_HEADS):
        for j in range(QK_ROPE):
            pk[j, hd * HEAD_PAD + QK_NOPE + j] = 1.0
    pk = jnp.asarray(pk, BF16)
    return (attn_norm_g[None, :], wqc, wkr, wglu, q_norm_g[None, :], wq,
            kv_norm_g[None, :], wk, wv, pk)


def _stripe_order(aff3, stripes, seq):
    per = ROW_TILE // stripes
    q = LANES // per
    assert per * stripes == ROW_TILE and per * q == LANES and stripes % q == 0 and seq % LANES == 0
    a = aff3.reshape(stripes // q, q, seq // LANES, N_EXPERTS, q, per)
    return a.transpose(2, 4, 0, 3, 1, 5).reshape(aff3.shape)


def _route_plan(base, cap):
    rows_per_tile = ROW_TILE // LANES
    start = base[::rows_per_tile, :, 0].astype(I32)
    end = jnp.concatenate([start[1:], jnp.full((1, N_EXPERTS), cap, I32)], axis=0)
    off = (start // BF16_ROWS) * BF16_ROWS
    span = end - off
    nwin = jnp.maximum(jnp.max((span + SLOT_WIN - 1) // SLOT_WIN, axis=1), 1).astype(I32)
    return off.reshape(-1), nwin


def kernel(x_prompt, x_sample, cache_ckv, cache_krope, c, c_ctx, w_mod, b_mod, attn_norm_g,
           w_in, q_norm_g, w_q_up, kv_norm_g, w_kv_up, conv_w, conv_b, conv_ln_g, conv_ln_b,
           w_out, ffn_norm_g, w_router, w_gate, w_up, w_down, final_norm_g):
    depth = w_mod.shape[0]
    assert depth == 1
    bp, sp, _ = x_prompt.shape
    bs, ss, _ = x_sample.shape
    past = cache_ckv.shape[2]
    ctx_row = bs

    mod_rows = 16
    cc = jnp.concatenate([c, c_ctx[None, :], jnp.zeros((mod_rows - bs - 1, D_MODEL), F32)], axis=0)
    mod = _adaln(cc, w_mod[0], b_mod[0][None, :])
    mod3 = mod.reshape(mod_rows, 1, N_MOD * D_MODEL)

    fw = _prep_weights(w_in[0], q_norm_g[0], w_q_up[0], kv_norm_g[0], w_kv_up[0], attn_norm_g[0])
    wk, wv, pk = fw[7], fw[8], fw[9]
    wo = w_out[0].astype(BF16)
    wr = jnp.pad(jnp.tile(w_router[0], (1, 3)), ((0, 0), (0, LANES - 3 * N_EXPERTS)))
    wrh = wr.astype(BF16)
    wrl = (wr - wrh.astype(F32)).astype(BF16)
    cvw = (conv_w[0], conv_b[0][None, :], conv_ln_g[0][None, :], conv_ln_b[0][None, :])
    bw = (wo[:MLA_HEADS * V_DIM], wo[MLA_HEADS * V_DIM:], ffn_norm_g[0][None, :], wrh, wrl)

    xp2 = x_prompt.reshape(bp * sp, D_MODEL)
    xs2 = x_sample.reshape(bs * ss, D_MODEL)

    qp, kp, vp, cv_p, ckvp, krp = _front(xp2, mod3, ctx_row, sp, fw, cvw, None)
    attn_p = _attention(qp, [(kp, vp, sp)], bp, sp)
    x1p, h2p, afftp = _back(xp2, attn_p, cv_p, mod3, ctx_row, bw, sp)

    qs, ks, vs, cv_s = _front(xs2, mod3, None, ss, fw, cvw, _rope_tables(ss))
    ctx = (cache_ckv[:, 0].reshape(bs * past, KV_RANK),
           cache_krope[:, 0].reshape(bs * past, QK_ROPE), past, wk, wv, pk)
    attn_s = _attention(qs, [(ks, vs, ss)], bs, ss, ctx)
    x1s, h2s, affts = _back(xs2, attn_s, cv_s, mod3, None, bw, ss)

    stripes = (1, bs)
    caps = [EC_FACTOR * h2.shape[0] // N_EXPERTS for h2 in (h2p, h2s)]
    routed = _route([afftp, _stripe_order(affts, bs, ss)], caps)
    plans = []
    xs_list = []
    for g, (h2, cap) in enumerate(zip((h2p, h2s), caps)):
        pos, base = routed[g], routed[len(caps) + g]
        offs, nwin = _route_plan(base, cap)
        plans.append((pos, offs, nwin, cap))
        xs_list.append(_dispatch(h2, pos, offs, nwin, cap, stripes[g]))
    ys_list = _experts(xs_list, w_gate[0], w_up[0], w_down[0])

    fg = final_norm_g[None, :]
    pos, offs, nwin, cap = plans[0]
    y_prompt = _combine(ys_list[0], pos, offs, nwin, x1p, mod3, ctx_row, stripes[0], fg, cap)
    pos, offs, nwin, cap = plans[1]
    y_sample = _combine(ys_list[1], pos, offs, nwin, x1s, mod3, None, stripes[1], fg, cap)

    return (y_prompt.reshape(bp, sp, D_MODEL), y_sample.reshape(bs, ss, D_MODEL),
            ckvp.reshape(bp, depth, sp, KV_RANK), krp.reshape(bp, depth, sp, QK_ROPE))
```

```python
import functools

import jax
import jax.numpy as jnp
import numpy as np
from jax import lax
from jax.experimental import pallas as pl
from jax.experimental.pallas import tpu as pltpu

F32 = jnp.float32
BF16 = jnp.bfloat16
I32 = jnp.int32

LANES = 128
BF16_ROWS = 16
VMEM_LIMIT = 56 * 1024 * 1024

D_MODEL = 1024
GRID_W = 64
MLA_HEADS = 8
QK_NOPE = 64
QK_ROPE = 32
V_DIM = 64
Q_RANK = 768
KV_RANK = 256
CONV_CH = 512
CONV_WIDTH = 31
N_EXPERTS = 16
EXPERT_FF = 1024
EC_FACTOR = 2
ROPE_BASE = 10000.0
EPS = 1e-6
N_MOD = 6

HEAD_PAD = LANES
QK_WIDTH = MLA_HEADS * HEAD_PAD
ROW_TILE = 256
FRONT_TILE = 512
ATTN_TILE = 1024
BACK_TILE = 1024
BACK_SUB = 256
SLOT_WIN = 64
AUG = LANES
XS_W = D_MODEL + AUG
LOG2_E = 1.4426950408889634


def _params(n_axes, vmem=None):
    return pltpu.CompilerParams(
        dimension_semantics=("arbitrary",) * n_axes,
        vmem_limit_bytes=vmem)


def _rms(x, g):
    ms = jnp.mean(x * x, axis=-1, keepdims=True)
    return x * lax.rsqrt(ms + EPS) * g


def _bdot(a, b):
    return jnp.dot(a, b, preferred_element_type=F32)


def _adaln_kernel(c_ref, w_ref, b_ref, o_ref):
    c = c_ref[...]
    s = c * jax.nn.sigmoid(c)
    w = w_ref[...]
    sh = s.astype(BF16)
    sl = (s - sh.astype(F32)).astype(BF16)
    wh = w.astype(BF16)
    wl = (w - wh.astype(F32)).astype(BF16)
    o_ref[...] = _bdot(sh, wh) + (_bdot(sl, wh) + _bdot(sh, wl)) + b_ref[...]


def _adaln(cc, w_mod, b_mod):
    rows = cc.shape[0]
    n_out = w_mod.shape[1]
    blk = D_MODEL
    return pl.pallas_call(
        _adaln_kernel,
        out_shape=jax.ShapeDtypeStruct((rows, n_out), F32),
        grid=(n_out // blk,),
        in_specs=[pl.BlockSpec((rows, D_MODEL), lambda j: (0, 0)),
                  pl.BlockSpec((D_MODEL, blk), lambda j: (0, j)),
                  pl.BlockSpec((1, blk), lambda j: (0, j))],
        out_specs=pl.BlockSpec((rows, blk), lambda j: (0, j)),
        compiler_params=_params(1),
        name="adaln",
    )(cc, w_mod, b_mod)


def _rot_half(x, lane, base):
    rel = lane - base
    first = ((rel >= 0) & (rel < 8)) | ((rel >= 16) & (rel < 24))
    return jnp.where(first, -pltpu.roll(x, LANES - 8, axis=1), pltpu.roll(x, 8, axis=1))


def _front_kernel(rope, seq, x_ref, sh_ref, sc_ref, g_ref, wqc_ref, wkr_ref, wglu_ref,
                  qg_ref, wq_ref, kvg_ref, wk_ref, wv_ref, pk_ref, cw_ref, cb_ref, lg_ref, lb_ref,
                  *rest):
    if rope:
        cq_ref, sq_ref, q_ref, k_ref, v_ref, cv_ref, vpad_ref, shift_ref = rest
    else:
        q_ref, k_ref, v_ref, cv_ref, ckv_ref, kr_ref, vpad_ref, shift_ref = rest
    rows = x_ref.shape[0]
    steps_per_seq = max(seq // rows, 1)
    scale = (QK_NOPE + QK_ROPE) ** -0.5 * LOG2_E
    lane = lax.broadcasted_iota(I32, (ROW_TILE, LANES), 1)
    conv_refs = (cw_ref, cb_ref, lg_ref, lb_ref, shift_ref)

    pending = []

    def tick(n=1):
        for _ in range(n):
            if pending:
                pending.pop(0)()

    def gated_values(qd, row_in_seq, slot, cv_row):
        r = slice(qd * ROW_TILE, (qd + 1) * ROW_TILE)
        h = _rms(x_ref[r, :], g_ref[...]) * (1.0 + sc_ref[0]) + sh_ref[0]
        hb = h.astype(BF16)
        glu = _bdot(hb, wglu_ref[...])
        if row_in_seq == 0:
            vpad_ref[slot, 0:CONV_HALO, :] = jnp.zeros((CONV_HALO, CONV_CH), F32)
        if row_in_seq + ROW_TILE == seq:
            vpad_ref[slot, CONV_HALO + seq:, :] = jnp.zeros((CONV_HALO, CONV_CH), F32)
        vpad_ref[slot, CONV_HALO + row_in_seq:CONV_HALO + row_in_seq + ROW_TILE, :] = (
            glu[:, :CONV_CH] * jax.nn.sigmoid(glu[:, CONV_CH:]))

        ready = ([row_in_seq - ROW_TILE] if row_in_seq > 0 else []) + \
                ([row_in_seq] if row_in_seq + ROW_TILE == seq else [])
        for t0 in ready:
            steps, finish = _conv_stages(CONV_STAGES, vpad_ref, slot, t0, *conv_refs)
            dst = slice(cv_row + t0, cv_row + t0 + ROW_TILE)
            pending.extend(steps + [functools.partial(_store_conv, cv_ref, dst, finish)])
        return hb

    def projections(qd, row_in_seq, hb, ticks):
        r = slice(qd * ROW_TILE, (qd + 1) * ROW_TILE)
        qc = _bdot(hb, wqc_ref[...])
        tick(ticks)
        kr = _bdot(hb, wkr_ref[...])
        tick(ticks)
        qn = _rms(qc[:, :Q_RANK], qg_ref[...]).astype(BF16)
        q = _bdot(qn, wq_ref[...])
        tick(ticks)
        ckv = _rms(qc[:, Q_RANK:], kvg_ref[...])
        ckv_b = ckv.astype(BF16)
        kk = _bdot(ckv_b, wk_ref[...])
        tick(ticks)
        v_ref[r, :] = _bdot(ckv_b, wv_ref[...]).astype(BF16)
        tick(ticks)
        if rope:
            t = slice(row_in_seq % rows, row_in_seq % rows + ROW_TILE)
            cq = cq_ref[t, :]
            sq = sq_ref[t, :]
            for hd in range(MLA_HEADS):
                blk = q[:, hd * HEAD_PAD:(hd + 1) * HEAD_PAD]
                rot = blk * cq + _rot_half(blk, lane, QK_NOPE) * sq
                q_ref[r, hd * HEAD_PAD:(hd + 1) * HEAD_PAD] = (rot * scale).astype(BF16)
            krs = pltpu.roll(kr, QK_NOPE, axis=1)
            kr = pltpu.roll(krs * cq + _rot_half(krs, lane, QK_NOPE) * sq, LANES - QK_NOPE, axis=1)
            kr = jnp.where(lane < QK_ROPE, kr, 0.0)
        else:
            q_ref[r, :] = (q * scale).astype(BF16)
            ckv_ref[r, :] = ckv
            kr_ref[r, :] = kr[:, :QK_ROPE]
        tick(ticks)
        k_ref[r, :] = (kk + _bdot(kr.astype(BF16), pk_ref[...])).astype(BF16)

    def step(k):
        n_q = rows // ROW_TILE
        hbs = []
        for qd in range(n_q):
            row = k * rows + qd * ROW_TILE
            if seq >= rows:
                hbs.append((row, gated_values(qd, row, 0, 0)))
            else:
                hbs.append((row % seq, gated_values(qd, row % seq, row // seq, (row // seq) * seq)))
        ticks = -(-len(pending) // (6 * n_q))
        for qd, (row_in_seq, hb) in enumerate(hbs):
            projections(qd, row_in_seq, hb, ticks)
        tick(len(pending))

    if steps_per_seq == 1:
        step(0)
    else:
        for k in range(steps_per_seq):
            pl.when(pl.program_id(0) % steps_per_seq == k)(functools.partial(step, k))


def _store_conv(cv_ref, dst, finish):
    cv_ref[dst, :] = finish()


def _front(x2d, mod3, mod_row, seq, wts, conv_wts, rope_tabs):
    n = x2d.shape[0]
    rope = rope_tabs is not None
    tile = FRONT_TILE
    steps_per_seq = max(seq // tile, 1)
    assert seq % ROW_TILE == 0 and (tile % seq == 0 or seq % tile == 0)
    full = lambda a: pl.BlockSpec(a.shape, lambda i: (0,) * a.ndim)

    def mod_spec(col):
        if mod_row is None:
            assert seq % tile == 0
            return pl.BlockSpec((1, 1, D_MODEL), lambda i: (i // steps_per_seq, 0, col))
        return pl.BlockSpec((1, 1, D_MODEL), lambda i: (mod_row, 0, col))

    row = lambda w: pl.BlockSpec((tile, w), lambda i: (i, 0))
    cv_rows = max(seq, tile)
    cv_spec = pl.BlockSpec((cv_rows, CONV_CH), lambda i: (i // steps_per_seq, 0))
    in_specs = [row(D_MODEL), mod_spec(0), mod_spec(1)] + [full(w) for w in wts + conv_wts]
    args = [x2d, mod3, mod3] + list(wts + conv_wts)
    out_shape = [jax.ShapeDtypeStruct((n, QK_WIDTH), BF16)] * 3 + \
                [jax.ShapeDtypeStruct((n, CONV_CH), BF16)]
    out_specs = [row(QK_WIDTH)] * 3 + [cv_spec]
    if rope:
        tab = pl.BlockSpec((tile, LANES), lambda i: (i % steps_per_seq, 0))
        in_specs += [tab, tab]
        args += list(rope_tabs)
    else:
        out_shape += [jax.ShapeDtypeStruct((n, KV_RANK), F32),
                      jax.ShapeDtypeStruct((n, QK_ROPE), F32)]
        out_specs += [row(KV_RANK), row(QK_ROPE)]
    slots = max(tile // seq, 1)
    return pl.pallas_call(
        functools.partial(_front_kernel, rope, seq),
        out_shape=out_shape,
        grid=(n // tile,),
        in_specs=in_specs,
        out_specs=out_specs,
        scratch_shapes=[pltpu.VMEM((slots, seq + 2 * CONV_HALO, CONV_CH), F32),
                        pltpu.VMEM((SUBLANES - 1, SHIFT_ROWS, CONV_CH), F32)],
        compiler_params=_params(1, VMEM_LIMIT),
        name="front_rope" if rope else "front",
    )(*args)


CONV_PAD = (CONV_WIDTH - 1) // 2
CONV_HALO = 16
SUBLANES = 8
SHIFT_ROWS = ROW_TILE + 2 * CONV_HALO - SUBLANES


CONV_STAGES = 6


def _conv_stages(n_stages, vpad_ref, slot, t0, cw_ref, cb_ref, lg_ref, lb_ref, shift_ref):
    state = {}

    def taps(lo, hi):
        if lo == 0:
            win = vpad_ref[slot, t0:t0 + ROW_TILE + 2 * CONV_HALO, :]
            for ph in range(1, SUBLANES):
                shift_ref[ph - 1] = win[ph:ph + SHIFT_ROWS, :]
            state["acc"] = jnp.broadcast_to(cb_ref[...], (ROW_TILE, CONV_CH))
        acc = state["acc"]
        for t in range(lo, hi):
            blk, ph = divmod(t + CONV_HALO - CONV_PAD, SUBLANES)
            if ph == 0:
                tap = vpad_ref[slot, t0 + blk * SUBLANES:t0 + blk * SUBLANES + ROW_TILE, :]
            else:
                tap = shift_ref[ph - 1, blk * SUBLANES:blk * SUBLANES + ROW_TILE, :]
            acc = acc + tap * cw_ref[t:t + 1, :]
        state["acc"] = acc

    def finish():
        acc = state["acc"]
        mu = jnp.mean(acc, axis=-1, keepdims=True)
        cen = acc - mu
        var = jnp.mean(cen * cen, axis=-1, keepdims=True)
        ln = cen * lax.rsqrt(var + EPS) * lg_ref[...] + lb_ref[...]
        return (ln * jax.nn.sigmoid(ln)).astype(BF16)

    bounds = [CONV_WIDTH * s // n_stages for s in range(n_stages + 1)]
    stages = [functools.partial(taps, bounds[s], bounds[s + 1]) for s in range(n_stages)]
    return stages, finish


def _attn_kernel(n_kv, n_seq, with_ctx, q_ref, *rest):
    kv_refs = list(rest[:2 * n_kv])
    if with_ctx:
        ckv_ref, kr_ref, wk_ref, wv_ref, pk_ref, o_ref, kc_ref, vc_ref = rest[2 * n_kv:]

        @pl.when(pl.program_id(1) == 0)
        def _():
            cb = ckv_ref[...].astype(BF16)
            kc_ref[...] = (_bdot(cb, wk_ref[...]) +
                           _bdot(kr_ref[...].astype(BF16), pk_ref[...])).astype(BF16)
            vc_ref[...] = _bdot(cb, wv_ref[...]).astype(BF16)

        kv_refs += [kc_ref, vc_ref]
        n_kv += 1
    else:
        o_ref = rest[2 * n_kv]
    nt = (((1,), (1,)), ((), ()))
    tq = q_ref.shape[0] // n_seq
    work = [(b, hd) for b in range(n_seq) for hd in range(MLA_HEADS)]

    def kv_rows(ref, b):
        t = ref.shape[0] // n_seq
        return slice(b * t, (b + 1) * t)

    def scores(item):
        b, hd = item
        sl = slice(hd * HEAD_PAD, (hd + 1) * HEAD_PAD)
        qh = q_ref[b * tq:(b + 1) * tq, sl]
        return [lax.dot_general(qh, kv_refs[2 * j][kv_rows(kv_refs[2 * j], b), sl], nt,
                                preferred_element_type=F32) for j in range(n_kv)]

    outs = {}
    nxt = scores(work[0])
    for w, (b, hd) in enumerate(work):
        sl = slice(hd * HEAD_PAD, (hd + 1) * HEAD_PAD)
        ss = nxt
        if w + 1 < len(work):
            nxt = scores(work[w + 1])
        m = ss[0].max(axis=-1, keepdims=True)
        for s in ss[1:]:
            m = jnp.maximum(m, s.max(axis=-1, keepdims=True))
        ps = [jnp.exp2(s - m) for s in ss]
        l = ps[0].sum(axis=-1, keepdims=True)
        for p in ps[1:]:
            l = l + p.sum(axis=-1, keepdims=True)
        o = None
        for j in range(n_kv):
            v_ref = kv_refs[2 * j + 1]
            pv = _bdot(ps[j].astype(BF16), v_ref[kv_rows(v_ref, b), sl])
            o = pv if o is None else o + pv
        outs[hd] = o / l
        if hd % 2 == 1:
            o_ref[b * tq:(b + 1) * tq, (hd // 2) * LANES:(hd // 2 + 1) * LANES] = (
                outs[hd - 1] + outs[hd]).astype(BF16)


def _attention(q, kvs, batch, seq, ctx=None):
    n = q.shape[0]
    n_seq = max(ATTN_TILE // seq, 1)
    tile = min(seq, ATTN_TILE) * n_seq
    q_tiles = max(seq // tile, 1)
    assert batch % n_seq == 0
    full = lambda a: pl.BlockSpec(a.shape, lambda b, i: (0,) * a.ndim)
    row = lambda w: pl.BlockSpec((tile, w), lambda b, i: (b * q_tiles + i, 0))
    in_specs = [row(QK_WIDTH)]
    args = [q]
    for k, v, t in kvs:
        spec = pl.BlockSpec((n_seq * t, QK_WIDTH), lambda b, i: (b, 0))
        in_specs += [spec, spec]
        args += [k, v]
    scratch = []
    if ctx is not None:
        assert n_seq == 1
        ckv, kr, past, wk, wv, pk = ctx
        pk = pk[:QK_ROPE]
        in_specs += [pl.BlockSpec((past, KV_RANK), lambda b, i: (b, 0)),
                     pl.BlockSpec((past, QK_ROPE), lambda b, i: (b, 0)), full(wk), full(wv), full(pk)]
        args += [ckv, kr, wk, wv, pk]
        scratch = [pltpu.VMEM((past, QK_WIDTH), BF16), pltpu.VMEM((past, QK_WIDTH), BF16)]
    return pl.pallas_call(
        functools.partial(_attn_kernel, len(kvs), n_seq, ctx is not None),
        out_shape=jax.ShapeDtypeStruct((n, MLA_HEADS * V_DIM), BF16),
        grid=(batch // n_seq, q_tiles),
        in_specs=in_specs,
        out_specs=row(MLA_HEADS * V_DIM),
        scratch_shapes=scratch,
        compiler_params=_params(2, VMEM_LIMIT),
        name="attention",
    )(*args)


def _back_kernel(x_ref, attn_ref, cv_ref, woa_ref, woc_ref, g1_ref, sh_ref, sc_ref, fg_ref,
                 wrh_ref, wrl_ref, x1_ref, h2_ref, afft_ref):
    n_sub = x_ref.shape[0] // BACK_SUB
    rows = [slice(s * BACK_SUB, (s + 1) * BACK_SUB) for s in range(n_sub)]

    def project(r):
        return _bdot(attn_ref[r, :], woa_ref[...]) + _bdot(cv_ref[r, :], woc_ref[...])

    def normalise(r, mix):
        x1 = x_ref[r, :] + g1_ref[0] * mix
        x1_ref[r, :] = x1
        h2 = _rms(x1, fg_ref[...]) * (1.0 + sc_ref[0]) + sh_ref[0]
        h2b = h2.astype(BF16)
        h2_ref[r, :D_MODEL] = h2b
        h2l = (h2 - h2b.astype(F32)).astype(BF16)
        wrh = wrh_ref[...]
        return _bdot(h2b, wrh) + (_bdot(h2l, wrh) + _bdot(h2b, wrl_ref[...]))

    def route(s, logits):
        r = rows[s]
        lane = lax.broadcasted_iota(I32, logits.shape, 1)
        real = lane < N_EXPERTS
        m = jnp.max(jnp.where(real, logits, -jnp.inf), axis=-1, keepdims=True)
        e = jnp.exp(logits - m)
        aff = e / jnp.sum(jnp.where(real, e, 0.0), axis=-1, keepdims=True)
        aff_t = aff.T
        for blk in range(BACK_SUB // LANES):
            afft_ref[s * (BACK_SUB // LANES) + blk] = aff_t[:N_EXPERTS, blk * LANES:(blk + 1) * LANES]
        hi = aff.astype(BF16)
        r1 = aff - hi.astype(F32)
        mid = r1.astype(BF16)
        lo = (r1 - mid.astype(F32)).astype(BF16)
        zero = jnp.zeros_like(hi)
        h2_ref[r, D_MODEL:] = jnp.where(
            real, hi, jnp.where(lane < 2 * N_EXPERTS, mid, jnp.where(lane < 3 * N_EXPERTS, lo, zero)))

    mixes = {0: project(rows[0])}
    logits = {}
    for s in range(n_sub):
        if s + 1 < n_sub:
            mixes[s + 1] = project(rows[s + 1])
        logits[s] = normalise(rows[s], mixes.pop(s))
        if s > 0:
            route(s - 1, logits.pop(s - 1))
    route(n_sub - 1, logits.pop(n_sub - 1))


def _back(x2d, attn, cv, mod3, mod_row, wts, seq):
    n = x2d.shape[0]
    tile = BACK_TILE
    full = lambda a: pl.BlockSpec(a.shape, lambda i: (0,) * a.ndim)

    def mod_spec(col):
        if mod_row is None:
            assert seq % tile == 0
            return pl.BlockSpec((1, 1, D_MODEL), lambda i: (i // (seq // tile), 0, col))
        return pl.BlockSpec((1, 1, D_MODEL), lambda i: (mod_row, 0, col))

    row = lambda w: pl.BlockSpec((tile, w), lambda i: (i, 0))
    woa, woc, fg, wrh, wrl = wts
    in_specs = [row(D_MODEL), row(MLA_HEADS * V_DIM), row(CONV_CH), full(woa), full(woc),
                mod_spec(2), mod_spec(3), mod_spec(4), full(fg), full(wrh), full(wrl)]
    return pl.pallas_call(
        _back_kernel,
        out_shape=[jax.ShapeDtypeStruct((n, D_MODEL), F32),
                   jax.ShapeDtypeStruct((n, XS_W), BF16),
                   jax.ShapeDtypeStruct((n // LANES, N_EXPERTS, LANES), F32)],
        grid=(n // tile,),
        in_specs=in_specs,
        out_specs=[row(D_MODEL), row(XS_W),
                   pl.BlockSpec((tile // LANES, N_EXPERTS, LANES), lambda i: (i, 0, 0))],
        compiler_params=_params(1, VMEM_LIMIT),
        name="back",
    )(x2d, attn, cv, woa, woc, mod3, mod3, mod3, fg, wrh, wrl)


def _route_kernel(caps, *refs):
    n_g = len(caps)
    affs = [r[...] for r in refs[:n_g]]
    pos_refs = refs[n_g:2 * n_g]
    base_refs = refs[2 * n_g:]
    ones = jnp.ones((LANES, LANES), BF16)
    l_r = lax.broadcasted_iota(I32, (LANES, LANES), 0)
    l_c = lax.broadcasted_iota(I32, (LANES, LANES), 1)
    tri = jnp.where(l_r <= l_c, 1.0, 0.0).astype(BF16)

    def lane_dot(mask3, w):
        r, e, _ = mask3.shape
        mb = jnp.where(mask3, 1.0, 0.0).astype(BF16).reshape(r * e, LANES)
        return _bdot(mb, w).reshape(r, e, LANES)

    def expert_count(mask3):
        return jnp.sum(lane_dot(mask3, ones), axis=0, keepdims=True)

    def excl_cumsum(mask3):
        rows = mask3.shape[0]
        tot = lane_dot(mask3, ones)
        run = tot
        s = 1
        while s < rows:
            run = run + jnp.concatenate([jnp.zeros((s,) + run.shape[1:], F32), run[:rows - s]], axis=0)
            s *= 2
        before = run - tot
        return before + lane_dot(mask3, tri) - jnp.where(mask3, 1.0, 0.0), before

    def step(t, thrs):
        bit = lax.shift_left(jnp.int32(1), 30 - t)
        out = []
        for aff, thr, cap in zip(affs, thrs, caps):
            cand = thr | bit
            cnt = expert_count(aff >= pltpu.bitcast(cand, F32))
            out.append(jnp.where(cnt >= cap, cand, thr))
        return tuple(out)

    init = tuple(jnp.zeros((1,) + a.shape[1:], I32) for a in affs)
    thrs = lax.fori_loop(0, 31, step, init)
    for aff, thr_bits, cap, pos_ref, base_ref in zip(affs, thrs, caps, pos_refs, base_refs):
        thr = pltpu.bitcast(thr_bits, F32)
        gt = aff > thr
        eq = aff == thr
        need = cap - expert_count(gt)
        tie_rank, _ = excl_cumsum(eq)
        sel = gt | (eq & (tie_rank < need))
        pos, before = excl_cumsum(sel)
        pos_ref[...] = jnp.where(sel, pos, -1.0).astype(I32)
        base_ref[...] = before


def _route(affs, caps):
    shapes = [a.shape for a in affs]
    return pl.pallas_call(
        functools.partial(_route_kernel, tuple(caps)),
        out_shape=[jax.ShapeDtypeStruct(s, I32) for s in shapes] +
                  [jax.ShapeDtypeStruct(s, F32) for s in shapes],
        compiler_params=pltpu.CompilerParams(vmem_limit_bytes=VMEM_LIMIT),
        name="route",
    )(*affs)


def _window(off_ref, e, i, n_tiles, w, cap):
    del n_tiles
    first = off_ref[i * N_EXPERTS + e] + w * SLOT_WIN
    return first, pl.multiple_of(jnp.minimum(first, cap - SLOT_WIN), BF16_ROWS)


def _one_hot_rows(pos_ref, wins, row0=0):
    r = lax.broadcasted_iota(I32, (SLOT_WIN, LANES), 0)
    blocks = []
    for e, (first, off) in enumerate(wins):
        halves = []
        for h in range(ROW_TILE // LANES):
            pe = pos_ref[row0 + h, e:e + 1, :]
            rel = jnp.where(pe >= first, pe - off, -1)
            halves.append(jnp.where(rel == r, 1.0, 0.0).astype(BF16))
        blocks.append(jnp.concatenate(halves, axis=1))
    return jnp.concatenate(blocks, axis=0)


DISPATCH_TILES = 2
FLUSH_PARTS = 4


def _dispatch_kernel(cap, n_steps, off_ref, nwin_ref, flush_ref, h2_ref, pos_ref, xs_hbm,
                     xs_ref, sems):
    i = pl.program_id(0)
    per = h2_ref.shape[1] // DISPATCH_TILES

    @pl.when(i == 0)
    def _():
        def zero(e, carry):
            xs_ref[e] = jnp.zeros(xs_ref.shape[1:], BF16)
            return carry
        lax.fori_loop(0, N_EXPERTS, zero, 0)

    def gather(t, w):
        tile = i * DISPATCH_TILES + t
        wins = [_window(off_ref, e, tile, None, w, cap) for e in range(N_EXPERTS)]
        p = _one_hot_rows(pos_ref, wins, t * (ROW_TILE // LANES))
        h2 = h2_ref[:, t * per:(t + 1) * per, :].reshape(ROW_TILE, XS_W)
        return wins, _bdot(p, h2)

    def merge(wins, slab):
        for e, (_, off) in enumerate(wins):
            cur = xs_ref[e, pl.ds(off, SLOT_WIN), :].astype(F32)
            xs_ref[e, pl.ds(off, SLOT_WIN), :] = (
                cur + slab[e * SLOT_WIN:(e + 1) * SLOT_WIN]).astype(BF16)

    for wins, slab in [gather(t, 0) for t in range(DISPATCH_TILES)]:
        merge(wins, slab)
    for t in range(DISPATCH_TILES):
        def extra(w, carry, t=t):
            merge(*gather(t, w))
            return carry
        lax.fori_loop(1, nwin_ref[i * DISPATCH_TILES + t], extra, 0)

    part = cap // FLUSH_PARTS

    def flush(q):
        rows = pl.ds(q * part, part)
        return pltpu.make_async_copy(xs_ref.at[:, rows, :], xs_hbm.at[:, rows, :], sems.at[q])

    for q in range(FLUSH_PARTS):
        @pl.when(i == flush_ref[q])
        def _(q=q):
            flush(q).start()

    @pl.when(i == n_steps - 1)
    def _():
        for q in range(FLUSH_PARTS):
            flush(q).wait()


def _tile_spec(stripes, width, tiles=1):
    return pl.BlockSpec((stripes, tiles * ROW_TILE // stripes, width), lambda i, *_: (0, i, 0))


def _dispatch(h2aug, pos, offs, nwin, cap, stripes):
    n = h2aug.shape[0]
    n_steps = n // (ROW_TILE * DISPATCH_TILES)
    low = jnp.min(offs.reshape(-1, N_EXPERTS), axis=1)[DISPATCH_TILES::DISPATCH_TILES]
    low = jnp.concatenate([low, jnp.full((1,), cap, I32)])
    bounds = (jnp.arange(FLUSH_PARTS, dtype=I32) + 1) * (cap // FLUSH_PARTS)
    flush = jnp.argmax(low[None, :] >= bounds[:, None], axis=1).astype(I32)
    return pl.pallas_call(
        functools.partial(_dispatch_kernel, cap, n_steps),
        out_shape=jax.ShapeDtypeStruct((N_EXPERTS, cap, XS_W), BF16),
        grid_spec=pltpu.PrefetchScalarGridSpec(
            num_scalar_prefetch=3,
            grid=(n_steps,),
            in_specs=[_tile_spec(stripes, XS_W, DISPATCH_TILES),
                      pl.BlockSpec((DISPATCH_TILES * ROW_TILE // LANES, N_EXPERTS, LANES),
                                   lambda i, *_: (i, 0, 0))],
            out_specs=pl.BlockSpec(memory_space=pl.ANY),
            scratch_shapes=[pltpu.VMEM((N_EXPERTS, cap, XS_W), BF16),
                            pltpu.SemaphoreType.DMA((FLUSH_PARTS,))]),
        compiler_params=_params(1, VMEM_LIMIT),
        name="dispatch",
    )(offs, nwin, flush, h2aug.reshape(stripes, n // stripes, XS_W), pos)


FF_CHUNK = 512
EXP_ROWS = 512


def _experts_kernel(n_groups, n_chunks, *refs):
    xs_refs = refs[:n_groups]
    wg_ref, wu_ref, wd_ref = refs[n_groups:n_groups + 3]
    ys_refs = refs[n_groups + 3:2 * n_groups + 3]
    acc_ref = refs[2 * n_groups + 3]
    e = pl.program_id(0)
    cap = xs_refs[0].shape[1]
    blocks = [(g, slice(r * EXP_ROWS, (r + 1) * EXP_ROWS))
              for g in range(n_groups) for r in range(cap // EXP_ROWS)]

    def chunk(first, last):
        wg = wg_ref[0].astype(BF16)
        wu = wu_ref[0].astype(BF16)
        wd = wd_ref[0].astype(BF16)

        def up(blk):
            g, rows = blk
            x = xs_refs[g][0, rows, :D_MODEL]
            return _bdot(x, wg), _bdot(x, wu)

        nxt = up(blocks[0])
        for k, (g, rows) in enumerate(blocks):
            a, u = nxt
            if k + 1 < len(blocks):
                nxt = up(blocks[k + 1])
            hm = (a * jax.nn.sigmoid(a) * u).astype(BF16)
            y = _bdot(hm, wd)
            if not first:
                y = acc_ref[g, rows, :] + y
            if last:
                aug = xs_refs[g][0, rows, D_MODEL:].astype(F32)
                lane = lax.broadcasted_iota(I32, aug.shape, 1)
                mine = (lane == e) | (lane == e + N_EXPERTS) | (lane == e + 2 * N_EXPERTS)
                gate = jnp.sum(jnp.where(mine, aug, 0.0), axis=-1, keepdims=True)
                ys_refs[g][0, rows, :] = (y * gate).astype(BF16)
            else:
                acc_ref[g, rows, :] = y

    for c in range(n_chunks):
        pl.when(pl.program_id(1) == c)(functools.partial(chunk, c == 0, c == n_chunks - 1))


def _experts(xs_list, w_gate, w_up, w_down):
    n_groups = len(xs_list)
    cap = xs_list[0].shape[1]
    ff = w_gate.shape[2]
    n_chunks = ff // FF_CHUNK
    xs_spec = pl.BlockSpec((1, cap, XS_W), lambda e, j: (e, 0, 0))
    up_spec = pl.BlockSpec((1, D_MODEL, FF_CHUNK), lambda e, j: (e, 0, j))
    down_spec = pl.BlockSpec((1, FF_CHUNK, D_MODEL), lambda e, j: (e, j, 0))
    ys_spec = pl.BlockSpec((1, cap, D_MODEL), lambda e, j: (e, 0, 0))
    return pl.pallas_call(
        functools.partial(_experts_kernel, n_groups, n_chunks),
        out_shape=[jax.ShapeDtypeStruct((N_EXPERTS, cap, D_MODEL), BF16)] * n_groups,
        grid=(N_EXPERTS, n_chunks),
        in_specs=[xs_spec] * n_groups + [up_spec, up_spec, down_spec],
        out_specs=[ys_spec] * n_groups,
        scratch_shapes=[pltpu.VMEM((n_groups, cap, D_MODEL), F32)],
        compiler_params=_params(2, VMEM_LIMIT),
        name="experts",
    )(*xs_list, w_gate, w_up, w_down)


COMBINE_TILES = 4


def _combine_kernel(cap, n_steps, off_ref, nwin_ref, ys_hbm, pos_ref, x1_ref, g2_ref, fg_ref,
                    y_ref, buf_ref, sems):
    i = pl.program_id(0)
    slot = i % 2
    per = x1_ref.shape[1] // COMBINE_TILES

    def copies(step, t, w, s):
        out = []
        for e in range(N_EXPERTS):
            _, off = _window(off_ref, e, step * COMBINE_TILES + t, None, w, cap)
            out.append(pltpu.make_async_copy(
                ys_hbm.at[e, pl.ds(off, SLOT_WIN), :],
                buf_ref.at[s, t, pl.ds(e * SLOT_WIN, SLOT_WIN), :],
                sems.at[s, t]))
        return out

    @pl.when(i == 0)
    def _():
        for t in range(COMBINE_TILES):
            for cp in copies(0, t, 0, 0):
                cp.start()

    @pl.when(i + 1 < n_steps)
    def _():
        for t in range(COMBINE_TILES):
            for cp in copies(i + 1, t, 0, 1 - slot):
                cp.start()

    tn = (((0,), (0,)), ((), ()))

    def one_hot(t, w):
        wins = [_window(off_ref, e, i * COMBINE_TILES + t, None, w, cap) for e in range(N_EXPERTS)]
        return _one_hot_rows(pos_ref, wins, t * (ROW_TILE // LANES))

    def scattered(t, w, p):
        for cp in copies(i, t, w, slot):
            cp.wait()
        return lax.dot_general(p, buf_ref[slot, t], tn, preferred_element_type=F32)

    ps = [one_hot(t, 0) for t in range(COMBINE_TILES)]
    moes = [scattered(t, 0, ps[t]) for t in range(COMBINE_TILES)]
    for t in range(COMBINE_TILES):
        def extra_window(w, acc, t=t):
            for cp in copies(i, t, w, slot):
                cp.start()
            return acc + scattered(t, w, one_hot(t, w))
        moe = lax.fori_loop(1, nwin_ref[i * COMBINE_TILES + t], extra_window, moes[t])
        rows = slice(t * per, (t + 1) * per)
        x1 = x1_ref[:, rows, :]
        out = x1 + g2_ref[...] * moe.reshape(x1.shape)
        y_ref[:, rows, :] = _rms(out, fg_ref[...])


def _combine(ys, pos, offs, nwin, x1, mod3, mod_row, stripes, final_g, cap):
    n = x1.shape[0]
    n_steps = n // (ROW_TILE * COMBINE_TILES)
    g2 = pl.BlockSpec((stripes, 1, D_MODEL), lambda i, *_: (0 if mod_row is None else mod_row, 0, 5))
    y = pl.pallas_call(
        functools.partial(_combine_kernel, cap, n_steps),
        out_shape=jax.ShapeDtypeStruct((stripes, n // stripes, D_MODEL), F32),
        grid_spec=pltpu.PrefetchScalarGridSpec(
            num_scalar_prefetch=2,
            grid=(n_steps,),
            in_specs=[pl.BlockSpec(memory_space=pl.ANY),
                      pl.BlockSpec((COMBINE_TILES * ROW_TILE // LANES, N_EXPERTS, LANES),
                                   lambda i, *_: (i, 0, 0)),
                      _tile_spec(stripes, D_MODEL, COMBINE_TILES),
                      g2,
                      pl.BlockSpec((1, D_MODEL), lambda i, *_: (0, 0))],
            out_specs=_tile_spec(stripes, D_MODEL, COMBINE_TILES),
            scratch_shapes=[pltpu.VMEM((2, COMBINE_TILES, N_EXPERTS * SLOT_WIN, D_MODEL), BF16),
                            pltpu.SemaphoreType.DMA((2, COMBINE_TILES))]),
        compiler_params=_params(1, VMEM_LIMIT),
        name="combine",
    )(offs, nwin, ys, pos, x1.reshape(stripes, n // stripes, D_MODEL), mod3, final_g)
    return y.reshape(n, D_MODEL)


def _head_cols(w3):
    k, _, d = w3.shape
    return jnp.pad(w3, ((0, 0), (0, 0), (0, HEAD_PAD - d))).reshape(k, QK_WIDTH)


def _value_cols(v3):
    k = v3.shape[0]
    pair = v3.reshape(k, MLA_HEADS // 2, 2, V_DIM)
    zeros = jnp.zeros((k, MLA_HEADS // 2, V_DIM), v3.dtype)
    even = jnp.concatenate([pair[:, :, 0], zeros], axis=-1)
    odd = jnp.concatenate([zeros, pair[:, :, 1]], axis=-1)
    return jnp.stack([even, odd], axis=2).reshape(k, QK_WIDTH)


def _rope_tables(seq):
    half = QK_ROPE // 2
    pos = jnp.arange(seq)
    inv_freq = 1.0 / (ROPE_BASE ** (jnp.arange(0, half, 2, dtype=F32) / half))

    def cs(p):
        ang = p.astype(F32)[:, None] * inv_freq[None, :]
        ang = jnp.concatenate([ang, ang], axis=-1)
        return jnp.cos(ang), jnp.sin(ang)

    cr, sr = cs(pos // GRID_W)
    cc, sc = cs(pos % GRID_W)
    ones = jnp.ones((seq, QK_NOPE), F32)
    zpad = jnp.zeros((seq, HEAD_PAD - QK_NOPE - QK_ROPE), F32)
    cq = jnp.concatenate([ones, cr, cc, zpad], axis=-1)
    sq = jnp.concatenate([0.0 * ones, sr, sc, zpad], axis=-1)
    return cq, sq


def _prep_weights(w_in, q_norm_g, w_q_up, kv_norm_g, w_kv_up, attn_norm_g):
    wqc = w_in[:, :Q_RANK + KV_RANK].astype(BF16)
    wkr = jnp.pad(w_in[:, Q_RANK + KV_RANK:Q_RANK + KV_RANK + QK_ROPE],
                  ((0, 0), (0, LANES - QK_ROPE))).astype(BF16)
    wglu = w_in[:, Q_RANK + KV_RANK + QK_ROPE:].astype(BF16)
    wq = _head_cols(w_q_up.astype(BF16).reshape(Q_RANK, MLA_HEADS, QK_NOPE + QK_ROPE))
    kv3 = w_kv_up.astype(BF16).reshape(KV_RANK, MLA_HEADS, QK_NOPE + V_DIM)
    wk = _head_cols(kv3[:, :, :QK_NOPE])
    wv = _value_cols(kv3[:, :, QK_NOPE:])
    pk = np.zeros((LANES, QK_WIDTH), np.float32)
    for hd in range(MLA_HEADS):
        for j in range(QK_ROPE):
            pk[j, hd * HEAD_PAD + QK_NOPE + j] = 1.0
    pk = jnp.asarray(pk, BF16)
    return (attn_norm_g[None, :], wqc, wkr, wglu, q_norm_g[None, :], wq,
            kv_norm_g[None, :], wk, wv, pk)


def _stripe_order(aff3, stripes, seq):
    per = ROW_TILE // stripes
    q = LANES // per
    assert per * stripes == ROW_TILE and per * q == LANES and stripes % q == 0 and seq % LANES == 0
    a = aff3.reshape(stripes // q, q, seq // LANES, N_EXPERTS, q, per)
    return a.transpose(2, 4, 0, 3, 1, 5).reshape(aff3.shape)


def _route_plan(base, cap):
    rows_per_tile = ROW_TILE // LANES
    start = base[::rows_per_tile, :, 0].astype(I32)
    end = jnp.concatenate([start[1:], jnp.full((1, N_EXPERTS), cap, I32)], axis=0)
    off = (start // BF16_ROWS) * BF16_ROWS
    span = end - off
    nwin = jnp.maximum(jnp.max((span + SLOT_WIN - 1) // SLOT_WIN, axis=1), 1).astype(I32)
    return off.reshape(-1), nwin


def kernel(x_prompt, x_sample, cache_ckv, cache_krope, c, c_ctx, w_mod, b_mod, attn_norm_g,
           w_in, q_norm_g, w_q_up, kv_norm_g, w_kv_up, conv_w, conv_b, conv_ln_g, conv_ln_b,
           w_out, ffn_norm_g, w_router, w_gate, w_up, w_down, final_norm_g):
    depth = w_mod.shape[0]
    assert depth == 1
    bp, sp, _ = x_prompt.shape
    bs, ss, _ = x_sample.shape
    past = cache_ckv.shape[2]
    ctx_row = bs

    mod_rows = 16
    cc = jnp.concatenate([c, c_ctx[None, :], jnp.zeros((mod_rows - bs - 1, D_MODEL), F32)], axis=0)
    mod = _adaln(cc, w_mod[0], b_mod[0][None, :])
    mod3 = mod.reshape(mod_rows, 1, N_MOD * D_MODEL)

    fw = _prep_weights(w_in[0], q_norm_g[0], w_q_up[0], kv_norm_g[0], w_kv_up[0], attn_norm_g[0])
    wk, wv, pk = fw[7], fw[8], fw[9]
    wo = w_out[0].astype(BF16)
    wr = jnp.pad(jnp.tile(w_router[0], (1, 3)), ((0, 0), (0, LANES - 3 * N_EXPERTS)))
    wrh = wr.astype(BF16)
    wrl = (wr - wrh.astype(F32)).astype(BF16)
    cvw = (conv_w[0], conv_b[0][None, :], conv_ln_g[0][None, :], conv_ln_b[0][None, :])
    bw = (wo[:MLA_HEADS * V_DIM], wo[MLA_HEADS * V_DIM:], ffn_norm_g[0][None, :], wrh, wrl)

    xp2 = x_prompt.reshape(bp * sp, D_MODEL)
    xs2 = x_sample.reshape(bs * ss, D_MODEL)

    qp, kp, vp, cv_p, ckvp, krp = _front(xp2, mod3, ctx_row, sp, fw, cvw, None)
    attn_p = _attention(qp, [(kp, vp, sp)], bp, sp)
    x1p, h2p, afftp = _back(xp2, attn_p, cv_p, mod3, ctx_row, bw, sp)

    qs, ks, vs, cv_s = _front(xs2, mod3, None, ss, fw, cvw, _rope_tables(ss))
    ctx = (cache_ckv[:, 0].reshape(bs * past, KV_RANK),
           cache_krope[:, 0].reshape(bs * past, QK_ROPE), past, wk, wv, pk)
    attn_s = _attention(qs, [(ks, vs, ss)], bs, ss, ctx)
    x1s, h2s, affts = _back(xs2, attn_s, cv_s, mod3, None, bw, ss)

    stripes = (1, bs)
    caps = [EC_FACTOR * h2.shape[0] // N_EXPERTS for h2 in (h2p, h2s)]
    routed = _route([afftp, _stripe_order(affts, bs, ss)], caps)
    plans = []
    xs_list = []
    for g, (h2, cap) in enumerate(zip((h2p, h2s), caps)):
        pos, base = routed[g], routed[len(caps) + g]
        offs, nwin = _route_plan(base, cap)
        plans.append((pos, offs, nwin, cap))
        xs_list.append(_dispatch(h2, pos, offs, nwin, cap, stripes[g]))
    ys_list = _experts(xs_list, w_gate[0], w_up[0], w_down[0])

    fg = final_norm_g[None, :]
    pos, offs, nwin, cap = plans[0]
    y_prompt = _combine(ys_list[0], pos, offs, nwin, x1p, mod3, ctx_row, stripes[0], fg, cap)
    pos, offs, nwin, cap = plans[1]
    y_sample = _combine(ys_list[1], pos, offs, nwin, x1s, mod3, None, stripes[1], fg, cap)

    return (y_prompt.reshape(bp, sp, D_MODEL), y_sample.reshape(bs, ss, D_MODEL),
            ckvp.reshape(bp, depth, sp, KV_RANK), krp.reshape(bp, depth, sp, QK_ROPE))
```

```python
import functools

import jax
import jax.numpy as jnp
import numpy as np
from jax import lax
from jax.experimental import pallas as pl
from jax.experimental.pallas import tpu as pltpu

F32 = jnp.float32
BF16 = jnp.bfloat16
I32 = jnp.int32

LANES = 128
BF16_ROWS = 16
VMEM_LIMIT = 56 * 1024 * 1024

D_MODEL = 1024
GRID_W = 64
MLA_HEADS = 8
QK_NOPE = 64
QK_ROPE = 32
V_DIM = 64
Q_RANK = 768
KV_RANK = 256
CONV_CH = 512
CONV_WIDTH = 31
N_EXPERTS = 16
EXPERT_FF = 1024
EC_FACTOR = 2
ROPE_BASE = 10000.0
EPS = 1e-6
N_MOD = 6

HEAD_PAD = LANES
QK_WIDTH = MLA_HEADS * HEAD_PAD
ROW_TILE = 256
FRONT_TILE = 512
ATTN_TILE = 512
ATTN_PACK = 1024
BACK_TILE = 1024
BACK_SUB = 256
SLOT_WIN = 64
AUG = LANES
XS_W = D_MODEL + AUG
LOG2_E = 1.4426950408889634


def _params(n_axes, vmem=None):
    return pltpu.CompilerParams(
        dimension_semantics=("arbitrary",) * n_axes,
        vmem_limit_bytes=vmem)


def _rms(x, g):
    ms = jnp.mean(x * x, axis=-1, keepdims=True)
    return x * lax.rsqrt(ms + EPS) * g


def _bdot(a, b):
    return jnp.dot(a, b, preferred_element_type=F32)


def _adaln_kernel(c_ref, w_ref, b_ref, o_ref):
    c = c_ref[...]
    s = c * jax.nn.sigmoid(c)
    w = w_ref[...]
    sh = s.astype(BF16)
    sl = (s - sh.astype(F32)).astype(BF16)
    wh = w.astype(BF16)
    wl = (w - wh.astype(F32)).astype(BF16)
    o_ref[...] = _bdot(sh, wh) + (_bdot(sl, wh) + _bdot(sh, wl)) + b_ref[...]


def _adaln(cc, w_mod, b_mod):
    rows = cc.shape[0]
    n_out = w_mod.shape[1]
    blk = D_MODEL
    return pl.pallas_call(
        _adaln_kernel,
        out_shape=jax.ShapeDtypeStruct((rows, n_out), F32),
        grid=(n_out // blk,),
        in_specs=[pl.BlockSpec((rows, D_MODEL), lambda j: (0, 0)),
                  pl.BlockSpec((D_MODEL, blk), lambda j: (0, j)),
                  pl.BlockSpec((1, blk), lambda j: (0, j))],
        out_specs=pl.BlockSpec((rows, blk), lambda j: (0, j)),
        compiler_params=_params(1),
        name="adaln",
    )(cc, w_mod, b_mod)


def _rot_half(x, lane, base):
    rel = lane - base
    first = ((rel >= 0) & (rel < 8)) | ((rel >= 16) & (rel < 24))
    return jnp.where(first, -pltpu.roll(x, LANES - 8, axis=1), pltpu.roll(x, 8, axis=1))


def _front_kernel(rope, seq, x_ref, sh_ref, sc_ref, g_ref, wqc_ref, wkr_ref, wglu_ref,
                  qg_ref, wq_ref, kvg_ref, wk_ref, wv_ref, pk_ref, cw_ref, cb_ref, lg_ref, lb_ref,
                  *rest):
    if rope:
        cq_ref, sq_ref, q_ref, k_ref, v_ref, cv_ref, vpad_ref, shift_ref = rest
    else:
        q_ref, k_ref, v_ref, cv_ref, ckv_ref, kr_ref, vpad_ref, shift_ref = rest
    rows = x_ref.shape[0]
    steps_per_seq = max(seq // rows, 1)
    scale = (QK_NOPE + QK_ROPE) ** -0.5 * LOG2_E
    lane = lax.broadcasted_iota(I32, (ROW_TILE, LANES), 1)
    conv_refs = (cw_ref, cb_ref, lg_ref, lb_ref, shift_ref)

    pending = []

    def tick(n=1):
        for _ in range(n):
            if pending:
                pending.pop(0)()

    def gated_values(qd, row_in_seq, slot, cv_row):
        r = slice(qd * ROW_TILE, (qd + 1) * ROW_TILE)
        h = _rms(x_ref[r, :], g_ref[...]) * (1.0 + sc_ref[0]) + sh_ref[0]
        hb = h.astype(BF16)
        glu = _bdot(hb, wglu_ref[...])
        if row_in_seq == 0:
            vpad_ref[slot, 0:CONV_HALO, :] = jnp.zeros((CONV_HALO, CONV_CH), F32)
        if row_in_seq + ROW_TILE == seq:
            vpad_ref[slot, CONV_HALO + seq:, :] = jnp.zeros((CONV_HALO, CONV_CH), F32)
        vpad_ref[slot, CONV_HALO + row_in_seq:CONV_HALO + row_in_seq + ROW_TILE, :] = (
            glu[:, :CONV_CH] * jax.nn.sigmoid(glu[:, CONV_CH:]))

        ready = ([row_in_seq - ROW_TILE] if row_in_seq > 0 else []) + \
                ([row_in_seq] if row_in_seq + ROW_TILE == seq else [])
        for t0 in ready:
            steps, finish = _conv_stages(CONV_STAGES, vpad_ref, slot, t0, *conv_refs)
            dst = slice(cv_row + t0, cv_row + t0 + ROW_TILE)
            pending.extend(steps + [functools.partial(_store_conv, cv_ref, dst, finish)])
        return hb

    def projections(qd, row_in_seq, hb, ticks):
        r = slice(qd * ROW_TILE, (qd + 1) * ROW_TILE)
        qc = _bdot(hb, wqc_ref[...])
        tick(ticks)
        kr = _bdot(hb, wkr_ref[...])
        tick(ticks)
        qn = _rms(qc[:, :Q_RANK], qg_ref[...]).astype(BF16)
        q = _bdot(qn, wq_ref[...])
        tick(ticks)
        ckv = _rms(qc[:, Q_RANK:], kvg_ref[...])
        ckv_b = ckv.astype(BF16)
        kk = _bdot(ckv_b, wk_ref[...])
        tick(ticks)
        v_ref[r, :] = _bdot(ckv_b, wv_ref[...]).astype(BF16)
        tick(ticks)
        if rope:
            t = slice(row_in_seq % rows, row_in_seq % rows + ROW_TILE)
            cq = cq_ref[t, :]
            sq = sq_ref[t, :]
            for hd in range(MLA_HEADS):
                blk = q[:, hd * HEAD_PAD:(hd + 1) * HEAD_PAD]
                rot = blk * cq + _rot_half(blk, lane, QK_NOPE) * sq
                q_ref[r, hd * HEAD_PAD:(hd + 1) * HEAD_PAD] = (rot * scale).astype(BF16)
            krs = pltpu.roll(kr, QK_NOPE, axis=1)
            kr = pltpu.roll(krs * cq + _rot_half(krs, lane, QK_NOPE) * sq, LANES - QK_NOPE, axis=1)
            kr = jnp.where(lane < QK_ROPE, kr, 0.0)
        else:
            q_ref[r, :] = (q * scale).astype(BF16)
            ckv_ref[r, :] = ckv
            kr_ref[r, :] = kr[:, :QK_ROPE]
        tick(ticks)
        k_ref[r, :] = (kk + _bdot(kr.astype(BF16), pk_ref[...])).astype(BF16)

    def step(k):
        n_q = rows // ROW_TILE
        hbs = []
        for qd in range(n_q):
            row = k * rows + qd * ROW_TILE
            if seq >= rows:
                hbs.append((row, gated_values(qd, row, 0, 0)))
            else:
                hbs.append((row % seq, gated_values(qd, row % seq, row // seq, (row // seq) * seq)))
        ticks = -(-len(pending) // (6 * n_q))
        for qd, (row_in_seq, hb) in enumerate(hbs):
            projections(qd, row_in_seq, hb, ticks)
        tick(len(pending))

    if steps_per_seq == 1:
        step(0)
    else:
        for k in range(steps_per_seq):
            pl.when(pl.program_id(0) % steps_per_seq == k)(functools.partial(step, k))


def _store_conv(cv_ref, dst, finish):
    cv_ref[dst, :] = finish()


def _front(x2d, mod3, mod_row, seq, wts, conv_wts, rope_tabs):
    n = x2d.shape[0]
    rope = rope_tabs is not None
    tile = FRONT_TILE
    steps_per_seq = max(seq // tile, 1)
    assert seq % ROW_TILE == 0 and (tile % seq == 0 or seq % tile == 0)
    full = lambda a: pl.BlockSpec(a.shape, lambda i: (0,) * a.ndim)

    def mod_spec(col):
        if mod_row is None:
            assert seq % tile == 0
            return pl.BlockSpec((1, 1, D_MODEL), lambda i: (i // steps_per_seq, 0, col))
        return pl.BlockSpec((1, 1, D_MODEL), lambda i: (mod_row, 0, col))

    row = lambda w: pl.BlockSpec((tile, w), lambda i: (i, 0))
    cv_rows = max(seq, tile)
    cv_spec = pl.BlockSpec((cv_rows, CONV_CH), lambda i: (i // steps_per_seq, 0))
    in_specs = [row(D_MODEL), mod_spec(0), mod_spec(1)] + [full(w) for w in wts + conv_wts]
    args = [x2d, mod3, mod3] + list(wts + conv_wts)
    out_shape = [jax.ShapeDtypeStruct((n, QK_WIDTH), BF16)] * 3 + \
                [jax.ShapeDtypeStruct((n, CONV_CH), BF16)]
    out_specs = [row(QK_WIDTH)] * 3 + [cv_spec]
    if rope:
        tab = pl.BlockSpec((tile, LANES), lambda i: (i % steps_per_seq, 0))
        in_specs += [tab, tab]
        args += list(rope_tabs)
    else:
        out_shape += [jax.ShapeDtypeStruct((n, KV_RANK), F32),
                      jax.ShapeDtypeStruct((n, QK_ROPE), F32)]
        out_specs += [row(KV_RANK), row(QK_ROPE)]
    slots = max(tile // seq, 1)
    return pl.pallas_call(
        functools.partial(_front_kernel, rope, seq),
        out_shape=out_shape,
        grid=(n // tile,),
        in_specs=in_specs,
        out_specs=out_specs,
        scratch_shapes=[pltpu.VMEM((slots, seq + 2 * CONV_HALO, CONV_CH), F32),
                        pltpu.VMEM((SUBLANES - 1, SHIFT_ROWS, CONV_CH), F32)],
        compiler_params=_params(1, VMEM_LIMIT),
        name="front_rope" if rope else "front",
    )(*args)


CONV_PAD = (CONV_WIDTH - 1) // 2
CONV_HALO = 16
SUBLANES = 8
SHIFT_ROWS = ROW_TILE + 2 * CONV_HALO - SUBLANES


CONV_STAGES = 6


def _conv_stages(n_stages, vpad_ref, slot, t0, cw_ref, cb_ref, lg_ref, lb_ref, shift_ref):
    state = {}

    def taps(lo, hi):
        if lo == 0:
            win = vpad_ref[slot, t0:t0 + ROW_TILE + 2 * CONV_HALO, :]
            for ph in range(1, SUBLANES):
                shift_ref[ph - 1] = win[ph:ph + SHIFT_ROWS, :]
            state["acc"] = jnp.broadcast_to(cb_ref[...], (ROW_TILE, CONV_CH))
        acc = state["acc"]
        for t in range(lo, hi):
            blk, ph = divmod(t + CONV_HALO - CONV_PAD, SUBLANES)
            if ph == 0:
                tap = vpad_ref[slot, t0 + blk * SUBLANES:t0 + blk * SUBLANES + ROW_TILE, :]
            else:
                tap = shift_ref[ph - 1, blk * SUBLANES:blk * SUBLANES + ROW_TILE, :]
            acc = acc + tap * cw_ref[t:t + 1, :]
        state["acc"] = acc

    def finish():
        acc = state["acc"]
        mu = jnp.mean(acc, axis=-1, keepdims=True)
        cen = acc - mu
        var = jnp.mean(cen * cen, axis=-1, keepdims=True)
        ln = cen * lax.rsqrt(var + EPS) * lg_ref[...] + lb_ref[...]
        return (ln * jax.nn.sigmoid(ln)).astype(BF16)

    bounds = [CONV_WIDTH * s // n_stages for s in range(n_stages + 1)]
    stages = [functools.partial(taps, bounds[s], bounds[s + 1]) for s in range(n_stages)]
    return stages, finish


def _attn_kernel(n_kv, n_seq, with_ctx, q_ref, *rest):
    kv_refs = list(rest[:2 * n_kv])
    if with_ctx:
        ckv_ref, kr_ref, wk_ref, wv_ref, pk_ref, o_ref, kc_ref, vc_ref = rest[2 * n_kv:]

        @pl.when(pl.program_id(1) == 0)
        def _():
            cb = ckv_ref[...].astype(BF16)
            kc_ref[...] = (_bdot(cb, wk_ref[...]) +
                           _bdot(kr_ref[...].astype(BF16), pk_ref[...])).astype(BF16)
            vc_ref[...] = _bdot(cb, wv_ref[...]).astype(BF16)

        kv_refs += [kc_ref, vc_ref]
        n_kv += 1
    else:
        o_ref = rest[2 * n_kv]
    nt = (((1,), (1,)), ((), ()))
    tq = q_ref.shape[0] // n_seq
    work = [(b, hd) for b in range(n_seq) for hd in range(MLA_HEADS)]

    def kv_rows(ref, b):
        t = ref.shape[0] // n_seq
        return slice(b * t, (b + 1) * t)

    def scores(item):
        b, hd = item
        sl = slice(hd * HEAD_PAD, (hd + 1) * HEAD_PAD)
        qh = q_ref[b * tq:(b + 1) * tq, sl]
        return [lax.dot_general(qh, kv_refs[2 * j][kv_rows(kv_refs[2 * j], b), sl], nt,
                                preferred_element_type=F32) for j in range(n_kv)]

    outs = {}
    nxt = scores(work[0])
    for w, (b, hd) in enumerate(work):
        sl = slice(hd * HEAD_PAD, (hd + 1) * HEAD_PAD)
        ss = nxt
        if w + 1 < len(work):
            nxt = scores(work[w + 1])
        m = ss[0].max(axis=-1, keepdims=True)
        for s in ss[1:]:
            m = jnp.maximum(m, s.max(axis=-1, keepdims=True))
        ps = [jnp.exp2(s - m) for s in ss]
        l = ps[0].sum(axis=-1, keepdims=True)
        for p in ps[1:]:
            l = l + p.sum(axis=-1, keepdims=True)
        o = None
        for j in range(n_kv):
            v_ref = kv_refs[2 * j + 1]
            pv = _bdot(ps[j].astype(BF16), v_ref[kv_rows(v_ref, b), sl])
            o = pv if o is None else o + pv
        outs[hd] = o / l
        if hd % 2 == 1:
            o_ref[b * tq:(b + 1) * tq, (hd // 2) * LANES:(hd // 2 + 1) * LANES] = (
                outs[hd - 1] + outs[hd]).astype(BF16)


def _attention(q, kvs, batch, seq, ctx=None):
    n = q.shape[0]
    n_seq = max(ATTN_PACK // seq, 1) if seq <= ATTN_TILE else 1
    tile = min(seq, ATTN_TILE) * n_seq
    q_tiles = max(seq // tile, 1)
    assert batch % n_seq == 0
    full = lambda a: pl.BlockSpec(a.shape, lambda b, i: (0,) * a.ndim)
    row = lambda w: pl.BlockSpec((tile, w), lambda b, i: (b * q_tiles + i, 0))
    in_specs = [row(QK_WIDTH)]
    args = [q]
    for k, v, t in kvs:
        spec = pl.BlockSpec((n_seq * t, QK_WIDTH), lambda b, i: (b, 0))
        in_specs += [spec, spec]
        args += [k, v]
    scratch = []
    if ctx is not None:
        assert n_seq == 1
        ckv, kr, past, wk, wv, pk = ctx
        pk = pk[:QK_ROPE]
        in_specs += [pl.BlockSpec((past, KV_RANK), lambda b, i: (b, 0)),
                     pl.BlockSpec((past, QK_ROPE), lambda b, i: (b, 0)), full(wk), full(wv), full(pk)]
        args += [ckv, kr, wk, wv, pk]
        scratch = [pltpu.VMEM((past, QK_WIDTH), BF16), pltpu.VMEM((past, QK_WIDTH), BF16)]
    return pl.pallas_call(
        functools.partial(_attn_kernel, len(kvs), n_seq, ctx is not None),
        out_shape=jax.ShapeDtypeStruct((n, MLA_HEADS * V_DIM), BF16),
        grid=(batch // n_seq, q_tiles),
        in_specs=in_specs,
        out_specs=row(MLA_HEADS * V_DIM),
        scratch_shapes=scratch,
        compiler_params=_params(2, VMEM_LIMIT),
        name="attention",
    )(*args)


def _back_kernel(x_ref, attn_ref, cv_ref, woa_ref, woc_ref, g1_ref, sh_ref, sc_ref, fg_ref,
                 wrh_ref, wrl_ref, x1_ref, h2_ref, afft_ref):
    n_sub = x_ref.shape[0] // BACK_SUB
    rows = [slice(s * BACK_SUB, (s + 1) * BACK_SUB) for s in range(n_sub)]

    def project(r):
        return _bdot(attn_ref[r, :], woa_ref[...]) + _bdot(cv_ref[r, :], woc_ref[...])

    def normalise(r, mix):
        x1 = x_ref[r, :] + g1_ref[0] * mix
        x1_ref[r, :] = x1
        h2 = _rms(x1, fg_ref[...]) * (1.0 + sc_ref[0]) + sh_ref[0]
        h2b = h2.astype(BF16)
        h2_ref[r, :D_MODEL] = h2b
        h2l = (h2 - h2b.astype(F32)).astype(BF16)
        wrh = wrh_ref[...]
        return _bdot(h2b, wrh) + (_bdot(h2l, wrh) + _bdot(h2b, wrl_ref[...]))

    def route(s, logits):
        r = rows[s]
        lane = lax.broadcasted_iota(I32, logits.shape, 1)
        real = lane < N_EXPERTS
        m = jnp.max(jnp.where(real, logits, -jnp.inf), axis=-1, keepdims=True)
        e = jnp.exp(logits - m)
        aff = e / jnp.sum(jnp.where(real, e, 0.0), axis=-1, keepdims=True)
        aff_t = aff.T
        for blk in range(BACK_SUB // LANES):
            afft_ref[s * (BACK_SUB // LANES) + blk] = aff_t[:N_EXPERTS, blk * LANES:(blk + 1) * LANES]
        hi = aff.astype(BF16)
        r1 = aff - hi.astype(F32)
        mid = r1.astype(BF16)
        lo = (r1 - mid.astype(F32)).astype(BF16)
        zero = jnp.zeros_like(hi)
        h2_ref[r, D_MODEL:] = jnp.where(
            real, hi, jnp.where(lane < 2 * N_EXPERTS, mid, jnp.where(lane < 3 * N_EXPERTS, lo, zero)))

    mixes = {0: project(rows[0])}
    logits = {}
    for s in range(n_sub):
        if s + 1 < n_sub:
            mixes[s + 1] = project(rows[s + 1])
        logits[s] = normalise(rows[s], mixes.pop(s))
        if s > 0:
            route(s - 1, logits.pop(s - 1))
    route(n_sub - 1, logits.pop(n_sub - 1))


def _back(x2d, attn, cv, mod3, mod_row, wts, seq):
    n = x2d.shape[0]
    tile = BACK_TILE
    full = lambda a: pl.BlockSpec(a.shape, lambda i: (0,) * a.ndim)

    def mod_spec(col):
        if mod_row is None:
            assert seq % tile == 0
            return pl.BlockSpec((1, 1, D_MODEL), lambda i: (i // (seq // tile), 0, col))
        return pl.BlockSpec((1, 1, D_MODEL), lambda i: (mod_row, 0, col))

    row = lambda w: pl.BlockSpec((tile, w), lambda i: (i, 0))
    woa, woc, fg, wrh, wrl = wts
    in_specs = [row(D_MODEL), row(MLA_HEADS * V_DIM), row(CONV_CH), full(woa), full(woc),
                mod_spec(2), mod_spec(3), mod_spec(4), full(fg), full(wrh), full(wrl)]
    return pl.pallas_call(
        _back_kernel,
        out_shape=[jax.ShapeDtypeStruct((n, D_MODEL), F32),
                   jax.ShapeDtypeStruct((n, XS_W), BF16),
                   jax.ShapeDtypeStruct((n // LANES, N_EXPERTS, LANES), F32)],
        grid=(n // tile,),
        in_specs=in_specs,
        out_specs=[row(D_MODEL), row(XS_W),
                   pl.BlockSpec((tile // LANES, N_EXPERTS, LANES), lambda i: (i, 0, 0))],
        compiler_params=_params(1, VMEM_LIMIT),
        name="back",
    )(x2d, attn, cv, woa, woc, mod3, mod3, mod3, fg, wrh, wrl)


def _route_kernel(caps, *refs):
    n_g = len(caps)
    affs = [r[...] for r in refs[:n_g]]
    pos_refs = refs[n_g:2 * n_g]
    base_refs = refs[2 * n_g:]
    ones = jnp.ones((LANES, LANES), BF16)
    l_r = lax.broadcasted_iota(I32, (LANES, LANES), 0)
    l_c = lax.broadcasted_iota(I32, (LANES, LANES), 1)
    tri = jnp.where(l_r <= l_c, 1.0, 0.0).astype(BF16)

    def lane_dot(mask3, w):
        r, e, _ = mask3.shape
        mb = jnp.where(mask3, 1.0, 0.0).astype(BF16).reshape(r * e, LANES)
        return _bdot(mb, w).reshape(r, e, LANES)

    def expert_count(mask3):
        return jnp.sum(lane_dot(mask3, ones), axis=0, keepdims=True)

    def excl_cumsum(mask3):
        rows = mask3.shape[0]
        tot = lane_dot(mask3, ones)
        run = tot
        s = 1
        while s < rows:
            run = run + jnp.concatenate([jnp.zeros((s,) + run.shape[1:], F32), run[:rows - s]], axis=0)
            s *= 2
        before = run - tot
        return before + lane_dot(mask3, tri) - jnp.where(mask3, 1.0, 0.0), before

    def step(t, thrs):
        bit = lax.shift_left(jnp.int32(1), 30 - t)
        out = []
        for aff, thr, cap in zip(affs, thrs, caps):
            cand = thr | bit
            cnt = expert_count(aff >= pltpu.bitcast(cand, F32))
            out.append(jnp.where(cnt >= cap, cand, thr))
        return tuple(out)

    init = tuple(jnp.zeros((1,) + a.shape[1:], I32) for a in affs)
    thrs = lax.fori_loop(0, 31, step, init)
    for aff, thr_bits, cap, pos_ref, base_ref in zip(affs, thrs, caps, pos_refs, base_refs):
        thr = pltpu.bitcast(thr_bits, F32)
        gt = aff > thr
        eq = aff == thr
        need = cap - expert_count(gt)
        tie_rank, _ = excl_cumsum(eq)
        sel = gt | (eq & (tie_rank < need))
        pos, before = excl_cumsum(sel)
        pos_ref[...] = jnp.where(sel, pos, -1.0).astype(I32)
        base_ref[...] = before


def _route(affs, caps):
    shapes = [a.shape for a in affs]
    return pl.pallas_call(
        functools.partial(_route_kernel, tuple(caps)),
        out_shape=[jax.ShapeDtypeStruct(s, I32) for s in shapes] +
                  [jax.ShapeDtypeStruct(s, F32) for s in shapes],
        compiler_params=pltpu.CompilerParams(vmem_limit_bytes=VMEM_LIMIT),
        name="route",
    )(*affs)


def _window(off_ref, e, i, n_tiles, w, cap):
    del n_tiles
    first = off_ref[i * N_EXPERTS + e] + w * SLOT_WIN
    return first, pl.multiple_of(jnp.minimum(first, cap - SLOT_WIN), BF16_ROWS)


def _one_hot_rows(pos_ref, wins, row0=0):
    r = lax.broadcasted_iota(I32, (SLOT_WIN, LANES), 0)
    blocks = []
    for e, (first, off) in enumerate(wins):
        halves = []
        for h in range(ROW_TILE // LANES):
            pe = pos_ref[row0 + h, e:e + 1, :]
            rel = jnp.where(pe >= first, pe - off, -1)
            halves.append(jnp.where(rel == r, 1.0, 0.0).astype(BF16))
        blocks.append(jnp.concatenate(halves, axis=1))
    return jnp.concatenate(blocks, axis=0)


DISPATCH_TILES = 2
FLUSH_PARTS = 4


def _dispatch_kernel(cap, n_steps, off_ref, nwin_ref, flush_ref, h2_ref, pos_ref, xs_hbm,
                     xs_ref, sems):
    i = pl.program_id(0)
    per = h2_ref.shape[1] // DISPATCH_TILES

    @pl.when(i == 0)
    def _():
        def zero(e, carry):
            xs_ref[e] = jnp.zeros(xs_ref.shape[1:], BF16)
            return carry
        lax.fori_loop(0, N_EXPERTS, zero, 0)

    def gather(t, w):
        tile = i * DISPATCH_TILES + t
        wins = [_window(off_ref, e, tile, None, w, cap) for e in range(N_EXPERTS)]
        p = _one_hot_rows(pos_ref, wins, t * (ROW_TILE // LANES))
        h2 = h2_ref[:, t * per:(t + 1) * per, :].reshape(ROW_TILE, XS_W)
        return wins, _bdot(p, h2)

    def merge(wins, slab):
        for e, (_, off) in enumerate(wins):
            cur = xs_ref[e, pl.ds(off, SLOT_WIN), :].astype(F32)
            xs_ref[e, pl.ds(off, SLOT_WIN), :] = (
                cur + slab[e * SLOT_WIN:(e + 1) * SLOT_WIN]).astype(BF16)

    for wins, slab in [gather(t, 0) for t in range(DISPATCH_TILES)]:
        merge(wins, slab)
    for t in range(DISPATCH_TILES):
        def extra(w, carry, t=t):
            merge(*gather(t, w))
            return carry
        lax.fori_loop(1, nwin_ref[i * DISPATCH_TILES + t], extra, 0)

    part = cap // FLUSH_PARTS

    def flush(q):
        rows = pl.ds(q * part, part)
        return pltpu.make_async_copy(xs_ref.at[:, rows, :], xs_hbm.at[:, rows, :], sems.at[q])

    for q in range(FLUSH_PARTS):
        @pl.when(i == flush_ref[q])
        def _(q=q):
            flush(q).start()

    @pl.when(i == n_steps - 1)
    def _():
        for q in range(FLUSH_PARTS):
            flush(q).wait()


def _tile_spec(stripes, width, tiles=1):
    return pl.BlockSpec((stripes, tiles * ROW_TILE // stripes, width), lambda i, *_: (0, i, 0))


def _dispatch(h2aug, pos, offs, nwin, cap, stripes):
    n = h2aug.shape[0]
    n_steps = n // (ROW_TILE * DISPATCH_TILES)
    low = jnp.min(offs.reshape(-1, N_EXPERTS), axis=1)[DISPATCH_TILES::DISPATCH_TILES]
    low = jnp.concatenate([low, jnp.full((1,), cap, I32)])
    bounds = (jnp.arange(FLUSH_PARTS, dtype=I32) + 1) * (cap // FLUSH_PARTS)
    flush = jnp.argmax(low[None, :] >= bounds[:, None], axis=1).astype(I32)
    return pl.pallas_call(
        functools.partial(_dispatch_kernel, cap, n_steps),
        out_shape=jax.ShapeDtypeStruct((N_EXPERTS, cap, XS_W), BF16),
        grid_spec=pltpu.PrefetchScalarGridSpec(
            num_scalar_prefetch=3,
            grid=(n_steps,),
            in_specs=[_tile_spec(stripes, XS_W, DISPATCH_TILES),
                      pl.BlockSpec((DISPATCH_TILES * ROW_TILE // LANES, N_EXPERTS, LANES),
                                   lambda i, *_: (i, 0, 0))],
            out_specs=pl.BlockSpec(memory_space=pl.ANY),
            scratch_shapes=[pltpu.VMEM((N_EXPERTS, cap, XS_W), BF16),
                            pltpu.SemaphoreType.DMA((FLUSH_PARTS,))]),
        compiler_params=_params(1, VMEM_LIMIT),
        name="dispatch",
    )(offs, nwin, flush, h2aug.reshape(stripes, n // stripes, XS_W), pos)


FF_CHUNK = 512
EXP_ROWS = 512


def _experts_kernel(n_groups, n_chunks, *refs):
    xs_refs = refs[:n_groups]
    wg_ref, wu_ref, wd_ref = refs[n_groups:n_groups + 3]
    ys_refs = refs[n_groups + 3:2 * n_groups + 3]
    acc_ref = refs[2 * n_groups + 3]
    e = pl.program_id(0)
    cap = xs_refs[0].shape[1]
    blocks = [(g, slice(r * EXP_ROWS, (r + 1) * EXP_ROWS))
              for g in range(n_groups) for r in range(cap // EXP_ROWS)]

    def chunk(first, last):
        wg = wg_ref[0].astype(BF16)
        wu = wu_ref[0].astype(BF16)
        wd = wd_ref[0].astype(BF16)

        def up(blk):
            g, rows = blk
            x = xs_refs[g][0, rows, :D_MODEL]
            return _bdot(x, wg), _bdot(x, wu)

        nxt = up(blocks[0])
        for k, (g, rows) in enumerate(blocks):
            a, u = nxt
            if k + 1 < len(blocks):
                nxt = up(blocks[k + 1])
            hm = (a * jax.nn.sigmoid(a) * u).astype(BF16)
            y = _bdot(hm, wd)
            if not first:
                y = acc_ref[g, rows, :] + y
            if last:
                aug = xs_refs[g][0, rows, D_MODEL:].astype(F32)
                lane = lax.broadcasted_iota(I32, aug.shape, 1)
                mine = (lane == e) | (lane == e + N_EXPERTS) | (lane == e + 2 * N_EXPERTS)
                gate = jnp.sum(jnp.where(mine, aug, 0.0), axis=-1, keepdims=True)
                ys_refs[g][0, rows, :] = (y * gate).astype(BF16)
            else:
                acc_ref[g, rows, :] = y

    for c in range(n_chunks):
        pl.when(pl.program_id(1) == c)(functools.partial(chunk, c == 0, c == n_chunks - 1))


def _experts(xs_list, w_gate, w_up, w_down):
    n_groups = len(xs_list)
    cap = xs_list[0].shape[1]
    ff = w_gate.shape[2]
    n_chunks = ff // FF_CHUNK
    xs_spec = pl.BlockSpec((1, cap, XS_W), lambda e, j: (e, 0, 0))
    up_spec = pl.BlockSpec((1, D_MODEL, FF_CHUNK), lambda e, j: (e, 0, j))
    down_spec = pl.BlockSpec((1, FF_CHUNK, D_MODEL), lambda e, j: (e, j, 0))
    ys_spec = pl.BlockSpec((1, cap, D_MODEL), lambda e, j: (e, 0, 0))
    return pl.pallas_call(
        functools.partial(_experts_kernel, n_groups, n_chunks),
        out_shape=[jax.ShapeDtypeStruct((N_EXPERTS, cap, D_MODEL), BF16)] * n_groups,
        grid=(N_EXPERTS, n_chunks),
        in_specs=[xs_spec] * n_groups + [up_spec, up_spec, down_spec],
        out_specs=[ys_spec] * n_groups,
        scratch_shapes=[pltpu.VMEM((n_groups, cap, D_MODEL), F32)],
        compiler_params=_params(2, VMEM_LIMIT),
        name="experts",
    )(*xs_list, w_gate, w_up, w_down)


COMBINE_TILES = 4


def _combine_kernel(cap, n_steps, off_ref, nwin_ref, ys_hbm, pos_ref, x1_ref, g2_ref, fg_ref,
                    y_ref, buf_ref, sems):
    i = pl.program_id(0)
    slot = i % 2
    per = x1_ref.shape[1] // COMBINE_TILES

    def copies(step, t, w, s):
        out = []
        for e in range(N_EXPERTS):
            _, off = _window(off_ref, e, step * COMBINE_TILES + t, None, w, cap)
            out.append(pltpu.make_async_copy(
                ys_hbm.at[e, pl.ds(off, SLOT_WIN), :],
                buf_ref.at[s, t, pl.ds(e * SLOT_WIN, SLOT_WIN), :],
                sems.at[s, t]))
        return out

    @pl.when(i == 0)
    def _():
        for t in range(COMBINE_TILES):
            for cp in copies(0, t, 0, 0):
                cp.start()

    @pl.when(i + 1 < n_steps)
    def _():
        for t in range(COMBINE_TILES):
            for cp in copies(i + 1, t, 0, 1 - slot):
                cp.start()

    tn = (((0,), (0,)), ((), ()))

    def one_hot(t, w):
        wins = [_window(off_ref, e, i * COMBINE_TILES + t, None, w, cap) for e in range(N_EXPERTS)]
        return _one_hot_rows(pos_ref, wins, t * (ROW_TILE // LANES))

    def scattered(t, w, p):
        for cp in copies(i, t, w, slot):
            cp.wait()
        return lax.dot_general(p, buf_ref[slot, t], tn, preferred_element_type=F32)

    ps = [one_hot(t, 0) for t in range(COMBINE_TILES)]
    moes = [scattered(t, 0, ps[t]) for t in range(COMBINE_TILES)]
    for t in range(COMBINE_TILES):
        def extra_window(w, acc, t=t):
            for cp in copies(i, t, w, slot):
                cp.start()
            return acc + scattered(t, w, one_hot(t, w))
        moe = lax.fori_loop(1, nwin_ref[i * COMBINE_TILES + t], extra_window, moes[t])
        rows = slice(t * per, (t + 1) * per)
        x1 = x1_ref[:, rows, :]
        out = x1 + g2_ref[...] * moe.reshape(x1.shape)
        y_ref[:, rows, :] = _rms(out, fg_ref[...])


def _combine(ys, pos, offs, nwin, x1, mod3, mod_row, stripes, final_g, cap):
    n = x1.shape[0]
    n_steps = n // (ROW_TILE * COMBINE_TILES)
    g2 = pl.BlockSpec((stripes, 1, D_MODEL), lambda i, *_: (0 if mod_row is None else mod_row, 0, 5))
    y = pl.pallas_call(
        functools.partial(_combine_kernel, cap, n_steps),
        out_shape=jax.ShapeDtypeStruct((stripes, n // stripes, D_MODEL), F32),
        grid_spec=pltpu.PrefetchScalarGridSpec(
            num_scalar_prefetch=2,
            grid=(n_steps,),
            in_specs=[pl.BlockSpec(memory_space=pl.ANY),
                      pl.BlockSpec((COMBINE_TILES * ROW_TILE // LANES, N_EXPERTS, LANES),
                                   lambda i, *_: (i, 0, 0)),
                      _tile_spec(stripes, D_MODEL, COMBINE_TILES),
                      g2,
                      pl.BlockSpec((1, D_MODEL), lambda i, *_: (0, 0))],
            out_specs=_tile_spec(stripes, D_MODEL, COMBINE_TILES),
            scratch_shapes=[pltpu.VMEM((2, COMBINE_TILES, N_EXPERTS * SLOT_WIN, D_MODEL), BF16),
                            pltpu.SemaphoreType.DMA((2, COMBINE_TILES))]),
        compiler_params=_params(1, VMEM_LIMIT),
        name="combine",
    )(offs, nwin, ys, pos, x1.reshape(stripes, n // stripes, D_MODEL), mod3, final_g)
    return y.reshape(n, D_MODEL)


def _head_cols(w3):
    k, _, d = w3.shape
    return jnp.pad(w3, ((0, 0), (0, 0), (0, HEAD_PAD - d))).reshape(k, QK_WIDTH)


def _value_cols(v3):
    k = v3.shape[0]
    pair = v3.reshape(k, MLA_HEADS // 2, 2, V_DIM)
    zeros = jnp.zeros((k, MLA_HEADS // 2, V_DIM), v3.dtype)
    even = jnp.concatenate([pair[:, :, 0], zeros], axis=-1)
    odd = jnp.concatenate([zeros, pair[:, :, 1]], axis=-1)
    return jnp.stack([even, odd], axis=2).reshape(k, QK_WIDTH)


def _rope_tables(seq):
    half = QK_ROPE // 2
    pos = jnp.arange(seq)
    inv_freq = 1.0 / (ROPE_BASE ** (jnp.arange(0, half, 2, dtype=F32) / half))

    def cs(p):
        ang = p.astype(F32)[:, None] * inv_freq[None, :]
        ang = jnp.concatenate([ang, ang], axis=-1)
        return jnp.cos(ang), jnp.sin(ang)

    cr, sr = cs(pos // GRID_W)
    cc, sc = cs(pos % GRID_W)
    ones = jnp.ones((seq, QK_NOPE), F32)
    zpad = jnp.zeros((seq, HEAD_PAD - QK_NOPE - QK_ROPE), F32)
    cq = jnp.concatenate([ones, cr, cc, zpad], axis=-1)
    sq = jnp.concatenate([0.0 * ones, sr, sc, zpad], axis=-1)
    return cq, sq


def _prep_weights(w_in, q_norm_g, w_q_up, kv_norm_g, w_kv_up, attn_norm_g):
    wqc = w_in[:, :Q_RANK + KV_RANK].astype(BF16)
    wkr = jnp.pad(w_in[:, Q_RANK + KV_RANK:Q_RANK + KV_RANK + QK_ROPE],
                  ((0, 0), (0, LANES - QK_ROPE))).astype(BF16)
    wglu = w_in[:, Q_RANK + KV_RANK + QK_ROPE:].astype(BF16)
    wq = _head_cols(w_q_up.astype(BF16).reshape(Q_RANK, MLA_HEADS, QK_NOPE + QK_ROPE))
    kv3 = w_kv_up.astype(BF16).reshape(KV_RANK, MLA_HEADS, QK_NOPE + V_DIM)
    wk = _head_cols(kv3[:, :, :QK_NOPE])
    wv = _value_cols(kv3[:, :, QK_NOPE:])
    pk = np.zeros((LANES, QK_WIDTH), np.float32)
    for hd in range(MLA_HEADS):
        for j in range(QK_ROPE):
            pk[j, hd * HEAD_PAD + QK_NOPE + j] = 1.0
    pk = jnp.asarray(pk, BF16)
    return (attn_norm_g[None, :], wqc, wkr, wglu, q_norm_g[None, :], wq,
            kv_norm_g[None, :], wk, wv, pk)


def _stripe_order(aff3, stripes, seq):
    per = ROW_TILE // stripes
    q = LANES // per
    assert per * stripes == ROW_TILE and per * q == LANES and stripes % q == 0 and seq % LANES == 0
    a = aff3.reshape(stripes // q, q, seq // LANES, N_EXPERTS, q, per)
    return a.transpose(2, 4, 0, 3, 1, 5).reshape(aff3.shape)


def _route_plan(base, cap):
    rows_per_tile = ROW_TILE // LANES
    start = base[::rows_per_tile, :, 0].astype(I32)
    end = jnp.concatenate([start[1:], jnp.full((1, N_EXPERTS), cap, I32)], axis=0)
    off = (start // BF16_ROWS) * BF16_ROWS
    span = end - off
    nwin = jnp.maximum(jnp.max((span + SLOT_WIN - 1) // SLOT_WIN, axis=1), 1).astype(I32)
    return off.reshape(-1), nwin


def kernel(x_prompt, x_sample, cache_ckv, cache_krope, c, c_ctx, w_mod, b_mod, attn_norm_g,
           w_in, q_norm_g, w_q_up, kv_norm_g, w_kv_up, conv_w, conv_b, conv_ln_g, conv_ln_b,
           w_out, ffn_norm_g, w_router, w_gate, w_up, w_down, final_norm_g):
    depth = w_mod.shape[0]
    assert depth == 1
    bp, sp, _ = x_prompt.shape
    bs, ss, _ = x_sample.shape
    past = cache_ckv.shape[2]
    ctx_row = bs

    mod_rows = 16
    cc = jnp.concatenate([c, c_ctx[None, :], jnp.zeros((mod_rows - bs - 1, D_MODEL), F32)], axis=0)
    mod = _adaln(cc, w_mod[0], b_mod[0][None, :])
    mod3 = mod.reshape(mod_rows, 1, N_MOD * D_MODEL)

    fw = _prep_weights(w_in[0], q_norm_g[0], w_q_up[0], kv_norm_g[0], w_kv_up[0], attn_norm_g[0])
    wk, wv, pk = fw[7], fw[8], fw[9]
    wo = w_out[0].astype(BF16)
    wr = jnp.pad(jnp.tile(w_router[0], (1, 3)), ((0, 0), (0, LANES - 3 * N_EXPERTS)))
    wrh = wr.astype(BF16)
    wrl = (wr - wrh.astype(F32)).astype(BF16)
    cvw = (conv_w[0], conv_b[0][None, :], conv_ln_g[0][None, :], conv_ln_b[0][None, :])
    bw = (wo[:MLA_HEADS * V_DIM], wo[MLA_HEADS * V_DIM:], ffn_norm_g[0][None, :], wrh, wrl)

    xp2 = x_prompt.reshape(bp * sp, D_MODEL)
    xs2 = x_sample.reshape(bs * ss, D_MODEL)

    qp, kp, vp, cv_p, ckvp, krp = _front(xp2, mod3, ctx_row, sp, fw, cvw, None)
    attn_p = _attention(qp, [(kp, vp, sp)], bp, sp)
    x1p, h2p, afftp = _back(xp2, attn_p, cv_p, mod3, ctx_row, bw, sp)

    qs, ks, vs, cv_s = _front(xs2, mod3, None, ss, fw, cvw, _rope_tables(ss))
    ctx = (cache_ckv[:, 0].reshape(bs * past, KV_RANK),
           cache_krope[:, 0].reshape(bs * past, QK_ROPE), past, wk, wv, pk)
    attn_s = _attention(qs, [(ks, vs, ss)], bs, ss, ctx)
    x1s, h2s, affts = _back(xs2, attn_s, cv_s, mod3, None, bw, ss)

    stripes = (1, bs)
    caps = [EC_FACTOR * h2.shape[0] // N_EXPERTS for h2 in (h2p, h2s)]
    routed = _route([afftp, _stripe_order(affts, bs, ss)], caps)
    plans = []
    xs_list = []
    for g, (h2, cap) in enumerate(zip((h2p, h2s), caps)):
        pos, base = routed[g], routed[len(caps) + g]
        offs, nwin = _route_plan(base, cap)
        plans.append((pos, offs, nwin, cap))
        xs_list.append(_dispatch(h2, pos, offs, nwin, cap, stripes[g]))
    ys_list = _experts(xs_list, w_gate[0], w_up[0], w_down[0])

    fg = final_norm_g[None, :]
    pos, offs, nwin, cap = plans[0]
    y_prompt = _combine(ys_list[0], pos, offs, nwin, x1p, mod3, ctx_row, stripes[0], fg, cap)
    pos, offs, nwin, cap = plans[1]
    y_sample = _combine(ys_list[1], pos, offs, nwin, x1s, mod3, None, stripes[1], fg, cap)

    return (y_prompt.reshape(bp, sp, D_MODEL), y_sample.reshape(bs, ss, D_MODEL),
            ckvp.reshape(bp, depth, sp, KV_RANK), krp.reshape(bp, depth, sp, QK_ROPE))
```

```python
import functools

import jax
import jax.numpy as jnp
import numpy as np
from jax import lax
from jax.experimental import pallas as pl
from jax.experimental.pallas import tpu as pltpu

F32 = jnp.float32
BF16 = jnp.bfloat16
I32 = jnp.int32

LANES = 128
BF16_ROWS = 16
VMEM_LIMIT = 56 * 1024 * 1024

D_MODEL = 1024
GRID_W = 64
MLA_HEADS = 8
QK_NOPE = 64
QK_ROPE = 32
V_DIM = 64
Q_RANK = 768
KV_RANK = 256
CONV_CH = 512
CONV_WIDTH = 31
N_EXPERTS = 16
EXPERT_FF = 1024
EC_FACTOR = 2
ROPE_BASE = 10000.0
EPS = 1e-6
N_MOD = 6

HEAD_PAD = LANES
QK_WIDTH = MLA_HEADS * HEAD_PAD
ROW_TILE = 256
FRONT_TILE = 512
ATTN_TILE = 512
ATTN_PACK = 1024
BACK_TILE = 1024
BACK_SUB = 256
SLOT_WIN = 64
AUG = LANES
XS_W = D_MODEL + AUG
LOG2_E = 1.4426950408889634


def _params(n_axes, vmem=None):
    return pltpu.CompilerParams(
        dimension_semantics=("arbitrary",) * n_axes,
        vmem_limit_bytes=vmem)


def _rms(x, g):
    ms = jnp.mean(x * x, axis=-1, keepdims=True)
    return x * lax.rsqrt(ms + EPS) * g


def _bdot(a, b):
    return jnp.dot(a, b, preferred_element_type=F32)


def _adaln_kernel(c_ref, w_ref, b_ref, o_ref):
    c = c_ref[...]
    s = c * jax.nn.sigmoid(c)
    w = w_ref[...]
    sh = s.astype(BF16)
    sl = (s - sh.astype(F32)).astype(BF16)
    wh = w.astype(BF16)
    wl = (w - wh.astype(F32)).astype(BF16)
    o_ref[...] = _bdot(sh, wh) + (_bdot(sl, wh) + _bdot(sh, wl)) + b_ref[...]


def _adaln(cc, w_mod, b_mod):
    rows = cc.shape[0]
    n_out = w_mod.shape[1]
    blk = D_MODEL
    return pl.pallas_call(
        _adaln_kernel,
        out_shape=jax.ShapeDtypeStruct((rows, n_out), F32),
        grid=(n_out // blk,),
        in_specs=[pl.BlockSpec((rows, D_MODEL), lambda j: (0, 0)),
                  pl.BlockSpec((D_MODEL, blk), lambda j: (0, j)),
                  pl.BlockSpec((1, blk), lambda j: (0, j))],
        out_specs=pl.BlockSpec((rows, blk), lambda j: (0, j)),
        compiler_params=_params(1),
        name="adaln",
    )(cc, w_mod, b_mod)


def _rot_half(x, lane, base):
    rel = lane - base
    first = ((rel >= 0) & (rel < 8)) | ((rel >= 16) & (rel < 24))
    return jnp.where(first, -pltpu.roll(x, LANES - 8, axis=1), pltpu.roll(x, 8, axis=1))


def _front_kernel(rope, seq, x_ref, sh_ref, sc_ref, g_ref, wqc_ref, wkr_ref, wglu_ref,
                  qg_ref, wq_ref, kvg_ref, wk_ref, wv_ref, pk_ref, cw_ref, cb_ref, lg_ref, lb_ref,
                  *rest):
    if rope:
        cq_ref, sq_ref, q_ref, k_ref, v_ref, cv_ref, vpad_ref, shift_ref = rest
    else:
        q_ref, k_ref, v_ref, cv_ref, ckv_ref, kr_ref, vpad_ref, shift_ref = rest
    rows = x_ref.shape[0]
    steps_per_seq = max(seq // rows, 1)
    scale = (QK_NOPE + QK_ROPE) ** -0.5 * LOG2_E
    lane = lax.broadcasted_iota(I32, (ROW_TILE, LANES), 1)
    conv_refs = (cw_ref, cb_ref, lg_ref, lb_ref, shift_ref)

    pending = []

    def tick(n=1):
        for _ in range(n):
            if pending:
                pending.pop(0)()

    def gated_values(qd, row_in_seq, slot, cv_row):
        r = slice(qd * ROW_TILE, (qd + 1) * ROW_TILE)
        h = _rms(x_ref[r, :], g_ref[...]) * (1.0 + sc_ref[0]) + sh_ref[0]
        hb = h.astype(BF16)
        glu = _bdot(hb, wglu_ref[...])
        if row_in_seq == 0:
            vpad_ref[slot, 0:CONV_HALO, :] = jnp.zeros((CONV_HALO, CONV_CH), F32)
        if row_in_seq + ROW_TILE == seq:
            vpad_ref[slot, CONV_HALO + seq:, :] = jnp.zeros((CONV_HALO, CONV_CH), F32)
        vpad_ref[slot, CONV_HALO + row_in_seq:CONV_HALO + row_in_seq + ROW_TILE, :] = (
            glu[:, :CONV_CH] * jax.nn.sigmoid(glu[:, CONV_CH:]))

        ready = ([row_in_seq - ROW_TILE] if row_in_seq > 0 else []) + \
                ([row_in_seq] if row_in_seq + ROW_TILE == seq else [])
        for t0 in ready:
            steps, finish = _conv_stages(CONV_STAGES, vpad_ref, slot, t0, *conv_refs)
            dst = slice(cv_row + t0, cv_row + t0 + ROW_TILE)
            pending.extend(steps + [functools.partial(_store_conv, cv_ref, dst, finish)])
        return hb

    def projections(qd, row_in_seq, hb, ticks):
        r = slice(qd * ROW_TILE, (qd + 1) * ROW_TILE)
        qc = _bdot(hb, wqc_ref[...])
        tick(ticks)
        kr = _bdot(hb, wkr_ref[...])
        tick(ticks)
        qn = _rms(qc[:, :Q_RANK], qg_ref[...]).astype(BF16)
        q = _bdot(qn, wq_ref[...])
        tick(ticks)
        ckv = _rms(qc[:, Q_RANK:], kvg_ref[...])
        ckv_b = ckv.astype(BF16)
        kk = _bdot(ckv_b, wk_ref[...])
        tick(ticks)
        v_ref[r, :] = _bdot(ckv_b, wv_ref[...]).astype(BF16)
        tick(ticks)
        if rope:
            t = slice(row_in_seq % rows, row_in_seq % rows + ROW_TILE)
            cq = cq_ref[t, :]
            sq = sq_ref[t, :]
            for hd in range(MLA_HEADS):
                blk = q[:, hd * HEAD_PAD:(hd + 1) * HEAD_PAD]
                rot = blk * cq + _rot_half(blk, lane, QK_NOPE) * sq
                q_ref[r, hd * HEAD_PAD:(hd + 1) * HEAD_PAD] = (rot * scale).astype(BF16)
            krs = pltpu.roll(kr, QK_NOPE, axis=1)
            kr = pltpu.roll(krs * cq + _rot_half(krs, lane, QK_NOPE) * sq, LANES - QK_NOPE, axis=1)
            kr = jnp.where(lane < QK_ROPE, kr, 0.0)
        else:
            q_ref[r, :] = (q * scale).astype(BF16)
            ckv_ref[r, :] = ckv
            kr_ref[r, :] = kr[:, :QK_ROPE]
        tick(ticks)
        k_ref[r, :] = (kk + _bdot(kr.astype(BF16), pk_ref[...])).astype(BF16)

    def step(k):
        n_q = rows // ROW_TILE
        hbs = []
        for qd in range(n_q):
            row = k * rows + qd * ROW_TILE
            if seq >= rows:
                hbs.append((row, gated_values(qd, row, 0, 0)))
            else:
                hbs.append((row % seq, gated_values(qd, row % seq, row // seq, (row // seq) * seq)))
        ticks = -(-len(pending) // (6 * n_q))
        for qd, (row_in_seq, hb) in enumerate(hbs):
            projections(qd, row_in_seq, hb, ticks)
        tick(len(pending))

    if steps_per_seq == 1:
        step(0)
    else:
        for k in range(steps_per_seq):
            pl.when(pl.program_id(0) % steps_per_seq == k)(functools.partial(step, k))


def _store_conv(cv_ref, dst, finish):
    cv_ref[dst, :] = finish()


def _front(x2d, mod3, mod_row, seq, wts, conv_wts, rope_tabs):
    n = x2d.shape[0]
    rope = rope_tabs is not None
    tile = FRONT_TILE
    steps_per_seq = max(seq // tile, 1)
    assert seq % ROW_TILE == 0 and (tile % seq == 0 or seq % tile == 0)
    full = lambda a: pl.BlockSpec(a.shape, lambda i: (0,) * a.ndim)

    def mod_spec(col):
        if mod_row is None:
            assert seq % tile == 0
            return pl.BlockSpec((1, 1, D_MODEL), lambda i: (i // steps_per_seq, 0, col))
        return pl.BlockSpec((1, 1, D_MODEL), lambda i: (mod_row, 0, col))

    row = lambda w: pl.BlockSpec((tile, w), lambda i: (i, 0))
    cv_rows = max(seq, tile)
    cv_spec = pl.BlockSpec((cv_rows, CONV_CH), lambda i: (i // steps_per_seq, 0))
    in_specs = [row(D_MODEL), mod_spec(0), mod_spec(1)] + [full(w) for w in wts + conv_wts]
    args = [x2d, mod3, mod3] + list(wts + conv_wts)
    out_shape = [jax.ShapeDtypeStruct((n, QK_WIDTH), BF16)] * 3 + \
                [jax.ShapeDtypeStruct((n, CONV_CH), BF16)]
    out_specs = [row(QK_WIDTH)] * 3 + [cv_spec]
    if rope:
        tab = pl.BlockSpec((tile, LANES), lambda i: (i % steps_per_seq, 0))
        in_specs += [tab, tab]
        args += list(rope_tabs)
    else:
        out_shape += [jax.ShapeDtypeStruct((n, KV_RANK), F32),
                      jax.ShapeDtypeStruct((n, QK_ROPE), F32)]
        out_specs += [row(KV_RANK), row(QK_ROPE)]
    slots = max(tile // seq, 1)
    return pl.pallas_call(
        functools.partial(_front_kernel, rope, seq),
        out_shape=out_shape,
        grid=(n // tile,),
        in_specs=in_specs,
        out_specs=out_specs,
        scratch_shapes=[pltpu.VMEM((slots, seq + 2 * CONV_HALO, CONV_CH), F32),
                        pltpu.VMEM((SUBLANES - 1, SHIFT_ROWS, CONV_CH), F32)],
        compiler_params=_params(1, VMEM_LIMIT),
        name="front_rope" if rope else "front",
    )(*args)


CONV_PAD = (CONV_WIDTH - 1) // 2
CONV_HALO = 16
SUBLANES = 8
SHIFT_ROWS = ROW_TILE + 2 * CONV_HALO - SUBLANES


CONV_STAGES = 6


def _conv_stages(n_stages, vpad_ref, slot, t0, cw_ref, cb_ref, lg_ref, lb_ref, shift_ref):
    state = {}

    def taps(lo, hi):
        if lo == 0:
            win = vpad_ref[slot, t0:t0 + ROW_TILE + 2 * CONV_HALO, :]
            for ph in range(1, SUBLANES):
                shift_ref[ph - 1] = win[ph:ph + SHIFT_ROWS, :]
            state["acc"] = jnp.broadcast_to(cb_ref[...], (ROW_TILE, CONV_CH))
        acc = state["acc"]
        for t in range(lo, hi):
            blk, ph = divmod(t + CONV_HALO - CONV_PAD, SUBLANES)
            if ph == 0:
                tap = vpad_ref[slot, t0 + blk * SUBLANES:t0 + blk * SUBLANES + ROW_TILE, :]
            else:
                tap = shift_ref[ph - 1, blk * SUBLANES:blk * SUBLANES + ROW_TILE, :]
            acc = acc + tap * cw_ref[t:t + 1, :]
        state["acc"] = acc

    def finish():
        acc = state["acc"]
        mu = jnp.mean(acc, axis=-1, keepdims=True)
        cen = acc - mu
        var = jnp.mean(cen * cen, axis=-1, keepdims=True)
        ln = cen * lax.rsqrt(var + EPS) * lg_ref[...] + lb_ref[...]
        return (ln * jax.nn.sigmoid(ln)).astype(BF16)

    bounds = [CONV_WIDTH * s // n_stages for s in range(n_stages + 1)]
    stages = [functools.partial(taps, bounds[s], bounds[s + 1]) for s in range(n_stages)]
    return stages, finish


def _attn_kernel(n_kv, n_seq, with_ctx, q_ref, *rest):
    kv_refs = list(rest[:2 * n_kv])
    if with_ctx:
        ckv_ref, kr_ref, wk_ref, wv_ref, pk_ref, o_ref, kc_ref, vc_ref = rest[2 * n_kv:]

        @pl.when(pl.program_id(1) == 0)
        def _():
            cb = ckv_ref[...].astype(BF16)
            kc_ref[...] = (_bdot(cb, wk_ref[...]) +
                           _bdot(kr_ref[...].astype(BF16), pk_ref[...])).astype(BF16)
            vc_ref[...] = _bdot(cb, wv_ref[...]).astype(BF16)

        kv_refs += [kc_ref, vc_ref]
        n_kv += 1
    else:
        o_ref = rest[2 * n_kv]
    nt = (((1,), (1,)), ((), ()))
    tq = q_ref.shape[0] // n_seq
    work = [(b, hd) for b in range(n_seq) for hd in range(MLA_HEADS)]

    def kv_rows(ref, b):
        t = ref.shape[0] // n_seq
        return slice(b * t, (b + 1) * t)

    def scores(item):
        b, hd = item
        sl = slice(hd * HEAD_PAD, (hd + 1) * HEAD_PAD)
        qh = q_ref[b * tq:(b + 1) * tq, sl]
        return [lax.dot_general(qh, kv_refs[2 * j][kv_rows(kv_refs[2 * j], b), sl], nt,
                                preferred_element_type=F32) for j in range(n_kv)]

    outs = {}
    nxt = scores(work[0])
    for w, (b, hd) in enumerate(work):
        sl = slice(hd * HEAD_PAD, (hd + 1) * HEAD_PAD)
        ss = nxt
        if w + 1 < len(work):
            nxt = scores(work[w + 1])
        m = ss[0].max(axis=-1, keepdims=True)
        for s in ss[1:]:
            m = jnp.maximum(m, s.max(axis=-1, keepdims=True))
        ps = [jnp.exp2(s - m) for s in ss]
        l = ps[0].sum(axis=-1, keepdims=True)
        for p in ps[1:]:
            l = l + p.sum(axis=-1, keepdims=True)
        o = None
        for j in range(n_kv):
            v_ref = kv_refs[2 * j + 1]
            pv = _bdot(ps[j].astype(BF16), v_ref[kv_rows(v_ref, b), sl])
            o = pv if o is None else o + pv
        outs[hd] = o / l
        if hd % 2 == 1:
            o_ref[b * tq:(b + 1) * tq, (hd // 2) * LANES:(hd // 2 + 1) * LANES] = (
                outs[hd - 1] + outs[hd]).astype(BF16)


def _attention(q, kvs, batch, seq, ctx=None):
    n = q.shape[0]
    n_seq = max(ATTN_PACK // seq, 1) if seq <= ATTN_TILE else 1
    tile = min(seq, ATTN_TILE) * n_seq
    q_tiles = max(seq // tile, 1)
    assert batch % n_seq == 0
    full = lambda a: pl.BlockSpec(a.shape, lambda b, i: (0,) * a.ndim)
    row = lambda w: pl.BlockSpec((tile, w), lambda b, i: (b * q_tiles + i, 0))
    in_specs = [row(QK_WIDTH)]
    args = [q]
    for k, v, t in kvs:
        spec = pl.BlockSpec((n_seq * t, QK_WIDTH), lambda b, i: (b, 0))
        in_specs += [spec, spec]
        args += [k, v]
    scratch = []
    if ctx is not None:
        assert n_seq == 1
        ckv, kr, past, wk, wv, pk = ctx
        pk = pk[:QK_ROPE]
        in_specs += [pl.BlockSpec((past, KV_RANK), lambda b, i: (b, 0)),
                     pl.BlockSpec((past, QK_ROPE), lambda b, i: (b, 0)), full(wk), full(wv), full(pk)]
        args += [ckv, kr, wk, wv, pk]
        scratch = [pltpu.VMEM((past, QK_WIDTH), BF16), pltpu.VMEM((past, QK_WIDTH), BF16)]
    return pl.pallas_call(
        functools.partial(_attn_kernel, len(kvs), n_seq, ctx is not None),
        out_shape=jax.ShapeDtypeStruct((n, MLA_HEADS * V_DIM), BF16),
        grid=(batch // n_seq, q_tiles),
        in_specs=in_specs,
        out_specs=row(MLA_HEADS * V_DIM),
        scratch_shapes=scratch,
        compiler_params=_params(2, VMEM_LIMIT),
        name="attention",
    )(*args)


def _back_kernel(x_ref, attn_ref, cv_ref, woa_ref, woc_ref, g1_ref, sh_ref, sc_ref, fg_ref,
                 wrh_ref, wrl_ref, x1_ref, h2_ref, afft_ref):
    n_sub = x_ref.shape[0] // BACK_SUB
    rows = [slice(s * BACK_SUB, (s + 1) * BACK_SUB) for s in range(n_sub)]

    def project(r):
        return _bdot(attn_ref[r, :], woa_ref[...]) + _bdot(cv_ref[r, :], woc_ref[...])

    def normalise(r, mix):
        x1 = x_ref[r, :] + g1_ref[0] * mix
        x1_ref[r, :] = x1
        h2 = _rms(x1, fg_ref[...]) * (1.0 + sc_ref[0]) + sh_ref[0]
        h2b = h2.astype(BF16)
        h2_ref[r, :D_MODEL] = h2b
        h2l = (h2 - h2b.astype(F32)).astype(BF16)
        wrh = wrh_ref[...]
        return _bdot(h2b, wrh) + (_bdot(h2l, wrh) + _bdot(h2b, wrl_ref[...]))

    def route(s, logits):
        r = rows[s]
        lane = lax.broadcasted_iota(I32, logits.shape, 1)
        real = lane < N_EXPERTS
        m = jnp.max(jnp.where(real, logits, -jnp.inf), axis=-1, keepdims=True)
        e = jnp.exp(logits - m)
        aff = e / jnp.sum(jnp.where(real, e, 0.0), axis=-1, keepdims=True)
        aff_t = aff.T
        for blk in range(BACK_SUB // LANES):
            afft_ref[s * (BACK_SUB // LANES) + blk] = aff_t[:N_EXPERTS, blk * LANES:(blk + 1) * LANES]
        hi = aff.astype(BF16)
        r1 = aff - hi.astype(F32)
        mid = r1.astype(BF16)
        lo = (r1 - mid.astype(F32)).astype(BF16)
        zero = jnp.zeros_like(hi)
        h2_ref[r, D_MODEL:] = jnp.where(
            real, hi, jnp.where(lane < 2 * N_EXPERTS, mid, jnp.where(lane < 3 * N_EXPERTS, lo, zero)))

    mixes = {0: project(rows[0])}
    logits = {}
    for s in range(n_sub):
        if s + 1 < n_sub:
            mixes[s + 1] = project(rows[s + 1])
        logits[s] = normalise(rows[s], mixes.pop(s))
        if s > 0:
            route(s - 1, logits.pop(s - 1))
    route(n_sub - 1, logits.pop(n_sub - 1))


def _back(x2d, attn, cv, mod3, mod_row, wts, seq):
    n = x2d.shape[0]
    tile = BACK_TILE
    full = lambda a: pl.BlockSpec(a.shape, lambda i: (0,) * a.ndim)

    def mod_spec(col):
        if mod_row is None:
            assert seq % tile == 0
            return pl.BlockSpec((1, 1, D_MODEL), lambda i: (i // (seq // tile), 0, col))
        return pl.BlockSpec((1, 1, D_MODEL), lambda i: (mod_row, 0, col))

    row = lambda w: pl.BlockSpec((tile, w), lambda i: (i, 0))
    wo, fg, wrh, wrl = wts
    assert MLA_HEADS * V_DIM == CONV_CH == wo.shape[0] // 2
    half = lambda k: pl.BlockSpec((wo.shape[0] // 2, wo.shape[1]), lambda i: (k, 0))
    woa = woc = wo
    in_specs = [row(D_MODEL), row(MLA_HEADS * V_DIM), row(CONV_CH), half(0), half(1),
                mod_spec(2), mod_spec(3), mod_spec(4), full(fg), full(wrh), full(wrl)]
    return pl.pallas_call(
        _back_kernel,
        out_shape=[jax.ShapeDtypeStruct((n, D_MODEL), F32),
                   jax.ShapeDtypeStruct((n, XS_W), BF16),
                   jax.ShapeDtypeStruct((n // LANES, N_EXPERTS, LANES), F32)],
        grid=(n // tile,),
        in_specs=in_specs,
        out_specs=[row(D_MODEL), row(XS_W),
                   pl.BlockSpec((tile // LANES, N_EXPERTS, LANES), lambda i: (i, 0, 0))],
        compiler_params=_params(1, VMEM_LIMIT),
        name="back",
    )(x2d, attn, cv, woa, woc, mod3, mod3, mod3, fg, wrh, wrl)


def _route_kernel(caps, *refs):
    n_g = len(caps)
    affs = [r[...] for r in refs[:n_g]]
    pos_refs = refs[n_g:2 * n_g]
    base_refs = refs[2 * n_g:]
    ones = jnp.ones((LANES, LANES), BF16)
    l_r = lax.broadcasted_iota(I32, (LANES, LANES), 0)
    l_c = lax.broadcasted_iota(I32, (LANES, LANES), 1)
    tri = jnp.where(l_r <= l_c, 1.0, 0.0).astype(BF16)

    def lane_dot(mask3, w):
        r, e, _ = mask3.shape
        mb = jnp.where(mask3, 1.0, 0.0).astype(BF16).reshape(r * e, LANES)
        return _bdot(mb, w).reshape(r, e, LANES)

    def expert_count(mask3):
        return jnp.sum(lane_dot(mask3, ones), axis=0, keepdims=True)

    def excl_cumsum(mask3):
        rows = mask3.shape[0]
        tot = lane_dot(mask3, ones)
        run = tot
        s = 1
        while s < rows:
            run = run + jnp.concatenate([jnp.zeros((s,) + run.shape[1:], F32), run[:rows - s]], axis=0)
            s *= 2
        before = run - tot
        return before + lane_dot(mask3, tri) - jnp.where(mask3, 1.0, 0.0), before

    def step(t, thrs):
        bit = lax.shift_left(jnp.int32(1), 30 - t)
        out = []
        for aff, thr, cap in zip(affs, thrs, caps):
            cand = thr | bit
            cnt = expert_count(aff >= pltpu.bitcast(cand, F32))
            out.append(jnp.where(cnt >= cap, cand, thr))
        return tuple(out)

    init = tuple(jnp.zeros((1,) + a.shape[1:], I32) for a in affs)
    thrs = lax.fori_loop(0, 31, step, init)
    for aff, thr_bits, cap, pos_ref, base_ref in zip(affs, thrs, caps, pos_refs, base_refs):
        thr = pltpu.bitcast(thr_bits, F32)
        gt = aff > thr
        eq = aff == thr
        need = cap - expert_count(gt)
        tie_rank, _ = excl_cumsum(eq)
        sel = gt | (eq & (tie_rank < need))
        pos, before = excl_cumsum(sel)
        pos_ref[...] = jnp.where(sel, pos, -1.0).astype(I32)
        base_ref[...] = before


def _route(affs, caps):
    shapes = [a.shape for a in affs]
    return pl.pallas_call(
        functools.partial(_route_kernel, tuple(caps)),
        out_shape=[jax.ShapeDtypeStruct(s, I32) for s in shapes] +
                  [jax.ShapeDtypeStruct(s, F32) for s in shapes],
        compiler_params=pltpu.CompilerParams(vmem_limit_bytes=VMEM_LIMIT),
        name="route",
    )(*affs)


def _window(off_ref, e, i, n_tiles, w, cap):
    del n_tiles
    first = off_ref[i * N_EXPERTS + e] + w * SLOT_WIN
    return first, pl.multiple_of(jnp.minimum(first, cap - SLOT_WIN), BF16_ROWS)


def _one_hot_rows(pos_ref, wins, row0=0):
    r = lax.broadcasted_iota(I32, (SLOT_WIN, LANES), 0)
    blocks = []
    for e, (first, off) in enumerate(wins):
        halves = []
        for h in range(ROW_TILE // LANES):
            pe = pos_ref[row0 + h, e:e + 1, :]
            rel = jnp.where(pe >= first, pe - off, -1)
            halves.append(jnp.where(rel == r, 1.0, 0.0).astype(BF16))
        blocks.append(jnp.concatenate(halves, axis=1))
    return jnp.concatenate(blocks, axis=0)


DISPATCH_TILES = 2
FLUSH_PARTS = 4


def _dispatch_kernel(cap, n_steps, off_ref, nwin_ref, flush_ref, h2_ref, pos_ref, xs_hbm,
                     xs_ref, sems):
    i = pl.program_id(0)
    per = h2_ref.shape[1] // DISPATCH_TILES

    @pl.when(i == 0)
    def _():
        def zero(e, carry):
            xs_ref[e] = jnp.zeros(xs_ref.shape[1:], BF16)
            return carry
        lax.fori_loop(0, N_EXPERTS, zero, 0)

    def gather(t, w):
        tile = i * DISPATCH_TILES + t
        wins = [_window(off_ref, e, tile, None, w, cap) for e in range(N_EXPERTS)]
        p = _one_hot_rows(pos_ref, wins, t * (ROW_TILE // LANES))
        h2 = h2_ref[:, t * per:(t + 1) * per, :].reshape(ROW_TILE, XS_W)
        return wins, _bdot(p, h2)

    def merge(wins, slab):
        for e, (_, off) in enumerate(wins):
            cur = xs_ref[e, pl.ds(off, SLOT_WIN), :].astype(F32)
            xs_ref[e, pl.ds(off, SLOT_WIN), :] = (
                cur + slab[e * SLOT_WIN:(e + 1) * SLOT_WIN]).astype(BF16)

    for wins, slab in [gather(t, 0) for t in range(DISPATCH_TILES)]:
        merge(wins, slab)
    for t in range(DISPATCH_TILES):
        def extra(w, carry, t=t):
            merge(*gather(t, w))
            return carry
        lax.fori_loop(1, nwin_ref[i * DISPATCH_TILES + t], extra, 0)

    part = cap // FLUSH_PARTS

    def flush(q):
        rows = pl.ds(q * part, part)
        return pltpu.make_async_copy(xs_ref.at[:, rows, :], xs_hbm.at[:, rows, :], sems.at[q])

    for q in range(FLUSH_PARTS):
        @pl.when(i == flush_ref[q])
        def _(q=q):
            flush(q).start()

    @pl.when(i == n_steps - 1)
    def _():
        for q in range(FLUSH_PARTS):
            flush(q).wait()


def _tile_spec(stripes, width, tiles=1):
    return pl.BlockSpec((stripes, tiles * ROW_TILE // stripes, width), lambda i, *_: (0, i, 0))


def _dispatch(h2aug, pos, offs, nwin, cap, stripes):
    n = h2aug.shape[0]
    n_steps = n // (ROW_TILE * DISPATCH_TILES)
    low = jnp.min(offs.reshape(-1, N_EXPERTS), axis=1)[DISPATCH_TILES::DISPATCH_TILES]
    low = jnp.concatenate([low, jnp.full((1,), cap, I32)])
    bounds = (jnp.arange(FLUSH_PARTS, dtype=I32) + 1) * (cap // FLUSH_PARTS)
    flush = jnp.argmax(low[None, :] >= bounds[:, None], axis=1).astype(I32)
    return pl.pallas_call(
        functools.partial(_dispatch_kernel, cap, n_steps),
        out_shape=jax.ShapeDtypeStruct((N_EXPERTS, cap, XS_W), BF16),
        grid_spec=pltpu.PrefetchScalarGridSpec(
            num_scalar_prefetch=3,
            grid=(n_steps,),
            in_specs=[_tile_spec(stripes, XS_W, DISPATCH_TILES),
                      pl.BlockSpec((DISPATCH_TILES * ROW_TILE // LANES, N_EXPERTS, LANES),
                                   lambda i, *_: (i, 0, 0))],
            out_specs=pl.BlockSpec(memory_space=pl.ANY),
            scratch_shapes=[pltpu.VMEM((N_EXPERTS, cap, XS_W), BF16),
                            pltpu.SemaphoreType.DMA((FLUSH_PARTS,))]),
        compiler_params=_params(1, VMEM_LIMIT),
        name="dispatch",
    )(offs, nwin, flush, h2aug.reshape(stripes, n // stripes, XS_W), pos)


FF_CHUNK = 512
EXP_ROWS = 512


def _experts_kernel(n_groups, n_chunks, *refs):
    xs_refs = refs[:n_groups]
    wg_ref, wu_ref, wd_ref = refs[n_groups:n_groups + 3]
    ys_refs = refs[n_groups + 3:2 * n_groups + 3]
    acc_ref = refs[2 * n_groups + 3]
    e = pl.program_id(0)
    cap = xs_refs[0].shape[1]
    blocks = [(g, slice(r * EXP_ROWS, (r + 1) * EXP_ROWS))
              for g in range(n_groups) for r in range(cap // EXP_ROWS)]

    def chunk(first, last):
        wg = wg_ref[0].astype(BF16)
        wu = wu_ref[0].astype(BF16)
        wd = wd_ref[0].astype(BF16)

        def up(blk):
            g, rows = blk
            x = xs_refs[g][0, rows, :D_MODEL]
            return _bdot(x, wg), _bdot(x, wu)

        nxt = up(blocks[0])
        for k, (g, rows) in enumerate(blocks):
            a, u = nxt
            if k + 1 < len(blocks):
                nxt = up(blocks[k + 1])
            hm = (a * jax.nn.sigmoid(a) * u).astype(BF16)
            y = _bdot(hm, wd)
            if not first:
                y = acc_ref[g, rows, :] + y
            if last:
                aug = xs_refs[g][0, rows, D_MODEL:].astype(F32)
                lane = lax.broadcasted_iota(I32, aug.shape, 1)
                mine = (lane == e) | (lane == e + N_EXPERTS) | (lane == e + 2 * N_EXPERTS)
                gate = jnp.sum(jnp.where(mine, aug, 0.0), axis=-1, keepdims=True)
                ys_refs[g][0, rows, :] = (y * gate).astype(BF16)
            else:
                acc_ref[g, rows, :] = y

    for c in range(n_chunks):
        pl.when(pl.program_id(1) == c)(functools.partial(chunk, c == 0, c == n_chunks - 1))


def _experts(xs_list, w_gate, w_up, w_down):
    n_groups = len(xs_list)
    cap = xs_list[0].shape[1]
    ff = w_gate.shape[2]
    n_chunks = ff // FF_CHUNK
    xs_spec = pl.BlockSpec((1, cap, XS_W), lambda e, j: (e, 0, 0))
    up_spec = pl.BlockSpec((1, D_MODEL, FF_CHUNK), lambda e, j: (e, 0, j))
    down_spec = pl.BlockSpec((1, FF_CHUNK, D_MODEL), lambda e, j: (e, j, 0))
    ys_spec = pl.BlockSpec((1, cap, D_MODEL), lambda e, j: (e, 0, 0))
    return pl.pallas_call(
        functools.partial(_experts_kernel, n_groups, n_chunks),
        out_shape=[jax.ShapeDtypeStruct((N_EXPERTS, cap, D_MODEL), BF16)] * n_groups,
        grid=(N_EXPERTS, n_chunks),
        in_specs=[xs_spec] * n_groups + [up_spec, up_spec, down_spec],
        out_specs=[ys_spec] * n_groups,
        scratch_shapes=[pltpu.VMEM((n_groups, cap, D_MODEL), F32)],
        compiler_params=_params(2, VMEM_LIMIT),
        name="experts",
    )(*xs_list, w_gate, w_up, w_down)


COMBINE_TILES = 4


def _combine_kernel(cap, n_steps, off_ref, nwin_ref, ys_hbm, pos_ref, x1_ref, g2_ref, fg_ref,
                    y_ref, buf_ref, sems):
    i = pl.program_id(0)
    slot = i % 2
    per = x1_ref.shape[1] // COMBINE_TILES

    def copies(step, t, w, s):
        out = []
        for e in range(N_EXPERTS):
            _, off = _window(off_ref, e, step * COMBINE_TILES + t, None, w, cap)
            out.append(pltpu.make_async_copy(
                ys_hbm.at[e, pl.ds(off, SLOT_WIN), :],
                buf_ref.at[s, t, pl.ds(e * SLOT_WIN, SLOT_WIN), :],
                sems.at[s, t]))
        return out

    @pl.when(i == 0)
    def _():
        for t in range(COMBINE_TILES):
            for cp in copies(0, t, 0, 0):
                cp.start()

    @pl.when(i + 1 < n_steps)
    def _():
        for t in range(COMBINE_TILES):
            for cp in copies(i + 1, t, 0, 1 - slot):
                cp.start()

    tn = (((0,), (0,)), ((), ()))

    def one_hot(t, w):
        wins = [_window(off_ref, e, i * COMBINE_TILES + t, None, w, cap) for e in range(N_EXPERTS)]
        return _one_hot_rows(pos_ref, wins, t * (ROW_TILE // LANES))

    def scattered(t, w, p):
        for cp in copies(i, t, w, slot):
            cp.wait()
        return lax.dot_general(p, buf_ref[slot, t], tn, preferred_element_type=F32)

    ps = [one_hot(t, 0) for t in range(COMBINE_TILES)]
    moes = [scattered(t, 0, ps[t]) for t in range(COMBINE_TILES)]
    for t in range(COMBINE_TILES):
        def extra_window(w, acc, t=t):
            for cp in copies(i, t, w, slot):
                cp.start()
            return acc + scattered(t, w, one_hot(t, w))
        moe = lax.fori_loop(1, nwin_ref[i * COMBINE_TILES + t], extra_window, moes[t])
        rows = slice(t * per, (t + 1) * per)
        x1 = x1_ref[:, rows, :]
        out = x1 + g2_ref[...] * moe.reshape(x1.shape)
        y_ref[:, rows, :] = _rms(out, fg_ref[...])


def _combine(ys, pos, offs, nwin, x1, mod3, mod_row, stripes, final_g, cap):
    n = x1.shape[0]
    n_steps = n // (ROW_TILE * COMBINE_TILES)
    g2 = pl.BlockSpec((stripes, 1, D_MODEL), lambda i, *_: (0 if mod_row is None else mod_row, 0, 5))
    y = pl.pallas_call(
        functools.partial(_combine_kernel, cap, n_steps),
        out_shape=jax.ShapeDtypeStruct((stripes, n // stripes, D_MODEL), F32),
        grid_spec=pltpu.PrefetchScalarGridSpec(
            num_scalar_prefetch=2,
            grid=(n_steps,),
            in_specs=[pl.BlockSpec(memory_space=pl.ANY),
                      pl.BlockSpec((COMBINE_TILES * ROW_TILE // LANES, N_EXPERTS, LANES),
                                   lambda i, *_: (i, 0, 0)),
                      _tile_spec(stripes, D_MODEL, COMBINE_TILES),
                      g2,
                      pl.BlockSpec((1, D_MODEL), lambda i, *_: (0, 0))],
            out_specs=_tile_spec(stripes, D_MODEL, COMBINE_TILES),
            scratch_shapes=[pltpu.VMEM((2, COMBINE_TILES, N_EXPERTS * SLOT_WIN, D_MODEL), BF16),
                            pltpu.SemaphoreType.DMA((2, COMBINE_TILES))]),
        compiler_params=_params(1, VMEM_LIMIT),
        name="combine",
    )(offs, nwin, ys, pos, x1.reshape(stripes, n // stripes, D_MODEL), mod3, final_g)
    return y.reshape(n, D_MODEL)


def _head_cols(w3):
    k, _, d = w3.shape
    return jnp.pad(w3, ((0, 0), (0, 0), (0, HEAD_PAD - d))).reshape(k, QK_WIDTH)


def _value_cols(v3):
    k = v3.shape[0]
    pair = v3.reshape(k, MLA_HEADS // 2, 2, V_DIM)
    zeros = jnp.zeros((k, MLA_HEADS // 2, V_DIM), v3.dtype)
    even = jnp.concatenate([pair[:, :, 0], zeros], axis=-1)
    odd = jnp.concatenate([zeros, pair[:, :, 1]], axis=-1)
    return jnp.stack([even, odd], axis=2).reshape(k, QK_WIDTH)


def _rope_tables(seq):
    half = QK_ROPE // 2
    pos = jnp.arange(seq)
    inv_freq = 1.0 / (ROPE_BASE ** (jnp.arange(0, half, 2, dtype=F32) / half))

    def cs(p):
        ang = p.astype(F32)[:, None] * inv_freq[None, :]
        ang = jnp.concatenate([ang, ang], axis=-1)
        return jnp.cos(ang), jnp.sin(ang)

    cr, sr = cs(pos // GRID_W)
    cc, sc = cs(pos % GRID_W)
    ones = jnp.ones((seq, QK_NOPE), F32)
    zpad = jnp.zeros((seq, HEAD_PAD - QK_NOPE - QK_ROPE), F32)
    cq = jnp.concatenate([ones, cr, cc, zpad], axis=-1)
    sq = jnp.concatenate([0.0 * ones, sr, sc, zpad], axis=-1)
    return cq, sq


def _prep_weights(w_in, q_norm_g, w_q_up, kv_norm_g, w_kv_up, attn_norm_g):
    wqc = w_in[:, :Q_RANK + KV_RANK].astype(BF16)
    wkr = jnp.pad(w_in[:, Q_RANK + KV_RANK:Q_RANK + KV_RANK + QK_ROPE],
                  ((0, 0), (0, LANES - QK_ROPE))).astype(BF16)
    wglu = w_in[:, Q_RANK + KV_RANK + QK_ROPE:].astype(BF16)
    wq = _head_cols(w_q_up.astype(BF16).reshape(Q_RANK, MLA_HEADS, QK_NOPE + QK_ROPE))
    kv3 = w_kv_up.astype(BF16).reshape(KV_RANK, MLA_HEADS, QK_NOPE + V_DIM)
    wk = _head_cols(kv3[:, :, :QK_NOPE])
    wv = _value_cols(kv3[:, :, QK_NOPE:])
    pk = np.zeros((LANES, QK_WIDTH), np.float32)
    for hd in range(MLA_HEADS):
        for j in range(QK_ROPE):
            pk[j, hd * HEAD_PAD + QK_NOPE + j] = 1.0
    pk = jnp.asarray(pk, BF16)
    return (attn_norm_g[None, :], wqc, wkr, wglu, q_norm_g[None, :], wq,
            kv_norm_g[None, :], wk, wv, pk)


def _stripe_order(aff3, stripes, seq):
    per = ROW_TILE // stripes
    q = LANES // per
    assert per * stripes == ROW_TILE and per * q == LANES and stripes % q == 0 and seq % LANES == 0
    a = aff3.reshape(stripes // q, q, seq // LANES, N_EXPERTS, q, per)
    return a.transpose(2, 4, 0, 3, 1, 5).reshape(aff3.shape)


def _route_plan(base, cap):
    rows_per_tile = ROW_TILE // LANES
    start = base[::rows_per_tile, :, 0].astype(I32)
    end = jnp.concatenate([start[1:], jnp.full((1, N_EXPERTS), cap, I32)], axis=0)
    off = (start // BF16_ROWS) * BF16_ROWS
    span = end - off
    nwin = jnp.maximum(jnp.max((span + SLOT_WIN - 1) // SLOT_WIN, axis=1), 1).astype(I32)
    return off.reshape(-1), nwin


def kernel(x_prompt, x_sample, cache_ckv, cache_krope, c, c_ctx, w_mod, b_mod, attn_norm_g,
           w_in, q_norm_g, w_q_up, kv_norm_g, w_kv_up, conv_w, conv_b, conv_ln_g, conv_ln_b,
           w_out, ffn_norm_g, w_router, w_gate, w_up, w_down, final_norm_g):
    depth = w_mod.shape[0]
    assert depth == 1
    bp, sp, _ = x_prompt.shape
    bs, ss, _ = x_sample.shape
    past = cache_ckv.shape[2]
    ctx_row = bs

    mod_rows = 16
    cc = jnp.concatenate([c, c_ctx[None, :], jnp.zeros((mod_rows - bs - 1, D_MODEL), F32)], axis=0)
    mod = _adaln(cc, w_mod[0], b_mod[0][None, :])
    mod3 = mod.reshape(mod_rows, 1, N_MOD * D_MODEL)

    fw = _prep_weights(w_in[0], q_norm_g[0], w_q_up[0], kv_norm_g[0], w_kv_up[0], attn_norm_g[0])
    wk, wv, pk = fw[7], fw[8], fw[9]
    wo = w_out[0].astype(BF16)
    wr = jnp.pad(jnp.tile(w_router[0], (1, 3)), ((0, 0), (0, LANES - 3 * N_EXPERTS)))
    wrh = wr.astype(BF16)
    wrl = (wr - wrh.astype(F32)).astype(BF16)
    cvw = (conv_w[0], conv_b[0][None, :], conv_ln_g[0][None, :], conv_ln_b[0][None, :])
    bw = (wo, ffn_norm_g[0][None, :], wrh, wrl)

    xp2 = x_prompt.reshape(bp * sp, D_MODEL)
    xs2 = x_sample.reshape(bs * ss, D_MODEL)

    qp, kp, vp, cv_p, ckvp, krp = _front(xp2, mod3, ctx_row, sp, fw, cvw, None)
    attn_p = _attention(qp, [(kp, vp, sp)], bp, sp)
    x1p, h2p, afftp = _back(xp2, attn_p, cv_p, mod3, ctx_row, bw, sp)

    qs, ks, vs, cv_s = _front(xs2, mod3, None, ss, fw, cvw, _rope_tables(ss))
    ctx = (cache_ckv[:, 0].reshape(bs * past, KV_RANK),
           cache_krope[:, 0].reshape(bs * past, QK_ROPE), past, wk, wv, pk)
    attn_s = _attention(qs, [(ks, vs, ss)], bs, ss, ctx)
    x1s, h2s, affts = _back(xs2, attn_s, cv_s, mod3, None, bw, ss)

    stripes = (1, bs)
    caps = [EC_FACTOR * h2.shape[0] // N_EXPERTS for h2 in (h2p, h2s)]
    routed = _route([afftp, _stripe_order(affts, bs, ss)], caps)
    plans = []
    xs_list = []
    for g, (h2, cap) in enumerate(zip((h2p, h2s), caps)):
        pos, base = routed[g], routed[len(caps) + g]
        offs, nwin = _route_plan(base, cap)
        plans.append((pos, offs, nwin, cap))
        xs_list.append(_dispatch(h2, pos, offs, nwin, cap, stripes[g]))
    ys_list = _experts(xs_list, w_gate[0], w_up[0], w_down[0])

    fg = final_norm_g[None, :]
    pos, offs, nwin, cap = plans[0]
    y_prompt = _combine(ys_list[0], pos, offs, nwin, x1p, mod3, ctx_row, stripes[0], fg, cap)
    pos, offs, nwin, cap = plans[1]
    y_sample = _combine(ys_list[1], pos, offs, nwin, x1s, mod3, None, stripes[1], fg, cap)

    return (y_prompt.reshape(bp, sp, D_MODEL), y_sample.reshape(bs, ss, D_MODEL),
            ckvp.reshape(bp, depth, sp, KV_RANK), krp.reshape(bp, depth, sp, QK_ROPE))
```

```python
import functools

import jax
import jax.numpy as jnp
import numpy as np
from jax import lax
from jax.experimental import pallas as pl
from jax.experimental.pallas import tpu as pltpu

F32 = jnp.float32
BF16 = jnp.bfloat16
I32 = jnp.int32

LANES = 128
BF16_ROWS = 16
VMEM_LIMIT = 56 * 1024 * 1024

D_MODEL = 1024
GRID_W = 64
MLA_HEADS = 8
QK_NOPE = 64
QK_ROPE = 32
V_DIM = 64
Q_RANK = 768
KV_RANK = 256
CONV_CH = 512
CONV_WIDTH = 31
N_EXPERTS = 16
EC_FACTOR = 2
ROPE_BASE = 10000.0
EPS = 1e-6
N_MOD = 6

HEAD_PAD = LANES
QK_WIDTH = MLA_HEADS * HEAD_PAD
ROW_TILE = 256
FRONT_TILE = 512
ATTN_TILE = 512
ATTN_PACK = 512
BACK_TILE = 1024
BACK_SUB = 256
SLOT_WIN = 64
AUG = LANES
XS_W = D_MODEL + AUG
LOG2_E = 1.4426950408889634


def _params(n_axes, vmem=None):
    return pltpu.CompilerParams(
        dimension_semantics=("arbitrary",) * n_axes,
        vmem_limit_bytes=vmem)


def _rms(x, g):
    ms = jnp.mean(x * x, axis=-1, keepdims=True)
    return x * lax.rsqrt(ms + EPS) * g


def _bdot(a, b):
    return jnp.dot(a, b, preferred_element_type=F32)


def _adaln_kernel(c_ref, w_ref, b_ref, o_ref):
    c = c_ref[...]
    s = c * jax.nn.sigmoid(c)
    w = w_ref[...]
    sh = s.astype(BF16)
    sl = (s - sh.astype(F32)).astype(BF16)
    wh = w.astype(BF16)
    wl = (w - wh.astype(F32)).astype(BF16)
    o_ref[...] = _bdot(sh, wh) + (_bdot(sl, wh) + _bdot(sh, wl)) + b_ref[...]


def _adaln(cc, w_mod, b_mod):
    rows = cc.shape[0]
    n_out = w_mod.shape[1]
    blk = D_MODEL
    return pl.pallas_call(
        _adaln_kernel,
        out_shape=jax.ShapeDtypeStruct((rows, n_out), F32),
        grid=(n_out // blk,),
        in_specs=[pl.BlockSpec((rows, D_MODEL), lambda j: (0, 0)),
                  pl.BlockSpec((D_MODEL, blk), lambda j: (0, j)),
                  pl.BlockSpec((1, blk), lambda j: (0, j))],
        out_specs=pl.BlockSpec((rows, blk), lambda j: (0, j)),
        compiler_params=_params(1),
        name="adaln",
    )(cc, w_mod, b_mod)


def _rot_half(x, lane, base):
    rel = lane - base
    first = ((rel >= 0) & (rel < 8)) | ((rel >= 16) & (rel < 24))
    return jnp.where(first, -pltpu.roll(x, LANES - 8, axis=1), pltpu.roll(x, 8, axis=1))


def _front_kernel(rope, seq, x_ref, sh_ref, sc_ref, g_ref, wqc_ref, wkr_ref, wglu_ref,
                  qg_ref, wq_ref, kvg_ref, wk_ref, wv_ref, pk_ref, cw_ref, cb_ref, lg_ref, lb_ref,
                  *rest):
    if rope:
        cq_ref, sq_ref, q_ref, k_ref, v_ref, cv_ref, vpad_ref, shift_ref = rest
    else:
        q_ref, k_ref, v_ref, cv_ref, ckv_ref, kr_ref, vpad_ref, shift_ref = rest
    rows = x_ref.shape[0]
    steps_per_seq = max(seq // rows, 1)
    scale = (QK_NOPE + QK_ROPE) ** -0.5 * LOG2_E
    lane = lax.broadcasted_iota(I32, (ROW_TILE, LANES), 1)
    conv_refs = (cw_ref, cb_ref, lg_ref, lb_ref, shift_ref)

    pending = []

    def tick(n=1):
        for _ in range(n):
            if pending:
                pending.pop(0)()

    def gated_values(qd, row_in_seq, slot, cv_row):
        r = slice(qd * ROW_TILE, (qd + 1) * ROW_TILE)
        h = _rms(x_ref[r, :], g_ref[...]) * (1.0 + sc_ref[0]) + sh_ref[0]
        hb = h.astype(BF16)
        glu = _bdot(hb, wglu_ref[...])
        if row_in_seq == 0:
            vpad_ref[slot, 0:CONV_HALO, :] = jnp.zeros((CONV_HALO, CONV_CH), F32)
        if row_in_seq + ROW_TILE == seq:
            vpad_ref[slot, CONV_HALO + seq:, :] = jnp.zeros((CONV_HALO, CONV_CH), F32)
        vpad_ref[slot, CONV_HALO + row_in_seq:CONV_HALO + row_in_seq + ROW_TILE, :] = (
            glu[:, :CONV_CH] * jax.nn.sigmoid(glu[:, CONV_CH:]))

        ready = ([row_in_seq - ROW_TILE] if row_in_seq > 0 else []) + \
                ([row_in_seq] if row_in_seq + ROW_TILE == seq else [])
        for t0 in ready:
            steps, finish = _conv_stages(CONV_STAGES, vpad_ref, slot, t0, *conv_refs)
            dst = slice(cv_row + t0, cv_row + t0 + ROW_TILE)
            pending.extend(steps + [functools.partial(_store_conv, cv_ref, dst, finish)])
        return hb

    def projections(qd, row_in_seq, hb, ticks):
        r = slice(qd * ROW_TILE, (qd + 1) * ROW_TILE)
        qc = _bdot(hb, wqc_ref[...])
        tick(ticks)
        kr = _bdot(hb, wkr_ref[...])
        tick(ticks)
        qn = _rms(qc[:, :Q_RANK], qg_ref[...]).astype(BF16)
        q = _bdot(qn, wq_ref[...])
        tick(ticks)
        ckv = _rms(qc[:, Q_RANK:], kvg_ref[...])
        ckv_b = ckv.astype(BF16)
        kk = _bdot(ckv_b, wk_ref[...])
        tick(ticks)
        v_ref[r, :] = _bdot(ckv_b, wv_ref[...]).astype(BF16)
        tick(ticks)
        if rope:
            t = slice(row_in_seq % rows, row_in_seq % rows + ROW_TILE)
            cq = cq_ref[t, :]
            sq = sq_ref[t, :]
            for hd in range(MLA_HEADS):
                blk = q[:, hd * HEAD_PAD:(hd + 1) * HEAD_PAD]
                rot = blk * cq + _rot_half(blk, lane, QK_NOPE) * sq
                q_ref[r, hd * HEAD_PAD:(hd + 1) * HEAD_PAD] = (rot * scale).astype(BF16)
            krs = pltpu.roll(kr, QK_NOPE, axis=1)
            kr = pltpu.roll(krs * cq + _rot_half(krs, lane, QK_NOPE) * sq, LANES - QK_NOPE, axis=1)
            kr = jnp.where(lane < QK_ROPE, kr, 0.0)
        else:
            q_ref[r, :] = (q * scale).astype(BF16)
            ckv_ref[r, :] = ckv
            kr_ref[r, :] = kr[:, :QK_ROPE]
        tick(ticks)
        k_ref[r, :] = (kk + _bdot(kr.astype(BF16), pk_ref[...])).astype(BF16)

    def step(k):
        n_q = rows // ROW_TILE
        hbs = []
        for qd in range(n_q):
            row = k * rows + qd * ROW_TILE
            if seq >= rows:
                hbs.append((row, gated_values(qd, row, 0, 0)))
            else:
                hbs.append((row % seq, gated_values(qd, row % seq, row // seq, (row // seq) * seq)))
        ticks = -(-len(pending) // (6 * n_q))
        for qd, (row_in_seq, hb) in enumerate(hbs):
            projections(qd, row_in_seq, hb, ticks)
        tick(len(pending))

    if steps_per_seq == 1:
        step(0)
    else:
        for k in range(steps_per_seq):
            pl.when(pl.program_id(0) % steps_per_seq == k)(functools.partial(step, k))


def _store_conv(cv_ref, dst, finish):
    cv_ref[dst, :] = finish()


def _front(x2d, mod3, mod_row, seq, wts, conv_wts, rope_tabs):
    n = x2d.shape[0]
    rope = rope_tabs is not None
    tile = FRONT_TILE
    steps_per_seq = max(seq // tile, 1)
    assert seq % ROW_TILE == 0 and (tile % seq == 0 or seq % tile == 0)
    full = lambda a: pl.BlockSpec(a.shape, lambda i: (0,) * a.ndim)

    def mod_spec(col):
        if mod_row is None:
            assert seq % tile == 0
            return pl.BlockSpec((1, 1, D_MODEL), lambda i: (i // steps_per_seq, 0, col))
        return pl.BlockSpec((1, 1, D_MODEL), lambda i: (mod_row, 0, col))

    row = lambda w: pl.BlockSpec((tile, w), lambda i: (i, 0))
    cv_rows = max(seq, tile)
    cv_spec = pl.BlockSpec((cv_rows, CONV_CH), lambda i: (i // steps_per_seq, 0))
    in_specs = [row(D_MODEL), mod_spec(0), mod_spec(1)] + [full(w) for w in wts + conv_wts]
    args = [x2d, mod3, mod3] + list(wts + conv_wts)
    out_shape = [jax.ShapeDtypeStruct((n, QK_WIDTH), BF16)] * 3 + \
                [jax.ShapeDtypeStruct((n, CONV_CH), BF16)]
    out_specs = [row(QK_WIDTH)] * 3 + [cv_spec]
    if rope:
        tab = pl.BlockSpec((tile, LANES), lambda i: (i % steps_per_seq, 0))
        in_specs += [tab, tab]
        args += list(rope_tabs)
    else:
        out_shape += [jax.ShapeDtypeStruct((n, KV_RANK), F32),
                      jax.ShapeDtypeStruct((n, QK_ROPE), F32)]
        out_specs += [row(KV_RANK), row(QK_ROPE)]
    slots = max(tile // seq, 1)
    return pl.pallas_call(
        functools.partial(_front_kernel, rope, seq),
        out_shape=out_shape,
        grid=(n // tile,),
        in_specs=in_specs,
        out_specs=out_specs,
        scratch_shapes=[pltpu.VMEM((slots, seq + 2 * CONV_HALO, CONV_CH), F32),
                        pltpu.VMEM((SUBLANES - 1, SHIFT_ROWS, CONV_CH), F32)],
        compiler_params=_params(1, VMEM_LIMIT),
        name="front_rope" if rope else "front",
    )(*args)


CONV_PAD = (CONV_WIDTH - 1) // 2
CONV_HALO = 16
SUBLANES = 8
SHIFT_ROWS = ROW_TILE + 2 * CONV_HALO - SUBLANES


CONV_STAGES = 6


def _conv_stages(n_stages, vpad_ref, slot, t0, cw_ref, cb_ref, lg_ref, lb_ref, shift_ref):
    state = {}

    def taps(lo, hi):
        if lo == 0:
            win = vpad_ref[slot, t0:t0 + ROW_TILE + 2 * CONV_HALO, :]
            for ph in range(1, SUBLANES):
                shift_ref[ph - 1] = win[ph:ph + SHIFT_ROWS, :]
            state["acc"] = jnp.broadcast_to(cb_ref[...], (ROW_TILE, CONV_CH))
        acc = state["acc"]
        for t in range(lo, hi):
            blk, ph = divmod(t + CONV_HALO - CONV_PAD, SUBLANES)
            if ph == 0:
                tap = vpad_ref[slot, t0 + blk * SUBLANES:t0 + blk * SUBLANES + ROW_TILE, :]
            else:
                tap = shift_ref[ph - 1, blk * SUBLANES:blk * SUBLANES + ROW_TILE, :]
            acc = acc + tap * cw_ref[t:t + 1, :]
        state["acc"] = acc

    def finish():
        acc = state["acc"]
        mu = jnp.mean(acc, axis=-1, keepdims=True)
        cen = acc - mu
        var = jnp.mean(cen * cen, axis=-1, keepdims=True)
        ln = cen * lax.rsqrt(var + EPS) * lg_ref[...] + lb_ref[...]
        return (ln * jax.nn.sigmoid(ln)).astype(BF16)

    bounds = [CONV_WIDTH * s // n_stages for s in range(n_stages + 1)]
    stages = [functools.partial(taps, bounds[s], bounds[s + 1]) for s in range(n_stages)]
    return stages, finish


def _attn_kernel(n_kv, n_seq, with_ctx, q_ref, *rest):
    kv_refs = list(rest[:2 * n_kv])
    if with_ctx:
        ckv_ref, kr_ref, wk_ref, wv_ref, pk_ref, o_ref, kc_ref, vc_ref = rest[2 * n_kv:]

        @pl.when(pl.program_id(1) == 0)
        def _():
            cb = ckv_ref[...].astype(BF16)
            kc_ref[...] = (_bdot(cb, wk_ref[...]) +
                           _bdot(kr_ref[...].astype(BF16), pk_ref[...])).astype(BF16)
            vc_ref[...] = _bdot(cb, wv_ref[...]).astype(BF16)

        kv_refs += [kc_ref, vc_ref]
        n_kv += 1
    else:
        o_ref = rest[2 * n_kv]
    nt = (((1,), (1,)), ((), ()))
    tq = q_ref.shape[0] // n_seq
    work = [(b, hd) for b in range(n_seq) for hd in range(MLA_HEADS)]

    def kv_rows(ref, b):
        t = ref.shape[0] // n_seq
        return slice(b * t, (b + 1) * t)

    def scores(item):
        b, hd = item
        sl = slice(hd * HEAD_PAD, (hd + 1) * HEAD_PAD)
        qh = q_ref[b * tq:(b + 1) * tq, sl]
        return [lax.dot_general(qh, kv_refs[2 * j][kv_rows(kv_refs[2 * j], b), sl], nt,
                                preferred_element_type=F32) for j in range(n_kv)]

    outs = {}
    nxt = scores(work[0])
    for w, (b, hd) in enumerate(work):
        sl = slice(hd * HEAD_PAD, (hd + 1) * HEAD_PAD)
        ss = nxt
        if w + 1 < len(work):
            nxt = scores(work[w + 1])
        m = ss[0].max(axis=-1, keepdims=True)
        for s in ss[1:]:
            m = jnp.maximum(m, s.max(axis=-1, keepdims=True))
        ps = [jnp.exp2(s - m) for s in ss]
        l = ps[0].sum(axis=-1, keepdims=True)
        for p in ps[1:]:
            l = l + p.sum(axis=-1, keepdims=True)
        o = None
        for j in range(n_kv):
            v_ref = kv_refs[2 * j + 1]
            pv = _bdot(ps[j].astype(BF16), v_ref[kv_rows(v_ref, b), sl])
            o = pv if o is None else o + pv
        outs[hd] = o / l
        if hd % 2 == 1:
            o_ref[b * tq:(b + 1) * tq, (hd // 2) * LANES:(hd // 2 + 1) * LANES] = (
                outs[hd - 1] + outs[hd]).astype(BF16)


def _attention(q, kvs, batch, seq, ctx=None):
    n = q.shape[0]
    n_seq = max(ATTN_PACK // seq, 1) if seq <= ATTN_TILE else 1
    tile = min(seq, ATTN_TILE) * n_seq
    q_tiles = max(seq // tile, 1)
    assert batch % n_seq == 0
    full = lambda a: pl.BlockSpec(a.shape, lambda b, i: (0,) * a.ndim)
    row = lambda w: pl.BlockSpec((tile, w), lambda b, i: (b * q_tiles + i, 0))
    in_specs = [row(QK_WIDTH)]
    args = [q]
    for k, v, t in kvs:
        spec = pl.BlockSpec((n_seq * t, QK_WIDTH), lambda b, i: (b, 0))
        in_specs += [spec, spec]
        args += [k, v]
    scratch = []
    if ctx is not None:
        assert n_seq == 1
        ckv, kr, past, wk, wv, pk = ctx
        pk = pk[:QK_ROPE]
        in_specs += [pl.BlockSpec((past, KV_RANK), lambda b, i: (b, 0)),
                     pl.BlockSpec((past, QK_ROPE), lambda b, i: (b, 0)), full(wk), full(wv), full(pk)]
        args += [ckv, kr, wk, wv, pk]
        scratch = [pltpu.VMEM((past, QK_WIDTH), BF16), pltpu.VMEM((past, QK_WIDTH), BF16)]
    return pl.pallas_call(
        functools.partial(_attn_kernel, len(kvs), n_seq, ctx is not None),
        out_shape=jax.ShapeDtypeStruct((n, MLA_HEADS * V_DIM), BF16),
        grid=(batch // n_seq, q_tiles),
        in_specs=in_specs,
        out_specs=row(MLA_HEADS * V_DIM),
        scratch_shapes=scratch,
        compiler_params=_params(2, VMEM_LIMIT),
        name="attention",
    )(*args)


def _back_kernel(x_ref, attn_ref, cv_ref, woa_ref, woc_ref, g1_ref, sh_ref, sc_ref, fg_ref,
                 wrh_ref, wrl_ref, x1_ref, h2_ref, afft_ref):
    n_sub = x_ref.shape[0] // BACK_SUB
    rows = [slice(s * BACK_SUB, (s + 1) * BACK_SUB) for s in range(n_sub)]

    def project(r):
        return _bdot(attn_ref[r, :], woa_ref[...]) + _bdot(cv_ref[r, :], woc_ref[...])

    def normalise(r, mix):
        x1 = x_ref[r, :] + g1_ref[0] * mix
        x1_ref[r, :] = x1
        h2 = _rms(x1, fg_ref[...]) * (1.0 + sc_ref[0]) + sh_ref[0]
        h2b = h2.astype(BF16)
        h2_ref[r, :D_MODEL] = h2b
        h2l = (h2 - h2b.astype(F32)).astype(BF16)
        wrh = wrh_ref[...]
        return _bdot(h2b, wrh) + (_bdot(h2l, wrh) + _bdot(h2b, wrl_ref[...]))

    def route(s, logits):
        r = rows[s]
        lane = lax.broadcasted_iota(I32, logits.shape, 1)
        real = lane < N_EXPERTS
        m = jnp.max(jnp.where(real, logits, -jnp.inf), axis=-1, keepdims=True)
        e = jnp.exp(logits - m)
        aff = e / jnp.sum(jnp.where(real, e, 0.0), axis=-1, keepdims=True)
        aff_t = aff.T
        for blk in range(BACK_SUB // LANES):
            afft_ref[s * (BACK_SUB // LANES) + blk] = aff_t[:N_EXPERTS, blk * LANES:(blk + 1) * LANES]
        hi = aff.astype(BF16)
        r1 = aff - hi.astype(F32)
        mid = r1.astype(BF16)
        lo = (r1 - mid.astype(F32)).astype(BF16)
        zero = jnp.zeros_like(hi)
        h2_ref[r, D_MODEL:] = jnp.where(
            real, hi, jnp.where(lane < 2 * N_EXPERTS, mid, jnp.where(lane < 3 * N_EXPERTS, lo, zero)))

    mixes = {0: project(rows[0])}
    logits = {}
    for s in range(n_sub):
        if s + 1 < n_sub:
            mixes[s + 1] = project(rows[s + 1])
        logits[s] = normalise(rows[s], mixes.pop(s))
        if s > 0:
            route(s - 1, logits.pop(s - 1))
    route(n_sub - 1, logits.pop(n_sub - 1))


def _back(x2d, attn, cv, mod3, mod_row, wts, seq):
    n = x2d.shape[0]
    tile = BACK_TILE
    full = lambda a: pl.BlockSpec(a.shape, lambda i: (0,) * a.ndim)

    def mod_spec(col):
        if mod_row is None:
            assert seq % tile == 0
            return pl.BlockSpec((1, 1, D_MODEL), lambda i: (i // (seq // tile), 0, col))
        return pl.BlockSpec((1, 1, D_MODEL), lambda i: (mod_row, 0, col))

    row = lambda w: pl.BlockSpec((tile, w), lambda i: (i, 0))
    wo, fg, wrh, wrl = wts
    assert MLA_HEADS * V_DIM == CONV_CH == wo.shape[0] // 2
    half = lambda k: pl.BlockSpec((wo.shape[0] // 2, wo.shape[1]), lambda i: (k, 0))
    woa = woc = wo
    in_specs = [row(D_MODEL), row(MLA_HEADS * V_DIM), row(CONV_CH), half(0), half(1),
                mod_spec(2), mod_spec(3), mod_spec(4), full(fg), full(wrh), full(wrl)]
    return pl.pallas_call(
        _back_kernel,
        out_shape=[jax.ShapeDtypeStruct((n, D_MODEL), F32),
                   jax.ShapeDtypeStruct((n, XS_W), BF16),
                   jax.ShapeDtypeStruct((n // LANES, N_EXPERTS, LANES), F32)],
        grid=(n // tile,),
        in_specs=in_specs,
        out_specs=[row(D_MODEL), row(XS_W),
                   pl.BlockSpec((tile // LANES, N_EXPERTS, LANES), lambda i: (i, 0, 0))],
        compiler_params=_params(1, VMEM_LIMIT),
        name="back",
    )(x2d, attn, cv, woa, woc, mod3, mod3, mod3, fg, wrh, wrl)


def _route_kernel(caps, *refs):
    n_g = len(caps)
    affs = [r[...] for r in refs[:n_g]]
    pos_refs = refs[n_g:2 * n_g]
    base_refs = refs[2 * n_g:]
    ones = jnp.ones((LANES, LANES), BF16)
    l_r = lax.broadcasted_iota(I32, (LANES, LANES), 0)
    l_c = lax.broadcasted_iota(I32, (LANES, LANES), 1)
    tri = jnp.where(l_r <= l_c, 1.0, 0.0).astype(BF16)

    def lane_dot(mask3, w):
        r, e, _ = mask3.shape
        mb = jnp.where(mask3, 1.0, 0.0).astype(BF16).reshape(r * e, LANES)
        return _bdot(mb, w).reshape(r, e, LANES)

    def expert_count(mask3):
        return jnp.sum(lane_dot(mask3, ones), axis=0, keepdims=True)

    def excl_cumsum(mask3):
        rows = mask3.shape[0]
        tot = lane_dot(mask3, ones)
        run = tot
        s = 1
        while s < rows:
            run = run + jnp.concatenate([jnp.zeros((s,) + run.shape[1:], F32), run[:rows - s]], axis=0)
            s *= 2
        before = run - tot
        return before + lane_dot(mask3, tri) - jnp.where(mask3, 1.0, 0.0), before

    def step(t, thrs):
        bit = lax.shift_left(jnp.int32(1), 30 - t)
        out = []
        for aff, thr, cap in zip(affs, thrs, caps):
            cand = thr | bit
            cnt = expert_count(aff >= pltpu.bitcast(cand, F32))
            out.append(jnp.where(cnt >= cap, cand, thr))
        return tuple(out)

    init = tuple(jnp.zeros((1,) + a.shape[1:], I32) for a in affs)
    thrs = lax.fori_loop(0, 31, step, init)
    for aff, thr_bits, cap, pos_ref, base_ref in zip(affs, thrs, caps, pos_refs, base_refs):
        thr = pltpu.bitcast(thr_bits, F32)
        gt = aff > thr
        eq = aff == thr
        need = cap - expert_count(gt)
        tie_rank, _ = excl_cumsum(eq)
        sel = gt | (eq & (tie_rank < need))
        pos, before = excl_cumsum(sel)
        pos_ref[...] = jnp.where(sel, pos, -1.0).astype(I32)
        base_ref[...] = before


def _route(affs, caps):
    shapes = [a.shape for a in affs]
    return pl.pallas_call(
        functools.partial(_route_kernel, tuple(caps)),
        out_shape=[jax.ShapeDtypeStruct(s, I32) for s in shapes] +
                  [jax.ShapeDtypeStruct(s, F32) for s in shapes],
        compiler_params=pltpu.CompilerParams(vmem_limit_bytes=VMEM_LIMIT),
        name="route",
    )(*affs)


def _window(off_ref, e, i, w, cap):
    first = off_ref[i * N_EXPERTS + e] + w * SLOT_WIN
    return first, pl.multiple_of(jnp.minimum(first, cap - SLOT_WIN), BF16_ROWS)


def _one_hot_rows(pos_ref, wins, row0=0):
    r = lax.broadcasted_iota(I32, (SLOT_WIN, LANES), 0)
    blocks = []
    for e, (first, off) in enumerate(wins):
        halves = []
        for h in range(ROW_TILE // LANES):
            pe = pos_ref[row0 + h, e:e + 1, :]
            rel = jnp.where(pe >= first, pe - off, -1)
            halves.append(jnp.where(rel == r, 1.0, 0.0).astype(BF16))
        blocks.append(jnp.concatenate(halves, axis=1))
    return jnp.concatenate(blocks, axis=0)


DISPATCH_TILES = 2
FLUSH_PARTS = 4


def _dispatch_kernel(cap, n_steps, off_ref, nwin_ref, flush_ref, h2_ref, pos_ref, xs_hbm,
                     xs_ref, sems):
    i = pl.program_id(0)
    per = h2_ref.shape[1] // DISPATCH_TILES

    @pl.when(i == 0)
    def _():
        def zero(e, carry):
            xs_ref[e] = jnp.zeros(xs_ref.shape[1:], BF16)
            return carry
        lax.fori_loop(0, N_EXPERTS, zero, 0)

    def gather(t, w):
        tile = i * DISPATCH_TILES + t
        wins = [_window(off_ref, e, tile, w, cap) for e in range(N_EXPERTS)]
        p = _one_hot_rows(pos_ref, wins, t * (ROW_TILE // LANES))
        h2 = h2_ref[:, t * per:(t + 1) * per, :].reshape(ROW_TILE, XS_W)
        return wins, _bdot(p, h2)

    def merge(wins, slab):
        for e, (_, off) in enumerate(wins):
            cur = xs_ref[e, pl.ds(off, SLOT_WIN), :].astype(F32)
            xs_ref[e, pl.ds(off, SLOT_WIN), :] = (
                cur + slab[e * SLOT_WIN:(e + 1) * SLOT_WIN]).astype(BF16)

    for wins, slab in [gather(t, 0) for t in range(DISPATCH_TILES)]:
        merge(wins, slab)
    for t in range(DISPATCH_TILES):
        def extra(w, carry, t=t):
            merge(*gather(t, w))
            return carry
        lax.fori_loop(1, nwin_ref[i * DISPATCH_TILES + t], extra, 0)

    part = cap // FLUSH_PARTS

    def flush(q):
        rows = pl.ds(q * part, part)
        return pltpu.make_async_copy(xs_ref.at[:, rows, :], xs_hbm.at[:, rows, :], sems.at[q])

    for q in range(FLUSH_PARTS):
        @pl.when(i == flush_ref[q])
        def _(q=q):
            flush(q).start()

    @pl.when(i == n_steps - 1)
    def _():
        for q in range(FLUSH_PARTS):
            flush(q).wait()


def _tile_spec(stripes, width, tiles=1):
    return pl.BlockSpec((stripes, tiles * ROW_TILE // stripes, width), lambda i, *_: (0, i, 0))


def _dispatch(h2aug, pos, offs, nwin, cap, stripes):
    n = h2aug.shape[0]
    n_steps = n // (ROW_TILE * DISPATCH_TILES)
    low = jnp.min(offs.reshape(-1, N_EXPERTS), axis=1)[DISPATCH_TILES::DISPATCH_TILES]
    low = jnp.concatenate([low, jnp.full((1,), cap, I32)])
    bounds = (jnp.arange(FLUSH_PARTS, dtype=I32) + 1) * (cap // FLUSH_PARTS)
    flush = jnp.argmax(low[None, :] >= bounds[:, None], axis=1).astype(I32)
    return pl.pallas_call(
        functools.partial(_dispatch_kernel, cap, n_steps),
        out_shape=jax.ShapeDtypeStruct((N_EXPERTS, cap, XS_W), BF16),
        grid_spec=pltpu.PrefetchScalarGridSpec(
            num_scalar_prefetch=3,
            grid=(n_steps,),
            in_specs=[_tile_spec(stripes, XS_W, DISPATCH_TILES),
                      pl.BlockSpec((DISPATCH_TILES * ROW_TILE // LANES, N_EXPERTS, LANES),
                                   lambda i, *_: (i, 0, 0))],
            out_specs=pl.BlockSpec(memory_space=pl.ANY),
            scratch_shapes=[pltpu.VMEM((N_EXPERTS, cap, XS_W), BF16),
                            pltpu.SemaphoreType.DMA((FLUSH_PARTS,))]),
        compiler_params=_params(1, VMEM_LIMIT),
        name="dispatch",
    )(offs, nwin, flush, h2aug.reshape(stripes, n // stripes, XS_W), pos)


FF_CHUNK = 512
EXP_ROWS = 512


def _experts_kernel(n_groups, n_chunks, *refs):
    xs_refs = refs[:n_groups]
    wg_ref, wu_ref, wd_ref = refs[n_groups:n_groups + 3]
    ys_refs = refs[n_groups + 3:2 * n_groups + 3]
    acc_ref = refs[2 * n_groups + 3]
    e = pl.program_id(0)
    cap = xs_refs[0].shape[1]
    blocks = [(g, slice(r * EXP_ROWS, (r + 1) * EXP_ROWS))
              for g in range(n_groups) for r in range(cap // EXP_ROWS)]

    def chunk(first, last):
        wg = wg_ref[0].astype(BF16)
        wu = wu_ref[0].astype(BF16)
        wd = wd_ref[0].astype(BF16)

        def up(blk):
            g, rows = blk
            x = xs_refs[g][0, rows, :D_MODEL]
            return _bdot(x, wg), _bdot(x, wu)

        nxt = up(blocks[0])
        for k, (g, rows) in enumerate(blocks):
            a, u = nxt
            if k + 1 < len(blocks):
                nxt = up(blocks[k + 1])
            hm = (a * jax.nn.sigmoid(a) * u).astype(BF16)
            y = _bdot(hm, wd)
            if not first:
                y = acc_ref[g, rows, :] + y
            if last:
                aug = xs_refs[g][0, rows, D_MODEL:].astype(F32)
                lane = lax.broadcasted_iota(I32, aug.shape, 1)
                mine = (lane == e) | (lane == e + N_EXPERTS) | (lane == e + 2 * N_EXPERTS)
                gate = jnp.sum(jnp.where(mine, aug, 0.0), axis=-1, keepdims=True)
                ys_refs[g][0, rows, :] = (y * gate).astype(BF16)
            else:
                acc_ref[g, rows, :] = y

    for c in range(n_chunks):
        pl.when(pl.program_id(1) == c)(functools.partial(chunk, c == 0, c == n_chunks - 1))


def _experts(xs_list, w_gate, w_up, w_down):
    n_groups = len(xs_list)
    cap = xs_list[0].shape[1]
    ff = w_gate.shape[2]
    n_chunks = ff // FF_CHUNK
    xs_spec = pl.BlockSpec((1, cap, XS_W), lambda e, j: (e, 0, 0))
    up_spec = pl.BlockSpec((1, D_MODEL, FF_CHUNK), lambda e, j: (e, 0, j))
    down_spec = pl.BlockSpec((1, FF_CHUNK, D_MODEL), lambda e, j: (e, j, 0))
    ys_spec = pl.BlockSpec((1, cap, D_MODEL), lambda e, j: (e, 0, 0))
    return pl.pallas_call(
        functools.partial(_experts_kernel, n_groups, n_chunks),
        out_shape=[jax.ShapeDtypeStruct((N_EXPERTS, cap, D_MODEL), BF16)] * n_groups,
        grid=(N_EXPERTS, n_chunks),
        in_specs=[xs_spec] * n_groups + [up_spec, up_spec, down_spec],
        out_specs=[ys_spec] * n_groups,
        scratch_shapes=[pltpu.VMEM((n_groups, cap, D_MODEL), F32)],
        compiler_params=_params(2, VMEM_LIMIT),
        name="experts",
    )(*xs_list, w_gate, w_up, w_down)


COMBINE_TILES = 4


def _combine_kernel(cap, n_steps, off_ref, nwin_ref, ys_hbm, pos_ref, x1_ref, g2_ref, fg_ref,
                    y_ref, buf_ref, sems):
    i = pl.program_id(0)
    slot = i % 2
    per = x1_ref.shape[1] // COMBINE_TILES

    def copies(step, t, w, s):
        out = []
        for e in range(N_EXPERTS):
            _, off = _window(off_ref, e, step * COMBINE_TILES + t, w, cap)
            out.append(pltpu.make_async_copy(
                ys_hbm.at[e, pl.ds(off, SLOT_WIN), :],
                buf_ref.at[s, t, pl.ds(e * SLOT_WIN, SLOT_WIN), :],
                sems.at[s, t]))
        return out

    @pl.when(i == 0)
    def _():
        for t in range(COMBINE_TILES):
            for cp in copies(0, t, 0, 0):
                cp.start()

    @pl.when(i + 1 < n_steps)
    def _():
        for t in range(COMBINE_TILES):
            for cp in copies(i + 1, t, 0, 1 - slot):
                cp.start()

    tn = (((0,), (0,)), ((), ()))

    def one_hot(t, w):
        wins = [_window(off_ref, e, i * COMBINE_TILES + t, w, cap) for e in range(N_EXPERTS)]
        return _one_hot_rows(pos_ref, wins, t * (ROW_TILE // LANES))

    def scattered(t, w, p):
        for cp in copies(i, t, w, slot):
            cp.wait()
        return lax.dot_general(p, buf_ref[slot, t], tn, preferred_element_type=F32)

    ps = [one_hot(t, 0) for t in range(COMBINE_TILES)]
    moes = [scattered(t, 0, ps[t]) for t in range(COMBINE_TILES)]
    for t in range(COMBINE_TILES):
        def extra_window(w, acc, t=t):
            for cp in copies(i, t, w, slot):
                cp.start()
            return acc + scattered(t, w, one_hot(t, w))
        moe = lax.fori_loop(1, nwin_ref[i * COMBINE_TILES + t], extra_window, moes[t])
        rows = slice(t * per, (t + 1) * per)
        x1 = x1_ref[:, rows, :]
        out = x1 + g2_ref[...] * moe.reshape(x1.shape)
        y_ref[:, rows, :] = _rms(out, fg_ref[...])


def _combine(ys, pos, offs, nwin, x1, mod3, mod_row, stripes, final_g, cap):
    n = x1.shape[0]
    n_steps = n // (ROW_TILE * COMBINE_TILES)
    g2 = pl.BlockSpec((stripes, 1, D_MODEL), lambda i, *_: (0 if mod_row is None else mod_row, 0, 5))
    y = pl.pallas_call(
        functools.partial(_combine_kernel, cap, n_steps),
        out_shape=jax.ShapeDtypeStruct((stripes, n // stripes, D_MODEL), F32),
        grid_spec=pltpu.PrefetchScalarGridSpec(
            num_scalar_prefetch=2,
            grid=(n_steps,),
            in_specs=[pl.BlockSpec(memory_space=pl.ANY),
                      pl.BlockSpec((COMBINE_TILES * ROW_TILE // LANES, N_EXPERTS, LANES),
                                   lambda i, *_: (i, 0, 0)),
                      _tile_spec(stripes, D_MODEL, COMBINE_TILES),
                      g2,
                      pl.BlockSpec((1, D_MODEL), lambda i, *_: (0, 0))],
            out_specs=_tile_spec(stripes, D_MODEL, COMBINE_TILES),
            scratch_shapes=[pltpu.VMEM((2, COMBINE_TILES, N_EXPERTS * SLOT_WIN, D_MODEL), BF16),
                            pltpu.SemaphoreType.DMA((2, COMBINE_TILES))]),
        compiler_params=_params(1, VMEM_LIMIT),
        name="combine",
    )(offs, nwin, ys, pos, x1.reshape(stripes, n // stripes, D_MODEL), mod3, final_g)
    return y.reshape(n, D_MODEL)


def _head_cols(w3):
    k, _, d = w3.shape
    return jnp.pad(w3, ((0, 0), (0, 0), (0, HEAD_PAD - d))).reshape(k, QK_WIDTH)


def _value_cols(v3):
    k = v3.shape[0]
    pair = v3.reshape(k, MLA_HEADS // 2, 2, V_DIM)
    zeros = jnp.zeros((k, MLA_HEADS // 2, V_DIM), v3.dtype)
    even = jnp.concatenate([pair[:, :, 0], zeros], axis=-1)
    odd = jnp.concatenate([zeros, pair[:, :, 1]], axis=-1)
    return jnp.stack([even, odd], axis=2).reshape(k, QK_WIDTH)


def _rope_tables(seq):
    half = QK_ROPE // 2
    pos = jnp.arange(seq)
    inv_freq = 1.0 / (ROPE_BASE ** (jnp.arange(0, half, 2, dtype=F32) / half))

    def cs(p):
        ang = p.astype(F32)[:, None] * inv_freq[None, :]
        ang = jnp.concatenate([ang, ang], axis=-1)
        return jnp.cos(ang), jnp.sin(ang)

    cr, sr = cs(pos // GRID_W)
    cc, sc = cs(pos % GRID_W)
    ones = jnp.ones((seq, QK_NOPE), F32)
    zpad = jnp.zeros((seq, HEAD_PAD - QK_NOPE - QK_ROPE), F32)
    cq = jnp.concatenate([ones, cr, cc, zpad], axis=-1)
    sq = jnp.concatenate([0.0 * ones, sr, sc, zpad], axis=-1)
    return cq, sq


def _prep_weights(w_in, q_norm_g, w_q_up, kv_norm_g, w_kv_up, attn_norm_g):
    wqc = w_in[:, :Q_RANK + KV_RANK].astype(BF16)
    wkr = jnp.pad(w_in[:, Q_RANK + KV_RANK:Q_RANK + KV_RANK + QK_ROPE],
                  ((0, 0), (0, LANES - QK_ROPE))).astype(BF16)
    wglu = w_in[:, Q_RANK + KV_RANK + QK_ROPE:].astype(BF16)
    wq = _head_cols(w_q_up.astype(BF16).reshape(Q_RANK, MLA_HEADS, QK_NOPE + QK_ROPE))
    kv3 = w_kv_up.astype(BF16).reshape(KV_RANK, MLA_HEADS, QK_NOPE + V_DIM)
    wk = _head_cols(kv3[:, :, :QK_NOPE])
    wv = _value_cols(kv3[:, :, QK_NOPE:])
    pk = np.zeros((LANES, QK_WIDTH), np.float32)
    for hd in range(MLA_HEADS):
        for j in range(QK_ROPE):
            pk[j, hd * HEAD_PAD + QK_NOPE + j] = 1.0
    pk = jnp.asarray(pk, BF16)
    return (attn_norm_g[None, :], wqc, wkr, wglu, q_norm_g[None, :], wq,
            kv_norm_g[None, :], wk, wv, pk)


def _stripe_order(aff3, stripes, seq):
    per = ROW_TILE // stripes
    q = LANES // per
    assert per * stripes == ROW_TILE and per * q == LANES and stripes % q == 0 and seq % LANES == 0
    a = aff3.reshape(stripes // q, q, seq // LANES, N_EXPERTS, q, per)
    return a.transpose(2, 4, 0, 3, 1, 5).reshape(aff3.shape)


def _route_plan(base, cap):
    rows_per_tile = ROW_TILE // LANES
    start = base[::rows_per_tile, :, 0].astype(I32)
    end = jnp.concatenate([start[1:], jnp.full((1, N_EXPERTS), cap, I32)], axis=0)
    off = (start // BF16_ROWS) * BF16_ROWS
    span = end - off
    nwin = jnp.maximum(jnp.max((span + SLOT_WIN - 1) // SLOT_WIN, axis=1), 1).astype(I32)
    return off.reshape(-1), nwin


def kernel(x_prompt, x_sample, cache_ckv, cache_krope, c, c_ctx, w_mod, b_mod, attn_norm_g,
           w_in, q_norm_g, w_q_up, kv_norm_g, w_kv_up, conv_w, conv_b, conv_ln_g, conv_ln_b,
           w_out, ffn_norm_g, w_router, w_gate, w_up, w_down, final_norm_g):
    depth = w_mod.shape[0]
    assert depth == 1
    bp, sp, _ = x_prompt.shape
    bs, ss, _ = x_sample.shape
    past = cache_ckv.shape[2]
    ctx_row = bs

    mod_rows = 16
    cc = jnp.concatenate([c, c_ctx[None, :], jnp.zeros((mod_rows - bs - 1, D_MODEL), F32)], axis=0)
    mod = _adaln(cc, w_mod[0], b_mod[0][None, :])
    mod3 = mod.reshape(mod_rows, 1, N_MOD * D_MODEL)

    fw = _prep_weights(w_in[0], q_norm_g[0], w_q_up[0], kv_norm_g[0], w_kv_up[0], attn_norm_g[0])
    wk, wv, pk = fw[7], fw[8], fw[9]
    wo = w_out[0].astype(BF16)
    wr = jnp.pad(jnp.tile(w_router[0], (1, 3)), ((0, 0), (0, LANES - 3 * N_EXPERTS)))
    wrh = wr.astype(BF16)
    wrl = (wr - wrh.astype(F32)).astype(BF16)
    cvw = (conv_w[0], conv_b[0][None, :], conv_ln_g[0][None, :], conv_ln_b[0][None, :])
    bw = (wo, ffn_norm_g[0][None, :], wrh, wrl)

    xp2 = x_prompt.reshape(bp * sp, D_MODEL)
    xs2 = x_sample.reshape(bs * ss, D_MODEL)

    qp, kp, vp, cv_p, ckvp, krp = _front(xp2, mod3, ctx_row, sp, fw, cvw, None)
    attn_p = _attention(qp, [(kp, vp, sp)], bp, sp)
    x1p, h2p, afftp = _back(xp2, attn_p, cv_p, mod3, ctx_row, bw, sp)

    qs, ks, vs, cv_s = _front(xs2, mod3, None, ss, fw, cvw, _rope_tables(ss))
    ctx = (cache_ckv[:, 0].reshape(bs * past, KV_RANK),
           cache_krope[:, 0].reshape(bs * past, QK_ROPE), past, wk, wv, pk)
    attn_s = _attention(qs, [(ks, vs, ss)], bs, ss, ctx)
    x1s, h2s, affts = _back(xs2, attn_s, cv_s, mod3, None, bw, ss)

    stripes = (1, bs)
    caps = [EC_FACTOR * h2.shape[0] // N_EXPERTS for h2 in (h2p, h2s)]
    routed = _route([afftp, _stripe_order(affts, bs, ss)], caps)
    plans = []
    xs_list = []
    for g, (h2, cap) in enumerate(zip((h2p, h2s), caps)):
        pos, base = routed[g], routed[len(caps) + g]
        offs, nwin = _route_plan(base, cap)
        plans.append((pos, offs, nwin, cap))
        xs_list.append(_dispatch(h2, pos, offs, nwin, cap, stripes[g]))
    ys_list = _experts(xs_list, w_gate[0], w_up[0], w_down[0])

    fg = final_norm_g[None, :]
    pos, offs, nwin, cap = plans[0]
    y_prompt = _combine(ys_list[0], pos, offs, nwin, x1p, mod3, ctx_row, stripes[0], fg, cap)
    pos, offs, nwin, cap = plans[1]
    y_sample = _combine(ys_list[1], pos, offs, nwin, x1s, mod3, None, stripes[1], fg, cap)

    return (y_prompt.reshape(bp, sp, D_MODEL), y_sample.reshape(bs, ss, D_MODEL),
            ckvp.reshape(bp, depth, sp, KV_RANK), krp.reshape(bp, depth, sp, QK_ROPE))
```

```python
import functools

import jax
import jax.numpy as jnp
import numpy as np
from jax import lax
from jax.experimental import pallas as pl
from jax.experimental.pallas import tpu as pltpu

F32 = jnp.float32
BF16 = jnp.bfloat16
I32 = jnp.int32

LANES = 128
BF16_ROWS = 16
VMEM_LIMIT = 56 * 1024 * 1024

D_MODEL = 1024
GRID_W = 64
MLA_HEADS = 8
QK_NOPE = 64
QK_ROPE = 32
V_DIM = 64
Q_RANK = 768
KV_RANK = 256
CONV_CH = 512
CONV_WIDTH = 31
N_EXPERTS = 16
EC_FACTOR = 2
ROPE_BASE = 10000.0
EPS = 1e-6
N_MOD = 6

HEAD_PAD = LANES
QK_WIDTH = MLA_HEADS * HEAD_PAD
ROW_TILE = 256
FRONT_TILE = 512
ATTN_TILE = 512
ATTN_PACK = 512
BACK_TILE = 1024
BACK_SUB = 256
SLOT_WIN = 64
AUG = LANES
XS_W = D_MODEL + AUG
LOG2_E = 1.4426950408889634


def _params(n_axes, vmem=None):
    return pltpu.CompilerParams(
        dimension_semantics=("arbitrary",) * n_axes,
        vmem_limit_bytes=vmem)


def _rms(x, g):
    ms = jnp.mean(x * x, axis=-1, keepdims=True)
    return x * lax.rsqrt(ms + EPS) * g


def _bdot(a, b):
    return jnp.dot(a, b, preferred_element_type=F32)


def _adaln_kernel(c_ref, w_ref, b_ref, o_ref):
    c = c_ref[...]
    s = c * jax.nn.sigmoid(c)
    w = w_ref[...]
    sh = s.astype(BF16)
    sl = (s - sh.astype(F32)).astype(BF16)
    wh = w.astype(BF16)
    wl = (w - wh.astype(F32)).astype(BF16)
    o_ref[...] = _bdot(sh, wh) + (_bdot(sl, wh) + _bdot(sh, wl)) + b_ref[...]


def _adaln(cc, w_mod, b_mod):
    rows = cc.shape[0]
    n_out = w_mod.shape[1]
    blk = 2 * D_MODEL
    return pl.pallas_call(
        _adaln_kernel,
        out_shape=jax.ShapeDtypeStruct((rows, n_out), F32),
        grid=(n_out // blk,),
        in_specs=[pl.BlockSpec((rows, D_MODEL), lambda j: (0, 0)),
                  pl.BlockSpec((D_MODEL, blk), lambda j: (0, j)),
                  pl.BlockSpec((1, blk), lambda j: (0, j))],
        out_specs=pl.BlockSpec((rows, blk), lambda j: (0, j)),
        compiler_params=_params(1),
        name="adaln",
    )(cc, w_mod, b_mod)


def _rot_half(x, lane, base):
    rel = lane - base
    first = ((rel >= 0) & (rel < 8)) | ((rel >= 16) & (rel < 24))
    return jnp.where(first, -pltpu.roll(x, LANES - 8, axis=1), pltpu.roll(x, 8, axis=1))


def _front_kernel(rope, seq, x_ref, sh_ref, sc_ref, g_ref, wqc_ref, wkr_ref, wglu_ref,
                  qg_ref, wq_ref, kvg_ref, wk_ref, wv_ref, pk_ref, cw_ref, cb_ref, lg_ref, lb_ref,
                  *rest):
    if rope:
        cq_ref, sq_ref, q_ref, k_ref, v_ref, cv_ref, vpad_ref, shift_ref = rest
    else:
        q_ref, k_ref, v_ref, cv_ref, ckv_ref, kr_ref, vpad_ref, shift_ref = rest
    rows = x_ref.shape[0]
    steps_per_seq = max(seq // rows, 1)
    scale = (QK_NOPE + QK_ROPE) ** -0.5 * LOG2_E
    lane = lax.broadcasted_iota(I32, (ROW_TILE, LANES), 1)
    conv_refs = (cw_ref, cb_ref, lg_ref, lb_ref, shift_ref)

    pending = []

    def tick(n=1):
        for _ in range(n):
            if pending:
                pending.pop(0)()

    def gated_values(qd, row_in_seq, slot, cv_row):
        r = slice(qd * ROW_TILE, (qd + 1) * ROW_TILE)
        h = _rms(x_ref[r, :], g_ref[...]) * (1.0 + sc_ref[0]) + sh_ref[0]
        hb = h.astype(BF16)
        glu = _bdot(hb, wglu_ref[...])
        if row_in_seq == 0:
            vpad_ref[slot, 0:CONV_HALO, :] = jnp.zeros((CONV_HALO, CONV_CH), F32)
        if row_in_seq + ROW_TILE == seq:
            vpad_ref[slot, CONV_HALO + seq:, :] = jnp.zeros((CONV_HALO, CONV_CH), F32)
        vpad_ref[slot, CONV_HALO + row_in_seq:CONV_HALO + row_in_seq + ROW_TILE, :] = (
            glu[:, :CONV_CH] * jax.nn.sigmoid(glu[:, CONV_CH:]))

        ready = ([row_in_seq - ROW_TILE] if row_in_seq > 0 else []) + \
                ([row_in_seq] if row_in_seq + ROW_TILE == seq else [])
        for t0 in ready:
            steps, finish = _conv_stages(CONV_STAGES, vpad_ref, slot, t0, *conv_refs)
            dst = slice(cv_row + t0, cv_row + t0 + ROW_TILE)
            pending.extend(steps + [functools.partial(_store_conv, cv_ref, dst, finish)])
        return hb

    def projections(qd, row_in_seq, hb, ticks):
        r = slice(qd * ROW_TILE, (qd + 1) * ROW_TILE)
        qc = _bdot(hb, wqc_ref[...])
        tick(ticks)
        kr = _bdot(hb, wkr_ref[...])
        tick(ticks)
        qn = _rms(qc[:, :Q_RANK], qg_ref[...]).astype(BF16)
        q = _bdot(qn, wq_ref[...])
        tick(ticks)
        ckv = _rms(qc[:, Q_RANK:], kvg_ref[...])
        ckv_b = ckv.astype(BF16)
        kk = _bdot(ckv_b, wk_ref[...])
        tick(ticks)
        v_ref[r, :] = _bdot(ckv_b, wv_ref[...]).astype(BF16)
        tick(ticks)
        if rope:
            t = slice(row_in_seq % rows, row_in_seq % rows + ROW_TILE)
            cq = cq_ref[t, :]
            sq = sq_ref[t, :]
            for hd in range(MLA_HEADS):
                blk = q[:, hd * HEAD_PAD:(hd + 1) * HEAD_PAD]
                rot = blk * cq + _rot_half(blk, lane, QK_NOPE) * sq
                q_ref[r, hd * HEAD_PAD:(hd + 1) * HEAD_PAD] = (rot * scale).astype(BF16)
            krs = pltpu.roll(kr, QK_NOPE, axis=1)
            kr = pltpu.roll(krs * cq + _rot_half(krs, lane, QK_NOPE) * sq, LANES - QK_NOPE, axis=1)
            kr = jnp.where(lane < QK_ROPE, kr, 0.0)
        else:
            q_ref[r, :] = (q * scale).astype(BF16)
            ckv_ref[r, :] = ckv
            kr_ref[r, :] = kr[:, :QK_ROPE]
        tick(ticks)
        k_ref[r, :] = (kk + _bdot(kr.astype(BF16), pk_ref[...])).astype(BF16)

    def step(k):
        n_q = rows // ROW_TILE
        hbs = []
        for qd in range(n_q):
            row = k * rows + qd * ROW_TILE
            if seq >= rows:
                hbs.append((row, gated_values(qd, row, 0, 0)))
            else:
                hbs.append((row % seq, gated_values(qd, row % seq, row // seq, (row // seq) * seq)))
        ticks = -(-len(pending) // (6 * n_q))
        for qd, (row_in_seq, hb) in enumerate(hbs):
            projections(qd, row_in_seq, hb, ticks)
        tick(len(pending))

    if steps_per_seq == 1:
        step(0)
    else:
        for k in range(steps_per_seq):
            pl.when(pl.program_id(0) % steps_per_seq == k)(functools.partial(step, k))


def _store_conv(cv_ref, dst, finish):
    cv_ref[dst, :] = finish()


def _front(x2d, mod3, mod_row, seq, wts, conv_wts, rope_tabs):
    n = x2d.shape[0]
    rope = rope_tabs is not None
    tile = FRONT_TILE
    steps_per_seq = max(seq // tile, 1)
    assert seq % ROW_TILE == 0 and (tile % seq == 0 or seq % tile == 0)
    full = lambda a: pl.BlockSpec(a.shape, lambda i: (0,) * a.ndim)

    def mod_spec(col):
        if mod_row is None:
            assert seq % tile == 0
            return pl.BlockSpec((1, 1, D_MODEL), lambda i: (i // steps_per_seq, 0, col))
        return pl.BlockSpec((1, 1, D_MODEL), lambda i: (mod_row, 0, col))

    row = lambda w: pl.BlockSpec((tile, w), lambda i: (i, 0))
    cv_rows = max(seq, tile)
    cv_spec = pl.BlockSpec((cv_rows, CONV_CH), lambda i: (i // steps_per_seq, 0))
    in_specs = [row(D_MODEL), mod_spec(0), mod_spec(1)] + [full(w) for w in wts + conv_wts]
    args = [x2d, mod3, mod3] + list(wts + conv_wts)
    out_shape = [jax.ShapeDtypeStruct((n, QK_WIDTH), BF16)] * 3 + \
                [jax.ShapeDtypeStruct((n, CONV_CH), BF16)]
    out_specs = [row(QK_WIDTH)] * 3 + [cv_spec]
    if rope:
        tab = pl.BlockSpec((tile, LANES), lambda i: (i % steps_per_seq, 0))
        in_specs += [tab, tab]
        args += list(rope_tabs)
    else:
        out_shape += [jax.ShapeDtypeStruct((n, KV_RANK), F32),
                      jax.ShapeDtypeStruct((n, QK_ROPE), F32)]
        out_specs += [row(KV_RANK), row(QK_ROPE)]
    slots = max(tile // seq, 1)
    return pl.pallas_call(
        functools.partial(_front_kernel, rope, seq),
        out_shape=out_shape,
        grid=(n // tile,),
        in_specs=in_specs,
        out_specs=out_specs,
        scratch_shapes=[pltpu.VMEM((slots, seq + 2 * CONV_HALO, CONV_CH), F32),
                        pltpu.VMEM((SUBLANES - 1, SHIFT_ROWS, CONV_CH), F32)],
        compiler_params=_params(1, VMEM_LIMIT),
        name="front_rope" if rope else "front",
    )(*args)


CONV_PAD = (CONV_WIDTH - 1) // 2
CONV_HALO = 16
SUBLANES = 8
SHIFT_ROWS = ROW_TILE + 2 * CONV_HALO - SUBLANES


CONV_STAGES = 6


def _conv_stages(n_stages, vpad_ref, slot, t0, cw_ref, cb_ref, lg_ref, lb_ref, shift_ref):
    state = {}

    def taps(lo, hi):
        if lo == 0:
            win = vpad_ref[slot, t0:t0 + ROW_TILE + 2 * CONV_HALO, :]
            for ph in range(1, SUBLANES):
                shift_ref[ph - 1] = win[ph:ph + SHIFT_ROWS, :]
            state["acc"] = jnp.broadcast_to(cb_ref[...], (ROW_TILE, CONV_CH))
        acc = state["acc"]
        for t in range(lo, hi):
            blk, ph = divmod(t + CONV_HALO - CONV_PAD, SUBLANES)
            if ph == 0:
                tap = vpad_ref[slot, t0 + blk * SUBLANES:t0 + blk * SUBLANES + ROW_TILE, :]
            else:
                tap = shift_ref[ph - 1, blk * SUBLANES:blk * SUBLANES + ROW_TILE, :]
            acc = acc + tap * cw_ref[t:t + 1, :]
        state["acc"] = acc

    def finish():
        acc = state["acc"]
        mu = jnp.mean(acc, axis=-1, keepdims=True)
        cen = acc - mu
        var = jnp.mean(cen * cen, axis=-1, keepdims=True)
        ln = cen * lax.rsqrt(var + EPS) * lg_ref[...] + lb_ref[...]
        return (ln * jax.nn.sigmoid(ln)).astype(BF16)

    bounds = [CONV_WIDTH * s // n_stages for s in range(n_stages + 1)]
    stages = [functools.partial(taps, bounds[s], bounds[s + 1]) for s in range(n_stages)]
    return stages, finish


def _attn_kernel(n_kv, n_seq, with_ctx, q_ref, *rest):
    kv_refs = list(rest[:2 * n_kv])
    if with_ctx:
        ckv_ref, kr_ref, wk_ref, wv_ref, pk_ref, o_ref, kc_ref, vc_ref = rest[2 * n_kv:]

        @pl.when(pl.program_id(1) == 0)
        def _():
            cb = ckv_ref[...].astype(BF16)
            kc_ref[...] = (_bdot(cb, wk_ref[...]) +
                           _bdot(kr_ref[...].astype(BF16), pk_ref[...])).astype(BF16)
            vc_ref[...] = _bdot(cb, wv_ref[...]).astype(BF16)

        kv_refs += [kc_ref, vc_ref]
        n_kv += 1
    else:
        o_ref = rest[2 * n_kv]
    nt = (((1,), (1,)), ((), ()))
    tq = q_ref.shape[0] // n_seq
    work = [(b, hd) for b in range(n_seq) for hd in range(MLA_HEADS)]

    def kv_rows(ref, b):
        t = ref.shape[0] // n_seq
        return slice(b * t, (b + 1) * t)

    def scores(item):
        b, hd = item
        sl = slice(hd * HEAD_PAD, (hd + 1) * HEAD_PAD)
        qh = q_ref[b * tq:(b + 1) * tq, sl]
        return [lax.dot_general(qh, kv_refs[2 * j][kv_rows(kv_refs[2 * j], b), sl], nt,
                                preferred_element_type=F32) for j in range(n_kv)]

    outs = {}
    nxt = scores(work[0])
    for w, (b, hd) in enumerate(work):
        sl = slice(hd * HEAD_PAD, (hd + 1) * HEAD_PAD)
        ss = nxt
        if w + 1 < len(work):
            nxt = scores(work[w + 1])
        m = ss[0].max(axis=-1, keepdims=True)
        for s in ss[1:]:
            m = jnp.maximum(m, s.max(axis=-1, keepdims=True))
        ps = [jnp.exp2(s - m) for s in ss]
        l = ps[0].sum(axis=-1, keepdims=True)
        for p in ps[1:]:
            l = l + p.sum(axis=-1, keepdims=True)
        o = None
        for j in range(n_kv):
            v_ref = kv_refs[2 * j + 1]
            pv = _bdot(ps[j].astype(BF16), v_ref[kv_rows(v_ref, b), sl])
            o = pv if o is None else o + pv
        outs[hd] = o / l
        if hd % 2 == 1:
            o_ref[b * tq:(b + 1) * tq, (hd // 2) * LANES:(hd // 2 + 1) * LANES] = (
                outs[hd - 1] + outs[hd]).astype(BF16)


def _attention(q, kvs, batch, seq, ctx=None):
    n = q.shape[0]
    n_seq = max(ATTN_PACK // seq, 1) if seq <= ATTN_TILE else 1
    tile = min(seq, ATTN_TILE) * n_seq
    q_tiles = max(seq // tile, 1)
    assert batch % n_seq == 0
    full = lambda a: pl.BlockSpec(a.shape, lambda b, i: (0,) * a.ndim)
    row = lambda w: pl.BlockSpec((tile, w), lambda b, i: (b * q_tiles + i, 0))
    in_specs = [row(QK_WIDTH)]
    args = [q]
    for k, v, t in kvs:
        spec = pl.BlockSpec((n_seq * t, QK_WIDTH), lambda b, i: (b, 0))
        in_specs += [spec, spec]
        args += [k, v]
    scratch = []
    if ctx is not None:
        assert n_seq == 1
        ckv, kr, past, wk, wv, pk = ctx
        pk = pk[:QK_ROPE]
        in_specs += [pl.BlockSpec((past, KV_RANK), lambda b, i: (b, 0)),
                     pl.BlockSpec((past, QK_ROPE), lambda b, i: (b, 0)), full(wk), full(wv), full(pk)]
        args += [ckv, kr, wk, wv, pk]
        scratch = [pltpu.VMEM((past, QK_WIDTH), BF16), pltpu.VMEM((past, QK_WIDTH), BF16)]
    return pl.pallas_call(
        functools.partial(_attn_kernel, len(kvs), n_seq, ctx is not None),
        out_shape=jax.ShapeDtypeStruct((n, MLA_HEADS * V_DIM), BF16),
        grid=(batch // n_seq, q_tiles),
        in_specs=in_specs,
        out_specs=row(MLA_HEADS * V_DIM),
        scratch_shapes=scratch,
        compiler_params=_params(2, VMEM_LIMIT),
        name="attention",
    )(*args)


def _back_kernel(x_ref, attn_ref, cv_ref, woa_ref, woc_ref, g1_ref, sh_ref, sc_ref, fg_ref,
                 wrh_ref, wrl_ref, x1_ref, h2_ref, afft_ref):
    n_sub = x_ref.shape[0] // BACK_SUB
    rows = [slice(s * BACK_SUB, (s + 1) * BACK_SUB) for s in range(n_sub)]

    def project(r):
        return _bdot(attn_ref[r, :], woa_ref[...]) + _bdot(cv_ref[r, :], woc_ref[...])

    def normalise(r, mix):
        x1 = x_ref[r, :] + g1_ref[0] * mix
        x1_ref[r, :] = x1
        h2 = _rms(x1, fg_ref[...]) * (1.0 + sc_ref[0]) + sh_ref[0]
        h2b = h2.astype(BF16)
        h2_ref[r, :D_MODEL] = h2b
        h2l = (h2 - h2b.astype(F32)).astype(BF16)
        wrh = wrh_ref[...]
        return _bdot(h2b, wrh) + (_bdot(h2l, wrh) + _bdot(h2b, wrl_ref[...]))

    def route(s, logits):
        r = rows[s]
        lane = lax.broadcasted_iota(I32, logits.shape, 1)
        real = lane < N_EXPERTS
        m = jnp.max(jnp.where(real, logits, -jnp.inf), axis=-1, keepdims=True)
        e = jnp.exp(logits - m)
        aff = e / jnp.sum(jnp.where(real, e, 0.0), axis=-1, keepdims=True)
        aff_t = aff.T
        for blk in range(BACK_SUB // LANES):
            afft_ref[s * (BACK_SUB // LANES) + blk] = aff_t[:N_EXPERTS, blk * LANES:(blk + 1) * LANES]
        hi = aff.astype(BF16)
        r1 = aff - hi.astype(F32)
        mid = r1.astype(BF16)
        lo = (r1 - mid.astype(F32)).astype(BF16)
        zero = jnp.zeros_like(hi)
        h2_ref[r, D_MODEL:] = jnp.where(
            real, hi, jnp.where(lane < 2 * N_EXPERTS, mid, jnp.where(lane < 3 * N_EXPERTS, lo, zero)))

    mixes = {0: project(rows[0])}
    logits = {}
    for s in range(n_sub):
        if s + 1 < n_sub:
            mixes[s + 1] = project(rows[s + 1])
        logits[s] = normalise(rows[s], mixes.pop(s))
        if s > 0:
            route(s - 1, logits.pop(s - 1))
    route(n_sub - 1, logits.pop(n_sub - 1))


def _back(x2d, attn, cv, mod3, mod_row, wts, seq):
    n = x2d.shape[0]
    tile = BACK_TILE
    full = lambda a: pl.BlockSpec(a.shape, lambda i: (0,) * a.ndim)

    def mod_spec(col):
        if mod_row is None:
            assert seq % tile == 0
            return pl.BlockSpec((1, 1, D_MODEL), lambda i: (i // (seq // tile), 0, col))
        return pl.BlockSpec((1, 1, D_MODEL), lambda i: (mod_row, 0, col))

    row = lambda w: pl.BlockSpec((tile, w), lambda i: (i, 0))
    wo, fg, wrh, wrl = wts
    assert MLA_HEADS * V_DIM == CONV_CH == wo.shape[0] // 2
    half = lambda k: pl.BlockSpec((wo.shape[0] // 2, wo.shape[1]), lambda i: (k, 0))
    woa = woc = wo
    in_specs = [row(D_MODEL), row(MLA_HEADS * V_DIM), row(CONV_CH), half(0), half(1),
                mod_spec(2), mod_spec(3), mod_spec(4), full(fg), full(wrh), full(wrl)]
    return pl.pallas_call(
        _back_kernel,
        out_shape=[jax.ShapeDtypeStruct((n, D_MODEL), F32),
                   jax.ShapeDtypeStruct((n, XS_W), BF16),
                   jax.ShapeDtypeStruct((n // LANES, N_EXPERTS, LANES), F32)],
        grid=(n // tile,),
        in_specs=in_specs,
        out_specs=[row(D_MODEL), row(XS_W),
                   pl.BlockSpec((tile // LANES, N_EXPERTS, LANES), lambda i: (i, 0, 0))],
        compiler_params=_params(1, VMEM_LIMIT),
        name="back",
    )(x2d, attn, cv, woa, woc, mod3, mod3, mod3, fg, wrh, wrl)


def _route_kernel(caps, *refs):
    n_g = len(caps)
    affs = [r[...] for r in refs[:n_g]]
    pos_refs = refs[n_g:2 * n_g]
    base_refs = refs[2 * n_g:]
    ones = jnp.ones((LANES, LANES), BF16)
    l_r = lax.broadcasted_iota(I32, (LANES, LANES), 0)
    l_c = lax.broadcasted_iota(I32, (LANES, LANES), 1)
    tri = jnp.where(l_r <= l_c, 1.0, 0.0).astype(BF16)

    def lane_dot(mask3, w):
        r, e, _ = mask3.shape
        mb = jnp.where(mask3, 1.0, 0.0).astype(BF16).reshape(r * e, LANES)
        return _bdot(mb, w).reshape(r, e, LANES)

    def expert_count(mask3):
        return jnp.sum(lane_dot(mask3, ones), axis=0, keepdims=True)

    def excl_cumsum(mask3):
        rows = mask3.shape[0]
        tot = lane_dot(mask3, ones)
        run = tot
        s = 1
        while s < rows:
            run = run + jnp.concatenate([jnp.zeros((s,) + run.shape[1:], F32), run[:rows - s]], axis=0)
            s *= 2
        before = run - tot
        return before + lane_dot(mask3, tri) - jnp.where(mask3, 1.0, 0.0), before

    def step(t, thrs):
        bit = lax.shift_left(jnp.int32(1), 30 - t)
        out = []
        for aff, thr, cap in zip(affs, thrs, caps):
            cand = thr | bit
            cnt = expert_count(aff >= pltpu.bitcast(cand, F32))
            out.append(jnp.where(cnt >= cap, cand, thr))
        return tuple(out)

    init = tuple(jnp.zeros((1,) + a.shape[1:], I32) for a in affs)
    thrs = lax.fori_loop(0, 31, step, init)
    for aff, thr_bits, cap, pos_ref, base_ref in zip(affs, thrs, caps, pos_refs, base_refs):
        thr = pltpu.bitcast(thr_bits, F32)
        gt = aff > thr
        eq = aff == thr
        need = cap - expert_count(gt)
        tie_rank, _ = excl_cumsum(eq)
        sel = gt | (eq & (tie_rank < need))
        pos, before = excl_cumsum(sel)
        pos_ref[...] = jnp.where(sel, pos, -1.0).astype(I32)
        base_ref[...] = before


def _route(affs, caps):
    shapes = [a.shape for a in affs]
    return pl.pallas_call(
        functools.partial(_route_kernel, tuple(caps)),
        out_shape=[jax.ShapeDtypeStruct(s, I32) for s in shapes] +
                  [jax.ShapeDtypeStruct(s, F32) for s in shapes],
        compiler_params=pltpu.CompilerParams(vmem_limit_bytes=VMEM_LIMIT),
        name="route",
    )(*affs)


def _window(off_ref, e, i, w, cap):
    first = off_ref[i * N_EXPERTS + e] + w * SLOT_WIN
    return first, pl.multiple_of(jnp.minimum(first, cap - SLOT_WIN), BF16_ROWS)


def _one_hot_rows(pos_ref, wins, row0=0):
    r = lax.broadcasted_iota(I32, (SLOT_WIN, LANES), 0)
    blocks = []
    for e, (first, off) in enumerate(wins):
        halves = []
        for h in range(ROW_TILE // LANES):
            pe = pos_ref[row0 + h, e:e + 1, :]
            rel = jnp.where(pe >= first, pe - off, -1)
            halves.append(jnp.where(rel == r, 1.0, 0.0).astype(BF16))
        blocks.append(jnp.concatenate(halves, axis=1))
    return jnp.concatenate(blocks, axis=0)


DISPATCH_TILES = 2
FLUSH_PARTS = 4


def _dispatch_kernel(cap, n_steps, off_ref, nwin_ref, flush_ref, h2_ref, pos_ref, xs_hbm,
                     xs_ref, sems):
    i = pl.program_id(0)
    per = h2_ref.shape[1] // DISPATCH_TILES

    @pl.when(i == 0)
    def _():
        def zero(e, carry):
            xs_ref[e] = jnp.zeros(xs_ref.shape[1:], BF16)
            return carry
        lax.fori_loop(0, N_EXPERTS, zero, 0)

    def gather(t, w):
        tile = i * DISPATCH_TILES + t
        wins = [_window(off_ref, e, tile, w, cap) for e in range(N_EXPERTS)]
        p = _one_hot_rows(pos_ref, wins, t * (ROW_TILE // LANES))
        h2 = h2_ref[:, t * per:(t + 1) * per, :].reshape(ROW_TILE, XS_W)
        return wins, _bdot(p, h2)

    def merge(wins, slab):
        for e, (_, off) in enumerate(wins):
            cur = xs_ref[e, pl.ds(off, SLOT_WIN), :].astype(F32)
            xs_ref[e, pl.ds(off, SLOT_WIN), :] = (
                cur + slab[e * SLOT_WIN:(e + 1) * SLOT_WIN]).astype(BF16)

    for wins, slab in [gather(t, 0) for t in range(DISPATCH_TILES)]:
        merge(wins, slab)
    for t in range(DISPATCH_TILES):
        def extra(w, carry, t=t):
            merge(*gather(t, w))
            return carry
        lax.fori_loop(1, nwin_ref[i * DISPATCH_TILES + t], extra, 0)

    part = cap // FLUSH_PARTS

    def flush(q):
        rows = pl.ds(q * part, part)
        return pltpu.make_async_copy(xs_ref.at[:, rows, :], xs_hbm.at[:, rows, :], sems.at[q])

    for q in range(FLUSH_PARTS):
        @pl.when(i == flush_ref[q])
        def _(q=q):
            flush(q).start()

    @pl.when(i == n_steps - 1)
    def _():
        for q in range(FLUSH_PARTS):
            flush(q).wait()


def _tile_spec(stripes, width, tiles=1):
    return pl.BlockSpec((stripes, tiles * ROW_TILE // stripes, width), lambda i, *_: (0, i, 0))


def _dispatch(h2aug, pos, offs, nwin, cap, stripes):
    n = h2aug.shape[0]
    n_steps = n // (ROW_TILE * DISPATCH_TILES)
    low = jnp.min(offs.reshape(-1, N_EXPERTS), axis=1)[DISPATCH_TILES::DISPATCH_TILES]
    low = jnp.concatenate([low, jnp.full((1,), cap, I32)])
    bounds = (jnp.arange(FLUSH_PARTS, dtype=I32) + 1) * (cap // FLUSH_PARTS)
    flush = jnp.argmax(low[None, :] >= bounds[:, None], axis=1).astype(I32)
    return pl.pallas_call(
        functools.partial(_dispatch_kernel, cap, n_steps),
        out_shape=jax.ShapeDtypeStruct((N_EXPERTS, cap, XS_W), BF16),
        grid_spec=pltpu.PrefetchScalarGridSpec(
            num_scalar_prefetch=3,
            grid=(n_steps,),
            in_specs=[_tile_spec(stripes, XS_W, DISPATCH_TILES),
                      pl.BlockSpec((DISPATCH_TILES * ROW_TILE // LANES, N_EXPERTS, LANES),
                                   lambda i, *_: (i, 0, 0))],
            out_specs=pl.BlockSpec(memory_space=pl.ANY),
            scratch_shapes=[pltpu.VMEM((N_EXPERTS, cap, XS_W), BF16),
                            pltpu.SemaphoreType.DMA((FLUSH_PARTS,))]),
        compiler_params=_params(1, VMEM_LIMIT),
        name="dispatch",
    )(offs, nwin, flush, h2aug.reshape(stripes, n // stripes, XS_W), pos)


FF_CHUNK = 512
EXP_ROWS = 512


def _experts_kernel(n_groups, n_chunks, *refs):
    xs_refs = refs[:n_groups]
    wg_ref, wu_ref, wd_ref = refs[n_groups:n_groups + 3]
    ys_refs = refs[n_groups + 3:2 * n_groups + 3]
    acc_ref = refs[2 * n_groups + 3]
    e = pl.program_id(0)
    cap = xs_refs[0].shape[1]
    blocks = [(g, slice(r * EXP_ROWS, (r + 1) * EXP_ROWS))
              for g in range(n_groups) for r in range(cap // EXP_ROWS)]

    def chunk(first, last):
        wg = wg_ref[0].astype(BF16)
        wu = wu_ref[0].astype(BF16)
        wd = wd_ref[0].astype(BF16)

        def up(blk):
            g, rows = blk
            x = xs_refs[g][0, rows, :D_MODEL]
            return _bdot(x, wg), _bdot(x, wu)

        nxt = up(blocks[0])
        for k, (g, rows) in enumerate(blocks):
            a, u = nxt
            if k + 1 < len(blocks):
                nxt = up(blocks[k + 1])
            hm = (a * jax.nn.sigmoid(a) * u).astype(BF16)
            y = _bdot(hm, wd)
            if not first:
                y = acc_ref[g, rows, :] + y
            if last:
                aug = xs_refs[g][0, rows, D_MODEL:].astype(F32)
                lane = lax.broadcasted_iota(I32, aug.shape, 1)
                mine = (lane == e) | (lane == e + N_EXPERTS) | (lane == e + 2 * N_EXPERTS)
                gate = jnp.sum(jnp.where(mine, aug, 0.0), axis=-1, keepdims=True)
                ys_refs[g][0, rows, :] = (y * gate).astype(BF16)
            else:
                acc_ref[g, rows, :] = y

    for c in range(n_chunks):
        pl.when(pl.program_id(1) == c)(functools.partial(chunk, c == 0, c == n_chunks - 1))


def _experts(xs_list, w_gate, w_up, w_down):
    n_groups = len(xs_list)
    cap = xs_list[0].shape[1]
    ff = w_gate.shape[2]
    n_chunks = ff // FF_CHUNK
    xs_spec = pl.BlockSpec((1, cap, XS_W), lambda e, j: (e, 0, 0))
    up_spec = pl.BlockSpec((1, D_MODEL, FF_CHUNK), lambda e, j: (e, 0, j))
    down_spec = pl.BlockSpec((1, FF_CHUNK, D_MODEL), lambda e, j: (e, j, 0))
    ys_spec = pl.BlockSpec((1, cap, D_MODEL), lambda e, j: (e, 0, 0))
    return pl.pallas_call(
        functools.partial(_experts_kernel, n_groups, n_chunks),
        out_shape=[jax.ShapeDtypeStruct((N_EXPERTS, cap, D_MODEL), BF16)] * n_groups,
        grid=(N_EXPERTS, n_chunks),
        in_specs=[xs_spec] * n_groups + [up_spec, up_spec, down_spec],
        out_specs=[ys_spec] * n_groups,
        scratch_shapes=[pltpu.VMEM((n_groups, cap, D_MODEL), F32)],
        compiler_params=_params(2, VMEM_LIMIT),
        name="experts",
    )(*xs_list, w_gate, w_up, w_down)


COMBINE_TILES = 4


def _combine_kernel(cap, n_steps, off_ref, nwin_ref, ys_hbm, pos_ref, x1_ref, g2_ref, fg_ref,
                    y_ref, buf_ref, sems):
    i = pl.program_id(0)
    slot = i % 2
    per = x1_ref.shape[1] // COMBINE_TILES

    def copies(step, t, w, s):
        out = []
        for e in range(N_EXPERTS):
            _, off = _window(off_ref, e, step * COMBINE_TILES + t, w, cap)
            out.append(pltpu.make_async_copy(
                ys_hbm.at[e, pl.ds(off, SLOT_WIN), :],
                buf_ref.at[s, t, pl.ds(e * SLOT_WIN, SLOT_WIN), :],
                sems.at[s, t]))
        return out

    @pl.when(i == 0)
    def _():
        for t in range(COMBINE_TILES):
            for cp in copies(0, t, 0, 0):
                cp.start()

    @pl.when(i + 1 < n_steps)
    def _():
        for t in range(COMBINE_TILES):
            for cp in copies(i + 1, t, 0, 1 - slot):
                cp.start()

    tn = (((0,), (0,)), ((), ()))

    def one_hot(t, w):
        wins = [_window(off_ref, e, i * COMBINE_TILES + t, w, cap) for e in range(N_EXPERTS)]
        return _one_hot_rows(pos_ref, wins, t * (ROW_TILE // LANES))

    def scattered(t, w, p):
        for cp in copies(i, t, w, slot):
            cp.wait()
        return lax.dot_general(p, buf_ref[slot, t], tn, preferred_element_type=F32)

    ps = [one_hot(t, 0) for t in range(COMBINE_TILES)]
    moes = [scattered(t, 0, ps[t]) for t in range(COMBINE_TILES)]
    for t in range(COMBINE_TILES):
        def extra_window(w, acc, t=t):
            for cp in copies(i, t, w, slot):
                cp.start()
            return acc + scattered(t, w, one_hot(t, w))
        moe = lax.fori_loop(1, nwin_ref[i * COMBINE_TILES + t], extra_window, moes[t])
        rows = slice(t * per, (t + 1) * per)
        x1 = x1_ref[:, rows, :]
        out = x1 + g2_ref[...] * moe.reshape(x1.shape)
        y_ref[:, rows, :] = _rms(out, fg_ref[...])


def _combine(ys, pos, offs, nwin, x1, mod3, mod_row, stripes, final_g, cap):
    n = x1.shape[0]
    n_steps = n // (ROW_TILE * COMBINE_TILES)
    g2 = pl.BlockSpec((stripes, 1, D_MODEL), lambda i, *_: (0 if mod_row is None else mod_row, 0, 5))
    y = pl.pallas_call(
        functools.partial(_combine_kernel, cap, n_steps),
        out_shape=jax.ShapeDtypeStruct((stripes, n // stripes, D_MODEL), F32),
        grid_spec=pltpu.PrefetchScalarGridSpec(
            num_scalar_prefetch=2,
            grid=(n_steps,),
            in_specs=[pl.BlockSpec(memory_space=pl.ANY),
                      pl.BlockSpec((COMBINE_TILES * ROW_TILE // LANES, N_EXPERTS, LANES),
                                   lambda i, *_: (i, 0, 0)),
                      _tile_spec(stripes, D_MODEL, COMBINE_TILES),
                      g2,
                      pl.BlockSpec((1, D_MODEL), lambda i, *_: (0, 0))],
            out_specs=_tile_spec(stripes, D_MODEL, COMBINE_TILES),
            scratch_shapes=[pltpu.VMEM((2, COMBINE_TILES, N_EXPERTS * SLOT_WIN, D_MODEL), BF16),
                            pltpu.SemaphoreType.DMA((2, COMBINE_TILES))]),
        compiler_params=_params(1, VMEM_LIMIT),
        name="combine",
    )(offs, nwin, ys, pos, x1.reshape(stripes, n // stripes, D_MODEL), mod3, final_g)
    return y.reshape(n, D_MODEL)


def _head_cols(w3):
    k, _, d = w3.shape
    return jnp.pad(w3, ((0, 0), (0, 0), (0, HEAD_PAD - d))).reshape(k, QK_WIDTH)


def _value_cols(v3):
    k = v3.shape[0]
    pair = v3.reshape(k, MLA_HEADS // 2, 2, V_DIM)
    zeros = jnp.zeros((k, MLA_HEADS // 2, V_DIM), v3.dtype)
    even = jnp.concatenate([pair[:, :, 0], zeros], axis=-1)
    odd = jnp.concatenate([zeros, pair[:, :, 1]], axis=-1)
    return jnp.stack([even, odd], axis=2).reshape(k, QK_WIDTH)


def _rope_tables(seq):
    half = QK_ROPE // 2
    pos = jnp.arange(seq)
    inv_freq = 1.0 / (ROPE_BASE ** (jnp.arange(0, half, 2, dtype=F32) / half))

    def cs(p):
        ang = p.astype(F32)[:, None] * inv_freq[None, :]
        ang = jnp.concatenate([ang, ang], axis=-1)
        return jnp.cos(ang), jnp.sin(ang)

    cr, sr = cs(pos // GRID_W)
    cc, sc = cs(pos % GRID_W)
    ones = jnp.ones((seq, QK_NOPE), F32)
    zpad = jnp.zeros((seq, HEAD_PAD - QK_NOPE - QK_ROPE), F32)
    cq = jnp.concatenate([ones, cr, cc, zpad], axis=-1)
    sq = jnp.concatenate([0.0 * ones, sr, sc, zpad], axis=-1)
    return cq, sq


def _prep_weights(w_in, q_norm_g, w_q_up, kv_norm_g, w_kv_up, attn_norm_g):
    wqc = w_in[:, :Q_RANK + KV_RANK].astype(BF16)
    wkr = jnp.pad(w_in[:, Q_RANK + KV_RANK:Q_RANK + KV_RANK + QK_ROPE],
                  ((0, 0), (0, LANES - QK_ROPE))).astype(BF16)
    wglu = w_in[:, Q_RANK + KV_RANK + QK_ROPE:].astype(BF16)
    wq = _head_cols(w_q_up.astype(BF16).reshape(Q_RANK, MLA_HEADS, QK_NOPE + QK_ROPE))
    kv3 = w_kv_up.astype(BF16).reshape(KV_RANK, MLA_HEADS, QK_NOPE + V_DIM)
    wk = _head_cols(kv3[:, :, :QK_NOPE])
    wv = _value_cols(kv3[:, :, QK_NOPE:])
    pk = np.zeros((LANES, QK_WIDTH), np.float32)
    for hd in range(MLA_HEADS):
        for j in range(QK_ROPE):
            pk[j, hd * HEAD_PAD + QK_NOPE + j] = 1.0
    pk = jnp.asarray(pk, BF16)
    return (attn_norm_g[None, :], wqc, wkr, wglu, q_norm_g[None, :], wq,
            kv_norm_g[None, :], wk, wv, pk)


def _stripe_order(aff3, stripes, seq):
    per = ROW_TILE // stripes
    q = LANES // per
    assert per * stripes == ROW_TILE and per * q == LANES and stripes % q == 0 and seq % LANES == 0
    a = aff3.reshape(stripes // q, q, seq // LANES, N_EXPERTS, q, per)
    return a.transpose(2, 4, 0, 3, 1, 5).reshape(aff3.shape)


def _route_plan(base, cap):
    rows_per_tile = ROW_TILE // LANES
    start = base[::rows_per_tile, :, 0].astype(I32)
    end = jnp.concatenate([start[1:], jnp.full((1, N_EXPERTS), cap, I32)], axis=0)
    off = (start // BF16_ROWS) * BF16_ROWS
    span = end - off
    nwin = jnp.maximum(jnp.max((span + SLOT_WIN - 1) // SLOT_WIN, axis=1), 1).astype(I32)
    return off.reshape(-1), nwin


def kernel(x_prompt, x_sample, cache_ckv, cache_krope, c, c_ctx, w_mod, b_mod, attn_norm_g,
           w_in, q_norm_g, w_q_up, kv_norm_g, w_kv_up, conv_w, conv_b, conv_ln_g, conv_ln_b,
           w_out, ffn_norm_g, w_router, w_gate, w_up, w_down, final_norm_g):
    depth = w_mod.shape[0]
    assert depth == 1
    bp, sp, _ = x_prompt.shape
    bs, ss, _ = x_sample.shape
    past = cache_ckv.shape[2]
    ctx_row = bs

    mod_rows = 16
    cc = jnp.concatenate([c, c_ctx[None, :], jnp.zeros((mod_rows - bs - 1, D_MODEL), F32)], axis=0)
    mod = _adaln(cc, w_mod[0], b_mod[0][None, :])
    mod3 = mod.reshape(mod_rows, 1, N_MOD * D_MODEL)

    fw = _prep_weights(w_in[0], q_norm_g[0], w_q_up[0], kv_norm_g[0], w_kv_up[0], attn_norm_g[0])
    wk, wv, pk = fw[7], fw[8], fw[9]
    wo = w_out[0].astype(BF16)
    wr = jnp.pad(jnp.tile(w_router[0], (1, 3)), ((0, 0), (0, LANES - 3 * N_EXPERTS)))
    wrh = wr.astype(BF16)
    wrl = (wr - wrh.astype(F32)).astype(BF16)
    cvw = (conv_w[0], conv_b[0][None, :], conv_ln_g[0][None, :], conv_ln_b[0][None, :])
    bw = (wo, ffn_norm_g[0][None, :], wrh, wrl)

    xp2 = x_prompt.reshape(bp * sp, D_MODEL)
    xs2 = x_sample.reshape(bs * ss, D_MODEL)

    qp, kp, vp, cv_p, ckvp, krp = _front(xp2, mod3, ctx_row, sp, fw, cvw, None)
    attn_p = _attention(qp, [(kp, vp, sp)], bp, sp)
    x1p, h2p, afftp = _back(xp2, attn_p, cv_p, mod3, ctx_row, bw, sp)

    qs, ks, vs, cv_s = _front(xs2, mod3, None, ss, fw, cvw, _rope_tables(ss))
    ctx = (cache_ckv[:, 0].reshape(bs * past, KV_RANK),
           cache_krope[:, 0].reshape(bs * past, QK_ROPE), past, wk, wv, pk)
    attn_s = _attention(qs, [(ks, vs, ss)], bs, ss, ctx)
    x1s, h2s, affts = _back(xs2, attn_s, cv_s, mod3, None, bw, ss)

    stripes = (1, bs)
    caps = [EC_FACTOR * h2.shape[0] // N_EXPERTS for h2 in (h2p, h2s)]
    routed = _route([afftp, _stripe_order(affts, bs, ss)], caps)
    plans = []
    xs_list = []
    for g, (h2, cap) in enumerate(zip((h2p, h2s), caps)):
        pos, base = routed[g], routed[len(caps) + g]
        offs, nwin = _route_plan(base, cap)
        plans.append((pos, offs, nwin, cap))
        xs_list.append(_dispatch(h2, pos, offs, nwin, cap, stripes[g]))
    ys_list = _experts(xs_list, w_gate[0], w_up[0], w_down[0])

    fg = final_norm_g[None, :]
    pos, offs, nwin, cap = plans[0]
    y_prompt = _combine(ys_list[0], pos, offs, nwin, x1p, mod3, ctx_row, stripes[0], fg, cap)
    pos, offs, nwin, cap = plans[1]
    y_sample = _combine(ys_list[1], pos, offs, nwin, x1s, mod3, None, stripes[1], fg, cap)

    return (y_prompt.reshape(bp, sp, D_MODEL), y_sample.reshape(bs, ss, D_MODEL),
            ckvp.reshape(bp, depth, sp, KV_RANK), krp.reshape(bp, depth, sp, QK_ROPE))
```
